```python
import math
import jax, jax.numpy as jnp
from jax import lax
import numpy as np

D_MODEL = 1024
BATCH = 4
SEQ = 4096
DEPTH = 1

CHUNK = 64
MEM_LEN = 256
Q_BLOCK = 128
ROPE_THETA = 10000.0
NORM_EPS = 1e-6

SB_HEADS = D_MODEL // 128
SB_HEAD_DIM = 64
SB_WIDTH = SB_HEADS * SB_HEAD_DIM
DIFF_HEADS = D_MODEL // 256
DIFF_HEAD_DIM = 64
DIFF_V_DIM = 2 * DIFF_HEAD_DIM
DIFF_QK_WIDTH = DIFF_HEADS * 2 * DIFF_HEAD_DIM
DIFF_WIDTH = DIFF_HEADS * DIFF_V_DIM
N_BRANCH = 2
N_IN = 3 * SB_WIDTH + 2 * DIFF_QK_WIDTH + DIFF_WIDTH + N_BRANCH * D_MODEL
MEM_HEADS = 4
MEM_HEAD_DIM = D_MODEL // MEM_HEADS
N_EXPERTS = 32
TOP_K = 4
D_FF = D_MODEL
SWIGLU_LIMIT = 7.0
SWIGLU_ALPHA = 1.702
MOE_BLOCK = 128

kernel_name = "hybrid_stickbreak_diffattn_moe_block"


def rms_norm(x, g):
    xf = x.astype(jnp.float32)
    y = xf * lax.rsqrt(jnp.mean(xf * xf, axis=-1, keepdims=True) + NORM_EPS)
    return (y * g.astype(jnp.float32)).astype(x.dtype)


def split_heads(t, n_heads):
    b, s, _ = t.shape
    return t.reshape(b, s, n_heads, -1).transpose(0, 2, 1, 3)


def merge_heads(t):
    b, n, s, d = t.shape
    return t.transpose(0, 2, 1, 3).reshape(b, s, n * d)


def rope(t, positions):
    half = t.shape[-1] // 2
    inv = ROPE_THETA ** (-jnp.arange(half, dtype=jnp.float32) / half)
    ang = positions.astype(jnp.float32)[:, None, :, None] * inv
    cos, sin = jnp.cos(ang), jnp.sin(ang)
    tf = t.astype(jnp.float32)
    t1, t2 = tf[..., :half], tf[..., half:]
    return jnp.concatenate([t1 * cos - t2 * sin, t2 * cos + t1 * sin], axis=-1).astype(t.dtype)


def to_blocks(t):
    b, h, s, d = t.shape
    return t.reshape(b, h, s // Q_BLOCK, Q_BLOCK, d).transpose(2, 0, 1, 3, 4)


def from_blocks(t):
    nb, b, h, qb, d = t.shape
    return t.transpose(1, 2, 0, 3, 4).reshape(b, h, nb * qb, d)


def stick_breaking_attention(q, k, v):
    s_len = q.shape[2]
    scale = 1.0 / math.sqrt(q.shape[-1])
    key_pos = jnp.arange(s_len)

    def block(args):
        qb, start = args
        z = jnp.einsum('bhqd,bhkd->bhqk', qb, k, preferred_element_type=jnp.float32) * scale
        q_pos = start + jnp.arange(Q_BLOCK)
        mask = key_pos[None, :] < q_pos[:, None]
        log_rem = jnp.where(mask, jax.nn.log_sigmoid(-z), 0.0)
        after = lax.cumsum(log_rem, axis=log_rem.ndim - 1, reverse=True) - log_rem
        w = jnp.where(mask, jnp.exp(jax.nn.log_sigmoid(z) + after), 0.0)
        return jnp.einsum('bhqk,bhkd->bhqd', w.astype(v.dtype), v)

    starts = jnp.arange(s_len // Q_BLOCK) * Q_BLOCK
    return from_blocks(lax.map(block, (to_blocks(q), starts)))


def differential_attention(q1, q2, k1, k2, v, lam):
    s_len = q1.shape[2]
    scale = 1.0 / math.sqrt(q1.shape[-1])
    key_chunk = jnp.arange(s_len) // CHUNK

    def block(args):
        q1b, q2b, start = args
        q_chunk = (start + jnp.arange(Q_BLOCK)) // CHUNK
        mask = key_chunk[None, :] <= q_chunk[:, None]

        def probs(qb, kk):
            z = jnp.einsum('bhqd,bhkd->bhqk', qb, kk, preferred_element_type=jnp.float32) * scale
            return jax.nn.softmax(jnp.where(mask, z, -jnp.inf), axis=-1)

        p = probs(q1b, k1) - lam * probs(q2b, k2)
        return jnp.einsum('bhqk,bhkd->bhqd', p.astype(v.dtype), v)

    starts = jnp.arange(s_len // Q_BLOCK) * Q_BLOCK
    return from_blocks(lax.map(block, (to_blocks(q1), to_blocks(q2), starts)))


def memory_cross_attention(h, hm, w_q, w_kv, w_o):
    q = split_heads(h @ w_q, MEM_HEADS)
    k, v = jnp.split(hm @ w_kv, 2, axis=-1)
    k, v = split_heads(k, MEM_HEADS), split_heads(v, MEM_HEADS)
    z = jnp.einsum('bhqd,bhkd->bhqk', q, k, preferred_element_type=jnp.float32) / math.sqrt(MEM_HEAD_DIM)
    p = jax.nn.softmax(z, axis=-1)
    o = jnp.einsum('bhqk,bhkd->bhqd', p.astype(v.dtype), v)
    return merge_heads(o) @ w_o


def moe_ffn(h, w_router, b_router, w_gu, b_gu, w_down, b_down):
    b, s, d = h.shape
    n_tok = b * s
    xf = h.reshape(n_tok, d)
    logits = (xf @ w_router).astype(jnp.float32) + b_router.astype(jnp.float32)
    top_val, top_idx = lax.top_k(logits, TOP_K)
    gate = jax.nn.softmax(top_val, axis=-1)

    n_assign = n_tok * TOP_K
    flat_e = top_idx.reshape(-1)
    order = jnp.argsort(flat_e)
    sorted_e = flat_e[order]
    sorted_tok = (order // TOP_K).astype(jnp.int32)
    sorted_gate = gate.reshape(-1)[order]
    counts = jnp.bincount(flat_e, length=N_EXPERTS)
    padded = (counts + MOE_BLOCK - 1) // MOE_BLOCK * MOE_BLOCK
    start = jnp.cumsum(counts) - counts
    pend = jnp.cumsum(padded)
    pstart = pend - padded
    dest = pstart[sorted_e] + jnp.arange(n_assign) - start[sorted_e]
    n_blocks = n_assign // MOE_BLOCK + N_EXPERTS
    n_rows = n_blocks * MOE_BLOCK
    row_tok = jnp.zeros((n_rows,), jnp.int32).at[dest].set(sorted_tok)
    row_gate = jnp.zeros((n_rows,), jnp.float32).at[dest].set(sorted_gate)
    block_expert = jnp.clip(jnp.searchsorted(pend, jnp.arange(n_blocks) * MOE_BLOCK, side='right'),
                            0, N_EXPERTS - 1)

    def run_block(args):
        tok, g, e = args
        xb = xf[tok]
        hgu = xb @ w_gu[e] + b_gu[e]
        glu, lin = jnp.split(hgu, 2, axis=-1)
        glu = jnp.minimum(glu, SWIGLU_LIMIT)
        lin = jnp.clip(lin, -SWIGLU_LIMIT, SWIGLU_LIMIT)
        act = glu * jax.nn.sigmoid(SWIGLU_ALPHA * glu) * (lin + 1.0)
        y = act @ w_down[e] + b_down[e]
        return y.astype(jnp.float32) * g[:, None]

    ys = lax.map(run_block, (row_tok.reshape(n_blocks, MOE_BLOCK),
                             row_gate.reshape(n_blocks, MOE_BLOCK), block_expert))
    out = jnp.zeros((n_tok, d), jnp.float32).at[row_tok].add(ys.reshape(n_rows, d))
    return out.astype(h.dtype).reshape(b, s, d)


def setup_inputs(seed: int = 0) -> dict:
    key = jax.random.key(seed)
    ks = jax.random.split(key, 28)
    f32 = jnp.float32

    def nrm(k, shape, scale):
        return jax.random.normal(k, shape, f32) * scale

    def gain(k, shape):
        return 1.0 + 0.05 * jax.random.normal(k, shape, f32)

    offset = jax.random.randint(ks[2], (BATCH, 1), 0, 4096, dtype=jnp.int32)
    positions = (offset + jnp.arange(SEQ, dtype=jnp.int32)[None, :]).astype(jnp.int32)
    return {
        "x": nrm(ks[0], (BATCH, SEQ, D_MODEL), 1.0),
        "mem": nrm(ks[1], (BATCH, MEM_LEN, D_MODEL), 1.0),
        "positions": positions,
        "norm_mix": gain(ks[3], (DEPTH, D_MODEL)),
        "w_in": nrm(ks[4], (DEPTH, D_MODEL, N_IN), D_MODEL ** -0.5),
        "lambda_q1": nrm(ks[5], (DEPTH, DIFF_HEAD_DIM), 0.1),
        "lambda_k1": nrm(ks[6], (DEPTH, DIFF_HEAD_DIM), 0.1),
        "lambda_q2": nrm(ks[7], (DEPTH, DIFF_HEAD_DIM), 0.1),
        "lambda_k2": nrm(ks[8], (DEPTH, DIFF_HEAD_DIM), 0.1),
        "diff_subln": gain(ks[9], (DEPTH, DIFF_V_DIM)),
        "w_up_sb": nrm(ks[10], (DEPTH, SB_WIDTH, D_MODEL), SB_WIDTH ** -0.5),
        "w_up_diff": nrm(ks[11], (DEPTH, DIFF_WIDTH, D_MODEL), DIFF_WIDTH ** -0.5),
        "w_out": nrm(ks[12], (DEPTH, D_MODEL, D_MODEL), D_MODEL ** -0.5),
        "norm_mem_q": gain(ks[13], (DEPTH, D_MODEL)),
        "norm_mem_kv": gain(ks[14], (DEPTH, D_MODEL)),
        "w_mem_q": nrm(ks[15], (DEPTH, D_MODEL, D_MODEL), D_MODEL ** -0.5),
        "w_mem_kv": nrm(ks[16], (DEPTH, D_MODEL, 2 * D_MODEL), D_MODEL ** -0.5),
        "w_mem_o": nrm(ks[17], (DEPTH, D_MODEL, D_MODEL), D_MODEL ** -0.5),
        "norm_ffn": gain(ks[18], (DEPTH, D_MODEL)),
        "w_router": nrm(ks[19], (DEPTH, D_MODEL, N_EXPERTS), D_MODEL ** -0.5),
        "b_router": nrm(ks[20], (DEPTH, N_EXPERTS), 0.01),
        "w_gate_up": nrm(ks[21], (DEPTH, N_EXPERTS, D_MODEL, 2 * D_FF), D_MODEL ** -0.5),
        "b_gate_up": nrm(ks[22], (DEPTH, N_EXPERTS, 2 * D_FF), 0.02),
        "w_down": nrm(ks[23], (DEPTH, N_EXPERTS, D_FF, D_MODEL), D_FF ** -0.5),
        "b_down": nrm(ks[24], (DEPTH, N_EXPERTS, D_MODEL), 0.02),
        "norm_final": gain(ks[25], (D_MODEL,)),
    }


def reference(x, mem, positions, norm_mix, w_in, lambda_q1, lambda_k1, lambda_q2, lambda_k2,
              diff_subln, w_up_sb, w_up_diff, w_out, norm_mem_q, norm_mem_kv, w_mem_q,
              w_mem_kv, w_mem_o, norm_ffn, w_router, b_router, w_gate_up, b_gate_up,
              w_down, b_down, norm_final):
    splits = [SB_WIDTH, 2 * SB_WIDTH, 3 * SB_WIDTH,
              3 * SB_WIDTH + DIFF_QK_WIDTH,
              3 * SB_WIDTH + 2 * DIFF_QK_WIDTH,
              3 * SB_WIDTH + 2 * DIFF_QK_WIDTH + DIFF_WIDTH,
              3 * SB_WIDTH + 2 * DIFF_QK_WIDTH + DIFF_WIDTH + D_MODEL]
    b, s, _ = x.shape
    for l in range(DEPTH):
        h = rms_norm(x, norm_mix[l])
        proj = h @ w_in[l]
        sb_q, sb_k, sb_v, d_q, d_k, d_v, g_sb, g_diff = jnp.split(proj, splits, axis=-1)

        o_sb = stick_breaking_attention(split_heads(sb_q, SB_HEADS),
                                        split_heads(sb_k, SB_HEADS),
                                        split_heads(sb_v, SB_HEADS))

        dq = d_q.reshape(b, s, DIFF_HEADS, 2, DIFF_HEAD_DIM).transpose(0, 2, 3, 1, 4)
        dk = d_k.reshape(b, s, DIFF_HEADS, 2, DIFF_HEAD_DIM).transpose(0, 2, 3, 1, 4)
        q1, q2 = rope(dq[:, :, 0], positions), rope(dq[:, :, 1], positions)
        k1, k2 = rope(dk[:, :, 0], positions), rope(dk[:, :, 1], positions)
        lambda_init = 0.8 - 0.6 * math.exp(-0.3 * l)
        lam = (jnp.exp(jnp.sum(lambda_q1[l].astype(jnp.float32) * lambda_k1[l].astype(jnp.float32)))
               - jnp.exp(jnp.sum(lambda_q2[l].astype(jnp.float32) * lambda_k2[l].astype(jnp.float32)))
               + lambda_init)
        o_diff = differential_attention(q1, q2, k1, k2, split_heads(d_v, DIFF_HEADS), lam)
        o_diff = rms_norm(o_diff, diff_subln[l]) * (1.0 - lambda_init)

        y_sb = merge_heads(o_sb) @ w_up_sb[l]
        y_diff = merge_heads(o_diff) @ w_up_diff[l]
        mixed = jax.nn.sigmoid(g_sb) * y_sb + jax.nn.sigmoid(g_diff) * y_diff
        x = x + mixed @ w_out[l]

        x = x + memory_cross_attention(rms_norm(x, norm_mem_q[l]), rms_norm(mem, norm_mem_kv[l]),
                                       w_mem_q[l], w_mem_kv[l], w_mem_o[l])

        x = x + moe_ffn(rms_norm(x, norm_ffn[l]), w_router[l], b_router[l], w_gate_up[l],
                        b_gate_up[l], w_down[l], b_down[l])
    return rms_norm(x, norm_final)
```

```python
import functools
import math

import jax
import jax.numpy as jnp
from jax import lax
from jax.experimental import pallas as pl
from jax.experimental.pallas import tpu as pltpu

F32 = jnp.float32
BF16 = jnp.bfloat16
I32 = jnp.int32

LANES = 128
BF16_SUBLANES = 16
VMEM_LIMIT_BYTES = 56 * 1024 * 1024

NORM_EPS = 1e-6
ROPE_THETA = 10000.0
CHUNK = 64
SB_HEAD_DIM = 64
DIFF_HEAD_DIM = 64
DIFF_V_DIM = 128
MEM_HEADS = 4
N_EXPERTS = 32
TOP_K = 4
SWIGLU_LIMIT = 7.0
SWIGLU_ALPHA = 1.702

SB_DEAD_LOG = -105.0
NEG_BIG = -1e30


def _params(semantics):
    return pltpu.CompilerParams(dimension_semantics=semantics,
                                vmem_limit_bytes=VMEM_LIMIT_BYTES)


def _const_spec(shape):
    nd = len(shape)
    return pl.BlockSpec(shape, lambda *_: (0,) * nd)


def _rms(x, g):
    return x * lax.rsqrt(jnp.mean(x * x, axis=-1, keepdims=True) + NORM_EPS) * g


def _rope_table_kernel(pos_ref, inv_ref, cos_ref, sin_ref):
    ang = pos_ref[...] * inv_ref[...]
    cos_ref[...] = jnp.cos(ang)
    sin_ref[...] = jnp.sin(ang)


def _rope_tables(positions):
    n_tok = positions.size
    half = DIFF_HEAD_DIM // 2
    per_row = LANES // half
    rows = n_tok // per_row
    pos = jnp.repeat(positions.reshape(rows, per_row).astype(F32), half, axis=1)
    inv = ROPE_THETA ** (-jnp.arange(half, dtype=F32) / half)
    inv = jnp.tile(inv, per_row).reshape(1, LANES)
    tr = min(rows, 512)
    cos, sin = pl.pallas_call(
        _rope_table_kernel,
        grid=(rows // tr,),
        in_specs=[pl.BlockSpec((tr, LANES), lambda i: (i, 0)), _const_spec((1, LANES))],
        out_specs=[pl.BlockSpec((tr, LANES), lambda i: (i, 0))] * 2,
        out_shape=[jax.ShapeDtypeStruct((rows, LANES), F32)] * 2,
        compiler_params=_params(("arbitrary",)),
        name="rope_table",
    )(pos, inv)
    cos = cos.reshape(n_tok, half)
    sin = sin.reshape(n_tok, half)
    cos_t = jnp.tile(cos, (1, LANES // half))
    sin_t = jnp.tile(jnp.concatenate([-sin, sin], axis=1), (1, LANES // (2 * half)))
    return cos_t, sin_t


def _inproj_kernel(x_ref, g_ref, w_ref, cos_ref, sin_ref, o_ref, *, chunk, rope_chunks,
                   scale_chunks, scale):
    h = _rms(x_ref[...], g_ref[...]).astype(BF16)
    lane = lax.broadcasted_iota(I32, (1, chunk), 1)
    first_half = (lane % DIFF_HEAD_DIM) < (DIFF_HEAD_DIM // 2)
    for c in range(w_ref.shape[1] // chunk):
        cols = slice(c * chunk, (c + 1) * chunk)
        acc = jnp.dot(h, w_ref[:, cols], preferred_element_type=F32)
        if c in rope_chunks:
            cos = jnp.tile(cos_ref[...], (1, chunk // LANES))
            sin = jnp.tile(sin_ref[...], (1, chunk // LANES))
            partner = jnp.where(first_half,
                                pltpu.roll(acc, chunk - DIFF_HEAD_DIM // 2, 1),
                                pltpu.roll(acc, DIFF_HEAD_DIM // 2, 1))
            acc = acc * cos + partner * sin
        if c in scale_chunks:
            acc = acc * scale
        o_ref[:, cols] = acc.astype(BF16)


def _in_proj(x2d, g, w_bf16, cos_t, sin_t, *, tm, chunk, rope_chunks, scale_chunks, scale):
    n_tok, d = x2d.shape
    n_in = w_bf16.shape[1]
    kern = functools.partial(_inproj_kernel, chunk=chunk, rope_chunks=rope_chunks,
                             scale_chunks=scale_chunks, scale=scale)
    return pl.pallas_call(
        kern,
        grid=(n_tok // tm,),
        in_specs=[pl.BlockSpec((tm, d), lambda i: (i, 0)),
                  _const_spec((1, d)),
                  _const_spec((d, n_in)),
                  pl.BlockSpec((tm, LANES), lambda i: (i, 0)),
                  pl.BlockSpec((tm, LANES), lambda i: (i, 0))],
        out_specs=pl.BlockSpec((tm, n_in), lambda i: (i, 0)),
        out_shape=jax.ShapeDtypeStruct((n_tok, n_in), BF16),
        compiler_params=_params(("arbitrary",)),
        name="in_proj",
    )(x2d, g, w_bf16, cos_t, sin_t)


def _sb_kernel(q_ref, k_ref, v_ref, o_ref, *, tq):
    i = pl.program_id(2)
    q = q_ref[0]
    lane = lax.broadcasted_iota(I32, (1, LANES), 1)
    row = lax.broadcasted_iota(I32, (tq, tq), 0)
    col = lax.broadcasted_iota(I32, (tq, tq), 1)
    causal = col < row
    uj = lax.broadcasted_iota(I32, (2 * tq, tq), 0) % tq
    us = lax.broadcasted_iota(I32, (2 * tq, tq), 1)
    suffix = jnp.where(uj > us, 1.0, 0.0).astype(BF16)

    def block(qh, j, carry, acc, masked):
        start = pl.multiple_of(j * tq, tq)
        kj = k_ref[0, pl.ds(start, tq), :]
        vj = v_ref[0, pl.ds(start, tq), :]
        z = lax.dot_general(qh, kj, (((1,), (1,)), ((), ())), preferred_element_type=F32)
        sp = jnp.maximum(z, 0.0) + jnp.log1p(jnp.exp(-jnp.abs(z)))
        log_rem = -sp
        if masked:
            log_rem = jnp.where(causal, log_rem, 0.0)
        hi = log_rem.astype(BF16)
        lo = (log_rem - hi.astype(F32)).astype(BF16)
        after = jnp.dot(jnp.concatenate([hi, lo], axis=1), suffix,
                        preferred_element_type=F32)
        w = jnp.exp(z - sp + after + carry)
        if masked:
            w = jnp.where(causal, w, 0.0)
        acc = acc + jnp.dot(w.astype(BF16), vj, preferred_element_type=F32)
        carry = carry + after[:, 0:1] + log_rem[:, 0:1]
        return carry, acc

    outs = []
    for head in range(2):
        in_head = (lane < SB_HEAD_DIM) if head == 0 else (lane >= SB_HEAD_DIM)
        qh = jnp.where(in_head, q, jnp.zeros_like(q))
        carry, acc = block(qh, i, jnp.zeros((tq, 1), F32), jnp.zeros((tq, LANES), F32), True)

        def cond(state):
            j, _, _, alive = state
            return jnp.logical_and(j >= 0, alive > SB_DEAD_LOG)

        def body(state, qh=qh):
            j, carry, acc, _ = state
            carry, acc = block(qh, j, carry, acc, False)
            return j - 1, carry, acc, jnp.max(carry)

        _, _, acc, _ = lax.while_loop(cond, body, (i - 1, carry, acc, jnp.max(carry)))
        outs.append(acc)
    o_ref[0] = jnp.where(lane < SB_HEAD_DIM, outs[0], outs[1]).astype(BF16)


def _sb_attention(proj3, *, q_col, k_col, v_col, n_pairs, tq):
    b, s, _ = proj3.shape
    return pl.pallas_call(
        functools.partial(_sb_kernel, tq=tq),
        grid=(b, n_pairs, s // tq),
        in_specs=[pl.BlockSpec((1, tq, LANES), lambda bi, p, i: (bi, i, q_col + p)),
                  pl.BlockSpec((1, s, LANES), lambda bi, p, i: (bi, 0, k_col + p)),
                  pl.BlockSpec((1, s, LANES), lambda bi, p, i: (bi, 0, v_col + p))],
        out_specs=pl.BlockSpec((1, tq, LANES), lambda bi, p, i: (bi, i, p)),
        out_shape=jax.ShapeDtypeStruct((b, s, n_pairs * LANES), BF16),
        compiler_params=_params(("arbitrary", "arbitrary", "arbitrary")),
        name="sb_attention",
    )(proj3, proj3, proj3)


def _diff_kernel(lq1_ref, lk1_ref, lq2_ref, lk2_ref, q_ref, k_ref, v_ref, g_ref, o_ref, *,
                 tq, lambda_init):
    i = pl.program_id(2)
    q = q_ref[0]
    lane = lax.broadcasted_iota(I32, (1, LANES), 1)
    zero = jnp.zeros_like(q)
    qs = (jnp.where(lane < DIFF_HEAD_DIM, q, zero), jnp.where(lane >= DIFF_HEAD_DIM, q, zero))
    row_chunk = lax.broadcasted_iota(I32, (tq, tq), 0) // CHUNK
    col_chunk = lax.broadcasted_iota(I32, (tq, tq), 1) // CHUNK
    visible = col_chunk <= row_chunk

    def step(j, state, masked):
        start = pl.multiple_of(j * tq, tq)
        kj = k_ref[0, pl.ds(start, tq), :]
        vj = v_ref[0, pl.ds(start, tq), :]
        new = []
        for (m, l, acc), qm in zip(state, qs):
            z = lax.dot_general(qm, kj, (((1,), (1,)), ((), ())), preferred_element_type=F32)
            if masked:
                z = jnp.where(visible, z, -jnp.inf)
            m_new = jnp.maximum(m, jnp.max(z, axis=1, keepdims=True))
            alpha = jnp.exp(m - m_new)
            p = jnp.exp(z - m_new)
            l = alpha * l + jnp.sum(p, axis=1, keepdims=True)
            acc = alpha * acc + jnp.dot(p.astype(BF16), vj, preferred_element_type=F32)
            new.append((m_new, l, acc))
        return tuple(new)

    init = tuple((jnp.full((tq, 1), -jnp.inf, F32), jnp.zeros((tq, 1), F32),
                  jnp.zeros((tq, LANES), F32)) for _ in range(2))
    state = lax.fori_loop(0, i, lambda j, st: step(j, st, False), init)
    (_, l1, a1), (_, l2, a2) = step(i, state, True)
    lam = (jnp.exp(jnp.sum(lq1_ref[...] * lk1_ref[...], axis=1, keepdims=True))
           - jnp.exp(jnp.sum(lq2_ref[...] * lk2_ref[...], axis=1, keepdims=True))
           + lambda_init)
    o = a1 / l1 - lam * (a2 / l2)
    o_ref[0] = (_rms(o, g_ref[...]) * (1.0 - lambda_init)).astype(BF16)


def _diff_attention(proj3, lq1, lk1, lq2, lk2, subln, *, q_col, k_col, v_col, n_heads, tq,
                    lambda_init):
    b, s, _ = proj3.shape
    lam_spec = _const_spec((1, DIFF_HEAD_DIM))
    return pl.pallas_call(
        functools.partial(_diff_kernel, tq=tq, lambda_init=lambda_init),
        grid=(b, n_heads, s // tq),
        in_specs=[lam_spec, lam_spec, lam_spec, lam_spec,
                  pl.BlockSpec((1, tq, LANES), lambda bi, h, i: (bi, i, q_col + h)),
                  pl.BlockSpec((1, s, LANES), lambda bi, h, i: (bi, 0, k_col + h)),
                  pl.BlockSpec((1, s, LANES), lambda bi, h, i: (bi, 0, v_col + h)),
                  _const_spec((1, DIFF_V_DIM))],
        out_specs=pl.BlockSpec((1, tq, LANES), lambda bi, h, i: (bi, i, h)),
        out_shape=jax.ShapeDtypeStruct((b, s, n_heads * DIFF_V_DIM), BF16),
        compiler_params=_params(("arbitrary", "arbitrary", "arbitrary")),
        name="diff_attention",
    )(lq1, lk1, lq2, lk2, proj3, proj3, proj3, subln)


def _mem_kv_kernel(mem_ref, g_ref, w_ref, o_ref):
    h = _rms(mem_ref[0], g_ref[...]).astype(BF16)
    o_ref[0] = jnp.dot(h, w_ref[...], preferred_element_type=F32).astype(BF16)


def _mem_kv(mem, g, w_bf16):
    b, m, d = mem.shape
    n = w_bf16.shape[1]
    return pl.pallas_call(
        _mem_kv_kernel,
        grid=(b,),
        in_specs=[pl.BlockSpec((1, m, d), lambda i: (i, 0, 0)), _const_spec((1, d)),
                  _const_spec((d, n))],
        out_specs=pl.BlockSpec((1, m, n), lambda i: (i, 0, 0)),
        out_shape=jax.ShapeDtypeStruct((b, m, n), BF16),
        compiler_params=_params(("arbitrary",)),
        name="mem_kv",
    )(mem, g, w_bf16)


def _split_bf16(v):
    hi = v.astype(BF16)
    return hi, (v - hi.astype(F32)).astype(BF16)


def _postmix_kernel(x_ref, osb_ref, odf_ref, gsb_ref, gdf_ref, wus_ref, wud_ref, wout_ref,
                    gq_ref, wq_ref, kv_ref, wo_ref, gf_ref, wr_ref, br_ref,
                    x2_ref, hf_ref, route_ref, gate_ref, cum_ref, count_ref, *, tm, d_model):
    step = pl.program_id(0)

    @pl.when(step == 0)
    def _():
        count_ref[...] = jnp.zeros_like(count_ref)

    y_sb = jnp.dot(osb_ref[...], wus_ref[...], preferred_element_type=F32)
    y_df = jnp.dot(odf_ref[...], wud_ref[...], preferred_element_type=F32)
    mixed = (jax.nn.sigmoid(gsb_ref[...].astype(F32)) * y_sb
             + jax.nn.sigmoid(gdf_ref[...].astype(F32)) * y_df)
    x1 = x_ref[...] + jnp.dot(mixed.astype(BF16), wout_ref[...], preferred_element_type=F32)

    hq = _rms(x1, gq_ref[...]).astype(BF16)
    hd = d_model // MEM_HEADS
    q = jnp.dot(hq, wq_ref[...], preferred_element_type=F32) * (1.0 / math.sqrt(hd))
    q = q.astype(BF16)
    heads = []
    for h in range(MEM_HEADS):
        kh = kv_ref[0, :, h * hd:(h + 1) * hd]
        vh = kv_ref[0, :, d_model + h * hd:d_model + (h + 1) * hd]
        z = lax.dot_general(q[:, h * hd:(h + 1) * hd], kh, (((1,), (1,)), ((), ())),
                            preferred_element_type=F32)
        p = jnp.exp(z - jnp.max(z, axis=1, keepdims=True))
        l = jnp.sum(p, axis=1, keepdims=True)
        heads.append((jnp.dot(p.astype(BF16), vh, preferred_element_type=F32) / l).astype(BF16))
    x2 = x1 + jnp.dot(jnp.concatenate(heads, axis=1), wo_ref[...], preferred_element_type=F32)
    x2_ref[...] = x2

    hf = _rms(x2, gf_ref[...])
    hf_ref[...] = hf.astype(BF16)
    h_hi, h_lo = _split_bf16(hf)
    w_hi, w_lo = _split_bf16(wr_ref[...])
    logits = (jnp.dot(h_hi, w_hi, preferred_element_type=F32)
              + jnp.dot(h_hi, w_lo, preferred_element_type=F32)
              + jnp.dot(h_lo, w_hi, preferred_element_type=F32)) + br_ref[...]
    lane = lax.broadcasted_iota(I32, (tm, LANES), 1)
    work = logits
    vals, idxs, hots = [], [], []
    for _ in range(TOP_K):
        mx = jnp.max(work, axis=1, keepdims=True)
        idx = jnp.min(jnp.where(work == mx, lane, LANES), axis=1, keepdims=True)
        hot = lane == idx
        work = jnp.where(hot, NEG_BIG, work)
        vals.append(mx)
        idxs.append(idx)
        hots.append(hot)
    exps = [jnp.exp(v - vals[0]) for v in vals]
    denom = exps[0] + exps[1] + exps[2] + exps[3]

    onehot_sum = jnp.zeros((tm, LANES), F32)
    for hot in hots:
        onehot_sum = onehot_sum + jnp.where(hot, 1.0, 0.0)
    r = lax.broadcasted_iota(I32, (tm, tm), 0)
    c = lax.broadcasted_iota(I32, (tm, tm), 1)
    lower = jnp.where(c < r, 1.0, 0.0).astype(BF16)
    rank = jnp.dot(lower, onehot_sum.astype(BF16), preferred_element_type=F32) + count_ref[...]
    route = jnp.zeros((tm, LANES), I32)
    gates = jnp.zeros((tm, LANES), F32)
    for k in range(TOP_K):
        pos = jnp.sum(jnp.where(hots[k], rank, 0.0), axis=1, keepdims=True).astype(I32)
        route = jnp.where(lane == k, idxs[k], route)
        route = jnp.where(lane == TOP_K + k, pos, route)
        gates = jnp.where(lane == k, exps[k] / denom, gates)
    route_ref[...] = route
    gate_ref[...] = gates
    count_ref[...] = count_ref[...] + jnp.sum(onehot_sum, axis=0, keepdims=True)
    cum_ref[0] = jnp.broadcast_to(count_ref[...], (8, LANES))


def _post_mix(x2d, proj2, o_sb, o_diff, w_up_sb, w_up_diff, w_out, g_memq, w_memq, kv, w_memo,
              g_ffn, w_router_pad, b_router_pad, *, tm, seq, gate_col):
    n_tok, d = x2d.shape
    n_tiles = n_tok // tm
    tiles_per_batch = seq // tm
    gw = d // LANES
    row = lambda i: (i, 0)
    in_specs = [
        pl.BlockSpec((tm, d), row),
        pl.BlockSpec((tm, o_sb.shape[1]), row),
        pl.BlockSpec((tm, o_diff.shape[1]), row),
        pl.BlockSpec((tm, d), lambda i: (i, gate_col)),
        pl.BlockSpec((tm, d), lambda i: (i, gate_col + 1)),
        _const_spec(w_up_sb.shape), _const_spec(w_up_diff.shape), _const_spec(w_out.shape),
        _const_spec((1, d)), _const_spec(w_memq.shape),
        pl.BlockSpec((1,) + kv.shape[1:], lambda i: (i // tiles_per_batch, 0, 0)),
        _const_spec(w_memo.shape), _const_spec((1, d)),
        _const_spec(w_router_pad.shape), _const_spec((1, LANES)),
    ]
    del gw
    out_specs = [
        pl.BlockSpec((tm, d), row),
        pl.BlockSpec((tm, d), row),
        pl.BlockSpec((tm, LANES), row),
        pl.BlockSpec((tm, LANES), row),
        pl.BlockSpec((1, 8, LANES), lambda i: (i, 0, 0)),
    ]
    out_shape = [
        jax.ShapeDtypeStruct((n_tok, d), F32),
        jax.ShapeDtypeStruct((n_tok, d), BF16),
        jax.ShapeDtypeStruct((n_tok, LANES), I32),
        jax.ShapeDtypeStruct((n_tok, LANES), F32),
        jax.ShapeDtypeStruct((n_tiles, 8, LANES), F32),
    ]
    return pl.pallas_call(
        functools.partial(_postmix_kernel, tm=tm, d_model=d),
        grid=(n_tiles,),
        in_specs=in_specs,
        out_specs=out_specs,
        out_shape=out_shape,
        scratch_shapes=[pltpu.VMEM((1, LANES), F32)],
        compiler_params=_params(("arbitrary",)),
        name="post_mix",
    )(x2d, o_sb, o_diff, proj2, proj2, w_up_sb, w_up_diff, w_out, g_memq, w_memq, kv, w_memo,
      g_ffn, w_router_pad, b_router_pad)


def _expert_kernel(be_ref, lo_ref, hi_ref, valid_ref, hf_ref, dest_ref, gate_ref, wgu_ref,
                   bgu_ref, wd_ref, bd_ref, o_ref, xb_ref, gacc_ref, *, bm, sub, d_ff):
    r = pl.program_id(0)

    @pl.when(valid_ref[r] == 0)
    def _():
        o_ref[...] = jnp.zeros_like(o_ref)

    @pl.when(valid_ref[r] != 0)
    def _():
        xb_ref[...] = jnp.zeros_like(xb_ref)
        gacc_ref[...] = jnp.zeros_like(gacc_ref)
        row_id = r * bm + lax.broadcasted_iota(I32, (bm, sub), 0)

        def gather(s, _):
            t0 = pl.multiple_of(s * sub, sub)
            dest = dest_ref[:, pl.ds(t0, sub)]
            gate = gate_ref[:, pl.ds(t0, sub)]
            sel = jnp.zeros((bm, sub), F32)
            gsel = jnp.zeros((bm, sub), F32)
            for k in range(TOP_K):
                eq = dest[k:k + 1, :] == row_id
                sel = jnp.where(eq, 1.0, sel)
                gsel = jnp.where(eq, gate[k:k + 1, :], gsel)
            xb_ref[...] += jnp.dot(sel.astype(BF16), hf_ref[pl.ds(t0, sub), :],
                                   preferred_element_type=F32)
            folded = gsel[:, 0:LANES]
            for c in range(1, sub // LANES):
                folded = folded + gsel[:, c * LANES:(c + 1) * LANES]
            gacc_ref[...] += folded
            return 0

        lax.fori_loop(lo_ref[r], hi_ref[r], gather, 0)
        xb = xb_ref[...].astype(BF16)
        hgu = jnp.dot(xb, wgu_ref[0], preferred_element_type=F32) + bgu_ref[0]
        glu = jnp.minimum(hgu[:, :d_ff], SWIGLU_LIMIT)
        lin = jnp.clip(hgu[:, d_ff:], -SWIGLU_LIMIT, SWIGLU_LIMIT)
        act = glu * jax.nn.sigmoid(SWIGLU_ALPHA * glu) * (lin + 1.0)
        y = jnp.dot(act.astype(BF16), wd_ref[0], preferred_element_type=F32) + bd_ref[0]
        row_gate = jnp.sum(gacc_ref[...], axis=1, keepdims=True)
        o_ref[...] = (y * row_gate).astype(BF16)


def _expert_ffn(blk_expert, blk_lo, blk_hi, blk_valid, hf, dest_t, gate_t, w_gu, b_gu, w_down,
                b_down, *, bm, sub):
    n_tok, d = hf.shape
    n_grid = blk_expert.shape[0]
    d_ff = w_down.shape[1]
    grid_spec = pltpu.PrefetchScalarGridSpec(
        num_scalar_prefetch=4,
        grid=(n_grid,),
        in_specs=[
            pl.BlockSpec((n_tok, d), lambda r, *_: (0, 0)),
            pl.BlockSpec(dest_t.shape, lambda r, *_: (0, 0)),
            pl.BlockSpec(gate_t.shape, lambda r, *_: (0, 0)),
            pl.BlockSpec((1, d, 2 * d_ff), lambda r, be, *_: (be[r], 0, 0)),
            pl.BlockSpec((1, 1, 2 * d_ff), lambda r, be, *_: (be[r], 0, 0)),
            pl.BlockSpec((1, d_ff, d), lambda r, be, *_: (be[r], 0, 0)),
            pl.BlockSpec((1, 1, d), lambda r, be, *_: (be[r], 0, 0)),
        ],
        out_specs=pl.BlockSpec((bm, d), lambda r, *_: (r, 0)),
        scratch_shapes=[pltpu.VMEM((bm, d), F32), pltpu.VMEM((bm, LANES), F32)],
    )
    return pl.pallas_call(
        functools.partial(_expert_kernel, bm=bm, sub=sub, d_ff=d_ff),
        grid_spec=grid_spec,
        out_shape=jax.ShapeDtypeStruct((n_grid * bm, d), BF16),
        compiler_params=_params(("arbitrary",)),
        name="expert_ffn",
    )(blk_expert, blk_lo, blk_hi, blk_valid, hf, dest_t, gate_t, w_gu, b_gu, w_down, b_down)


def _combine_kernel(ws_ref, wlo_ref, whi_ref, nw_ref, x_ref, dest_ref, g_ref, ys_hbm, o_ref,
                    buf, sem, dbc, acc_ref, *, tm, win, max_win):
    s = pl.program_id(0)
    n = nw_ref[s]
    base = s * max_win

    def window_copy(j, slot):
        start = pl.multiple_of(ws_ref[base + j], BF16_SUBLANES)
        return pltpu.make_async_copy(ys_hbm.at[pl.ds(start, win), :], buf.at[slot], sem.at[slot])

    @pl.when(n > 0)
    def _():
        window_copy(0, 0).start()

    for k in range(TOP_K):
        dbc[k] = jnp.broadcast_to(dest_ref[:, k:k + 1], (tm, win))
    acc_ref[...] = jnp.zeros_like(acc_ref)
    lane = lax.broadcasted_iota(I32, (1, win), 1)

    def body(j, _):
        slot = j % 2
        window_copy(j, slot).wait()

        @pl.when(j + 1 < n)
        def _():
            window_copy(j + 1, 1 - slot).start()

        rows = ws_ref[base + j] + lane
        rows = jnp.where(jnp.logical_and(rows >= wlo_ref[base + j], rows < whi_ref[base + j]),
                         rows, -1)
        sel = jnp.zeros((tm, win), F32)
        for k in range(TOP_K):
            sel = jnp.where(dbc[k] == rows, 1.0, sel)
        acc_ref[...] += jnp.dot(sel.astype(BF16), buf[slot], preferred_element_type=F32)
        return 0

    lax.fori_loop(0, n, body, 0)
    o_ref[...] = _rms(x_ref[...] + acc_ref[...], g_ref[...])


def _combine(win_start, win_lo, win_hi, n_win, x2, dest, g_final, ys, *, tm, win, max_win):
    n_tok, d = x2.shape
    grid_spec = pltpu.PrefetchScalarGridSpec(
        num_scalar_prefetch=4,
        grid=(n_tok // tm,),
        in_specs=[
            pl.BlockSpec((tm, d), lambda i, *_: (i, 0)),
            pl.BlockSpec((tm, TOP_K), lambda i, *_: (i, 0)),
            pl.BlockSpec((1, d), lambda i, *_: (0, 0)),
            pl.BlockSpec(memory_space=pl.ANY),
        ],
        out_specs=pl.BlockSpec((tm, d), lambda i, *_: (i, 0)),
        scratch_shapes=[pltpu.VMEM((2, win, d), BF16), pltpu.SemaphoreType.DMA((2,)),
                        pltpu.VMEM((TOP_K, tm, win), I32), pltpu.VMEM((tm, d), F32)],
    )
    return pl.pallas_call(
        functools.partial(_combine_kernel, tm=tm, win=win, max_win=max_win),
        grid_spec=grid_spec,
        out_shape=jax.ShapeDtypeStruct((n_tok, d), F32),
        compiler_params=_params(("arbitrary",)),
        name="combine",
    )(win_start, win_lo, win_hi, n_win, x2, dest, g_final, ys)


def _routing_tables(route, cum, *, bm, win, max_win, n_grid):
    expert = route[:, 0:TOP_K]
    pos = route[:, TOP_K:2 * TOP_K]
    cum_incl = cum[:, 0, :N_EXPERTS].astype(I32)
    cum_excl = jnp.concatenate([jnp.zeros((1, N_EXPERTS), I32), cum_incl[:-1]], axis=0)
    counts = cum_incl[-1]
    padded = (counts + bm - 1) // bm * bm
    pend = jnp.cumsum(padded)
    pstart = pend - padded
    dest = pstart[expert] + pos

    blk_start = jnp.arange(n_grid, dtype=I32) * bm
    blk_expert = jnp.clip(jnp.searchsorted(pend, blk_start, side='right'), 0,
                          N_EXPERTS - 1).astype(I32)
    blk_valid = (blk_start < pend[-1]).astype(I32)
    r0 = blk_start - pstart[blk_expert]
    r1 = jnp.minimum(r0 + bm, counts[blk_expert])
    blk_lo = jnp.sum(cum_incl[:, blk_expert] <= r0[None, :], axis=0).astype(I32)
    blk_hi = jnp.sum(cum_excl[:, blk_expert] < r1[None, :], axis=0).astype(I32)

    a = pstart[None, :] + cum_excl
    cnt = cum_incl - cum_excl
    a0 = a // BF16_SUBLANES * BF16_SUBLANES
    nwin = jnp.where(cnt > 0, (a - a0 + cnt + win - 1) // win, 0)
    csum = jnp.cumsum(nwin, axis=1)
    n_win = csum[:, -1].astype(I32)
    slot = jnp.arange(max_win, dtype=I32)
    e_of = jnp.minimum(jnp.sum(csum[:, None, :] <= slot[None, :, None], axis=2), N_EXPERTS - 1)
    take = lambda t: jnp.take_along_axis(t, e_of, axis=1)
    w_of = slot[None, :] - (take(csum) - take(nwin))
    win_start = take(a0) + w_of * win
    win_lo = take(a)
    win_hi = take(a) + take(cnt)
    used = slot[None, :] < n_win[:, None]
    flat = lambda t: jnp.where(used, t, 0).astype(I32).reshape(-1)
    return (dest, blk_expert, blk_lo, blk_hi, blk_valid,
            flat(win_start), flat(win_lo), flat(win_hi), n_win)


def kernel(x, mem, positions, norm_mix, w_in, lambda_q1, lambda_k1, lambda_q2, lambda_k2,
           diff_subln, w_up_sb, w_up_diff, w_out, norm_mem_q, norm_mem_kv, w_mem_q, w_mem_kv,
           w_mem_o, norm_ffn, w_router, b_router, w_gate_up, b_gate_up, w_down, b_down,
           norm_final):
    b, s, d = x.shape
    n_tok = b * s
    depth = norm_mix.shape[0]
    sb_width = w_up_sb.shape[1]
    diff_width = w_up_diff.shape[1]
    n_in = w_in.shape[2]
    chunk = 512
    assert sb_width == chunk and diff_width == chunk and d == 2 * chunk
    sbq, sbk, sbv, dq, dk, dv = range(6)
    blocks_per_chunk = chunk // LANES
    scale = 1.0 / math.sqrt(SB_HEAD_DIM)
    assert SB_HEAD_DIM == DIFF_HEAD_DIM

    tm_in = min(512, n_tok)
    tq_sb = min(128, s)
    tq_diff = min(256, s)
    tm_post = min(256, s)
    bm = 256
    win = LANES
    n_grid = (n_tok * TOP_K) // bm + N_EXPERTS + 1
    max_win = (tm_post * TOP_K) // win + 2 * N_EXPERTS

    cos_t, sin_t = _rope_tables(positions)
    x2d = x.reshape(n_tok, d)
    for l in range(depth):
        lambda_init = 0.8 - 0.6 * math.exp(-0.3 * l)
        proj = _in_proj(x2d, norm_mix[l].reshape(1, d), w_in[l].astype(BF16), cos_t, sin_t,
                        tm=tm_in, chunk=chunk, rope_chunks=(dq, dk), scale_chunks=(sbq, dq),
                        scale=scale)
        proj3 = proj.reshape(b, s, n_in)
        o_sb = _sb_attention(proj3, q_col=sbq * blocks_per_chunk, k_col=sbk * blocks_per_chunk,
                             v_col=sbv * blocks_per_chunk, n_pairs=sb_width // LANES, tq=tq_sb)
        o_diff = _diff_attention(
            proj3, lambda_q1[l].reshape(1, -1), lambda_k1[l].reshape(1, -1),
            lambda_q2[l].reshape(1, -1), lambda_k2[l].reshape(1, -1),
            diff_subln[l].reshape(1, -1), q_col=dq * blocks_per_chunk,
            k_col=dk * blocks_per_chunk, v_col=dv * blocks_per_chunk,
            n_heads=diff_width // DIFF_V_DIM, tq=tq_diff, lambda_init=lambda_init)
        kv = _mem_kv(mem, norm_mem_kv[l].reshape(1, d), w_mem_kv[l].astype(BF16))
        w_router_pad = jnp.zeros((d, LANES), F32).at[:, :N_EXPERTS].set(w_router[l])
        b_router_pad = jnp.full((1, LANES), NEG_BIG, F32).at[0, :N_EXPERTS].set(b_router[l])
        x_res, hf, route, gates, cum = _post_mix(
            x2d, proj, o_sb.reshape(n_tok, sb_width), o_diff.reshape(n_tok, diff_width),
            w_up_sb[l].astype(BF16), w_up_diff[l].astype(BF16), w_out[l].astype(BF16),
            norm_mem_q[l].reshape(1, d), w_mem_q[l].astype(BF16), kv, w_mem_o[l].astype(BF16),
            norm_ffn[l].reshape(1, d), w_router_pad, b_router_pad,
            tm=tm_post, seq=s, gate_col=6 * chunk // d)
        (dest, blk_expert, blk_lo, blk_hi, blk_valid, win_start, win_lo, win_hi,
         n_win) = _routing_tables(route, cum, bm=bm, win=win, max_win=max_win, n_grid=n_grid)
        pad_rows = 8 - TOP_K
        dest_t = jnp.concatenate([dest.T, jnp.full((pad_rows, n_tok), -1, I32)], axis=0)
        gate_t = jnp.concatenate([gates[:, :TOP_K].T, jnp.zeros((pad_rows, n_tok), F32)], axis=0)
        ys = _expert_ffn(blk_expert, blk_lo, blk_hi, blk_valid, hf, dest_t, gate_t,
                         w_gate_up[l].astype(BF16), b_gate_up[l][:, None, :],
                         w_down[l].astype(BF16), b_down[l][:, None, :], bm=bm, sub=tm_post)
        g_next = norm_final if l == depth - 1 else None
        assert g_next is not None, "only depth 1 is wired: the final norm is fused into combine"
        x2d = _combine(win_start, win_lo, win_hi, n_win, x_res, dest, g_next.reshape(1, d), ys,
                       tm=tm_post, win=win, max_win=max_win)
    return x2d.reshape(b, s, d)
```

```python
import functools
import math

import jax
import jax.numpy as jnp
from jax import lax
from jax.experimental import pallas as pl
from jax.experimental.pallas import tpu as pltpu

F32 = jnp.float32
BF16 = jnp.bfloat16
I32 = jnp.int32

LANES = 128
BF16_SUBLANES = 16
VMEM_LIMIT_BYTES = 56 * 1024 * 1024

NORM_EPS = 1e-6
ROPE_THETA = 10000.0
CHUNK = 64
SB_HEAD_DIM = 64
DIFF_HEAD_DIM = 64
DIFF_V_DIM = 128
MEM_HEADS = 4
N_EXPERTS = 32
TOP_K = 4
SWIGLU_LIMIT = 7.0
SWIGLU_ALPHA = 1.702

SB_DEAD_LOG = -105.0
NEG_BIG = -1e30


def _params(semantics):
    return pltpu.CompilerParams(dimension_semantics=semantics,
                                vmem_limit_bytes=VMEM_LIMIT_BYTES)


def _const_spec(shape):
    nd = len(shape)
    return pl.BlockSpec(shape, lambda *_: (0,) * nd)


def _rms(x, g):
    return x * lax.rsqrt(jnp.mean(x * x, axis=-1, keepdims=True) + NORM_EPS) * g


def _rope_table_kernel(pos_ref, inv_ref, cos_ref, sin_ref):
    ang = pos_ref[...] * inv_ref[...]
    cos_ref[...] = jnp.cos(ang)
    sin_ref[...] = jnp.sin(ang)


def _rope_tables(positions):
    n_tok = positions.size
    half = DIFF_HEAD_DIM // 2
    per_row = LANES // half
    rows = n_tok // per_row
    pos = jnp.repeat(positions.reshape(rows, per_row).astype(F32), half, axis=1)
    inv = ROPE_THETA ** (-jnp.arange(half, dtype=F32) / half)
    inv = jnp.tile(inv, per_row).reshape(1, LANES)
    tr = min(rows, 512)
    cos, sin = pl.pallas_call(
        _rope_table_kernel,
        grid=(rows // tr,),
        in_specs=[pl.BlockSpec((tr, LANES), lambda i: (i, 0)), _const_spec((1, LANES))],
        out_specs=[pl.BlockSpec((tr, LANES), lambda i: (i, 0))] * 2,
        out_shape=[jax.ShapeDtypeStruct((rows, LANES), F32)] * 2,
        compiler_params=_params(("arbitrary",)),
        name="rope_table",
    )(pos, inv)
    cos = cos.reshape(n_tok, half)
    sin = sin.reshape(n_tok, half)
    cos_t = jnp.tile(cos, (1, LANES // half))
    sin_t = jnp.tile(jnp.concatenate([-sin, sin], axis=1), (1, LANES // (2 * half)))
    return cos_t, sin_t


def _inproj_kernel(x_ref, g_ref, w_ref, cos_ref, sin_ref, o_ref, *, chunk, rope_chunks,
                   scale_chunks, scale):
    h = _rms(x_ref[...], g_ref[...]).astype(BF16)
    lane = lax.broadcasted_iota(I32, (1, chunk), 1)
    first_half = (lane % DIFF_HEAD_DIM) < (DIFF_HEAD_DIM // 2)
    for c in range(w_ref.shape[1] // chunk):
        cols = slice(c * chunk, (c + 1) * chunk)
        acc = jnp.dot(h, w_ref[:, cols], preferred_element_type=F32)
        if c in rope_chunks:
            cos = jnp.tile(cos_ref[...], (1, chunk // LANES))
            sin = jnp.tile(sin_ref[...], (1, chunk // LANES))
            partner = jnp.where(first_half,
                                pltpu.roll(acc, chunk - DIFF_HEAD_DIM // 2, 1),
                                pltpu.roll(acc, DIFF_HEAD_DIM // 2, 1))
            acc = acc * cos + partner * sin
        if c in scale_chunks:
            acc = acc * scale
        o_ref[:, cols] = acc.astype(BF16)


def _in_proj(x2d, g, w_bf16, cos_t, sin_t, *, tm, chunk, rope_chunks, scale_chunks, scale):
    n_tok, d = x2d.shape
    n_in = w_bf16.shape[1]
    kern = functools.partial(_inproj_kernel, chunk=chunk, rope_chunks=rope_chunks,
                             scale_chunks=scale_chunks, scale=scale)
    return pl.pallas_call(
        kern,
        grid=(n_tok // tm,),
        in_specs=[pl.BlockSpec((tm, d), lambda i: (i, 0)),
                  _const_spec((1, d)),
                  _const_spec((d, n_in)),
                  pl.BlockSpec((tm, LANES), lambda i: (i, 0)),
                  pl.BlockSpec((tm, LANES), lambda i: (i, 0))],
        out_specs=pl.BlockSpec((tm, n_in), lambda i: (i, 0)),
        out_shape=jax.ShapeDtypeStruct((n_tok, n_in), BF16),
        compiler_params=_params(("arbitrary",)),
        name="in_proj",
    )(x2d, g, w_bf16, cos_t, sin_t)


def _sb_kernel(q_ref, k_ref, v_ref, o_ref, acc_ref, carry_ref, *, tq):
    i = pl.program_id(2)
    q = q_ref[0]
    lane = lax.broadcasted_iota(I32, (1, LANES), 1)
    zero = jnp.zeros_like(q)
    q2 = jnp.concatenate([jnp.where(lane < SB_HEAD_DIM, q, zero),
                          jnp.where(lane >= SB_HEAD_DIM, q, zero)], axis=0)
    uj = lax.broadcasted_iota(I32, (2 * tq, tq), 0) % tq
    us = lax.broadcasted_iota(I32, (2 * tq, tq), 1)
    suffix = jnp.where(uj > us, 1.0, 0.0).astype(BF16)

    def block(j, masked):
        start = pl.multiple_of(j * tq, tq)
        kj = k_ref[0, pl.ds(start, tq), :]
        vj = v_ref[0, pl.ds(start, tq), :]
        z = lax.dot_general(q2, kj, (((1,), (1,)), ((), ())), preferred_element_type=F32)
        sp = jnp.maximum(z, 0.0) + jnp.log(1.0 + jnp.exp(-jnp.abs(z)))
        log_rem = -sp
        if masked:
            causal = (lax.broadcasted_iota(I32, (2 * tq, tq), 1)
                      < lax.broadcasted_iota(I32, (2 * tq, tq), 0) % tq)
            log_rem = jnp.where(causal, log_rem, 0.0)
        hi = log_rem.astype(BF16)
        lo = (log_rem - hi.astype(F32)).astype(BF16)
        after = jnp.dot(jnp.concatenate([hi, lo], axis=1), suffix, preferred_element_type=F32)
        carry = carry_ref[...]
        w = jnp.exp(z - sp + after + carry)
        if masked:
            w = jnp.where(causal, w, 0.0)
        acc_ref[...] += jnp.dot(w.astype(BF16), vj, preferred_element_type=F32)
        carry = carry + after[:, 0:1] + log_rem[:, 0:1]
        carry_ref[...] = carry
        return jnp.max(carry)

    acc_ref[...] = jnp.zeros_like(acc_ref)
    carry_ref[...] = jnp.zeros_like(carry_ref)
    alive = block(i, True)

    def cond(state):
        j, alive = state
        return jnp.logical_and(j >= 0, alive > SB_DEAD_LOG)

    def body(state):
        j, _ = state
        return j - 1, block(j, False)

    lax.while_loop(cond, body, (i - 1, alive))
    o_ref[0] = jnp.where(lane < SB_HEAD_DIM, acc_ref[0:tq, :], acc_ref[tq:2 * tq, :]).astype(BF16)


def _sb_attention(proj3, *, q_col, k_col, v_col, n_pairs, tq):
    b, s, _ = proj3.shape
    return pl.pallas_call(
        functools.partial(_sb_kernel, tq=tq),
        grid=(b, n_pairs, s // tq),
        in_specs=[pl.BlockSpec((1, tq, LANES), lambda bi, p, i: (bi, i, q_col + p)),
                  pl.BlockSpec((1, s, LANES), lambda bi, p, i: (bi, 0, k_col + p)),
                  pl.BlockSpec((1, s, LANES), lambda bi, p, i: (bi, 0, v_col + p))],
        out_specs=pl.BlockSpec((1, tq, LANES), lambda bi, p, i: (bi, i, p)),
        out_shape=jax.ShapeDtypeStruct((b, s, n_pairs * LANES), BF16),
        scratch_shapes=[pltpu.VMEM((2 * tq, LANES), F32), pltpu.VMEM((2 * tq, 1), F32)],
        compiler_params=_params(("arbitrary", "arbitrary", "arbitrary")),
        name="sb_attention",
    )(proj3, proj3, proj3)


def _diff_kernel(lq1_ref, lk1_ref, lq2_ref, lk2_ref, q_ref, k_ref, v_ref, g_ref, o_ref, vt_ref,
                 acc_ref, *, tq, lambda_init):
    i = pl.program_id(2)
    s_len = v_ref.shape[1]

    @pl.when(i == 0)
    def _():
        for c in range(s_len // tq):
            cols = slice(c * tq, (c + 1) * tq)
            vt_ref[:, cols] = v_ref[0, cols, :].astype(F32).T.astype(BF16)

    q = q_ref[0]
    lane = lax.broadcasted_iota(I32, (1, LANES), 1)
    zero = jnp.zeros_like(q)
    q2 = jnp.concatenate([jnp.where(lane < DIFF_HEAD_DIM, q, zero),
                          jnp.where(lane >= DIFF_HEAD_DIM, q, zero)], axis=0)

    def step(j, stats, masked):
        m, l = stats
        start = pl.multiple_of(j * tq, tq)
        kj = k_ref[0, pl.ds(start, tq), :]
        vtj = vt_ref[:, pl.ds(start, tq)]
        zt = lax.dot_general(kj, q2, (((1,), (1,)), ((), ())), preferred_element_type=F32)
        if masked:
            visible = (lax.broadcasted_iota(I32, (tq, 2 * tq), 0) // CHUNK
                       <= (lax.broadcasted_iota(I32, (tq, 2 * tq), 1) % tq) // CHUNK)
            zt = jnp.where(visible, zt, -jnp.inf)
        m_new = jnp.maximum(m, jnp.max(zt, axis=0, keepdims=True))
        alpha = jnp.exp(m - m_new)
        pt = jnp.exp(zt - m_new)
        l = alpha * l + jnp.sum(pt, axis=0, keepdims=True)
        acc_ref[...] = alpha * acc_ref[...] + jnp.dot(vtj, pt.astype(BF16),
                                                      preferred_element_type=F32)
        return m_new, l

    acc_ref[...] = jnp.zeros_like(acc_ref)
    init = (jnp.full((1, 2 * tq), -jnp.inf, F32), jnp.zeros((1, 2 * tq), F32))
    stats = lax.fori_loop(0, i, lambda j, st: step(j, st, False), init)
    _, l = step(i, stats, True)
    lam = (jnp.exp(jnp.sum(lq1_ref[...] * lk1_ref[...], axis=1, keepdims=True))
           - jnp.exp(jnp.sum(lq2_ref[...] * lk2_ref[...], axis=1, keepdims=True))
           + lambda_init)
    ot = acc_ref[...] / l
    ot = ot[:, 0:tq] - lam * ot[:, tq:2 * tq]
    o_ref[0] = (_rms(ot.T, g_ref[...]) * (1.0 - lambda_init)).astype(BF16)


def _diff_attention(proj3, lq1, lk1, lq2, lk2, subln, *, q_col, k_col, v_col, n_heads, tq,
                    lambda_init):
    b, s, _ = proj3.shape
    lam_spec = _const_spec((1, DIFF_HEAD_DIM))
    return pl.pallas_call(
        functools.partial(_diff_kernel, tq=tq, lambda_init=lambda_init),
        grid=(b, n_heads, s // tq),
        in_specs=[lam_spec, lam_spec, lam_spec, lam_spec,
                  pl.BlockSpec((1, tq, LANES), lambda bi, h, i: (bi, i, q_col + h)),
                  pl.BlockSpec((1, s, LANES), lambda bi, h, i: (bi, 0, k_col + h)),
                  pl.BlockSpec((1, s, LANES), lambda bi, h, i: (bi, 0, v_col + h)),
                  _const_spec((1, DIFF_V_DIM))],
        out_specs=pl.BlockSpec((1, tq, LANES), lambda bi, h, i: (bi, i, h)),
        out_shape=jax.ShapeDtypeStruct((b, s, n_heads * DIFF_V_DIM), BF16),
        scratch_shapes=[pltpu.VMEM((DIFF_V_DIM, s), BF16), pltpu.VMEM((DIFF_V_DIM, 2 * tq), F32)],
        compiler_params=_params(("arbitrary", "arbitrary", "arbitrary")),
        name="diff_attention",
    )(lq1, lk1, lq2, lk2, proj3, proj3, proj3, subln)


def _mem_kv_kernel(mem_ref, g_ref, w_ref, o_ref):
    h = _rms(mem_ref[0], g_ref[...]).astype(BF16)
    o_ref[0] = jnp.dot(h, w_ref[...], preferred_element_type=F32).astype(BF16)


def _mem_kv(mem, g, w_bf16):
    b, m, d = mem.shape
    n = w_bf16.shape[1]
    return pl.pallas_call(
        _mem_kv_kernel,
        grid=(b,),
        in_specs=[pl.BlockSpec((1, m, d), lambda i: (i, 0, 0)), _const_spec((1, d)),
                  _const_spec((d, n))],
        out_specs=pl.BlockSpec((1, m, n), lambda i: (i, 0, 0)),
        out_shape=jax.ShapeDtypeStruct((b, m, n), BF16),
        compiler_params=_params(("arbitrary",)),
        name="mem_kv",
    )(mem, g, w_bf16)


def _split_bf16(v):
    hi = v.astype(BF16)
    return hi, (v - hi.astype(F32)).astype(BF16)


def _postmix_kernel(x_ref, osb_ref, odf_ref, gsb_ref, gdf_ref, wus_ref, wud_ref, wout_ref,
                    gq_ref, wq_ref, kv_ref, wo_ref, gf_ref, wr_ref, br_ref,
                    x2_ref, hf_ref, route_ref, gate_ref, cum_ref, count_ref, *, tm, d_model):
    step = pl.program_id(0)

    @pl.when(step == 0)
    def _():
        count_ref[...] = jnp.zeros_like(count_ref)

    y_sb = jnp.dot(osb_ref[...], wus_ref[...], preferred_element_type=F32)
    y_df = jnp.dot(odf_ref[...], wud_ref[...], preferred_element_type=F32)
    mixed = (jax.nn.sigmoid(gsb_ref[...].astype(F32)) * y_sb
             + jax.nn.sigmoid(gdf_ref[...].astype(F32)) * y_df)
    x1 = x_ref[...] + jnp.dot(mixed.astype(BF16), wout_ref[...], preferred_element_type=F32)

    hq = _rms(x1, gq_ref[...]).astype(BF16)
    hd = d_model // MEM_HEADS
    q = jnp.dot(hq, wq_ref[...], preferred_element_type=F32) * (1.0 / math.sqrt(hd))
    q = q.astype(BF16)
    heads = []
    for h in range(MEM_HEADS):
        kh = kv_ref[0, :, h * hd:(h + 1) * hd]
        vh = kv_ref[0, :, d_model + h * hd:d_model + (h + 1) * hd]
        z = lax.dot_general(q[:, h * hd:(h + 1) * hd], kh, (((1,), (1,)), ((), ())),
                            preferred_element_type=F32)
        p = jnp.exp(z - jnp.max(z, axis=1, keepdims=True))
        l = jnp.sum(p, axis=1, keepdims=True)
        heads.append((jnp.dot(p.astype(BF16), vh, preferred_element_type=F32) / l).astype(BF16))
    x2 = x1 + jnp.dot(jnp.concatenate(heads, axis=1), wo_ref[...], preferred_element_type=F32)
    x2_ref[...] = x2

    hf = _rms(x2, gf_ref[...])
    hf_ref[...] = hf.astype(BF16)
    h_hi, h_lo = _split_bf16(hf)
    w_hi, w_lo = _split_bf16(wr_ref[...])
    logits = (jnp.dot(h_hi, w_hi, preferred_element_type=F32)
              + jnp.dot(h_hi, w_lo, preferred_element_type=F32)
              + jnp.dot(h_lo, w_hi, preferred_element_type=F32)) + br_ref[...]
    lane = lax.broadcasted_iota(I32, (tm, LANES), 1)
    work = logits
    vals, idxs, hots = [], [], []
    for _ in range(TOP_K):
        mx = jnp.max(work, axis=1, keepdims=True)
        idx = jnp.min(jnp.where(work == mx, lane, LANES), axis=1, keepdims=True)
        hot = lane == idx
        work = jnp.where(hot, NEG_BIG, work)
        vals.append(mx)
        idxs.append(idx)
        hots.append(hot)
    exps = [jnp.exp(v - vals[0]) for v in vals]
    denom = exps[0] + exps[1] + exps[2] + exps[3]

    onehot_sum = jnp.zeros((tm, LANES), F32)
    for hot in hots:
        onehot_sum = onehot_sum + jnp.where(hot, 1.0, 0.0)
    r = lax.broadcasted_iota(I32, (tm, tm), 0)
    c = lax.broadcasted_iota(I32, (tm, tm), 1)
    lower = jnp.where(c < r, 1.0, 0.0).astype(BF16)
    rank = jnp.dot(lower, onehot_sum.astype(BF16), preferred_element_type=F32) + count_ref[...]
    route = jnp.zeros((tm, LANES), I32)
    gates = jnp.zeros((tm, LANES), F32)
    for k in range(TOP_K):
        pos = jnp.sum(jnp.where(hots[k], rank, 0.0), axis=1, keepdims=True).astype(I32)
        route = jnp.where(lane == k, idxs[k], route)
        route = jnp.where(lane == TOP_K + k, pos, route)
        gates = jnp.where(lane == k, exps[k] / denom, gates)
    route_ref[...] = route
    gate_ref[...] = gates
    count_ref[...] = count_ref[...] + jnp.sum(onehot_sum, axis=0, keepdims=True)
    cum_ref[0] = jnp.broadcast_to(count_ref[...], (8, LANES))


def _post_mix(x2d, proj2, o_sb, o_diff, w_up_sb, w_up_diff, w_out, g_memq, w_memq, kv, w_memo,
              g_ffn, w_router_pad, b_router_pad, *, tm, seq, gate_col):
    n_tok, d = x2d.shape
    n_tiles = n_tok // tm
    tiles_per_batch = seq // tm
    gw = d // LANES
    row = lambda i: (i, 0)
    in_specs = [
        pl.BlockSpec((tm, d), row),
        pl.BlockSpec((tm, o_sb.shape[1]), row),
        pl.BlockSpec((tm, o_diff.shape[1]), row),
        pl.BlockSpec((tm, d), lambda i: (i, gate_col)),
        pl.BlockSpec((tm, d), lambda i: (i, gate_col + 1)),
        _const_spec(w_up_sb.shape), _const_spec(w_up_diff.shape), _const_spec(w_out.shape),
        _const_spec((1, d)), _const_spec(w_memq.shape),
        pl.BlockSpec((1,) + kv.shape[1:], lambda i: (i // tiles_per_batch, 0, 0)),
        _const_spec(w_memo.shape), _const_spec((1, d)),
        _const_spec(w_router_pad.shape), _const_spec((1, LANES)),
    ]
    del gw
    out_specs = [
        pl.BlockSpec((tm, d), row),
        pl.BlockSpec((tm, d), row),
        pl.BlockSpec((tm, LANES), row),
        pl.BlockSpec((tm, LANES), row),
        pl.BlockSpec((1, 8, LANES), lambda i: (i, 0, 0)),
    ]
    out_shape = [
        jax.ShapeDtypeStruct((n_tok, d), F32),
        jax.ShapeDtypeStruct((n_tok, d), BF16),
        jax.ShapeDtypeStruct((n_tok, LANES), I32),
        jax.ShapeDtypeStruct((n_tok, LANES), F32),
        jax.ShapeDtypeStruct((n_tiles, 8, LANES), F32),
    ]
    return pl.pallas_call(
        functools.partial(_postmix_kernel, tm=tm, d_model=d),
        grid=(n_tiles,),
        in_specs=in_specs,
        out_specs=out_specs,
        out_shape=out_shape,
        scratch_shapes=[pltpu.VMEM((1, LANES), F32)],
        compiler_params=_params(("arbitrary",)),
        name="post_mix",
    )(x2d, o_sb, o_diff, proj2, proj2, w_up_sb, w_up_diff, w_out, g_memq, w_memq, kv, w_memo,
      g_ffn, w_router_pad, b_router_pad)


def _expert_kernel(be_ref, r0_ref, lo_ref, hi_ref, valid_ref, hf_ref, expert_ref, pos_ref,
                   gate_ref, wgu_ref, bgu_ref, wd_ref, bd_ref, o_ref, xb_ref, gacc_ref, *,
                   bm, sub, d_ff):
    r = pl.program_id(0)

    @pl.when(valid_ref[r] == 0)
    def _():
        o_ref[...] = jnp.zeros_like(o_ref)

    @pl.when(valid_ref[r] != 0)
    def _():
        xb_ref[...] = jnp.zeros_like(xb_ref)
        gacc_ref[...] = jnp.zeros_like(gacc_ref)
        row = lax.broadcasted_iota(I32, (bm, sub), 0)
        expert = be_ref[r]
        rank0 = r0_ref[r]

        def gather(s, _):
            t0 = pl.multiple_of(s * sub, sub)
            match = expert_ref[:, pl.ds(t0, sub)] == expert
            rel = jnp.max(jnp.where(match, pos_ref[:, pl.ds(t0, sub)] - rank0, -1), axis=0,
                          keepdims=True)
            gate = jnp.sum(jnp.where(match, gate_ref[:, pl.ds(t0, sub)], 0.0), axis=0,
                           keepdims=True)
            eq = rel == row
            sel = jnp.where(eq, 1.0, 0.0)
            gsel = jnp.where(eq, gate, 0.0)
            xb_ref[...] += jnp.dot(sel.astype(BF16), hf_ref[pl.ds(t0, sub), :],
                                   preferred_element_type=F32)
            folded = gsel[:, 0:LANES]
            for c in range(1, sub // LANES):
                folded = folded + gsel[:, c * LANES:(c + 1) * LANES]
            gacc_ref[...] += folded
            return 0

        lax.fori_loop(lo_ref[r], hi_ref[r], gather, 0)
        xb = xb_ref[...].astype(BF16)
        hgu = jnp.dot(xb, wgu_ref[0], preferred_element_type=F32) + bgu_ref[0]
        glu = jnp.minimum(hgu[:, :d_ff], SWIGLU_LIMIT)
        lin = jnp.clip(hgu[:, d_ff:], -SWIGLU_LIMIT, SWIGLU_LIMIT)
        act = glu * jax.nn.sigmoid(SWIGLU_ALPHA * glu) * (lin + 1.0)
        y = jnp.dot(act.astype(BF16), wd_ref[0], preferred_element_type=F32) + bd_ref[0]
        row_gate = jnp.sum(gacc_ref[...], axis=1, keepdims=True)
        o_ref[...] = (y * row_gate).astype(BF16)


def _expert_ffn(blk_expert, blk_r0, blk_lo, blk_hi, blk_valid, hf, expert_t, pos_t, gate_t, w_gu,
                b_gu, w_down, b_down, *, bm, sub):
    n_tok, d = hf.shape
    n_grid = blk_expert.shape[0]
    d_ff = w_down.shape[1]
    grid_spec = pltpu.PrefetchScalarGridSpec(
        num_scalar_prefetch=5,
        grid=(n_grid,),
        in_specs=[
            pl.BlockSpec((n_tok, d), lambda r, *_: (0, 0)),
            pl.BlockSpec(expert_t.shape, lambda r, *_: (0, 0)),
            pl.BlockSpec(pos_t.shape, lambda r, *_: (0, 0)),
            pl.BlockSpec(gate_t.shape, lambda r, *_: (0, 0)),
            pl.BlockSpec((1, d, 2 * d_ff), lambda r, be, *_: (be[r], 0, 0)),
            pl.BlockSpec((1, 1, 2 * d_ff), lambda r, be, *_: (be[r], 0, 0)),
            pl.BlockSpec((1, d_ff, d), lambda r, be, *_: (be[r], 0, 0)),
            pl.BlockSpec((1, 1, d), lambda r, be, *_: (be[r], 0, 0)),
        ],
        out_specs=pl.BlockSpec((bm, d), lambda r, *_: (r, 0)),
        scratch_shapes=[pltpu.VMEM((bm, d), F32), pltpu.VMEM((bm, LANES), F32)],
    )
    return pl.pallas_call(
        functools.partial(_expert_kernel, bm=bm, sub=sub, d_ff=d_ff),
        grid_spec=grid_spec,
        out_shape=jax.ShapeDtypeStruct((n_grid * bm, d), BF16),
        compiler_params=_params(("arbitrary",)),
        name="expert_ffn",
    )(blk_expert, blk_r0, blk_lo, blk_hi, blk_valid, hf, expert_t, pos_t, gate_t, w_gu, b_gu,
      w_down, b_down)


def _combine_kernel(ws_ref, wlo_ref, whi_ref, nw_ref, x_ref, dest_ref, g_ref, ys_hbm, o_ref,
                    buf, sem, dbc, acc_ref, *, tm, win, max_win, group):
    s = pl.program_id(0)
    n = nw_ref[s]
    base = s * max_win
    n_groups = (n + group - 1) // group

    @pl.when(s == 0)
    def _():
        buf[...] = jnp.zeros_like(buf)

    def window_copy(j, slot, g):
        start = pl.multiple_of(ws_ref[base + j], BF16_SUBLANES)
        return pltpu.make_async_copy(ys_hbm.at[pl.ds(start, win), :],
                                     buf.at[slot, pl.ds(g * win, win), :], sem.at[slot])

    def for_group(jg, slot, action):
        for g in range(group):
            j = jg * group + g

            @pl.when(j < n)
            def _(j=j, g=g):
                action(window_copy(j, slot, g))

    @pl.when(n > 0)
    def _():
        for_group(0, 0, lambda c: c.start())

    for k in range(TOP_K):
        dbc[k] = jnp.broadcast_to(dest_ref[:, k:k + 1], (tm, win))
    acc_ref[...] = jnp.zeros_like(acc_ref)
    lane = lax.broadcasted_iota(I32, (1, win), 1)

    def body(jg, _):
        slot = jg % 2
        for_group(jg, slot, lambda c: c.wait())

        @pl.when(jg + 1 < n_groups)
        def _():
            for_group(jg + 1, 1 - slot, lambda c: c.start())

        segs = []
        for g in range(group):
            j = base + jg * group + g
            rows = ws_ref[j] + lane
            rows = jnp.where(jnp.logical_and(rows >= wlo_ref[j], rows < whi_ref[j]), rows, -1)
            seg = jnp.zeros((tm, win), F32)
            for k in range(TOP_K):
                seg = jnp.where(dbc[k] == rows, 1.0, seg)
            segs.append(seg.astype(BF16))
        acc_ref[...] += jnp.dot(jnp.concatenate(segs, axis=1), buf[slot],
                                preferred_element_type=F32)
        return 0

    lax.fori_loop(0, n_groups, body, 0)
    o_ref[...] = _rms(x_ref[...] + acc_ref[...], g_ref[...])


def _combine(win_start, win_lo, win_hi, n_win, x2, dest, g_final, ys, *, tm, win, max_win,
             group):
    n_tok, d = x2.shape
    grid_spec = pltpu.PrefetchScalarGridSpec(
        num_scalar_prefetch=4,
        grid=(n_tok // tm,),
        in_specs=[
            pl.BlockSpec((tm, d), lambda i, *_: (i, 0)),
            pl.BlockSpec((tm, TOP_K), lambda i, *_: (i, 0)),
            pl.BlockSpec((1, d), lambda i, *_: (0, 0)),
            pl.BlockSpec(memory_space=pl.ANY),
        ],
        out_specs=pl.BlockSpec((tm, d), lambda i, *_: (i, 0)),
        scratch_shapes=[pltpu.VMEM((2, group * win, d), BF16), pltpu.SemaphoreType.DMA((2,)),
                        pltpu.VMEM((TOP_K, tm, win), I32), pltpu.VMEM((tm, d), F32)],
    )
    return pl.pallas_call(
        functools.partial(_combine_kernel, tm=tm, win=win, max_win=max_win, group=group),
        grid_spec=grid_spec,
        out_shape=jax.ShapeDtypeStruct((n_tok, d), F32),
        compiler_params=_params(("arbitrary",)),
        name="combine",
    )(win_start, win_lo, win_hi, n_win, x2, dest, g_final, ys)


def _routing_tables(route, cum, *, tm, bm, sub, win, max_win, n_grid):
    expert = route[:, 0:TOP_K]
    pos = route[:, TOP_K:2 * TOP_K]
    experts = jnp.arange(N_EXPERTS, dtype=I32)
    cum_incl = cum[:, 0, :N_EXPERTS].astype(I32)
    cum_excl = jnp.concatenate([jnp.zeros((1, N_EXPERTS), I32), cum_incl[:-1]], axis=0)
    counts = cum_incl[-1]
    padded = (counts + bm - 1) // bm * bm
    pend = jnp.cumsum(padded)
    pstart = pend - padded
    dest = pos + jnp.sum(jnp.where(expert[:, :, None] == experts, pstart, 0), axis=2)

    blk_start = jnp.arange(n_grid, dtype=I32) * bm
    blk_expert = jnp.minimum(jnp.sum(pend[None, :] <= blk_start[:, None], axis=1),
                             N_EXPERTS - 1).astype(I32)
    blk_valid = (blk_start < pend[-1]).astype(I32)
    owner = blk_expert[:, None] == experts[None, :]
    of_block = lambda v: jnp.sum(jnp.where(owner, v[None, :], 0), axis=1)
    blk_r0 = blk_start - of_block(pstart)
    r1 = jnp.minimum(blk_r0 + bm, of_block(counts))
    per_group = sub // tm
    grp_incl = cum_incl[per_group - 1::per_group]
    grp_excl = jnp.concatenate([jnp.zeros((1, N_EXPERTS), I32), grp_incl[:-1]], axis=0)
    at_owner = lambda t: jnp.sum(jnp.where(owner[None], t[:, None, :], 0), axis=2)
    blk_lo = jnp.sum(at_owner(grp_incl) <= blk_r0[None, :], axis=0).astype(I32)
    blk_hi = jnp.sum(at_owner(grp_excl) < r1[None, :], axis=0).astype(I32)

    a = pstart[None, :] + cum_excl
    cnt = cum_incl - cum_excl
    a0 = a // BF16_SUBLANES * BF16_SUBLANES
    nwin = jnp.where(cnt > 0, (a - a0 + cnt + win - 1) // win, 0)
    csum = jnp.cumsum(nwin, axis=1)
    n_win = csum[:, -1].astype(I32)
    slot = jnp.arange(max_win, dtype=I32)
    e_of = jnp.minimum(jnp.sum(csum[:, None, :] <= slot[None, :, None], axis=2), N_EXPERTS - 1)
    pick = e_of[:, :, None] == experts
    take = lambda t: jnp.sum(jnp.where(pick, t[:, None, :], 0), axis=2)
    w_of = slot[None, :] - (take(csum) - take(nwin))
    win_start = take(a0) + w_of * win
    win_lo = take(a)
    win_hi = take(a) + take(cnt)
    used = slot[None, :] < n_win[:, None]
    flat = lambda t: jnp.where(used, t, 0).astype(I32).reshape(-1)
    return (dest, blk_expert, blk_r0.astype(I32), blk_lo, blk_hi, blk_valid,
            flat(win_start), flat(win_lo), flat(win_hi), n_win)


def kernel(x, mem, positions, norm_mix, w_in, lambda_q1, lambda_k1, lambda_q2, lambda_k2,
           diff_subln, w_up_sb, w_up_diff, w_out, norm_mem_q, norm_mem_kv, w_mem_q, w_mem_kv,
           w_mem_o, norm_ffn, w_router, b_router, w_gate_up, b_gate_up, w_down, b_down,
           norm_final):
    b, s, d = x.shape
    n_tok = b * s
    depth = norm_mix.shape[0]
    sb_width = w_up_sb.shape[1]
    diff_width = w_up_diff.shape[1]
    n_in = w_in.shape[2]
    chunk = 512
    assert sb_width == chunk and diff_width == chunk and d == 2 * chunk
    sbq, sbk, sbv, dq, dk, dv = range(6)
    blocks_per_chunk = chunk // LANES
    scale = 1.0 / math.sqrt(SB_HEAD_DIM)
    assert SB_HEAD_DIM == DIFF_HEAD_DIM

    tm_in = min(512, n_tok)
    tq_sb = min(256, s)
    tq_diff = min(512, s)
    tm_post = min(256, s)
    bm = 256
    sub_tok = min(512, n_tok)
    win = LANES
    win_group = 8
    n_grid = (n_tok * TOP_K) // bm + N_EXPERTS + 1
    max_win = (tm_post * TOP_K) // win + 2 * N_EXPERTS
    assert max_win % win_group == 0 and sub_tok % tm_post == 0 and n_tok % sub_tok == 0

    cos_t, sin_t = _rope_tables(positions)
    x2d = x.reshape(n_tok, d)
    for l in range(depth):
        lambda_init = 0.8 - 0.6 * math.exp(-0.3 * l)
        proj = _in_proj(x2d, norm_mix[l].reshape(1, d), w_in[l].astype(BF16), cos_t, sin_t,
                        tm=tm_in, chunk=chunk, rope_chunks=(dq, dk), scale_chunks=(sbq, dq),
                        scale=scale)
        proj3 = proj.reshape(b, s, n_in)
        o_sb = _sb_attention(proj3, q_col=sbq * blocks_per_chunk, k_col=sbk * blocks_per_chunk,
                             v_col=sbv * blocks_per_chunk, n_pairs=sb_width // LANES, tq=tq_sb)
        o_diff = _diff_attention(
            proj3, lambda_q1[l].reshape(1, -1), lambda_k1[l].reshape(1, -1),
            lambda_q2[l].reshape(1, -1), lambda_k2[l].reshape(1, -1),
            diff_subln[l].reshape(1, -1), q_col=dq * blocks_per_chunk,
            k_col=dk * blocks_per_chunk, v_col=dv * blocks_per_chunk,
            n_heads=diff_width // DIFF_V_DIM, tq=tq_diff, lambda_init=lambda_init)
        kv = _mem_kv(mem, norm_mem_kv[l].reshape(1, d), w_mem_kv[l].astype(BF16))
        w_router_pad = jnp.zeros((d, LANES), F32).at[:, :N_EXPERTS].set(w_router[l])
        b_router_pad = jnp.full((1, LANES), NEG_BIG, F32).at[0, :N_EXPERTS].set(b_router[l])
        x_res, hf, route, gates, cum = _post_mix(
            x2d, proj, o_sb.reshape(n_tok, sb_width), o_diff.reshape(n_tok, diff_width),
            w_up_sb[l].astype(BF16), w_up_diff[l].astype(BF16), w_out[l].astype(BF16),
            norm_mem_q[l].reshape(1, d), w_mem_q[l].astype(BF16), kv, w_mem_o[l].astype(BF16),
            norm_ffn[l].reshape(1, d), w_router_pad, b_router_pad,
            tm=tm_post, seq=s, gate_col=6 * chunk // d)
        (dest, blk_expert, blk_r0, blk_lo, blk_hi, blk_valid, win_start, win_lo, win_hi,
         n_win) = _routing_tables(route, cum, tm=tm_post, bm=bm, sub=sub_tok, win=win,
                                  max_win=max_win, n_grid=n_grid)
        lanes_first = lambda t, fill: jnp.concatenate(
            [t.T, jnp.full((8 - TOP_K, n_tok), fill, t.dtype)], axis=0)
        ys = _expert_ffn(blk_expert, blk_r0, blk_lo, blk_hi, blk_valid, hf,
                         lanes_first(route[:, :TOP_K], -1),
                         lanes_first(route[:, TOP_K:2 * TOP_K], 0),
                         lanes_first(gates[:, :TOP_K], 0.0),
                         w_gate_up[l].astype(BF16), b_gate_up[l][:, None, :],
                         w_down[l].astype(BF16), b_down[l][:, None, :], bm=bm, sub=sub_tok)
        g_next = norm_final if l == depth - 1 else None
        assert g_next is not None, "only depth 1 is wired: the final norm is fused into combine"
        x2d = _combine(win_start, win_lo, win_hi, n_win, x_res, dest, g_next.reshape(1, d), ys,
                       tm=tm_post, win=win, max_win=max_win, group=win_group)
    return x2d.reshape(b, s, d)
```

```python
import functools
import math

import jax
import jax.numpy as jnp
from jax import lax
from jax.experimental import pallas as pl
from jax.experimental.pallas import tpu as pltpu

F32 = jnp.float32
BF16 = jnp.bfloat16
I32 = jnp.int32
U32 = jnp.uint32

LANES = 128
VMEM_LIMIT_BYTES = 56 * 1024 * 1024

NORM_EPS = 1e-6
ROPE_THETA = 10000.0
CHUNK = 64
SB_HEAD_DIM = 64
DIFF_HEAD_DIM = 64
DIFF_V_DIM = 128
MEM_HEADS = 4
N_EXPERTS = 32
TOP_K = 4
SWIGLU_LIMIT = 7.0
SWIGLU_ALPHA = 1.702

SB_DEAD_LOG = -105.0
NEG_BIG = -1e30


def _params(semantics):
    return pltpu.CompilerParams(dimension_semantics=semantics,
                                vmem_limit_bytes=VMEM_LIMIT_BYTES)


def _const_spec(shape):
    nd = len(shape)
    return pl.BlockSpec(shape, lambda *_: (0,) * nd)


def _rms(x, g):
    return x * lax.rsqrt(jnp.mean(x * x, axis=-1, keepdims=True) + NORM_EPS) * g


def _rope_table_kernel(pos_ref, inv_ref, cos_ref, sin_ref):
    ang = pos_ref[...] * inv_ref[...]
    cos_ref[...] = jnp.cos(ang)
    sin_ref[...] = jnp.sin(ang)


def _rope_tables(positions):
    n_tok = positions.size
    half = DIFF_HEAD_DIM // 2
    per_row = LANES // half
    rows = n_tok // per_row
    pos = jnp.repeat(positions.reshape(rows, per_row).astype(F32), half, axis=1)
    inv = ROPE_THETA ** (-jnp.arange(half, dtype=F32) / half)
    inv = jnp.tile(inv, per_row).reshape(1, LANES)
    tr = min(rows, 512)
    cos, sin = pl.pallas_call(
        _rope_table_kernel,
        grid=(rows // tr,),
        in_specs=[pl.BlockSpec((tr, LANES), lambda i: (i, 0)), _const_spec((1, LANES))],
        out_specs=[pl.BlockSpec((tr, LANES), lambda i: (i, 0))] * 2,
        out_shape=[jax.ShapeDtypeStruct((rows, LANES), F32)] * 2,
        compiler_params=_params(("arbitrary",)),
        name="rope_table",
    )(pos, inv)
    cos = cos.reshape(n_tok, half)
    sin = sin.reshape(n_tok, half)
    cos_t = jnp.tile(cos, (1, LANES // half))
    sin_t = jnp.tile(jnp.concatenate([-sin, sin], axis=1), (1, LANES // (2 * half)))
    return cos_t, sin_t


def _inproj_kernel(x_ref, g_ref, w_ref, cos_ref, sin_ref, o_ref, *, chunk, rope_chunks,
                   scale_chunks, scale):
    h = _rms(x_ref[...], g_ref[...]).astype(BF16)
    lane = lax.broadcasted_iota(I32, (1, chunk), 1)
    first_half = (lane % DIFF_HEAD_DIM) < (DIFF_HEAD_DIM // 2)
    for c in range(w_ref.shape[1] // chunk):
        cols = slice(c * chunk, (c + 1) * chunk)
        acc = jnp.dot(h, w_ref[:, cols], preferred_element_type=F32)
        if c in rope_chunks:
            cos = jnp.tile(cos_ref[...], (1, chunk // LANES))
            sin = jnp.tile(sin_ref[...], (1, chunk // LANES))
            partner = jnp.where(first_half,
                                pltpu.roll(acc, chunk - DIFF_HEAD_DIM // 2, 1),
                                pltpu.roll(acc, DIFF_HEAD_DIM // 2, 1))
            acc = acc * cos + partner * sin
        if c in scale_chunks:
            acc = acc * scale
        o_ref[:, cols] = acc.astype(BF16)


def _in_proj(x2d, g, w_bf16, cos_t, sin_t, *, tm, chunk, rope_chunks, scale_chunks, scale):
    n_tok, d = x2d.shape
    n_in = w_bf16.shape[1]
    kern = functools.partial(_inproj_kernel, chunk=chunk, rope_chunks=rope_chunks,
                             scale_chunks=scale_chunks, scale=scale)
    return pl.pallas_call(
        kern,
        grid=(n_tok // tm,),
        in_specs=[pl.BlockSpec((tm, d), lambda i: (i, 0)),
                  _const_spec((1, d)),
                  _const_spec((d, n_in)),
                  pl.BlockSpec((tm, LANES), lambda i: (i, 0)),
                  pl.BlockSpec((tm, LANES), lambda i: (i, 0))],
        out_specs=pl.BlockSpec((tm, n_in), lambda i: (i, 0)),
        out_shape=jax.ShapeDtypeStruct((n_tok, n_in), BF16),
        compiler_params=_params(("arbitrary",)),
        name="in_proj",
    )(x2d, g, w_bf16, cos_t, sin_t)


def _sb_kernel(q_ref, k_ref, v_ref, o_ref, acc_ref, carry_ref, *, tq):
    i = pl.program_id(2)
    q = q_ref[0]
    lane = lax.broadcasted_iota(I32, (1, LANES), 1)
    zero = jnp.zeros_like(q)
    q2 = jnp.concatenate([jnp.where(lane < SB_HEAD_DIM, q, zero),
                          jnp.where(lane >= SB_HEAD_DIM, q, zero)], axis=0)
    uj = lax.broadcasted_iota(I32, (2 * tq, tq), 0) % tq
    us = lax.broadcasted_iota(I32, (2 * tq, tq), 1)
    suffix = jnp.where(uj > us, 1.0, 0.0).astype(BF16)

    def block(j, masked):
        start = pl.multiple_of(j * tq, tq)
        kj = k_ref[0, pl.ds(start, tq), :]
        vj = v_ref[0, pl.ds(start, tq), :]
        z = lax.dot_general(q2, kj, (((1,), (1,)), ((), ())), preferred_element_type=F32)
        sp = jnp.maximum(z, 0.0) + jnp.log(1.0 + jnp.exp(-jnp.abs(z)))
        log_rem = -sp
        if masked:
            causal = (lax.broadcasted_iota(I32, (2 * tq, tq), 1)
                      < lax.broadcasted_iota(I32, (2 * tq, tq), 0) % tq)
            log_rem = jnp.where(causal, log_rem, 0.0)
        hi = log_rem.astype(BF16)
        lo = (log_rem - hi.astype(F32)).astype(BF16)
        after = jnp.dot(jnp.concatenate([hi, lo], axis=1), suffix, preferred_element_type=F32)
        carry = carry_ref[...]
        w = jnp.exp(z - sp + after + carry)
        if masked:
            w = jnp.where(causal, w, 0.0)
        acc_ref[...] += jnp.dot(w.astype(BF16), vj, preferred_element_type=F32)
        carry = carry + after[:, 0:1] + log_rem[:, 0:1]
        carry_ref[...] = carry
        return jnp.max(carry)

    acc_ref[...] = jnp.zeros_like(acc_ref)
    carry_ref[...] = jnp.zeros_like(carry_ref)
    alive = block(i, True)

    def cond(state):
        j, alive = state
        return jnp.logical_and(j >= 0, alive > SB_DEAD_LOG)

    def body(state):
        j, _ = state
        return j - 1, block(j, False)

    lax.while_loop(cond, body, (i - 1, alive))
    o_ref[0] = jnp.where(lane < SB_HEAD_DIM, acc_ref[0:tq, :], acc_ref[tq:2 * tq, :]).astype(BF16)


def _sb_attention(proj3, *, q_col, k_col, v_col, n_pairs, tq):
    b, s, _ = proj3.shape
    return pl.pallas_call(
        functools.partial(_sb_kernel, tq=tq),
        grid=(b, n_pairs, s // tq),
        in_specs=[pl.BlockSpec((1, tq, LANES), lambda bi, p, i: (bi, i, q_col + p)),
                  pl.BlockSpec((1, s, LANES), lambda bi, p, i: (bi, 0, k_col + p)),
                  pl.BlockSpec((1, s, LANES), lambda bi, p, i: (bi, 0, v_col + p))],
        out_specs=pl.BlockSpec((1, tq, LANES), lambda bi, p, i: (bi, i, p)),
        out_shape=jax.ShapeDtypeStruct((b, s, n_pairs * LANES), BF16),
        scratch_shapes=[pltpu.VMEM((2 * tq, LANES), F32), pltpu.VMEM((2 * tq, 1), F32)],
        compiler_params=_params(("arbitrary", "arbitrary", "arbitrary")),
        name="sb_attention",
    )(proj3, proj3, proj3)


def _diff_kernel(lq1_ref, lk1_ref, lq2_ref, lk2_ref, q_ref, k_ref, v_ref, g_ref, o_ref, vt_ref,
                 acc_ref, *, tq, lambda_init):
    i = pl.program_id(2)
    s_len = v_ref.shape[1]

    @pl.when(i == 0)
    def _():
        for c in range(s_len // tq):
            cols = slice(c * tq, (c + 1) * tq)
            vt_ref[:, cols] = v_ref[0, cols, :].astype(F32).T.astype(BF16)

    q = q_ref[0]
    lane = lax.broadcasted_iota(I32, (1, LANES), 1)
    zero = jnp.zeros_like(q)
    q2 = jnp.concatenate([jnp.where(lane < DIFF_HEAD_DIM, q, zero),
                          jnp.where(lane >= DIFF_HEAD_DIM, q, zero)], axis=0)

    def step(j, stats, masked):
        m, l = stats
        start = pl.multiple_of(j * tq, tq)
        kj = k_ref[0, pl.ds(start, tq), :]
        vtj = vt_ref[:, pl.ds(start, tq)]
        zt = lax.dot_general(kj, q2, (((1,), (1,)), ((), ())), preferred_element_type=F32)
        if masked:
            visible = (lax.broadcasted_iota(I32, (tq, 2 * tq), 0) // CHUNK
                       <= (lax.broadcasted_iota(I32, (tq, 2 * tq), 1) % tq) // CHUNK)
            zt = jnp.where(visible, zt, -jnp.inf)
        m_new = jnp.maximum(m, jnp.max(zt, axis=0, keepdims=True))
        alpha = jnp.exp(m - m_new)
        pt = jnp.exp(zt - m_new)
        l = alpha * l + jnp.sum(pt, axis=0, keepdims=True)
        acc_ref[...] = alpha * acc_ref[...] + jnp.dot(vtj, pt.astype(BF16),
                                                      preferred_element_type=F32)
        return m_new, l

    acc_ref[...] = jnp.zeros_like(acc_ref)
    init = (jnp.full((1, 2 * tq), -jnp.inf, F32), jnp.zeros((1, 2 * tq), F32))
    stats = lax.fori_loop(0, i, lambda j, st: step(j, st, False), init)
    _, l = step(i, stats, True)
    lam = (jnp.exp(jnp.sum(lq1_ref[...] * lk1_ref[...], axis=1, keepdims=True))
           - jnp.exp(jnp.sum(lq2_ref[...] * lk2_ref[...], axis=1, keepdims=True))
           + lambda_init)
    ot = acc_ref[...] / l
    ot = ot[:, 0:tq] - lam * ot[:, tq:2 * tq]
    o_ref[0] = (_rms(ot.T, g_ref[...]) * (1.0 - lambda_init)).astype(BF16)


def _diff_attention(proj3, lq1, lk1, lq2, lk2, subln, *, q_col, k_col, v_col, n_heads, tq,
                    lambda_init):
    b, s, _ = proj3.shape
    lam_spec = _const_spec((1, DIFF_HEAD_DIM))
    return pl.pallas_call(
        functools.partial(_diff_kernel, tq=tq, lambda_init=lambda_init),
        grid=(b, n_heads, s // tq),
        in_specs=[lam_spec, lam_spec, lam_spec, lam_spec,
                  pl.BlockSpec((1, tq, LANES), lambda bi, h, i: (bi, i, q_col + h)),
                  pl.BlockSpec((1, s, LANES), lambda bi, h, i: (bi, 0, k_col + h)),
                  pl.BlockSpec((1, s, LANES), lambda bi, h, i: (bi, 0, v_col + h)),
                  _const_spec((1, DIFF_V_DIM))],
        out_specs=pl.BlockSpec((1, tq, LANES), lambda bi, h, i: (bi, i, h)),
        out_shape=jax.ShapeDtypeStruct((b, s, n_heads * DIFF_V_DIM), BF16),
        scratch_shapes=[pltpu.VMEM((DIFF_V_DIM, s), BF16), pltpu.VMEM((DIFF_V_DIM, 2 * tq), F32)],
        compiler_params=_params(("arbitrary", "arbitrary", "arbitrary")),
        name="diff_attention",
    )(lq1, lk1, lq2, lk2, proj3, proj3, proj3, subln)


def _mem_kv_kernel(mem_ref, g_ref, w_ref, o_ref):
    h = _rms(mem_ref[0], g_ref[...]).astype(BF16)
    o_ref[0] = jnp.dot(h, w_ref[...], preferred_element_type=F32).astype(BF16)


def _mem_kv(mem, g, w_bf16):
    b, m, d = mem.shape
    n = w_bf16.shape[1]
    return pl.pallas_call(
        _mem_kv_kernel,
        grid=(b,),
        in_specs=[pl.BlockSpec((1, m, d), lambda i: (i, 0, 0)), _const_spec((1, d)),
                  _const_spec((d, n))],
        out_specs=pl.BlockSpec((1, m, n), lambda i: (i, 0, 0)),
        out_shape=jax.ShapeDtypeStruct((b, m, n), BF16),
        compiler_params=_params(("arbitrary",)),
        name="mem_kv",
    )(mem, g, w_bf16)


def _split_bf16(v):
    hi = v.astype(BF16)
    return hi, (v - hi.astype(F32)).astype(BF16)


def _pack_bf16_pairs(v):
    half = v.shape[1] // 2
    bits = lax.bitcast_convert_type(v.astype(BF16).astype(F32), U32)
    return (bits[:, :half] >> 16) | (bits[:, half:] & jnp.uint32(0xFFFF0000))


def _unpack_bf16_pairs(w):
    lo = lax.bitcast_convert_type(w << 16, F32)
    hi = lax.bitcast_convert_type(w & jnp.uint32(0xFFFF0000), F32)
    return jnp.concatenate([lo, hi], axis=1).astype(BF16)


def _postmix_kernel(x_ref, osb_ref, odf_ref, gsb_ref, gdf_ref, wus_ref, wud_ref, wout_ref,
                    gq_ref, wq_ref, kv_ref, wo_ref, gf_ref, wr_ref, br_ref,
                    x2_ref, hf_ref, route_ref, gate_ref, cum_ref, count_ref, *, tm, d_model):
    step = pl.program_id(0)

    @pl.when(step == 0)
    def _():
        count_ref[...] = jnp.zeros_like(count_ref)

    y_sb = jnp.dot(osb_ref[...], wus_ref[...], preferred_element_type=F32)
    y_df = jnp.dot(odf_ref[...], wud_ref[...], preferred_element_type=F32)
    mixed = (jax.nn.sigmoid(gsb_ref[...].astype(F32)) * y_sb
             + jax.nn.sigmoid(gdf_ref[...].astype(F32)) * y_df)
    x1 = x_ref[...] + jnp.dot(mixed.astype(BF16), wout_ref[...], preferred_element_type=F32)

    hq = _rms(x1, gq_ref[...]).astype(BF16)
    hd = d_model // MEM_HEADS
    q = jnp.dot(hq, wq_ref[...], preferred_element_type=F32) * (1.0 / math.sqrt(hd))
    q = q.astype(BF16)
    heads = []
    for h in range(MEM_HEADS):
        kh = kv_ref[0, :, h * hd:(h + 1) * hd]
        vh = kv_ref[0, :, d_model + h * hd:d_model + (h + 1) * hd]
        z = lax.dot_general(q[:, h * hd:(h + 1) * hd], kh, (((1,), (1,)), ((), ())),
                            preferred_element_type=F32)
        p = jnp.exp(z - jnp.max(z, axis=1, keepdims=True))
        l = jnp.sum(p, axis=1, keepdims=True)
        heads.append((jnp.dot(p.astype(BF16), vh, preferred_element_type=F32) / l).astype(BF16))
    x2 = x1 + jnp.dot(jnp.concatenate(heads, axis=1), wo_ref[...], preferred_element_type=F32)
    x2_ref[...] = x2

    hf = _rms(x2, gf_ref[...])
    hf_ref[...] = _pack_bf16_pairs(hf)
    h_hi, h_lo = _split_bf16(hf)
    w_hi, w_lo = _split_bf16(wr_ref[...])
    logits = (jnp.dot(h_hi, w_hi, preferred_element_type=F32)
              + jnp.dot(h_hi, w_lo, preferred_element_type=F32)
              + jnp.dot(h_lo, w_hi, preferred_element_type=F32)) + br_ref[...]
    lane = lax.broadcasted_iota(I32, (tm, LANES), 1)
    work = logits
    vals, idxs, hots = [], [], []
    for _ in range(TOP_K):
        mx = jnp.max(work, axis=1, keepdims=True)
        idx = jnp.min(jnp.where(work == mx, lane, LANES), axis=1, keepdims=True)
        hot = lane == idx
        work = jnp.where(hot, NEG_BIG, work)
        vals.append(mx)
        idxs.append(idx)
        hots.append(hot)
    exps = [jnp.exp(v - vals[0]) for v in vals]
    denom = exps[0] + exps[1] + exps[2] + exps[3]

    onehot_sum = jnp.zeros((tm, LANES), F32)
    for hot in hots:
        onehot_sum = onehot_sum + jnp.where(hot, 1.0, 0.0)
    r = lax.broadcasted_iota(I32, (tm, tm), 0)
    c = lax.broadcasted_iota(I32, (tm, tm), 1)
    lower = jnp.where(c < r, 1.0, 0.0).astype(BF16)
    rank = jnp.dot(lower, onehot_sum.astype(BF16), preferred_element_type=F32) + count_ref[...]
    route = jnp.zeros((tm, LANES), I32)
    gates = jnp.zeros((tm, LANES), F32)
    for k in range(TOP_K):
        pos = jnp.sum(jnp.where(hots[k], rank, 0.0), axis=1, keepdims=True).astype(I32)
        route = jnp.where(lane == k, idxs[k], route)
        route = jnp.where(lane == TOP_K + k, pos, route)
        gates = jnp.where(lane == k, exps[k] / denom, gates)
    route_ref[...] = route
    gate_ref[...] = gates
    count_ref[...] = count_ref[...] + jnp.sum(onehot_sum, axis=0, keepdims=True)
    cum_ref[0] = jnp.broadcast_to(count_ref[...], (8, LANES))


def _post_mix(x2d, proj2, o_sb, o_diff, w_up_sb, w_up_diff, w_out, g_memq, w_memq, kv, w_memo,
              g_ffn, w_router_pad, b_router_pad, *, tm, seq, gate_col):
    n_tok, d = x2d.shape
    n_tiles = n_tok // tm
    tiles_per_batch = seq // tm
    row = lambda i: (i, 0)
    in_specs = [
        pl.BlockSpec((tm, d), row),
        pl.BlockSpec((tm, o_sb.shape[1]), row),
        pl.BlockSpec((tm, o_diff.shape[1]), row),
        pl.BlockSpec((tm, d), lambda i: (i, gate_col)),
        pl.BlockSpec((tm, d), lambda i: (i, gate_col + 1)),
        _const_spec(w_up_sb.shape), _const_spec(w_up_diff.shape), _const_spec(w_out.shape),
        _const_spec((1, d)), _const_spec(w_memq.shape),
        pl.BlockSpec((1,) + kv.shape[1:], lambda i: (i // tiles_per_batch, 0, 0)),
        _const_spec(w_memo.shape), _const_spec((1, d)),
        _const_spec(w_router_pad.shape), _const_spec((1, LANES)),
    ]
    out_specs = [
        pl.BlockSpec((tm, d), row),
        pl.BlockSpec((tm, d // 2), row),
        pl.BlockSpec((tm, LANES), row),
        pl.BlockSpec((tm, LANES), row),
        pl.BlockSpec((1, 8, LANES), lambda i: (i, 0, 0)),
    ]
    out_shape = [
        jax.ShapeDtypeStruct((n_tok, d), F32),
        jax.ShapeDtypeStruct((n_tok, d // 2), U32),
        jax.ShapeDtypeStruct((n_tok, LANES), I32),
        jax.ShapeDtypeStruct((n_tok, LANES), F32),
        jax.ShapeDtypeStruct((n_tiles, 8, LANES), F32),
    ]
    return pl.pallas_call(
        functools.partial(_postmix_kernel, tm=tm, d_model=d),
        grid=(n_tiles,),
        in_specs=in_specs,
        out_specs=out_specs,
        out_shape=out_shape,
        scratch_shapes=[pltpu.VMEM((1, LANES), F32)],
        compiler_params=_params(("arbitrary",)),
        name="post_mix",
    )(x2d, o_sb, o_diff, proj2, proj2, w_up_sb, w_up_diff, w_out, g_memq, w_memq, kv, w_memo,
      g_ffn, w_router_pad, b_router_pad)


def _dispatch_kernel(zstart_ref, zflag_ref, nused_ref, dest_ref, hf_ref, xs_hbm, zbuf, sem, zsem,
                     *, tm, bm):
    s = pl.program_id(0)

    @pl.when(s == 0)
    def _():
        zbuf[...] = jnp.zeros_like(zbuf)

        def zero_copy(start):
            start = pl.multiple_of(start, bm)
            return pltpu.make_async_copy(zbuf, xs_hbm.at[pl.ds(start, bm), :], zsem)

        n_blocks = xs_hbm.shape[0] // bm
        for action in ("start", "wait"):
            for e in range(N_EXPERTS):
                @pl.when(zflag_ref[e] != 0)
                def _(e=e, action=action):
                    getattr(zero_copy(zstart_ref[e]), action)()

            def trailing(blk, _, action=action):
                getattr(zero_copy(blk * bm), action)()
                return 0

            lax.fori_loop(nused_ref[0], n_blocks, trailing, 0)

    def issue(t, _):
        for k in range(TOP_K):
            row = dest_ref[t * TOP_K + k]
            pltpu.make_async_copy(hf_ref.at[pl.ds(t, 1), :], xs_hbm.at[pl.ds(row, 1), :],
                                  sem).start()
        return 0

    lax.fori_loop(0, tm, issue, 0)
    for _ in range(TOP_K):
        pltpu.make_async_copy(hf_ref, xs_hbm.at[pl.ds(0, tm), :], sem).wait()


def _dispatch(zstart, zflag, n_used, dest_flat, hf_packed, *, tm, bm, n_rows):
    n_tok, width = hf_packed.shape
    grid_spec = pltpu.PrefetchScalarGridSpec(
        num_scalar_prefetch=3,
        grid=(n_tok // tm,),
        in_specs=[
            pl.BlockSpec((tm * TOP_K,), lambda i, *_: (i,), memory_space=pltpu.SMEM),
            pl.BlockSpec((tm, width), lambda i, *_: (i, 0)),
        ],
        out_specs=pl.BlockSpec(memory_space=pl.ANY),
        scratch_shapes=[pltpu.VMEM((bm, width), U32), pltpu.SemaphoreType.DMA(()),
                        pltpu.SemaphoreType.DMA(())],
    )
    return pl.pallas_call(
        functools.partial(_dispatch_kernel, tm=tm, bm=bm),
        grid_spec=grid_spec,
        out_shape=jax.ShapeDtypeStruct((n_rows, width), U32),
        compiler_params=_params(("arbitrary",)),
        name="dispatch",
    )(zstart, zflag, n_used, dest_flat, hf_packed)


def _expert_kernel(be_ref, valid_ref, src_ref, xs_ref, wgu_ref, bgu_ref, wd_ref, bd_ref, o_ref,
                   wgu_bf, wd_bf, *, d_ff):
    r = pl.program_id(0)
    valid = valid_ref[r] != 0
    fresh = jnp.logical_or(r == 0, be_ref[r] != be_ref[jnp.maximum(r - 1, 0)])

    @pl.when(jnp.logical_and(valid, fresh))
    def _():
        wgu_bf[...] = wgu_ref[0].astype(BF16)
        wd_bf[...] = wd_ref[0].astype(BF16)

    @pl.when(jnp.logical_not(valid))
    def _():
        o_ref[...] = jnp.zeros_like(o_ref)

    @pl.when(valid)
    def _():
        xb = _unpack_bf16_pairs(xs_ref[...])
        hgu = jnp.dot(xb, wgu_bf[...], preferred_element_type=F32) + bgu_ref[0]
        glu = jnp.minimum(hgu[:, :d_ff], SWIGLU_LIMIT)
        lin = jnp.clip(hgu[:, d_ff:], -SWIGLU_LIMIT, SWIGLU_LIMIT)
        act = glu * jax.nn.sigmoid(SWIGLU_ALPHA * glu) * (lin + 1.0)
        o_ref[...] = jnp.dot(act.astype(BF16), wd_bf[...], preferred_element_type=F32) + bd_ref[0]


def _expert_ffn(blk_expert, blk_valid, blk_src, xs, w_gu, b_gu, w_down, b_down, *, bm):
    width = xs.shape[1]
    n_grid = blk_expert.shape[0]
    d_ff, d = w_down.shape[1:]
    grid_spec = pltpu.PrefetchScalarGridSpec(
        num_scalar_prefetch=3,
        grid=(n_grid,),
        in_specs=[
            pl.BlockSpec((bm, width), lambda r, be, valid, src: (src[r], 0)),
            pl.BlockSpec((1, d, 2 * d_ff), lambda r, be, *_: (be[r], 0, 0)),
            pl.BlockSpec((1, 1, 2 * d_ff), lambda r, be, *_: (be[r], 0, 0)),
            pl.BlockSpec((1, d_ff, d), lambda r, be, *_: (be[r], 0, 0)),
            pl.BlockSpec((1, 1, d), lambda r, be, *_: (be[r], 0, 0)),
        ],
        out_specs=pl.BlockSpec((bm, d), lambda r, *_: (r, 0)),
        scratch_shapes=[pltpu.VMEM((d, 2 * d_ff), BF16), pltpu.VMEM((d_ff, d), BF16)],
    )
    return pl.pallas_call(
        functools.partial(_expert_kernel, d_ff=d_ff),
        grid_spec=grid_spec,
        out_shape=jax.ShapeDtypeStruct((n_grid * bm, d), F32),
        compiler_params=_params(("arbitrary",)),
        name="expert_ffn",
    )(blk_expert, blk_valid, blk_src, xs, w_gu, b_gu, w_down, b_down)


def _combine_kernel(dest_ref, x_ref, gate_ref, g_ref, ys_hbm, o_ref, buf, sem, *, tm):
    def issue(t, _):
        for k in range(TOP_K):
            row = dest_ref[t * TOP_K + k]
            pltpu.make_async_copy(ys_hbm.at[pl.ds(row, 1), :], buf.at[k, pl.ds(t, 1), :],
                                  sem).start()
        return 0

    lax.fori_loop(0, tm, issue, 0)
    for k in range(TOP_K):
        pltpu.make_async_copy(ys_hbm.at[pl.ds(0, tm), :], buf.at[k], sem).wait()
    gates = gate_ref[...]
    acc = x_ref[...]
    for k in range(TOP_K):
        acc = acc + gates[:, k:k + 1] * buf[k]
    o_ref[...] = _rms(acc, g_ref[...])


def _combine(dest_flat, x2, gates, g_final, ys, *, tm):
    n_tok, d = x2.shape
    return pl.pallas_call(
        functools.partial(_combine_kernel, tm=tm),
        grid=(n_tok // tm,),
        in_specs=[
            pl.BlockSpec((tm * TOP_K,), lambda i: (i,), memory_space=pltpu.SMEM),
            pl.BlockSpec((tm, d), lambda i: (i, 0)),
            pl.BlockSpec((tm, LANES), lambda i: (i, 0)),
            pl.BlockSpec((1, d), lambda i: (0, 0)),
            pl.BlockSpec(memory_space=pl.ANY),
        ],
        out_specs=pl.BlockSpec((tm, d), lambda i: (i, 0)),
        out_shape=jax.ShapeDtypeStruct((n_tok, d), F32),
        scratch_shapes=[pltpu.VMEM((TOP_K, tm, d), F32), pltpu.SemaphoreType.DMA(())],
        compiler_params=_params(("arbitrary",)),
        name="combine",
    )(dest_flat, x2, gates, g_final, ys)


def _routing_tables(route, cum, *, bm, n_grid):
    expert = route[:, 0:TOP_K]
    pos = route[:, TOP_K:2 * TOP_K]
    experts = jnp.arange(N_EXPERTS, dtype=I32)
    counts = cum[-1, 0, :N_EXPERTS].astype(I32)
    padded = (counts + bm - 1) // bm * bm
    pend = jnp.cumsum(padded)
    pstart = pend - padded
    dest = pos + jnp.sum(jnp.where(expert[:, :, None] == experts, pstart, 0), axis=2)

    blk = jnp.arange(n_grid, dtype=I32)
    blk_expert = jnp.minimum(jnp.sum(pend[None, :] <= (blk * bm)[:, None], axis=1),
                             N_EXPERTS - 1).astype(I32)
    n_used = pend[-1] // bm
    blk_valid = (blk < n_used).astype(I32)
    blk_src = jnp.minimum(blk, jnp.maximum(n_used - 1, 0)).astype(I32)
    zflag = (padded > 0).astype(I32)
    zstart = jnp.maximum(pend - bm, 0).astype(I32)
    return (dest.astype(I32).reshape(-1), blk_expert, blk_valid, blk_src, zstart, zflag,
            n_used.astype(I32).reshape(1))


def kernel(x, mem, positions, norm_mix, w_in, lambda_q1, lambda_k1, lambda_q2, lambda_k2,
           diff_subln, w_up_sb, w_up_diff, w_out, norm_mem_q, norm_mem_kv, w_mem_q, w_mem_kv,
           w_mem_o, norm_ffn, w_router, b_router, w_gate_up, b_gate_up, w_down, b_down,
           norm_final):
    b, s, d = x.shape
    n_tok = b * s
    depth = norm_mix.shape[0]
    sb_width = w_up_sb.shape[1]
    diff_width = w_up_diff.shape[1]
    n_in = w_in.shape[2]
    chunk = 512
    assert sb_width == chunk and diff_width == chunk and d == 2 * chunk
    sbq, sbk, sbv, dq, dk, dv = range(6)
    blocks_per_chunk = chunk // LANES
    scale = 1.0 / math.sqrt(SB_HEAD_DIM)
    assert SB_HEAD_DIM == DIFF_HEAD_DIM

    tm_in = min(512, n_tok)
    tq_sb = min(256, s)
    tq_diff = min(512, s)
    tm_post = min(256, s)
    bm = 256
    n_grid = (n_tok * TOP_K) // bm + N_EXPERTS

    cos_t, sin_t = _rope_tables(positions)
    x2d = x.reshape(n_tok, d)
    for l in range(depth):
        lambda_init = 0.8 - 0.6 * math.exp(-0.3 * l)
        proj = _in_proj(x2d, norm_mix[l].reshape(1, d), w_in[l].astype(BF16), cos_t, sin_t,
                        tm=tm_in, chunk=chunk, rope_chunks=(dq, dk), scale_chunks=(sbq, dq),
                        scale=scale)
        proj3 = proj.reshape(b, s, n_in)
        o_sb = _sb_attention(proj3, q_col=sbq * blocks_per_chunk, k_col=sbk * blocks_per_chunk,
                             v_col=sbv * blocks_per_chunk, n_pairs=sb_width // LANES, tq=tq_sb)
        o_diff = _diff_attention(
            proj3, lambda_q1[l].reshape(1, -1), lambda_k1[l].reshape(1, -1),
            lambda_q2[l].reshape(1, -1), lambda_k2[l].reshape(1, -1),
            diff_subln[l].reshape(1, -1), q_col=dq * blocks_per_chunk,
            k_col=dk * blocks_per_chunk, v_col=dv * blocks_per_chunk,
            n_heads=diff_width // DIFF_V_DIM, tq=tq_diff, lambda_init=lambda_init)
        kv = _mem_kv(mem, norm_mem_kv[l].reshape(1, d), w_mem_kv[l].astype(BF16))
        w_router_pad = jnp.zeros((d, LANES), F32).at[:, :N_EXPERTS].set(w_router[l])
        b_router_pad = jnp.full((1, LANES), NEG_BIG, F32).at[0, :N_EXPERTS].set(b_router[l])
        x_res, hf, route, gates, cum = _post_mix(
            x2d, proj, o_sb.reshape(n_tok, sb_width), o_diff.reshape(n_tok, diff_width),
            w_up_sb[l].astype(BF16), w_up_diff[l].astype(BF16), w_out[l].astype(BF16),
            norm_mem_q[l].reshape(1, d), w_mem_q[l].astype(BF16), kv, w_mem_o[l].astype(BF16),
            norm_ffn[l].reshape(1, d), w_router_pad, b_router_pad,
            tm=tm_post, seq=s, gate_col=6 * chunk // d)
        dest, blk_expert, blk_valid, blk_src, zstart, zflag, n_used = _routing_tables(
            route, cum, bm=bm, n_grid=n_grid)
        xs = _dispatch(zstart, zflag, n_used, dest, hf, tm=tm_post, bm=bm, n_rows=n_grid * bm)
        ys = _expert_ffn(blk_expert, blk_valid, blk_src, xs, w_gate_up[l],
                         b_gate_up[l][:, None, :], w_down[l], b_down[l][:, None, :], bm=bm)
        g_next = norm_final if l == depth - 1 else None
        assert g_next is not None, "only depth 1 is wired: the final norm is fused into combine"
        x2d = _combine(dest, x_res, gates, g_next.reshape(1, d), ys, tm=tm_post)
    return x2d.reshape(b, s, d)
```

```python
import functools
import math

import jax
import jax.numpy as jnp
from jax import lax
from jax.experimental import pallas as pl
from jax.experimental.pallas import tpu as pltpu

F32 = jnp.float32
BF16 = jnp.bfloat16
I32 = jnp.int32
U32 = jnp.uint32

LANES = 128
VMEM_LIMIT_BYTES = 56 * 1024 * 1024

NORM_EPS = 1e-6
ROPE_THETA = 10000.0
CHUNK = 64
SB_HEAD_DIM = 64
DIFF_HEAD_DIM = 64
DIFF_V_DIM = 128
MEM_HEADS = 4
N_EXPERTS = 32
TOP_K = 4
SWIGLU_LIMIT = 7.0
SWIGLU_ALPHA = 1.702

SB_DEAD_LOG = -105.0
NEG_BIG = -1e30


def _params(semantics):
    return pltpu.CompilerParams(dimension_semantics=semantics,
                                vmem_limit_bytes=VMEM_LIMIT_BYTES)


def _const_spec(shape):
    nd = len(shape)
    return pl.BlockSpec(shape, lambda *_: (0,) * nd)


def _rms(x, g):
    return x * lax.rsqrt(jnp.mean(x * x, axis=-1, keepdims=True) + NORM_EPS) * g


def _rope_table_kernel(pos_ref, inv_ref, cos_ref, sin_ref):
    ang = pos_ref[...] * inv_ref[...]
    cos_ref[...] = jnp.cos(ang)
    sin_ref[...] = jnp.sin(ang)


def _rope_tables(positions):
    n_tok = positions.size
    half = DIFF_HEAD_DIM // 2
    per_row = LANES // half
    rows = n_tok // per_row
    pos = jnp.repeat(positions.reshape(rows, per_row).astype(F32), half, axis=1)
    inv = ROPE_THETA ** (-jnp.arange(half, dtype=F32) / half)
    inv = jnp.tile(inv, per_row).reshape(1, LANES)
    tr = min(rows, 512)
    cos, sin = pl.pallas_call(
        _rope_table_kernel,
        grid=(rows // tr,),
        in_specs=[pl.BlockSpec((tr, LANES), lambda i: (i, 0)), _const_spec((1, LANES))],
        out_specs=[pl.BlockSpec((tr, LANES), lambda i: (i, 0))] * 2,
        out_shape=[jax.ShapeDtypeStruct((rows, LANES), F32)] * 2,
        compiler_params=_params(("arbitrary",)),
        name="rope_table",
    )(pos, inv)
    cos = cos.reshape(n_tok, half)
    sin = sin.reshape(n_tok, half)
    cos_t = jnp.tile(cos, (1, LANES // half))
    sin_t = jnp.tile(jnp.concatenate([-sin, sin], axis=1), (1, LANES // (2 * half)))
    return cos_t, sin_t


def _inproj_kernel(x_ref, g_ref, w_ref, cos_ref, sin_ref, o_ref, *, chunk, rope_chunks,
                   scale_chunks, scale):
    h = _rms(x_ref[...], g_ref[...]).astype(BF16)
    lane = lax.broadcasted_iota(I32, (1, chunk), 1)
    first_half = (lane % DIFF_HEAD_DIM) < (DIFF_HEAD_DIM // 2)
    for c in range(w_ref.shape[1] // chunk):
        cols = slice(c * chunk, (c + 1) * chunk)
        acc = jnp.dot(h, w_ref[:, cols], preferred_element_type=F32)
        if c in rope_chunks:
            cos = jnp.tile(cos_ref[...], (1, chunk // LANES))
            sin = jnp.tile(sin_ref[...], (1, chunk // LANES))
            partner = jnp.where(first_half,
                                pltpu.roll(acc, chunk - DIFF_HEAD_DIM // 2, 1),
                                pltpu.roll(acc, DIFF_HEAD_DIM // 2, 1))
            acc = acc * cos + partner * sin
        if c in scale_chunks:
            acc = acc * scale
        o_ref[:, cols] = acc.astype(BF16)


def _in_proj(x2d, g, w_bf16, cos_t, sin_t, *, tm, chunk, rope_chunks, scale_chunks, scale):
    n_tok, d = x2d.shape
    n_in = w_bf16.shape[1]
    kern = functools.partial(_inproj_kernel, chunk=chunk, rope_chunks=rope_chunks,
                             scale_chunks=scale_chunks, scale=scale)
    return pl.pallas_call(
        kern,
        grid=(n_tok // tm,),
        in_specs=[pl.BlockSpec((tm, d), lambda i: (i, 0)),
                  _const_spec((1, d)),
                  _const_spec((d, n_in)),
                  pl.BlockSpec((tm, LANES), lambda i: (i, 0)),
                  pl.BlockSpec((tm, LANES), lambda i: (i, 0))],
        out_specs=pl.BlockSpec((tm, n_in), lambda i: (i, 0)),
        out_shape=jax.ShapeDtypeStruct((n_tok, n_in), BF16),
        compiler_params=_params(("arbitrary",)),
        name="in_proj",
    )(x2d, g, w_bf16, cos_t, sin_t)


def _sb_kernel(q_ref, k_ref, v_ref, o_ref, acc_ref, carry_ref, *, tq):
    i = pl.program_id(2)
    q = q_ref[0]
    lane = lax.broadcasted_iota(I32, (1, LANES), 1)
    zero = jnp.zeros_like(q)
    q2 = jnp.concatenate([jnp.where(lane < SB_HEAD_DIM, q, zero),
                          jnp.where(lane >= SB_HEAD_DIM, q, zero)], axis=0)
    uj = lax.broadcasted_iota(I32, (2 * tq, tq), 0) % tq
    us = lax.broadcasted_iota(I32, (2 * tq, tq), 1)
    suffix = jnp.where(uj > us, 1.0, 0.0).astype(BF16)

    def block(j, masked):
        start = pl.multiple_of(j * tq, tq)
        kj = k_ref[0, pl.ds(start, tq), :]
        vj = v_ref[0, pl.ds(start, tq), :]
        z = lax.dot_general(q2, kj, (((1,), (1,)), ((), ())), preferred_element_type=F32)
        sp = jnp.maximum(z, 0.0) + jnp.log(1.0 + jnp.exp(-jnp.abs(z)))
        log_rem = -sp
        if masked:
            causal = (lax.broadcasted_iota(I32, (2 * tq, tq), 1)
                      < lax.broadcasted_iota(I32, (2 * tq, tq), 0) % tq)
            log_rem = jnp.where(causal, log_rem, 0.0)
        hi = log_rem.astype(BF16)
        lo = (log_rem - hi.astype(F32)).astype(BF16)
        after = jnp.dot(jnp.concatenate([hi, lo], axis=1), suffix, preferred_element_type=F32)
        carry = carry_ref[...]
        w = jnp.exp(z - sp + after + carry)
        if masked:
            w = jnp.where(causal, w, 0.0)
        acc_ref[...] += jnp.dot(w.astype(BF16), vj, preferred_element_type=F32)
        carry = carry + after[:, 0:1] + log_rem[:, 0:1]
        carry_ref[...] = carry
        return jnp.max(carry)

    acc_ref[...] = jnp.zeros_like(acc_ref)
    carry_ref[...] = jnp.zeros_like(carry_ref)
    alive = block(i, True)

    def cond(state):
        j, alive = state
        return jnp.logical_and(j >= 0, alive > SB_DEAD_LOG)

    def body(state):
        j, _ = state
        return j - 1, block(j, False)

    lax.while_loop(cond, body, (i - 1, alive))
    o_ref[0] = jnp.where(lane < SB_HEAD_DIM, acc_ref[0:tq, :], acc_ref[tq:2 * tq, :]).astype(BF16)


def _sb_attention(proj3, *, q_col, k_col, v_col, n_pairs, tq):
    b, s, _ = proj3.shape
    return pl.pallas_call(
        functools.partial(_sb_kernel, tq=tq),
        grid=(b, n_pairs, s // tq),
        in_specs=[pl.BlockSpec((1, tq, LANES), lambda bi, p, i: (bi, i, q_col + p)),
                  pl.BlockSpec((1, s, LANES), lambda bi, p, i: (bi, 0, k_col + p)),
                  pl.BlockSpec((1, s, LANES), lambda bi, p, i: (bi, 0, v_col + p))],
        out_specs=pl.BlockSpec((1, tq, LANES), lambda bi, p, i: (bi, i, p)),
        out_shape=jax.ShapeDtypeStruct((b, s, n_pairs * LANES), BF16),
        scratch_shapes=[pltpu.VMEM((2 * tq, LANES), F32), pltpu.VMEM((2 * tq, 1), F32)],
        compiler_params=_params(("arbitrary", "arbitrary", "arbitrary")),
        name="sb_attention",
    )(proj3, proj3, proj3)


def _diff_kernel(lq1_ref, lk1_ref, lq2_ref, lk2_ref, q_ref, k_ref, v_ref, g_ref, o_ref, vt_ref,
                 acc_ref, *, tq, lambda_init):
    i = pl.program_id(2)
    s_len = v_ref.shape[1]

    @pl.when(i == 0)
    def _():
        for c in range(s_len // tq):
            cols = slice(c * tq, (c + 1) * tq)
            vt_ref[:, cols] = v_ref[0, cols, :].astype(F32).T.astype(BF16)

    q = q_ref[0]
    lane = lax.broadcasted_iota(I32, (1, LANES), 1)
    zero = jnp.zeros_like(q)
    q2 = jnp.concatenate([jnp.where(lane < DIFF_HEAD_DIM, q, zero),
                          jnp.where(lane >= DIFF_HEAD_DIM, q, zero)], axis=0)

    def step(j, stats, masked):
        m, l = stats
        start = pl.multiple_of(j * tq, tq)
        kj = k_ref[0, pl.ds(start, tq), :]
        vtj = vt_ref[:, pl.ds(start, tq)]
        zt = lax.dot_general(kj, q2, (((1,), (1,)), ((), ())), preferred_element_type=F32)
        if masked:
            visible = (lax.broadcasted_iota(I32, (tq, 2 * tq), 0) // CHUNK
                       <= (lax.broadcasted_iota(I32, (tq, 2 * tq), 1) % tq) // CHUNK)
            zt = jnp.where(visible, zt, -jnp.inf)
        m_new = jnp.maximum(m, jnp.max(zt, axis=0, keepdims=True))
        alpha = jnp.exp(m - m_new)
        pt = jnp.exp(zt - m_new)
        l = alpha * l + jnp.sum(pt, axis=0, keepdims=True)
        acc_ref[...] = alpha * acc_ref[...] + jnp.dot(vtj, pt.astype(BF16),
                                                      preferred_element_type=F32)
        return m_new, l

    acc_ref[...] = jnp.zeros_like(acc_ref)
    init = (jnp.full((1, 2 * tq), -jnp.inf, F32), jnp.zeros((1, 2 * tq), F32))
    stats = lax.fori_loop(0, i, lambda j, st: step(j, st, False), init)
    _, l = step(i, stats, True)
    lam = (jnp.exp(jnp.sum(lq1_ref[...] * lk1_ref[...], axis=1, keepdims=True))
           - jnp.exp(jnp.sum(lq2_ref[...] * lk2_ref[...], axis=1, keepdims=True))
           + lambda_init)
    ot = acc_ref[...] / l
    ot = ot[:, 0:tq] - lam * ot[:, tq:2 * tq]
    o_ref[0] = (_rms(ot.T, g_ref[...]) * (1.0 - lambda_init)).astype(BF16)


def _diff_attention(proj3, lq1, lk1, lq2, lk2, subln, *, q_col, k_col, v_col, n_heads, tq,
                    lambda_init):
    b, s, _ = proj3.shape
    lam_spec = _const_spec((1, DIFF_HEAD_DIM))
    return pl.pallas_call(
        functools.partial(_diff_kernel, tq=tq, lambda_init=lambda_init),
        grid=(b, n_heads, s // tq),
        in_specs=[lam_spec, lam_spec, lam_spec, lam_spec,
                  pl.BlockSpec((1, tq, LANES), lambda bi, h, i: (bi, i, q_col + h)),
                  pl.BlockSpec((1, s, LANES), lambda bi, h, i: (bi, 0, k_col + h)),
                  pl.BlockSpec((1, s, LANES), lambda bi, h, i: (bi, 0, v_col + h)),
                  _const_spec((1, DIFF_V_DIM))],
        out_specs=pl.BlockSpec((1, tq, LANES), lambda bi, h, i: (bi, i, h)),
        out_shape=jax.ShapeDtypeStruct((b, s, n_heads * DIFF_V_DIM), BF16),
        scratch_shapes=[pltpu.VMEM((DIFF_V_DIM, s), BF16), pltpu.VMEM((DIFF_V_DIM, 2 * tq), F32)],
        compiler_params=_params(("arbitrary", "arbitrary", "arbitrary")),
        name="diff_attention",
    )(lq1, lk1, lq2, lk2, proj3, proj3, proj3, subln)


def _mem_kv_kernel(mem_ref, g_ref, w_ref, o_ref):
    h = _rms(mem_ref[0], g_ref[...]).astype(BF16)
    o_ref[0] = jnp.dot(h, w_ref[...], preferred_element_type=F32).astype(BF16)


def _mem_kv(mem, g, w_bf16):
    b, m, d = mem.shape
    n = w_bf16.shape[1]
    return pl.pallas_call(
        _mem_kv_kernel,
        grid=(b,),
        in_specs=[pl.BlockSpec((1, m, d), lambda i: (i, 0, 0)), _const_spec((1, d)),
                  _const_spec((d, n))],
        out_specs=pl.BlockSpec((1, m, n), lambda i: (i, 0, 0)),
        out_shape=jax.ShapeDtypeStruct((b, m, n), BF16),
        compiler_params=_params(("arbitrary",)),
        name="mem_kv",
    )(mem, g, w_bf16)


def _split_bf16(v):
    hi = v.astype(BF16)
    return hi, (v - hi.astype(F32)).astype(BF16)


def _pack_bf16_pairs(v):
    half = v.shape[1] // 2
    bits = lax.bitcast_convert_type(v.astype(BF16).astype(F32), U32)
    return (bits[:, :half] >> 16) | (bits[:, half:] & jnp.uint32(0xFFFF0000))


def _unpack_bf16_pairs(w):
    lo = lax.bitcast_convert_type(w << 16, F32)
    hi = lax.bitcast_convert_type(w & jnp.uint32(0xFFFF0000), F32)
    return jnp.concatenate([lo, hi], axis=1).astype(BF16)


def _postmix_kernel(x_ref, osb_ref, odf_ref, gsb_ref, gdf_ref, wus_ref, wud_ref, wout_ref,
                    gq_ref, wq_ref, kv_ref, wo_ref, gf_ref, wr_ref, br_ref,
                    x2_ref, hf_ref, route_ref, gate_ref, cum_ref, count_ref, *, tm, d_model):
    step = pl.program_id(0)

    @pl.when(step == 0)
    def _():
        count_ref[...] = jnp.zeros_like(count_ref)

    y_sb = jnp.dot(osb_ref[...], wus_ref[...], preferred_element_type=F32)
    y_df = jnp.dot(odf_ref[...], wud_ref[...], preferred_element_type=F32)
    mixed = (jax.nn.sigmoid(gsb_ref[...].astype(F32)) * y_sb
             + jax.nn.sigmoid(gdf_ref[...].astype(F32)) * y_df)
    x1 = x_ref[...] + jnp.dot(mixed.astype(BF16), wout_ref[...], preferred_element_type=F32)

    hq = _rms(x1, gq_ref[...]).astype(BF16)
    hd = d_model // MEM_HEADS
    q = jnp.dot(hq, wq_ref[...], preferred_element_type=F32) * (1.0 / math.sqrt(hd))
    q = q.astype(BF16)
    heads = []
    for h in range(MEM_HEADS):
        kh = kv_ref[0, :, h * hd:(h + 1) * hd]
        vh = kv_ref[0, :, d_model + h * hd:d_model + (h + 1) * hd]
        z = lax.dot_general(q[:, h * hd:(h + 1) * hd], kh, (((1,), (1,)), ((), ())),
                            preferred_element_type=F32)
        p = jnp.exp(z - jnp.max(z, axis=1, keepdims=True))
        l = jnp.sum(p, axis=1, keepdims=True)
        heads.append((jnp.dot(p.astype(BF16), vh, preferred_element_type=F32) / l).astype(BF16))
    x2 = x1 + jnp.dot(jnp.concatenate(heads, axis=1), wo_ref[...], preferred_element_type=F32)
    x2_ref[...] = x2

    hf = _rms(x2, gf_ref[...])
    hf_ref[...] = _pack_bf16_pairs(hf)
    h_hi, h_lo = _split_bf16(hf)
    w_hi, w_lo = _split_bf16(wr_ref[...])
    logits = (jnp.dot(h_hi, w_hi, preferred_element_type=F32)
              + jnp.dot(h_hi, w_lo, preferred_element_type=F32)
              + jnp.dot(h_lo, w_hi, preferred_element_type=F32)) + br_ref[...]
    lane = lax.broadcasted_iota(I32, (tm, LANES), 1)
    work = logits
    vals, idxs, hots = [], [], []
    for _ in range(TOP_K):
        mx = jnp.max(work, axis=1, keepdims=True)
        idx = jnp.min(jnp.where(work == mx, lane, LANES), axis=1, keepdims=True)
        hot = lane == idx
        work = jnp.where(hot, NEG_BIG, work)
        vals.append(mx)
        idxs.append(idx)
        hots.append(hot)
    exps = [jnp.exp(v - vals[0]) for v in vals]
    denom = exps[0] + exps[1] + exps[2] + exps[3]

    onehot_sum = jnp.zeros((tm, LANES), F32)
    for hot in hots:
        onehot_sum = onehot_sum + jnp.where(hot, 1.0, 0.0)
    r = lax.broadcasted_iota(I32, (tm, tm), 0)
    c = lax.broadcasted_iota(I32, (tm, tm), 1)
    lower = jnp.where(c < r, 1.0, 0.0).astype(BF16)
    rank = jnp.dot(lower, onehot_sum.astype(BF16), preferred_element_type=F32) + count_ref[...]
    route = jnp.zeros((tm, LANES), I32)
    gates = jnp.zeros((tm, LANES), F32)
    for k in range(TOP_K):
        pos = jnp.sum(jnp.where(hots[k], rank, 0.0), axis=1, keepdims=True).astype(I32)
        route = jnp.where(lane == k, idxs[k], route)
        route = jnp.where(lane == TOP_K + k, pos, route)
        gates = jnp.where(lane == k, exps[k] / denom, gates)
    route_ref[...] = route
    gate_ref[...] = gates
    count_ref[...] = count_ref[...] + jnp.sum(onehot_sum, axis=0, keepdims=True)
    cum_ref[0] = jnp.broadcast_to(count_ref[...], (8, LANES))


def _post_mix(x2d, proj2, o_sb, o_diff, w_up_sb, w_up_diff, w_out, g_memq, w_memq, kv, w_memo,
              g_ffn, w_router_pad, b_router_pad, *, tm, seq, gate_col):
    n_tok, d = x2d.shape
    n_tiles = n_tok // tm
    tiles_per_batch = seq // tm
    row = lambda i: (i, 0)
    in_specs = [
        pl.BlockSpec((tm, d), row),
        pl.BlockSpec((tm, o_sb.shape[1]), row),
        pl.BlockSpec((tm, o_diff.shape[1]), row),
        pl.BlockSpec((tm, d), lambda i: (i, gate_col)),
        pl.BlockSpec((tm, d), lambda i: (i, gate_col + 1)),
        _const_spec(w_up_sb.shape), _const_spec(w_up_diff.shape), _const_spec(w_out.shape),
        _const_spec((1, d)), _const_spec(w_memq.shape),
        pl.BlockSpec((1,) + kv.shape[1:], lambda i: (i // tiles_per_batch, 0, 0)),
        _const_spec(w_memo.shape), _const_spec((1, d)),
        _const_spec(w_router_pad.shape), _const_spec((1, LANES)),
    ]
    out_specs = [
        pl.BlockSpec((tm, d), row),
        pl.BlockSpec((tm, d // 2), row),
        pl.BlockSpec((tm, LANES), row),
        pl.BlockSpec((tm, LANES), row),
        pl.BlockSpec((1, 8, LANES), lambda i: (i, 0, 0)),
    ]
    out_shape = [
        jax.ShapeDtypeStruct((n_tok, d), F32),
        jax.ShapeDtypeStruct((n_tok, d // 2), U32),
        jax.ShapeDtypeStruct((n_tok, LANES), I32),
        jax.ShapeDtypeStruct((n_tok, LANES), F32),
        jax.ShapeDtypeStruct((n_tiles, 8, LANES), F32),
    ]
    return pl.pallas_call(
        functools.partial(_postmix_kernel, tm=tm, d_model=d),
        grid=(n_tiles,),
        in_specs=in_specs,
        out_specs=out_specs,
        out_shape=out_shape,
        scratch_shapes=[pltpu.VMEM((1, LANES), F32)],
        compiler_params=_params(("arbitrary",)),
        name="post_mix",
    )(x2d, o_sb, o_diff, proj2, proj2, w_up_sb, w_up_diff, w_out, g_memq, w_memq, kv, w_memo,
      g_ffn, w_router_pad, b_router_pad)


def _dispatch_kernel(zstart_ref, zflag_ref, nused_ref, dest_ref, hf_ref, xs_hbm, stage, zbuf, sem,
                     zsem, *, tm, bm, n_tok):
    s = pl.program_id(0)
    width = hf_ref.shape[1]

    @pl.when(s == 0)
    def _():
        trash = TOP_K * n_tok + lax.broadcasted_iota(I32, (bm, LANES), 0)
        for parity in range(2):
            zbuf[parity, :, 0:width] = jnp.zeros((bm, width), U32)
            zbuf[parity, :, width:width + LANES] = (trash + parity * bm).astype(U32)

        def fill_copy(start):
            start = pl.multiple_of(start, bm)
            return pltpu.make_async_copy(zbuf.at[(start // bm) % 2],
                                         xs_hbm.at[pl.ds(start, bm), :], zsem)

        n_blocks = xs_hbm.shape[0] // bm
        for action in ("start", "wait"):
            for e in range(N_EXPERTS):
                @pl.when(zflag_ref[e] != 0)
                def _(e=e, action=action):
                    getattr(fill_copy(zstart_ref[e]), action)()

            def trailing(blk, _, action=action):
                getattr(fill_copy(blk * bm), action)()
                return 0

            lax.fori_loop(nused_ref[0], n_blocks, trailing, 0)

    token = s * tm + lax.broadcasted_iota(I32, (tm, LANES), 0)
    for k in range(TOP_K):
        stage[k, :, 0:width] = hf_ref[...]
        stage[k, :, width:width + LANES] = (token + k * n_tok).astype(U32)

    def issue(t, _):
        for k in range(TOP_K):
            row = dest_ref[t * TOP_K + k]
            pltpu.make_async_copy(stage.at[k, pl.ds(t, 1), :], xs_hbm.at[pl.ds(row, 1), :],
                                  sem).start()
        return 0

    lax.fori_loop(0, tm, issue, 0)
    for k in range(TOP_K):
        pltpu.make_async_copy(stage.at[k], xs_hbm.at[pl.ds(0, tm), :], sem).wait()


def _dispatch(zstart, zflag, n_used, dest_flat, hf_packed, *, tm, bm, n_rows):
    n_tok, width = hf_packed.shape
    row_width = width + LANES
    grid_spec = pltpu.PrefetchScalarGridSpec(
        num_scalar_prefetch=3,
        grid=(n_tok // tm,),
        in_specs=[
            pl.BlockSpec((tm * TOP_K,), lambda i, *_: (i,), memory_space=pltpu.SMEM),
            pl.BlockSpec((tm, width), lambda i, *_: (i, 0)),
        ],
        out_specs=pl.BlockSpec(memory_space=pl.ANY),
        scratch_shapes=[pltpu.VMEM((TOP_K, tm, row_width), U32),
                        pltpu.VMEM((2, bm, row_width), U32),
                        pltpu.SemaphoreType.DMA(()), pltpu.SemaphoreType.DMA(())],
    )
    return pl.pallas_call(
        functools.partial(_dispatch_kernel, tm=tm, bm=bm, n_tok=n_tok),
        grid_spec=grid_spec,
        out_shape=jax.ShapeDtypeStruct((n_rows, row_width), U32),
        compiler_params=_params(("arbitrary",)),
        name="dispatch",
    )(zstart, zflag, n_used, dest_flat, hf_packed)


FFN_UP_CHUNKS = 8
FFN_DOWN_CHUNKS = 4


def _expert_kernel(be_ref, valid_ref, src_ref, xs_ref, wgu_ref, bgu_ref, wd_ref, bd_ref, ys_hbm,
                   wgu_bf, wd_bf, ybuf, slot_v, slot_s, sem_y, sem_s, sem_z, *, d_ff, bm, width,
                   n_slots):
    r = pl.program_id(0)
    valid = valid_ref[r] != 0
    prev_valid = jnp.logical_and(r > 0, valid_ref[jnp.maximum(r - 1, 0)] != 0)
    fresh = jnp.logical_or(r == 0, be_ref[r] != be_ref[jnp.maximum(r - 1, 0)])
    cur = r % 2
    prev = 1 - cur
    d = ybuf.shape[2]

    def rows_done(parity):
        return pltpu.make_async_copy(ybuf.at[parity], ys_hbm.at[pl.ds(0, bm), :], sem_y.at[parity])

    def slots_copy(parity):
        return pltpu.make_async_copy(slot_v.at[0:1, :], slot_s.at[pl.ds(parity, 1), :],
                                     sem_s.at[parity])

    def scatter_rows(lo, hi):
        for i in range(lo, hi):
            pltpu.make_async_copy(ybuf.at[prev, pl.ds(i, 1), :],
                                  ys_hbm.at[pl.ds(slot_s[prev, i], 1), :], sem_y.at[prev]).start()

    @pl.when(r == 0)
    def _():
        ybuf[...] = jnp.zeros_like(ybuf)
        for parity in range(2):
            pltpu.make_async_copy(ybuf.at[parity], ys_hbm.at[pl.ds(n_slots + parity * bm, bm), :],
                                  sem_z).start()
        for parity in range(2):
            pltpu.make_async_copy(ybuf.at[parity], ys_hbm.at[pl.ds(n_slots + parity * bm, bm), :],
                                  sem_z).wait()

    @pl.when(prev_valid)
    def _():
        slots_copy(prev).wait()

    @pl.when(jnp.logical_and(valid, fresh))
    def _():
        wgu_bf[...] = wgu_ref[0].astype(BF16)
        wd_bf[...] = wd_ref[0].astype(BF16)

    def ffn(interleave):
        n_chunks = FFN_UP_CHUNKS + FFN_DOWN_CHUNKS
        bounds = [bm * c // n_chunks for c in range(n_chunks + 1)]
        chunk = iter(range(n_chunks))

        def after_chunk():
            c = next(chunk)
            if interleave:
                scatter_rows(bounds[c], bounds[c + 1])

        xb = _unpack_bf16_pairs(xs_ref[:, 0:width])
        up_w = 2 * d_ff // FFN_UP_CHUNKS
        parts = []
        for c in range(FFN_UP_CHUNKS):
            cols = slice(c * up_w, (c + 1) * up_w)
            parts.append(jnp.dot(xb, wgu_bf[:, cols], preferred_element_type=F32)
                         + bgu_ref[0][:, cols])
            after_chunk()
        hgu = jnp.concatenate(parts, axis=1)
        glu = jnp.minimum(hgu[:, :d_ff], SWIGLU_LIMIT)
        lin = jnp.clip(hgu[:, d_ff:], -SWIGLU_LIMIT, SWIGLU_LIMIT)
        act = (glu * jax.nn.sigmoid(SWIGLU_ALPHA * glu) * (lin + 1.0)).astype(BF16)
        down_w = d // FFN_DOWN_CHUNKS
        parts = []
        for c in range(FFN_DOWN_CHUNKS):
            cols = slice(c * down_w, (c + 1) * down_w)
            parts.append(jnp.dot(act, wd_bf[:, cols], preferred_element_type=F32)
                         + bd_ref[0][:, cols])
            after_chunk()
        y = jnp.concatenate(parts, axis=1)

        @pl.when(r >= 2)
        def _():
            rows_done(cur).wait()

        ybuf[cur] = y
        slots = xs_ref[:, width:width + LANES].astype(I32).astype(F32).T
        slot_v[...] = slots[0:8, :].astype(I32)
        slots_copy(cur).start()

    @pl.when(jnp.logical_and(valid, prev_valid))
    def _():
        ffn(True)

    @pl.when(jnp.logical_and(valid, jnp.logical_not(prev_valid)))
    def _():
        ffn(False)

    @pl.when(jnp.logical_and(jnp.logical_not(valid), prev_valid))
    def _():
        scatter_rows(0, bm)

        @pl.when(r >= 2)
        def _():
            rows_done(cur).wait()

        rows_done(prev).wait()


def _expert_ffn(blk_expert, blk_valid, blk_src, xs, w_gu, b_gu, w_down, b_down, *, bm, n_slots):
    row_width = xs.shape[1]
    width = row_width - LANES
    n_grid = blk_expert.shape[0]
    d_ff, d = w_down.shape[1:]
    grid_spec = pltpu.PrefetchScalarGridSpec(
        num_scalar_prefetch=3,
        grid=(n_grid,),
        in_specs=[
            pl.BlockSpec((bm, row_width), lambda r, be, valid, src: (src[r], 0)),
            pl.BlockSpec((1, d, 2 * d_ff), lambda r, be, *_: (be[r], 0, 0)),
            pl.BlockSpec((1, 1, 2 * d_ff), lambda r, be, *_: (be[r], 0, 0)),
            pl.BlockSpec((1, d_ff, d), lambda r, be, *_: (be[r], 0, 0)),
            pl.BlockSpec((1, 1, d), lambda r, be, *_: (be[r], 0, 0)),
        ],
        out_specs=pl.BlockSpec(memory_space=pl.ANY),
        scratch_shapes=[pltpu.VMEM((d, 2 * d_ff), BF16), pltpu.VMEM((d_ff, d), BF16),
                        pltpu.VMEM((2, bm, d), F32), pltpu.VMEM((8, bm), I32),
                        pltpu.SMEM((2, bm), I32), pltpu.SemaphoreType.DMA((2,)),
                        pltpu.SemaphoreType.DMA((2,)), pltpu.SemaphoreType.DMA(())],
    )
    return pl.pallas_call(
        functools.partial(_expert_kernel, d_ff=d_ff, bm=bm, width=width, n_slots=n_slots),
        grid_spec=grid_spec,
        out_shape=jax.ShapeDtypeStruct((n_slots + 2 * bm, d), F32),
        compiler_params=_params(("arbitrary",)),
        name="expert_ffn",
    )(blk_expert, blk_valid, blk_src, xs, w_gu, b_gu, w_down, b_down)


def _combine_kernel(x_ref, gate_ref, g_ref, y0_ref, y1_ref, y2_ref, y3_ref, o_ref):
    gates = gate_ref[...]
    acc = x_ref[...]
    for k, y_ref in enumerate((y0_ref, y1_ref, y2_ref, y3_ref)):
        acc = acc + gates[:, k:k + 1] * y_ref[...]
    o_ref[...] = _rms(acc, g_ref[...])


def _combine(x2, gates, g_final, ys, *, tm):
    n_tok, d = x2.shape
    tiles = n_tok // tm
    slot_spec = lambda k: pl.BlockSpec((tm, d), lambda i: (k * tiles + i, 0))
    return pl.pallas_call(
        _combine_kernel,
        grid=(tiles,),
        in_specs=[
            pl.BlockSpec((tm, d), lambda i: (i, 0)),
            pl.BlockSpec((tm, LANES), lambda i: (i, 0)),
            pl.BlockSpec((1, d), lambda i: (0, 0)),
        ] + [slot_spec(k) for k in range(TOP_K)],
        out_specs=pl.BlockSpec((tm, d), lambda i: (i, 0)),
        out_shape=jax.ShapeDtypeStruct((n_tok, d), F32),
        compiler_params=_params(("arbitrary",)),
        name="combine",
    )(x2, gates, g_final, ys, ys, ys, ys)


def _routing_tables(route, cum, *, bm, n_grid):
    expert = route[:, 0:TOP_K]
    pos = route[:, TOP_K:2 * TOP_K]
    experts = jnp.arange(N_EXPERTS, dtype=I32)
    counts = cum[-1, 0, :N_EXPERTS].astype(I32)
    padded = (counts + bm - 1) // bm * bm
    pend = jnp.cumsum(padded)
    pstart = pend - padded
    dest = pos + jnp.sum(jnp.where(expert[:, :, None] == experts, pstart, 0), axis=2)

    blk = jnp.arange(n_grid, dtype=I32)
    blk_expert = jnp.minimum(jnp.sum(pend[None, :] <= (blk * bm)[:, None], axis=1),
                             N_EXPERTS - 1).astype(I32)
    n_used = pend[-1] // bm
    blk_valid = (blk < n_used).astype(I32)
    blk_src = jnp.minimum(blk, jnp.maximum(n_used - 1, 0)).astype(I32)
    zflag = (padded > 0).astype(I32)
    zstart = jnp.maximum(pend - bm, 0).astype(I32)
    return (dest.astype(I32).reshape(-1), blk_expert, blk_valid, blk_src, zstart, zflag,
            n_used.astype(I32).reshape(1))


def kernel(x, mem, positions, norm_mix, w_in, lambda_q1, lambda_k1, lambda_q2, lambda_k2,
           diff_subln, w_up_sb, w_up_diff, w_out, norm_mem_q, norm_mem_kv, w_mem_q, w_mem_kv,
           w_mem_o, norm_ffn, w_router, b_router, w_gate_up, b_gate_up, w_down, b_down,
           norm_final):
    b, s, d = x.shape
    n_tok = b * s
    depth = norm_mix.shape[0]
    sb_width = w_up_sb.shape[1]
    diff_width = w_up_diff.shape[1]
    n_in = w_in.shape[2]
    chunk = 512
    assert sb_width == chunk and diff_width == chunk and d == 2 * chunk
    sbq, sbk, sbv, dq, dk, dv = range(6)
    blocks_per_chunk = chunk // LANES
    scale = 1.0 / math.sqrt(SB_HEAD_DIM)
    assert SB_HEAD_DIM == DIFF_HEAD_DIM

    tm_in = min(512, n_tok)
    tq_sb = min(256, s)
    tq_diff = min(512, s)
    tm_post = min(256, s)
    bm = 256
    n_grid = (n_tok * TOP_K) // bm + N_EXPERTS

    cos_t, sin_t = _rope_tables(positions)
    x2d = x.reshape(n_tok, d)
    for l in range(depth):
        lambda_init = 0.8 - 0.6 * math.exp(-0.3 * l)
        proj = _in_proj(x2d, norm_mix[l].reshape(1, d), w_in[l].astype(BF16), cos_t, sin_t,
                        tm=tm_in, chunk=chunk, rope_chunks=(dq, dk), scale_chunks=(sbq, dq),
                        scale=scale)
        proj3 = proj.reshape(b, s, n_in)
        o_sb = _sb_attention(proj3, q_col=sbq * blocks_per_chunk, k_col=sbk * blocks_per_chunk,
                             v_col=sbv * blocks_per_chunk, n_pairs=sb_width // LANES, tq=tq_sb)
        o_diff = _diff_attention(
            proj3, lambda_q1[l].reshape(1, -1), lambda_k1[l].reshape(1, -1),
            lambda_q2[l].reshape(1, -1), lambda_k2[l].reshape(1, -1),
            diff_subln[l].reshape(1, -1), q_col=dq * blocks_per_chunk,
            k_col=dk * blocks_per_chunk, v_col=dv * blocks_per_chunk,
            n_heads=diff_width // DIFF_V_DIM, tq=tq_diff, lambda_init=lambda_init)
        kv = _mem_kv(mem, norm_mem_kv[l].reshape(1, d), w_mem_kv[l].astype(BF16))
        w_router_pad = jnp.zeros((d, LANES), F32).at[:, :N_EXPERTS].set(w_router[l])
        b_router_pad = jnp.full((1, LANES), NEG_BIG, F32).at[0, :N_EXPERTS].set(b_router[l])
        x_res, hf, route, gates, cum = _post_mix(
            x2d, proj, o_sb.reshape(n_tok, sb_width), o_diff.reshape(n_tok, diff_width),
            w_up_sb[l].astype(BF16), w_up_diff[l].astype(BF16), w_out[l].astype(BF16),
            norm_mem_q[l].reshape(1, d), w_mem_q[l].astype(BF16), kv, w_mem_o[l].astype(BF16),
            norm_ffn[l].reshape(1, d), w_router_pad, b_router_pad,
            tm=tm_post, seq=s, gate_col=6 * chunk // d)
        dest, blk_expert, blk_valid, blk_src, zstart, zflag, n_used = _routing_tables(
            route, cum, bm=bm, n_grid=n_grid + 1)
        xs = _dispatch(zstart, zflag, n_used, dest, hf, tm=tm_post, bm=bm, n_rows=n_grid * bm)
        ys = _expert_ffn(blk_expert, blk_valid, blk_src, xs, w_gate_up[l],
                         b_gate_up[l][:, None, :], w_down[l], b_down[l][:, None, :], bm=bm,
                         n_slots=TOP_K * n_tok)
        g_next = norm_final if l == depth - 1 else None
        assert g_next is not None, "only depth 1 is wired: the final norm is fused into combine"
        x2d = _combine(x_res, gates, g_next.reshape(1, d), ys, tm=tm_post)
    return x2d.reshape(b, s, d)
```

```python
import functools
import math

import jax
import jax.numpy as jnp
from jax import lax
from jax.experimental import pallas as pl
from jax.experimental.pallas import tpu as pltpu

F32 = jnp.float32
BF16 = jnp.bfloat16
I32 = jnp.int32
U32 = jnp.uint32

LANES = 128
VMEM_LIMIT_BYTES = 56 * 1024 * 1024

NORM_EPS = 1e-6
ROPE_THETA = 10000.0
CHUNK = 64
SB_HEAD_DIM = 64
DIFF_HEAD_DIM = 64
DIFF_V_DIM = 128
MEM_HEADS = 4
N_EXPERTS = 32
TOP_K = 4
SWIGLU_LIMIT = 7.0
SWIGLU_ALPHA = 1.702

SB_DEAD_LOG = -105.0
NEG_BIG = -1e30


def _params(semantics):
    return pltpu.CompilerParams(dimension_semantics=semantics,
                                vmem_limit_bytes=VMEM_LIMIT_BYTES)


def _const_spec(shape):
    nd = len(shape)
    return pl.BlockSpec(shape, lambda *_: (0,) * nd)


def _rms(x, g):
    return x * lax.rsqrt(jnp.mean(x * x, axis=-1, keepdims=True) + NORM_EPS) * g


def _rope_table_kernel(pos_ref, inv_ref, cos_ref, sin_ref):
    ang = pos_ref[...] * inv_ref[...]
    cos_ref[...] = jnp.cos(ang)
    sin_ref[...] = jnp.sin(ang)


def _rope_tables(positions):
    n_tok = positions.size
    half = DIFF_HEAD_DIM // 2
    per_row = LANES // half
    rows = n_tok // per_row
    pos = jnp.repeat(positions.reshape(rows, per_row).astype(F32), half, axis=1)
    inv = ROPE_THETA ** (-jnp.arange(half, dtype=F32) / half)
    inv = jnp.tile(inv, per_row).reshape(1, LANES)
    tr = min(rows, 512)
    cos, sin = pl.pallas_call(
        _rope_table_kernel,
        grid=(rows // tr,),
        in_specs=[pl.BlockSpec((tr, LANES), lambda i: (i, 0)), _const_spec((1, LANES))],
        out_specs=[pl.BlockSpec((tr, LANES), lambda i: (i, 0))] * 2,
        out_shape=[jax.ShapeDtypeStruct((rows, LANES), F32)] * 2,
        compiler_params=_params(("arbitrary",)),
        name="rope_table",
    )(pos, inv)
    cos = cos.reshape(n_tok, half)
    sin = sin.reshape(n_tok, half)
    cos_t = jnp.tile(cos, (1, LANES // half))
    sin_t = jnp.tile(jnp.concatenate([-sin, sin], axis=1), (1, LANES // (2 * half)))
    return cos_t, sin_t


def _inproj_kernel(x_ref, g_ref, w_ref, cos_ref, sin_ref, o_ref, *, chunk, rope_chunks,
                   scale_chunks, scale):
    h = _rms(x_ref[...], g_ref[...]).astype(BF16)
    lane = lax.broadcasted_iota(I32, (1, chunk), 1)
    first_half = (lane % DIFF_HEAD_DIM) < (DIFF_HEAD_DIM // 2)
    for c in range(w_ref.shape[1] // chunk):
        cols = slice(c * chunk, (c + 1) * chunk)
        acc = jnp.dot(h, w_ref[:, cols], preferred_element_type=F32)
        if c in rope_chunks:
            cos = jnp.tile(cos_ref[...], (1, chunk // LANES))
            sin = jnp.tile(sin_ref[...], (1, chunk // LANES))
            partner = jnp.where(first_half,
                                pltpu.roll(acc, chunk - DIFF_HEAD_DIM // 2, 1),
                                pltpu.roll(acc, DIFF_HEAD_DIM // 2, 1))
            acc = acc * cos + partner * sin
        if c in scale_chunks:
            acc = acc * scale
        o_ref[:, cols] = acc.astype(BF16)


def _in_proj(x2d, g, w_bf16, cos_t, sin_t, *, tm, chunk, rope_chunks, scale_chunks, scale):
    n_tok, d = x2d.shape
    n_in = w_bf16.shape[1]
    kern = functools.partial(_inproj_kernel, chunk=chunk, rope_chunks=rope_chunks,
                             scale_chunks=scale_chunks, scale=scale)
    return pl.pallas_call(
        kern,
        grid=(n_tok // tm,),
        in_specs=[pl.BlockSpec((tm, d), lambda i: (i, 0)),
                  _const_spec((1, d)),
                  _const_spec((d, n_in)),
                  pl.BlockSpec((tm, LANES), lambda i: (i, 0)),
                  pl.BlockSpec((tm, LANES), lambda i: (i, 0))],
        out_specs=pl.BlockSpec((tm, n_in), lambda i: (i, 0)),
        out_shape=jax.ShapeDtypeStruct((n_tok, n_in), BF16),
        compiler_params=_params(("arbitrary",)),
        name="in_proj",
    )(x2d, g, w_bf16, cos_t, sin_t)


def _sb_kernel(q_ref, k_ref, v_ref, o_ref, acc_ref, carry_ref, *, tq):
    i = pl.program_id(2)
    q = q_ref[0]
    lane = lax.broadcasted_iota(I32, (1, LANES), 1)
    zero = jnp.zeros_like(q)
    q2 = jnp.concatenate([jnp.where(lane < SB_HEAD_DIM, q, zero),
                          jnp.where(lane >= SB_HEAD_DIM, q, zero)], axis=0)
    uj = lax.broadcasted_iota(I32, (2 * tq, tq), 0) % tq
    us = lax.broadcasted_iota(I32, (2 * tq, tq), 1)
    suffix = jnp.where(uj > us, 1.0, 0.0).astype(BF16)

    def block(j, masked):
        start = pl.multiple_of(j * tq, tq)
        kj = k_ref[0, pl.ds(start, tq), :]
        vj = v_ref[0, pl.ds(start, tq), :]
        z = lax.dot_general(q2, kj, (((1,), (1,)), ((), ())), preferred_element_type=F32)
        sp = jnp.maximum(z, 0.0) + jnp.log(1.0 + jnp.exp(-jnp.abs(z)))
        log_rem = -sp
        if masked:
            causal = (lax.broadcasted_iota(I32, (2 * tq, tq), 1)
                      < lax.broadcasted_iota(I32, (2 * tq, tq), 0) % tq)
            log_rem = jnp.where(causal, log_rem, 0.0)
        hi = log_rem.astype(BF16)
        lo = (log_rem - hi.astype(F32)).astype(BF16)
        after = jnp.dot(jnp.concatenate([hi, lo], axis=1), suffix, preferred_element_type=F32)
        carry = carry_ref[...]
        w = jnp.exp(z - sp + after + carry)
        if masked:
            w = jnp.where(causal, w, 0.0)
        acc_ref[...] += jnp.dot(w.astype(BF16), vj, preferred_element_type=F32)
        carry = carry + after[:, 0:1] + log_rem[:, 0:1]
        carry_ref[...] = carry
        return jnp.max(carry)

    acc_ref[...] = jnp.zeros_like(acc_ref)
    carry_ref[...] = jnp.zeros_like(carry_ref)
    alive = block(i, True)

    def cond(state):
        j, alive = state
        return jnp.logical_and(j >= 0, alive > SB_DEAD_LOG)

    def body(state):
        j, _ = state
        return j - 1, block(j, False)

    lax.while_loop(cond, body, (i - 1, alive))
    o_ref[0] = jnp.where(lane < SB_HEAD_DIM, acc_ref[0:tq, :], acc_ref[tq:2 * tq, :]).astype(BF16)


def _sb_attention(proj3, *, q_col, k_col, v_col, n_pairs, tq):
    b, s, _ = proj3.shape
    return pl.pallas_call(
        functools.partial(_sb_kernel, tq=tq),
        grid=(b, n_pairs, s // tq),
        in_specs=[pl.BlockSpec((1, tq, LANES), lambda bi, p, i: (bi, i, q_col + p)),
                  pl.BlockSpec((1, s, LANES), lambda bi, p, i: (bi, 0, k_col + p)),
                  pl.BlockSpec((1, s, LANES), lambda bi, p, i: (bi, 0, v_col + p))],
        out_specs=pl.BlockSpec((1, tq, LANES), lambda bi, p, i: (bi, i, p)),
        out_shape=jax.ShapeDtypeStruct((b, s, n_pairs * LANES), BF16),
        scratch_shapes=[pltpu.VMEM((2 * tq, LANES), F32), pltpu.VMEM((2 * tq, 1), F32)],
        compiler_params=_params(("arbitrary", "arbitrary", "arbitrary")),
        name="sb_attention",
    )(proj3, proj3, proj3)


def _diff_kernel(lq1_ref, lk1_ref, lq2_ref, lk2_ref, q_ref, k_ref, v_ref, g_ref, o_ref, vt_ref,
                 acc_ref, *, tq, lambda_init):
    i = pl.program_id(2)
    s_len = v_ref.shape[1]

    @pl.when(i == 0)
    def _():
        for c in range(s_len // tq):
            cols = slice(c * tq, (c + 1) * tq)
            vt_ref[:, cols] = v_ref[0, cols, :].astype(F32).T.astype(BF16)

    q = q_ref[0]
    lane = lax.broadcasted_iota(I32, (1, LANES), 1)
    zero = jnp.zeros_like(q)
    q2 = jnp.concatenate([jnp.where(lane < DIFF_HEAD_DIM, q, zero),
                          jnp.where(lane >= DIFF_HEAD_DIM, q, zero)], axis=0)

    def step(j, stats, masked):
        m, l = stats
        start = pl.multiple_of(j * tq, tq)
        kj = k_ref[0, pl.ds(start, tq), :]
        vtj = vt_ref[:, pl.ds(start, tq)]
        zt = lax.dot_general(kj, q2, (((1,), (1,)), ((), ())), preferred_element_type=F32)
        if masked:
            visible = (lax.broadcasted_iota(I32, (tq, 2 * tq), 0) // CHUNK
                       <= (lax.broadcasted_iota(I32, (tq, 2 * tq), 1) % tq) // CHUNK)
            zt = jnp.where(visible, zt, -jnp.inf)
        m_new = jnp.maximum(m, jnp.max(zt, axis=0, keepdims=True))
        alpha = jnp.exp(m - m_new)
        pt = jnp.exp(zt - m_new)
        l = alpha * l + jnp.sum(pt, axis=0, keepdims=True)
        acc_ref[...] = alpha * acc_ref[...] + jnp.dot(vtj, pt.astype(BF16),
                                                      preferred_element_type=F32)
        return m_new, l

    acc_ref[...] = jnp.zeros_like(acc_ref)
    init = (jnp.full((1, 2 * tq), -jnp.inf, F32), jnp.zeros((1, 2 * tq), F32))
    stats = lax.fori_loop(0, i, lambda j, st: step(j, st, False), init)
    _, l = step(i, stats, True)
    lam = (jnp.exp(jnp.sum(lq1_ref[...] * lk1_ref[...], axis=1, keepdims=True))
           - jnp.exp(jnp.sum(lq2_ref[...] * lk2_ref[...], axis=1, keepdims=True))
           + lambda_init)
    ot = acc_ref[...] / l
    ot = ot[:, 0:tq] - lam * ot[:, tq:2 * tq]
    o_ref[0] = (_rms(ot.T, g_ref[...]) * (1.0 - lambda_init)).astype(BF16)


def _diff_attention(proj3, lq1, lk1, lq2, lk2, subln, *, q_col, k_col, v_col, n_heads, tq,
                    lambda_init):
    b, s, _ = proj3.shape
    lam_spec = _const_spec((1, DIFF_HEAD_DIM))
    return pl.pallas_call(
        functools.partial(_diff_kernel, tq=tq, lambda_init=lambda_init),
        grid=(b, n_heads, s // tq),
        in_specs=[lam_spec, lam_spec, lam_spec, lam_spec,
                  pl.BlockSpec((1, tq, LANES), lambda bi, h, i: (bi, i, q_col + h)),
                  pl.BlockSpec((1, s, LANES), lambda bi, h, i: (bi, 0, k_col + h)),
                  pl.BlockSpec((1, s, LANES), lambda bi, h, i: (bi, 0, v_col + h)),
                  _const_spec((1, DIFF_V_DIM))],
        out_specs=pl.BlockSpec((1, tq, LANES), lambda bi, h, i: (bi, i, h)),
        out_shape=jax.ShapeDtypeStruct((b, s, n_heads * DIFF_V_DIM), BF16),
        scratch_shapes=[pltpu.VMEM((DIFF_V_DIM, s), BF16), pltpu.VMEM((DIFF_V_DIM, 2 * tq), F32)],
        compiler_params=_params(("arbitrary", "arbitrary", "arbitrary")),
        name="diff_attention",
    )(lq1, lk1, lq2, lk2, proj3, proj3, proj3, subln)


def _mem_kv_kernel(mem_ref, g_ref, w_ref, o_ref):
    h = _rms(mem_ref[0], g_ref[...]).astype(BF16)
    o_ref[0] = jnp.dot(h, w_ref[...], preferred_element_type=F32).astype(BF16)


def _mem_kv(mem, g, w_bf16):
    b, m, d = mem.shape
    n = w_bf16.shape[1]
    return pl.pallas_call(
        _mem_kv_kernel,
        grid=(b,),
        in_specs=[pl.BlockSpec((1, m, d), lambda i: (i, 0, 0)), _const_spec((1, d)),
                  _const_spec((d, n))],
        out_specs=pl.BlockSpec((1, m, n), lambda i: (i, 0, 0)),
        out_shape=jax.ShapeDtypeStruct((b, m, n), BF16),
        compiler_params=_params(("arbitrary",)),
        name="mem_kv",
    )(mem, g, w_bf16)


def _split_bf16(v):
    hi = v.astype(BF16)
    return hi, (v - hi.astype(F32)).astype(BF16)


def _pack_bf16_pairs(v):
    half = v.shape[1] // 2
    bits = lax.bitcast_convert_type(v.astype(BF16).astype(F32), U32)
    return (bits[:, :half] >> 16) | (bits[:, half:] & jnp.uint32(0xFFFF0000))


def _unpack_bf16_pairs(w):
    lo = lax.bitcast_convert_type(w << 16, F32)
    hi = lax.bitcast_convert_type(w & jnp.uint32(0xFFFF0000), F32)
    return jnp.concatenate([lo, hi], axis=1).astype(BF16)


def _postmix_kernel(x_ref, osb_ref, odf_ref, gsb_ref, gdf_ref, wus_ref, wud_ref, wout_ref,
                    gq_ref, wq_ref, kv_ref, wo_ref, gf_ref, wr_ref, br_ref,
                    x2_ref, hf_ref, route_ref, gate_ref, cum_ref, count_ref, *, tm, d_model):
    step = pl.program_id(0)

    @pl.when(step == 0)
    def _():
        count_ref[...] = jnp.zeros_like(count_ref)

    y_sb = jnp.dot(osb_ref[...], wus_ref[...], preferred_element_type=F32)
    y_df = jnp.dot(odf_ref[...], wud_ref[...], preferred_element_type=F32)
    mixed = (jax.nn.sigmoid(gsb_ref[...].astype(F32)) * y_sb
             + jax.nn.sigmoid(gdf_ref[...].astype(F32)) * y_df)
    x1 = x_ref[...] + jnp.dot(mixed.astype(BF16), wout_ref[...], preferred_element_type=F32)

    hq = _rms(x1, gq_ref[...]).astype(BF16)
    hd = d_model // MEM_HEADS
    q = jnp.dot(hq, wq_ref[...], preferred_element_type=F32) * (1.0 / math.sqrt(hd))
    q = q.astype(BF16)
    heads = []
    for h in range(MEM_HEADS):
        kh = kv_ref[0, :, h * hd:(h + 1) * hd]
        vh = kv_ref[0, :, d_model + h * hd:d_model + (h + 1) * hd]
        z = lax.dot_general(q[:, h * hd:(h + 1) * hd], kh, (((1,), (1,)), ((), ())),
                            preferred_element_type=F32)
        p = jnp.exp(z - jnp.max(z, axis=1, keepdims=True))
        l = jnp.sum(p, axis=1, keepdims=True)
        heads.append((jnp.dot(p.astype(BF16), vh, preferred_element_type=F32) / l).astype(BF16))
    x2 = x1 + jnp.dot(jnp.concatenate(heads, axis=1), wo_ref[...], preferred_element_type=F32)
    x2_ref[...] = x2

    hf = _rms(x2, gf_ref[...])
    hf_ref[...] = _pack_bf16_pairs(hf)
    h_hi, h_lo = _split_bf16(hf)
    w_hi, w_lo = _split_bf16(wr_ref[...])
    logits = (jnp.dot(h_hi, w_hi, preferred_element_type=F32)
              + jnp.dot(h_hi, w_lo, preferred_element_type=F32)
              + jnp.dot(h_lo, w_hi, preferred_element_type=F32)) + br_ref[...]
    lane = lax.broadcasted_iota(I32, (tm, LANES), 1)
    work = logits
    vals, idxs, hots = [], [], []
    for _ in range(TOP_K):
        mx = jnp.max(work, axis=1, keepdims=True)
        idx = jnp.min(jnp.where(work == mx, lane, LANES), axis=1, keepdims=True)
        hot = lane == idx
        work = jnp.where(hot, NEG_BIG, work)
        vals.append(mx)
        idxs.append(idx)
        hots.append(hot)
    exps = [jnp.exp(v - vals[0]) for v in vals]
    denom = exps[0] + exps[1] + exps[2] + exps[3]

    onehot_sum = jnp.zeros((tm, LANES), F32)
    for hot in hots:
        onehot_sum = onehot_sum + jnp.where(hot, 1.0, 0.0)
    r = lax.broadcasted_iota(I32, (tm, tm), 0)
    c = lax.broadcasted_iota(I32, (tm, tm), 1)
    lower = jnp.where(c < r, 1.0, 0.0).astype(BF16)
    rank = jnp.dot(lower, onehot_sum.astype(BF16), preferred_element_type=F32) + count_ref[...]
    route = jnp.zeros((tm, LANES), I32)
    gates = jnp.zeros((tm, LANES), F32)
    for k in range(TOP_K):
        pos = jnp.sum(jnp.where(hots[k], rank, 0.0), axis=1, keepdims=True).astype(I32)
        route = jnp.where(lane == k, idxs[k], route)
        route = jnp.where(lane == TOP_K + k, pos, route)
        gates = jnp.where(lane == k, exps[k] / denom, gates)
    route_ref[...] = route
    gate_ref[...] = gates
    count_ref[...] = count_ref[...] + jnp.sum(onehot_sum, axis=0, keepdims=True)
    cum_ref[0] = jnp.broadcast_to(count_ref[...], (8, LANES))


def _post_mix(x2d, proj2, o_sb, o_diff, w_up_sb, w_up_diff, w_out, g_memq, w_memq, kv, w_memo,
              g_ffn, w_router_pad, b_router_pad, *, tm, seq, gate_col):
    n_tok, d = x2d.shape
    n_tiles = n_tok // tm
    tiles_per_batch = seq // tm
    row = lambda i: (i, 0)
    in_specs = [
        pl.BlockSpec((tm, d), row),
        pl.BlockSpec((tm, o_sb.shape[1]), row),
        pl.BlockSpec((tm, o_diff.shape[1]), row),
        pl.BlockSpec((tm, d), lambda i: (i, gate_col)),
        pl.BlockSpec((tm, d), lambda i: (i, gate_col + 1)),
        _const_spec(w_up_sb.shape), _const_spec(w_up_diff.shape), _const_spec(w_out.shape),
        _const_spec((1, d)), _const_spec(w_memq.shape),
        pl.BlockSpec((1,) + kv.shape[1:], lambda i: (i // tiles_per_batch, 0, 0)),
        _const_spec(w_memo.shape), _const_spec((1, d)),
        _const_spec(w_router_pad.shape), _const_spec((1, LANES)),
    ]
    out_specs = [
        pl.BlockSpec((tm, d), row),
        pl.BlockSpec((tm, d // 2), row),
        pl.BlockSpec((tm, LANES), row),
        pl.BlockSpec((tm, LANES), row),
        pl.BlockSpec((1, 8, LANES), lambda i: (i, 0, 0)),
    ]
    out_shape = [
        jax.ShapeDtypeStruct((n_tok, d), F32),
        jax.ShapeDtypeStruct((n_tok, d // 2), U32),
        jax.ShapeDtypeStruct((n_tok, LANES), I32),
        jax.ShapeDtypeStruct((n_tok, LANES), F32),
        jax.ShapeDtypeStruct((n_tiles, 8, LANES), F32),
    ]
    return pl.pallas_call(
        functools.partial(_postmix_kernel, tm=tm, d_model=d),
        grid=(n_tiles,),
        in_specs=in_specs,
        out_specs=out_specs,
        out_shape=out_shape,
        scratch_shapes=[pltpu.VMEM((1, LANES), F32)],
        compiler_params=_params(("arbitrary",)),
        name="post_mix",
    )(x2d, o_sb, o_diff, proj2, proj2, w_up_sb, w_up_diff, w_out, g_memq, w_memq, kv, w_memo,
      g_ffn, w_router_pad, b_router_pad)


def _dispatch_kernel(zstart_ref, zflag_ref, nused_ref, dest_ref, hf_ref, xs_hbm, stage, zbuf, sem,
                     zsem, *, tm, bm, n_tok):
    s = pl.program_id(0)
    width = hf_ref.shape[1]

    @pl.when(s == 0)
    def _():
        trash = TOP_K * n_tok + lax.broadcasted_iota(I32, (bm, LANES), 0)
        for parity in range(2):
            zbuf[parity, :, 0:width] = jnp.zeros((bm, width), U32)
            zbuf[parity, :, width:width + LANES] = (trash + parity * bm).astype(U32)

        def fill_copy(start):
            start = pl.multiple_of(start, bm)
            return pltpu.make_async_copy(zbuf.at[(start // bm) % 2],
                                         xs_hbm.at[pl.ds(start, bm), :], zsem)

        n_blocks = xs_hbm.shape[0] // bm
        for action in ("start", "wait"):
            for e in range(N_EXPERTS):
                @pl.when(zflag_ref[e] != 0)
                def _(e=e, action=action):
                    getattr(fill_copy(zstart_ref[e]), action)()

            def trailing(blk, _, action=action):
                getattr(fill_copy(blk * bm), action)()
                return 0

            lax.fori_loop(nused_ref[0], n_blocks, trailing, 0)

    cur = s % 2
    token = s * tm + lax.broadcasted_iota(I32, (tm, LANES), 0)
    for k in range(TOP_K):
        stage[cur, k, :, 0:width] = hf_ref[...]
        stage[cur, k, :, width:width + LANES] = (token + k * n_tok).astype(U32)

    def issue(t, _):
        for k in range(TOP_K):
            row = dest_ref[t * TOP_K + k]
            pltpu.make_async_copy(stage.at[cur, k, pl.ds(t, 1), :], xs_hbm.at[pl.ds(row, 1), :],
                                  sem.at[cur]).start(priority=k % 2)
        return 0

    lax.fori_loop(0, tm, issue, 0)

    def tile_done(parity):
        for k in range(TOP_K):
            pltpu.make_async_copy(stage.at[parity, k], xs_hbm.at[pl.ds(0, tm), :],
                                  sem.at[parity]).wait()

    @pl.when(s > 0)
    def _():
        tile_done(1 - cur)

    @pl.when(s == pl.num_programs(0) - 1)
    def _():
        tile_done(cur)


def _dispatch(zstart, zflag, n_used, dest_flat, hf_packed, *, tm, bm, n_rows):
    n_tok, width = hf_packed.shape
    row_width = width + LANES
    grid_spec = pltpu.PrefetchScalarGridSpec(
        num_scalar_prefetch=3,
        grid=(n_tok // tm,),
        in_specs=[
            pl.BlockSpec((tm * TOP_K,), lambda i, *_: (i,), memory_space=pltpu.SMEM),
            pl.BlockSpec((tm, width), lambda i, *_: (i, 0)),
        ],
        out_specs=pl.BlockSpec(memory_space=pl.ANY),
        scratch_shapes=[pltpu.VMEM((2, TOP_K, tm, row_width), U32),
                        pltpu.VMEM((2, bm, row_width), U32),
                        pltpu.SemaphoreType.DMA((2,)), pltpu.SemaphoreType.DMA(())],
    )
    return pl.pallas_call(
        functools.partial(_dispatch_kernel, tm=tm, bm=bm, n_tok=n_tok),
        grid_spec=grid_spec,
        out_shape=jax.ShapeDtypeStruct((n_rows, row_width), U32),
        compiler_params=_params(("arbitrary",)),
        name="dispatch",
    )(zstart, zflag, n_used, dest_flat, hf_packed)


FFN_UP_CHUNKS = 8
FFN_DOWN_CHUNKS = 4


def _expert_kernel(be_ref, valid_ref, src_ref, xs_ref, wgu_ref, bgu_ref, wd_ref, bd_ref, ys_hbm,
                   wgu_bf, wd_bf, ybuf, slot_v, slot_s, sem_y, sem_s, sem_z, *, d_ff, bm, width,
                   n_slots):
    r = pl.program_id(0)
    valid = valid_ref[r] != 0
    prev_valid = jnp.logical_and(r > 0, valid_ref[jnp.maximum(r - 1, 0)] != 0)
    fresh = jnp.logical_or(r == 0, be_ref[r] != be_ref[jnp.maximum(r - 1, 0)])
    cur = r % 2
    prev = 1 - cur
    d = ybuf.shape[2]

    def rows_done(parity):
        return pltpu.make_async_copy(ybuf.at[parity], ys_hbm.at[pl.ds(0, bm), :], sem_y.at[parity])

    def slots_copy(parity):
        return pltpu.make_async_copy(slot_v.at[0:1, :], slot_s.at[pl.ds(parity, 1), :],
                                     sem_s.at[parity])

    def scatter_rows(lo, hi):
        for i in range(lo, hi):
            pltpu.make_async_copy(ybuf.at[prev, pl.ds(i, 1), :],
                                  ys_hbm.at[pl.ds(slot_s[prev, i], 1), :],
                                  sem_y.at[prev]).start(priority=i % 2)

    @pl.when(r == 0)
    def _():
        ybuf[...] = jnp.zeros_like(ybuf)
        for parity in range(2):
            pltpu.make_async_copy(ybuf.at[parity], ys_hbm.at[pl.ds(n_slots + parity * bm, bm), :],
                                  sem_z).start()
        for parity in range(2):
            pltpu.make_async_copy(ybuf.at[parity], ys_hbm.at[pl.ds(n_slots + parity * bm, bm), :],
                                  sem_z).wait()

    @pl.when(prev_valid)
    def _():
        slots_copy(prev).wait()

    @pl.when(jnp.logical_and(valid, fresh))
    def _():
        wgu_bf[...] = wgu_ref[0].astype(BF16)
        wd_bf[...] = wd_ref[0].astype(BF16)

    def ffn(interleave):
        n_chunks = FFN_UP_CHUNKS + FFN_DOWN_CHUNKS
        bounds = [bm * c // n_chunks for c in range(n_chunks + 1)]
        chunk = iter(range(n_chunks))

        def after_chunk():
            c = next(chunk)
            if interleave:
                scatter_rows(bounds[c], bounds[c + 1])

        xb = _unpack_bf16_pairs(xs_ref[:, 0:width])
        up_w = 2 * d_ff // FFN_UP_CHUNKS
        parts = []
        for c in range(FFN_UP_CHUNKS):
            cols = slice(c * up_w, (c + 1) * up_w)
            parts.append(jnp.dot(xb, wgu_bf[:, cols], preferred_element_type=F32)
                         + bgu_ref[0][:, cols])
            after_chunk()
        hgu = jnp.concatenate(parts, axis=1)
        glu = jnp.minimum(hgu[:, :d_ff], SWIGLU_LIMIT)
        lin = jnp.clip(hgu[:, d_ff:], -SWIGLU_LIMIT, SWIGLU_LIMIT)
        act = (glu * jax.nn.sigmoid(SWIGLU_ALPHA * glu) * (lin + 1.0)).astype(BF16)
        down_w = d // FFN_DOWN_CHUNKS
        parts = []
        for c in range(FFN_DOWN_CHUNKS):
            cols = slice(c * down_w, (c + 1) * down_w)
            parts.append(jnp.dot(act, wd_bf[:, cols], preferred_element_type=F32)
                         + bd_ref[0][:, cols])
            after_chunk()
        y = jnp.concatenate(parts, axis=1)

        @pl.when(r >= 2)
        def _():
            rows_done(cur).wait()

        ybuf[cur] = y
        slots = xs_ref[:, width:width + LANES].astype(I32).astype(F32).T
        slot_v[...] = slots[0:8, :].astype(I32)
        slots_copy(cur).start()

    @pl.when(jnp.logical_and(valid, prev_valid))
    def _():
        ffn(True)

    @pl.when(jnp.logical_and(valid, jnp.logical_not(prev_valid)))
    def _():
        ffn(False)

    @pl.when(jnp.logical_and(jnp.logical_not(valid), prev_valid))
    def _():
        scatter_rows(0, bm)

        @pl.when(r >= 2)
        def _():
            rows_done(cur).wait()

        rows_done(prev).wait()


def _expert_ffn(blk_expert, blk_valid, blk_src, xs, w_gu, b_gu, w_down, b_down, *, bm, n_slots):
    row_width = xs.shape[1]
    width = row_width - LANES
    n_grid = blk_expert.shape[0]
    d_ff, d = w_down.shape[1:]
    grid_spec = pltpu.PrefetchScalarGridSpec(
        num_scalar_prefetch=3,
        grid=(n_grid,),
        in_specs=[
            pl.BlockSpec((bm, row_width), lambda r, be, valid, src: (src[r], 0)),
            pl.BlockSpec((1, d, 2 * d_ff), lambda r, be, *_: (be[r], 0, 0)),
            pl.BlockSpec((1, 1, 2 * d_ff), lambda r, be, *_: (be[r], 0, 0)),
            pl.BlockSpec((1, d_ff, d), lambda r, be, *_: (be[r], 0, 0)),
            pl.BlockSpec((1, 1, d), lambda r, be, *_: (be[r], 0, 0)),
        ],
        out_specs=pl.BlockSpec(memory_space=pl.ANY),
        scratch_shapes=[pltpu.VMEM((d, 2 * d_ff), BF16), pltpu.VMEM((d_ff, d), BF16),
                        pltpu.VMEM((2, bm, d), F32), pltpu.VMEM((8, bm), I32),
                        pltpu.SMEM((2, bm), I32), pltpu.SemaphoreType.DMA((2,)),
                        pltpu.SemaphoreType.DMA((2,)), pltpu.SemaphoreType.DMA(())],
    )
    return pl.pallas_call(
        functools.partial(_expert_kernel, d_ff=d_ff, bm=bm, width=width, n_slots=n_slots),
        grid_spec=grid_spec,
        out_shape=jax.ShapeDtypeStruct((n_slots + 2 * bm, d), F32),
        compiler_params=_params(("arbitrary",)),
        name="expert_ffn",
    )(blk_expert, blk_valid, blk_src, xs, w_gu, b_gu, w_down, b_down)


def _combine_kernel(x_ref, gate_ref, g_ref, y0_ref, y1_ref, y2_ref, y3_ref, o_ref):
    gates = gate_ref[...]
    acc = x_ref[...]
    for k, y_ref in enumerate((y0_ref, y1_ref, y2_ref, y3_ref)):
        acc = acc + gates[:, k:k + 1] * y_ref[...]
    o_ref[...] = _rms(acc, g_ref[...])


def _combine(x2, gates, g_final, ys, *, tm):
    n_tok, d = x2.shape
    tiles = n_tok // tm
    slot_spec = lambda k: pl.BlockSpec((tm, d), lambda i: (k * tiles + i, 0))
    return pl.pallas_call(
        _combine_kernel,
        grid=(tiles,),
        in_specs=[
            pl.BlockSpec((tm, d), lambda i: (i, 0)),
            pl.BlockSpec((tm, LANES), lambda i: (i, 0)),
            pl.BlockSpec((1, d), lambda i: (0, 0)),
        ] + [slot_spec(k) for k in range(TOP_K)],
        out_specs=pl.BlockSpec((tm, d), lambda i: (i, 0)),
        out_shape=jax.ShapeDtypeStruct((n_tok, d), F32),
        compiler_params=_params(("arbitrary",)),
        name="combine",
    )(x2, gates, g_final, ys, ys, ys, ys)


def _routing_tables(route, cum, *, bm, n_grid):
    expert = route[:, 0:TOP_K]
    pos = route[:, TOP_K:2 * TOP_K]
    experts = jnp.arange(N_EXPERTS, dtype=I32)
    counts = cum[-1, 0, :N_EXPERTS].astype(I32)
    padded = (counts + bm - 1) // bm * bm
    pend = jnp.cumsum(padded)
    pstart = pend - padded
    dest = pos + jnp.sum(jnp.where(expert[:, :, None] == experts, pstart, 0), axis=2)

    blk = jnp.arange(n_grid, dtype=I32)
    blk_expert = jnp.minimum(jnp.sum(pend[None, :] <= (blk * bm)[:, None], axis=1),
                             N_EXPERTS - 1).astype(I32)
    n_used = pend[-1] // bm
    blk_valid = (blk < n_used).astype(I32)
    blk_src = jnp.minimum(blk, jnp.maximum(n_used - 1, 0)).astype(I32)
    zflag = (padded > 0).astype(I32)
    zstart = jnp.maximum(pend - bm, 0).astype(I32)
    return (dest.astype(I32).reshape(-1), blk_expert, blk_valid, blk_src, zstart, zflag,
            n_used.astype(I32).reshape(1))


def kernel(x, mem, positions, norm_mix, w_in, lambda_q1, lambda_k1, lambda_q2, lambda_k2,
           diff_subln, w_up_sb, w_up_diff, w_out, norm_mem_q, norm_mem_kv, w_mem_q, w_mem_kv,
           w_mem_o, norm_ffn, w_router, b_router, w_gate_up, b_gate_up, w_down, b_down,
           norm_final):
    b, s, d = x.shape
    n_tok = b * s
    depth = norm_mix.shape[0]
    sb_width = w_up_sb.shape[1]
    diff_width = w_up_diff.shape[1]
    n_in = w_in.shape[2]
    chunk = 512
    assert sb_width == chunk and diff_width == chunk and d == 2 * chunk
    sbq, sbk, sbv, dq, dk, dv = range(6)
    blocks_per_chunk = chunk // LANES
    scale = 1.0 / math.sqrt(SB_HEAD_DIM)
    assert SB_HEAD_DIM == DIFF_HEAD_DIM

    tm_in = min(512, n_tok)
    tq_sb = min(256, s)
    tq_diff = min(512, s)
    tm_post = min(256, s)
    bm = 256
    n_grid = (n_tok * TOP_K) // bm + N_EXPERTS

    cos_t, sin_t = _rope_tables(positions)
    x2d = x.reshape(n_tok, d)
    for l in range(depth):
        lambda_init = 0.8 - 0.6 * math.exp(-0.3 * l)
        proj = _in_proj(x2d, norm_mix[l].reshape(1, d), w_in[l].astype(BF16), cos_t, sin_t,
                        tm=tm_in, chunk=chunk, rope_chunks=(dq, dk), scale_chunks=(sbq, dq),
                        scale=scale)
        proj3 = proj.reshape(b, s, n_in)
        o_sb = _sb_attention(proj3, q_col=sbq * blocks_per_chunk, k_col=sbk * blocks_per_chunk,
                             v_col=sbv * blocks_per_chunk, n_pairs=sb_width // LANES, tq=tq_sb)
        o_diff = _diff_attention(
            proj3, lambda_q1[l].reshape(1, -1), lambda_k1[l].reshape(1, -1),
            lambda_q2[l].reshape(1, -1), lambda_k2[l].reshape(1, -1),
            diff_subln[l].reshape(1, -1), q_col=dq * blocks_per_chunk,
            k_col=dk * blocks_per_chunk, v_col=dv * blocks_per_chunk,
            n_heads=diff_width // DIFF_V_DIM, tq=tq_diff, lambda_init=lambda_init)
        kv = _mem_kv(mem, norm_mem_kv[l].reshape(1, d), w_mem_kv[l].astype(BF16))
        w_router_pad = jnp.zeros((d, LANES), F32).at[:, :N_EXPERTS].set(w_router[l])
        b_router_pad = jnp.full((1, LANES), NEG_BIG, F32).at[0, :N_EXPERTS].set(b_router[l])
        x_res, hf, route, gates, cum = _post_mix(
            x2d, proj, o_sb.reshape(n_tok, sb_width), o_diff.reshape(n_tok, diff_width),
            w_up_sb[l].astype(BF16), w_up_diff[l].astype(BF16), w_out[l].astype(BF16),
            norm_mem_q[l].reshape(1, d), w_mem_q[l].astype(BF16), kv, w_mem_o[l].astype(BF16),
            norm_ffn[l].reshape(1, d), w_router_pad, b_router_pad,
            tm=tm_post, seq=s, gate_col=6 * chunk // d)
        dest, blk_expert, blk_valid, blk_src, zstart, zflag, n_used = _routing_tables(
            route, cum, bm=bm, n_grid=n_grid + 1)
        xs = _dispatch(zstart, zflag, n_used, dest, hf, tm=tm_post, bm=bm, n_rows=n_grid * bm)
        ys = _expert_ffn(blk_expert, blk_valid, blk_src, xs, w_gate_up[l],
                         b_gate_up[l][:, None, :], w_down[l], b_down[l][:, None, :], bm=bm,
                         n_slots=TOP_K * n_tok)
        g_next = norm_final if l == depth - 1 else None
        assert g_next is not None, "only depth 1 is wired: the final norm is fused into combine"
        x2d = _combine(x_res, gates, g_next.reshape(1, d), ys, tm=tm_post)
    return x2d.reshape(b, s, d)
```

```python
import functools
import math

import jax
import jax.numpy as jnp
from jax import lax
from jax.experimental import pallas as pl
from jax.experimental.pallas import tpu as pltpu

F32 = jnp.float32
BF16 = jnp.bfloat16
I32 = jnp.int32
U32 = jnp.uint32

LANES = 128
VMEM_LIMIT_BYTES = 56 * 1024 * 1024
ROW_TILE = 8
PACKED_SUBROWS = 4
META_SUBROW = 4

NORM_EPS = 1e-6
ROPE_THETA = 10000.0
CHUNK = 64
SB_HEAD_DIM = 64
DIFF_HEAD_DIM = 64
DIFF_V_DIM = 128
MEM_HEADS = 4
N_EXPERTS = 32
TOP_K = 4
SWIGLU_LIMIT = 7.0
SWIGLU_ALPHA = 1.702

SB_DEAD_LOG = -105.0
NEG_BIG = -1e30


def _params(semantics):
    return pltpu.CompilerParams(dimension_semantics=semantics,
                                vmem_limit_bytes=VMEM_LIMIT_BYTES)


def _const_spec(shape):
    nd = len(shape)
    return pl.BlockSpec(shape, lambda *_: (0,) * nd)


def _rms(x, g):
    return x * lax.rsqrt(jnp.mean(x * x, axis=-1, keepdims=True) + NORM_EPS) * g


def _rope_table_kernel(pos_ref, inv_ref, cos_ref, sin_ref):
    ang = pos_ref[...] * inv_ref[...]
    cos_ref[...] = jnp.cos(ang)
    sin_ref[...] = jnp.sin(ang)


def _rope_tables(positions):
    n_tok = positions.size
    half = DIFF_HEAD_DIM // 2
    per_row = LANES // half
    rows = n_tok // per_row
    pos = jnp.repeat(positions.reshape(rows, per_row).astype(F32), half, axis=1)
    inv = ROPE_THETA ** (-jnp.arange(half, dtype=F32) / half)
    inv = jnp.tile(inv, per_row).reshape(1, LANES)
    tr = min(rows, 512)
    cos, sin = pl.pallas_call(
        _rope_table_kernel,
        grid=(rows // tr,),
        in_specs=[pl.BlockSpec((tr, LANES), lambda i: (i, 0)), _const_spec((1, LANES))],
        out_specs=[pl.BlockSpec((tr, LANES), lambda i: (i, 0))] * 2,
        out_shape=[jax.ShapeDtypeStruct((rows, LANES), F32)] * 2,
        compiler_params=_params(("arbitrary",)),
        name="rope_table",
    )(pos, inv)
    cos = cos.reshape(n_tok, half)
    sin = sin.reshape(n_tok, half)
    cos_t = jnp.tile(cos, (1, LANES // half))
    sin_t = jnp.tile(jnp.concatenate([-sin, sin], axis=1), (1, LANES // (2 * half)))
    return cos_t, sin_t


def _inproj_kernel(x_ref, g_ref, w_ref, cos_ref, sin_ref, o_ref, *, chunk, rope_chunks,
                   scale_chunks, scale):
    h = _rms(x_ref[...], g_ref[...]).astype(BF16)
    lane = lax.broadcasted_iota(I32, (1, chunk), 1)
    first_half = (lane % DIFF_HEAD_DIM) < (DIFF_HEAD_DIM // 2)
    for c in range(w_ref.shape[1] // chunk):
        cols = slice(c * chunk, (c + 1) * chunk)
        acc = jnp.dot(h, w_ref[:, cols], preferred_element_type=F32)
        if c in rope_chunks:
            cos = jnp.tile(cos_ref[...], (1, chunk // LANES))
            sin = jnp.tile(sin_ref[...], (1, chunk // LANES))
            partner = jnp.where(first_half,
                                pltpu.roll(acc, chunk - DIFF_HEAD_DIM // 2, 1),
                                pltpu.roll(acc, DIFF_HEAD_DIM // 2, 1))
            acc = acc * cos + partner * sin
        if c in scale_chunks:
            acc = acc * scale
        o_ref[:, cols] = acc.astype(BF16)


def _in_proj(x2d, g, w_bf16, cos_t, sin_t, *, tm, chunk, rope_chunks, scale_chunks, scale):
    n_tok, d = x2d.shape
    n_in = w_bf16.shape[1]
    kern = functools.partial(_inproj_kernel, chunk=chunk, rope_chunks=rope_chunks,
                             scale_chunks=scale_chunks, scale=scale)
    return pl.pallas_call(
        kern,
        grid=(n_tok // tm,),
        in_specs=[pl.BlockSpec((tm, d), lambda i: (i, 0)),
                  _const_spec((1, d)),
                  _const_spec((d, n_in)),
                  pl.BlockSpec((tm, LANES), lambda i: (i, 0)),
                  pl.BlockSpec((tm, LANES), lambda i: (i, 0))],
        out_specs=pl.BlockSpec((tm, n_in), lambda i: (i, 0)),
        out_shape=jax.ShapeDtypeStruct((n_tok, n_in), BF16),
        compiler_params=_params(("arbitrary",)),
        name="in_proj",
    )(x2d, g, w_bf16, cos_t, sin_t)


def _sb_kernel(q_ref, k_ref, v_ref, o_ref, acc_ref, carry_ref, *, tq):
    i = pl.program_id(2)
    q = q_ref[0]
    lane = lax.broadcasted_iota(I32, (1, LANES), 1)
    zero = jnp.zeros_like(q)
    q2 = jnp.concatenate([jnp.where(lane < SB_HEAD_DIM, q, zero),
                          jnp.where(lane >= SB_HEAD_DIM, q, zero)], axis=0)
    uj = lax.broadcasted_iota(I32, (2 * tq, tq), 0) % tq
    us = lax.broadcasted_iota(I32, (2 * tq, tq), 1)
    suffix = jnp.where(uj > us, 1.0, 0.0).astype(BF16)

    def block(j, masked):
        start = pl.multiple_of(j * tq, tq)
        kj = k_ref[0, pl.ds(start, tq), :]
        vj = v_ref[0, pl.ds(start, tq), :]
        z = lax.dot_general(q2, kj, (((1,), (1,)), ((), ())), preferred_element_type=F32)
        sp = jnp.maximum(z, 0.0) + jnp.log(1.0 + jnp.exp(-jnp.abs(z)))
        log_rem = -sp
        if masked:
            causal = (lax.broadcasted_iota(I32, (2 * tq, tq), 1)
                      < lax.broadcasted_iota(I32, (2 * tq, tq), 0) % tq)
            log_rem = jnp.where(causal, log_rem, 0.0)
        hi = log_rem.astype(BF16)
        lo = (log_rem - hi.astype(F32)).astype(BF16)
        after = jnp.dot(jnp.concatenate([hi, lo], axis=1), suffix, preferred_element_type=F32)
        carry = carry_ref[...]
        w = jnp.exp(z - sp + after + carry)
        if masked:
            w = jnp.where(causal, w, 0.0)
        acc_ref[...] += jnp.dot(w.astype(BF16), vj, preferred_element_type=F32)
        carry = carry + after[:, 0:1] + log_rem[:, 0:1]
        carry_ref[...] = carry
        return jnp.max(carry)

    acc_ref[...] = jnp.zeros_like(acc_ref)
    carry_ref[...] = jnp.zeros_like(carry_ref)
    alive = block(i, True)

    def cond(state):
        j, alive = state
        return jnp.logical_and(j >= 0, alive > SB_DEAD_LOG)

    def body(state):
        j, _ = state
        return j - 1, block(j, False)

    lax.while_loop(cond, body, (i - 1, alive))
    o_ref[0] = jnp.where(lane < SB_HEAD_DIM, acc_ref[0:tq, :], acc_ref[tq:2 * tq, :]).astype(BF16)


def _sb_attention(proj3, *, q_col, k_col, v_col, n_pairs, tq):
    b, s, _ = proj3.shape
    return pl.pallas_call(
        functools.partial(_sb_kernel, tq=tq),
        grid=(b, n_pairs, s // tq),
        in_specs=[pl.BlockSpec((1, tq, LANES), lambda bi, p, i: (bi, i, q_col + p)),
                  pl.BlockSpec((1, s, LANES), lambda bi, p, i: (bi, 0, k_col + p)),
                  pl.BlockSpec((1, s, LANES), lambda bi, p, i: (bi, 0, v_col + p))],
        out_specs=pl.BlockSpec((1, tq, LANES), lambda bi, p, i: (bi, i, p)),
        out_shape=jax.ShapeDtypeStruct((b, s, n_pairs * LANES), BF16),
        scratch_shapes=[pltpu.VMEM((2 * tq, LANES), F32), pltpu.VMEM((2 * tq, 1), F32)],
        compiler_params=_params(("arbitrary", "arbitrary", "arbitrary")),
        name="sb_attention",
    )(proj3, proj3, proj3)


def _diff_kernel(lq1_ref, lk1_ref, lq2_ref, lk2_ref, q_ref, k_ref, v_ref, g_ref, o_ref, vt_ref,
                 acc_ref, *, tq, lambda_init):
    i = pl.program_id(2)
    s_len = v_ref.shape[1]

    @pl.when(i == 0)
    def _():
        for c in range(s_len // tq):
            cols = slice(c * tq, (c + 1) * tq)
            vt_ref[:, cols] = v_ref[0, cols, :].astype(F32).T.astype(BF16)

    q = q_ref[0]
    lane = lax.broadcasted_iota(I32, (1, LANES), 1)
    zero = jnp.zeros_like(q)
    q2 = jnp.concatenate([jnp.where(lane < DIFF_HEAD_DIM, q, zero),
                          jnp.where(lane >= DIFF_HEAD_DIM, q, zero)], axis=0)

    def step(j, stats, masked):
        m, l = stats
        start = pl.multiple_of(j * tq, tq)
        kj = k_ref[0, pl.ds(start, tq), :]
        vtj = vt_ref[:, pl.ds(start, tq)]
        zt = lax.dot_general(kj, q2, (((1,), (1,)), ((), ())), preferred_element_type=F32)
        if masked:
            visible = (lax.broadcasted_iota(I32, (tq, 2 * tq), 0) // CHUNK
                       <= (lax.broadcasted_iota(I32, (tq, 2 * tq), 1) % tq) // CHUNK)
            zt = jnp.where(visible, zt, -jnp.inf)
        m_new = jnp.maximum(m, jnp.max(zt, axis=0, keepdims=True))
        alpha = jnp.exp(m - m_new)
        pt = jnp.exp(zt - m_new)
        l = alpha * l + jnp.sum(pt, axis=0, keepdims=True)
        acc_ref[...] = alpha * acc_ref[...] + jnp.dot(vtj, pt.astype(BF16),
                                                      preferred_element_type=F32)
        return m_new, l

    acc_ref[...] = jnp.zeros_like(acc_ref)
    init = (jnp.full((1, 2 * tq), -jnp.inf, F32), jnp.zeros((1, 2 * tq), F32))
    stats = lax.fori_loop(0, i, lambda j, st: step(j, st, False), init)
    _, l = step(i, stats, True)
    lam = (jnp.exp(jnp.sum(lq1_ref[...] * lk1_ref[...], axis=1, keepdims=True))
           - jnp.exp(jnp.sum(lq2_ref[...] * lk2_ref[...], axis=1, keepdims=True))
           + lambda_init)
    ot = acc_ref[...] / l
    ot = ot[:, 0:tq] - lam * ot[:, tq:2 * tq]
    o_ref[0] = (_rms(ot.T, g_ref[...]) * (1.0 - lambda_init)).astype(BF16)


def _diff_attention(proj3, lq1, lk1, lq2, lk2, subln, *, q_col, k_col, v_col, n_heads, tq,
                    lambda_init):
    b, s, _ = proj3.shape
    lam_spec = _const_spec((1, DIFF_HEAD_DIM))
    return pl.pallas_call(
        functools.partial(_diff_kernel, tq=tq, lambda_init=lambda_init),
        grid=(b, n_heads, s // tq),
        in_specs=[lam_spec, lam_spec, lam_spec, lam_spec,
                  pl.BlockSpec((1, tq, LANES), lambda bi, h, i: (bi, i, q_col + h)),
                  pl.BlockSpec((1, s, LANES), lambda bi, h, i: (bi, 0, k_col + h)),
                  pl.BlockSpec((1, s, LANES), lambda bi, h, i: (bi, 0, v_col + h)),
                  _const_spec((1, DIFF_V_DIM))],
        out_specs=pl.BlockSpec((1, tq, LANES), lambda bi, h, i: (bi, i, h)),
        out_shape=jax.ShapeDtypeStruct((b, s, n_heads * DIFF_V_DIM), BF16),
        scratch_shapes=[pltpu.VMEM((DIFF_V_DIM, s), BF16), pltpu.VMEM((DIFF_V_DIM, 2 * tq), F32)],
        compiler_params=_params(("arbitrary", "arbitrary", "arbitrary")),
        name="diff_attention",
    )(lq1, lk1, lq2, lk2, proj3, proj3, proj3, subln)


def _mem_kv_kernel(mem_ref, g_ref, w_ref, o_ref):
    h = _rms(mem_ref[0], g_ref[...]).astype(BF16)
    o_ref[0] = jnp.dot(h, w_ref[...], preferred_element_type=F32).astype(BF16)


def _mem_kv(mem, g, w_bf16):
    b, m, d = mem.shape
    n = w_bf16.shape[1]
    return pl.pallas_call(
        _mem_kv_kernel,
        grid=(b,),
        in_specs=[pl.BlockSpec((1, m, d), lambda i: (i, 0, 0)), _const_spec((1, d)),
                  _const_spec((d, n))],
        out_specs=pl.BlockSpec((1, m, n), lambda i: (i, 0, 0)),
        out_shape=jax.ShapeDtypeStruct((b, m, n), BF16),
        compiler_params=_params(("arbitrary",)),
        name="mem_kv",
    )(mem, g, w_bf16)


def _split_bf16(v):
    hi = v.astype(BF16)
    return hi, (v - hi.astype(F32)).astype(BF16)


def _pack_bf16_pairs(v):
    half = v.shape[1] // 2
    bits = lax.bitcast_convert_type(v.astype(BF16).astype(F32), U32)
    return (bits[:, :half] >> 16) | (bits[:, half:] & jnp.uint32(0xFFFF0000))


def _unpack_bf16_pairs(w):
    lo = lax.bitcast_convert_type(w << 16, F32)
    hi = lax.bitcast_convert_type(w & jnp.uint32(0xFFFF0000), F32)
    return jnp.concatenate([lo, hi], axis=1).astype(BF16)


def _postmix_kernel(x_ref, osb_ref, odf_ref, gsb_ref, gdf_ref, wus_ref, wud_ref, wout_ref,
                    gq_ref, wq_ref, kv_ref, wo_ref, gf_ref, wr_ref, br_ref,
                    x2_ref, hf_ref, route_ref, gate_ref, cum_ref, count_ref, *, tm, d_model):
    step = pl.program_id(0)

    @pl.when(step == 0)
    def _():
        count_ref[...] = jnp.zeros_like(count_ref)

    y_sb = jnp.dot(osb_ref[...], wus_ref[...], preferred_element_type=F32)
    y_df = jnp.dot(odf_ref[...], wud_ref[...], preferred_element_type=F32)
    mixed = (jax.nn.sigmoid(gsb_ref[...].astype(F32)) * y_sb
             + jax.nn.sigmoid(gdf_ref[...].astype(F32)) * y_df)
    x1 = x_ref[...] + jnp.dot(mixed.astype(BF16), wout_ref[...], preferred_element_type=F32)

    hq = _rms(x1, gq_ref[...]).astype(BF16)
    hd = d_model // MEM_HEADS
    q = jnp.dot(hq, wq_ref[...], preferred_element_type=F32) * (1.0 / math.sqrt(hd))
    q = q.astype(BF16)
    heads = []
    for h in range(MEM_HEADS):
        kh = kv_ref[0, :, h * hd:(h + 1) * hd]
        vh = kv_ref[0, :, d_model + h * hd:d_model + (h + 1) * hd]
        z = lax.dot_general(q[:, h * hd:(h + 1) * hd], kh, (((1,), (1,)), ((), ())),
                            preferred_element_type=F32)
        p = jnp.exp(z - jnp.max(z, axis=1, keepdims=True))
        l = jnp.sum(p, axis=1, keepdims=True)
        heads.append((jnp.dot(p.astype(BF16), vh, preferred_element_type=F32) / l).astype(BF16))
    x2 = x1 + jnp.dot(jnp.concatenate(heads, axis=1), wo_ref[...], preferred_element_type=F32)
    x2_ref[...] = x2

    hf = _rms(x2, gf_ref[...])
    packed = _pack_bf16_pairs(hf)
    for j in range(PACKED_SUBROWS):
        hf_ref[pl.ds(j, tm, stride=ROW_TILE), :] = packed[:, j * LANES:(j + 1) * LANES]
    h_hi, h_lo = _split_bf16(hf)
    w_hi, w_lo = _split_bf16(wr_ref[...])
    logits = (jnp.dot(h_hi, w_hi, preferred_element_type=F32)
              + jnp.dot(h_hi, w_lo, preferred_element_type=F32)
              + jnp.dot(h_lo, w_hi, preferred_element_type=F32)) + br_ref[...]
    lane = lax.broadcasted_iota(I32, (tm, LANES), 1)
    work = logits
    vals, idxs, hots = [], [], []
    for _ in range(TOP_K):
        mx = jnp.max(work, axis=1, keepdims=True)
        idx = jnp.min(jnp.where(work == mx, lane, LANES), axis=1, keepdims=True)
        hot = lane == idx
        work = jnp.where(hot, NEG_BIG, work)
        vals.append(mx)
        idxs.append(idx)
        hots.append(hot)
    exps = [jnp.exp(v - vals[0]) for v in vals]
    denom = exps[0] + exps[1] + exps[2] + exps[3]

    onehot_sum = jnp.zeros((tm, LANES), F32)
    for hot in hots:
        onehot_sum = onehot_sum + jnp.where(hot, 1.0, 0.0)
    r = lax.broadcasted_iota(I32, (tm, tm), 0)
    c = lax.broadcasted_iota(I32, (tm, tm), 1)
    lower = jnp.where(c < r, 1.0, 0.0).astype(BF16)
    rank = jnp.dot(lower, onehot_sum.astype(BF16), preferred_element_type=F32) + count_ref[...]
    route = jnp.zeros((tm, LANES), I32)
    gates = jnp.zeros((tm, LANES), F32)
    for k in range(TOP_K):
        pos = jnp.sum(jnp.where(hots[k], rank, 0.0), axis=1, keepdims=True).astype(I32)
        route = jnp.where(lane == k, idxs[k], route)
        route = jnp.where(lane == TOP_K + k, pos, route)
        gates = jnp.where(lane == k, exps[k] / denom, gates)
    route_ref[...] = route
    gate_ref[...] = gates
    meta = jnp.where(lane == 0, step * tm + lax.broadcasted_iota(I32, (tm, LANES), 0), 0)
    for k in range(TOP_K):
        meta = jnp.where(lane == 1 + k, idxs[k], meta)
    hf_ref[pl.ds(META_SUBROW, tm, stride=ROW_TILE), :] = meta.astype(U32)
    for j in range(META_SUBROW + 1, ROW_TILE):
        hf_ref[pl.ds(j, tm, stride=ROW_TILE), :] = jnp.zeros((tm, LANES), U32)
    count_ref[...] = count_ref[...] + jnp.sum(onehot_sum, axis=0, keepdims=True)
    cum_ref[0] = jnp.broadcast_to(count_ref[...], (8, LANES))


def _post_mix(x2d, proj2, o_sb, o_diff, w_up_sb, w_up_diff, w_out, g_memq, w_memq, kv, w_memo,
              g_ffn, w_router_pad, b_router_pad, *, tm, seq, gate_col):
    n_tok, d = x2d.shape
    n_tiles = n_tok // tm
    tiles_per_batch = seq // tm
    row = lambda i: (i, 0)
    in_specs = [
        pl.BlockSpec((tm, d), row),
        pl.BlockSpec((tm, o_sb.shape[1]), row),
        pl.BlockSpec((tm, o_diff.shape[1]), row),
        pl.BlockSpec((tm, d), lambda i: (i, gate_col)),
        pl.BlockSpec((tm, d), lambda i: (i, gate_col + 1)),
        _const_spec(w_up_sb.shape), _const_spec(w_up_diff.shape), _const_spec(w_out.shape),
        _const_spec((1, d)), _const_spec(w_memq.shape),
        pl.BlockSpec((1,) + kv.shape[1:], lambda i: (i // tiles_per_batch, 0, 0)),
        _const_spec(w_memo.shape), _const_spec((1, d)),
        _const_spec(w_router_pad.shape), _const_spec((1, LANES)),
    ]
    out_specs = [
        pl.BlockSpec((tm, d), row),
        pl.BlockSpec((tm * ROW_TILE, LANES), row),
        pl.BlockSpec((tm, LANES), row),
        pl.BlockSpec((tm, LANES), row),
        pl.BlockSpec((1, 8, LANES), lambda i: (i, 0, 0)),
    ]
    assert d // 2 == PACKED_SUBROWS * LANES
    out_shape = [
        jax.ShapeDtypeStruct((n_tok, d), F32),
        jax.ShapeDtypeStruct((n_tok * ROW_TILE, LANES), U32),
        jax.ShapeDtypeStruct((n_tok, LANES), I32),
        jax.ShapeDtypeStruct((n_tok, LANES), F32),
        jax.ShapeDtypeStruct((n_tiles, 8, LANES), F32),
    ]
    return pl.pallas_call(
        functools.partial(_postmix_kernel, tm=tm, d_model=d),
        grid=(n_tiles,),
        in_specs=in_specs,
        out_specs=out_specs,
        out_shape=out_shape,
        scratch_shapes=[pltpu.VMEM((1, LANES), F32)],
        compiler_params=_params(("arbitrary",)),
        name="post_mix",
    )(x2d, o_sb, o_diff, proj2, proj2, w_up_sb, w_up_diff, w_out, g_memq, w_memq, kv, w_memo,
      g_ffn, w_router_pad, b_router_pad)


def _dispatch_kernel(zstart_ref, zflag_ref, nused_ref, dest_ref, hf_ref, xs_hbm, stage0, stage1,
                     zbuf, sem, zsem, *, tm, bm):
    s = pl.program_id(0)

    @pl.when(s == 0)
    def _():
        lane = lax.broadcasted_iota(I32, (bm, LANES), 1)
        zbuf[...] = jnp.zeros_like(zbuf)
        zbuf[pl.ds(META_SUBROW, bm, stride=ROW_TILE), :] = jnp.where(
            jnp.logical_and(lane >= 1, lane <= TOP_K), -1, 0).astype(U32)

        def fill_copy(start):
            start = pl.multiple_of(start * ROW_TILE, bm * ROW_TILE)
            return pltpu.make_async_copy(zbuf, xs_hbm.at[pl.ds(start, bm * ROW_TILE), :], zsem)

        n_blocks = xs_hbm.shape[0] // (bm * ROW_TILE)
        for action in ("start", "wait"):
            for e in range(N_EXPERTS):
                @pl.when(zflag_ref[e] != 0)
                def _(e=e, action=action):
                    getattr(fill_copy(zstart_ref[e]), action)()

            def trailing(blk, _, action=action):
                getattr(fill_copy(blk * bm), action)()
                return 0

            lax.fori_loop(nused_ref[0], n_blocks, trailing, 0)

    def tile_done(stage, parity):
        for _ in range(TOP_K):
            pltpu.make_async_copy(stage, xs_hbm.at[pl.ds(0, tm * ROW_TILE), :],
                                  sem.at[parity]).wait()

    def run(stage, other, parity):
        stage[...] = hf_ref[...]

        def issue(t, _):
            src = stage.at[pl.ds(pl.multiple_of(t * ROW_TILE, ROW_TILE), ROW_TILE), :]
            for k in range(TOP_K):
                row = pl.multiple_of(dest_ref[t * TOP_K + k] * ROW_TILE, ROW_TILE)
                pltpu.make_async_copy(src, xs_hbm.at[pl.ds(row, ROW_TILE), :],
                                      sem.at[parity]).start(priority=k % 2)
            return 0

        lax.fori_loop(0, tm, issue, 0)

        @pl.when(s > 0)
        def _():
            tile_done(other, 1 - parity)

        @pl.when(s == pl.num_programs(0) - 1)
        def _():
            tile_done(stage, parity)

    @pl.when(s % 2 == 0)
    def _():
        run(stage0, stage1, 0)

    @pl.when(s % 2 == 1)
    def _():
        run(stage1, stage0, 1)


def _dispatch(zstart, zflag, n_used, dest_flat, hf_rows, *, tm, bm, n_rows):
    n_tok = hf_rows.shape[0] // ROW_TILE
    grid_spec = pltpu.PrefetchScalarGridSpec(
        num_scalar_prefetch=3,
        grid=(n_tok // tm,),
        in_specs=[
            pl.BlockSpec((tm * TOP_K,), lambda i, *_: (i,), memory_space=pltpu.SMEM),
            pl.BlockSpec((tm * ROW_TILE, LANES), lambda i, *_: (i, 0)),
        ],
        out_specs=pl.BlockSpec(memory_space=pl.ANY),
        scratch_shapes=[pltpu.VMEM((tm * ROW_TILE, LANES), U32),
                        pltpu.VMEM((tm * ROW_TILE, LANES), U32),
                        pltpu.VMEM((bm * ROW_TILE, LANES), U32),
                        pltpu.SemaphoreType.DMA((2,)), pltpu.SemaphoreType.DMA(())],
    )
    return pl.pallas_call(
        functools.partial(_dispatch_kernel, tm=tm, bm=bm),
        grid_spec=grid_spec,
        out_shape=jax.ShapeDtypeStruct((n_rows * ROW_TILE, LANES), U32),
        compiler_params=_params(("arbitrary",)),
        name="dispatch",
    )(zstart, zflag, n_used, dest_flat, hf_rows)


FFN_UP_CHUNKS = 8
FFN_DOWN_CHUNKS = 4


def _expert_kernel(be_ref, valid_ref, src_ref, xs_ref, wgu_ref, bgu_ref, wd_ref, bd_ref, ys_hbm,
                   wgu_bf, wd_bf, ybuf, slot_v, slot_s, sem_y, sem_s, sem_z, *, d_ff, bm, n_tok):
    r = pl.program_id(0)
    valid = valid_ref[r] != 0
    prev_valid = jnp.logical_and(r > 0, valid_ref[jnp.maximum(r - 1, 0)] != 0)
    fresh = jnp.logical_or(r == 0, be_ref[r] != be_ref[jnp.maximum(r - 1, 0)])
    cur = r % 2
    prev = 1 - cur
    d = wd_bf.shape[1]
    n_slots = TOP_K * n_tok
    block_rows = bm * ROW_TILE

    def rows_done(parity):
        return pltpu.make_async_copy(ybuf.at[parity], ys_hbm.at[pl.ds(0, block_rows), :],
                                     sem_y.at[parity])

    def slots_copy(parity):
        return pltpu.make_async_copy(slot_v.at[0:1, :], slot_s.at[pl.ds(parity, 1), :],
                                     sem_s.at[parity])

    def scatter_rows(lo, hi):
        for i in range(lo, hi):
            dst = pl.multiple_of(slot_s[prev, i] * ROW_TILE, ROW_TILE)
            pltpu.make_async_copy(ybuf.at[prev, pl.ds(i * ROW_TILE, ROW_TILE), :],
                                  ys_hbm.at[pl.ds(dst, ROW_TILE), :],
                                  sem_y.at[prev]).start(priority=i % 2)

    @pl.when(r == 0)
    def _():
        ybuf[...] = jnp.zeros_like(ybuf)
        for action in ("start", "wait"):
            for parity in range(2):
                trash = (n_slots + parity * bm) * ROW_TILE
                getattr(pltpu.make_async_copy(ybuf.at[parity],
                                              ys_hbm.at[pl.ds(trash, block_rows), :], sem_z),
                        action)()

    @pl.when(prev_valid)
    def _():
        slots_copy(prev).wait()

    @pl.when(jnp.logical_and(valid, fresh))
    def _():
        wgu_bf[...] = wgu_ref[0].astype(BF16)
        wd_bf[...] = wd_ref[0].astype(BF16)

    def ffn(interleave):
        n_chunks = FFN_UP_CHUNKS + FFN_DOWN_CHUNKS
        bounds = [bm * c // n_chunks for c in range(n_chunks + 1)]
        chunk = iter(range(n_chunks))

        def after_chunk():
            c = next(chunk)
            if interleave:
                scatter_rows(bounds[c], bounds[c + 1])

        words = jnp.concatenate([xs_ref[pl.ds(j, bm, stride=ROW_TILE), :]
                                 for j in range(PACKED_SUBROWS)], axis=1)
        xb = _unpack_bf16_pairs(words)
        up_w = 2 * d_ff // FFN_UP_CHUNKS
        parts = []
        for c in range(FFN_UP_CHUNKS):
            cols = slice(c * up_w, (c + 1) * up_w)
            parts.append(jnp.dot(xb, wgu_bf[:, cols], preferred_element_type=F32)
                         + bgu_ref[0][:, cols])
            after_chunk()
        hgu = jnp.concatenate(parts, axis=1)
        glu = jnp.minimum(hgu[:, :d_ff], SWIGLU_LIMIT)
        lin = jnp.clip(hgu[:, d_ff:], -SWIGLU_LIMIT, SWIGLU_LIMIT)
        act = (glu * jax.nn.sigmoid(SWIGLU_ALPHA * glu) * (lin + 1.0)).astype(BF16)
        down_w = d // FFN_DOWN_CHUNKS
        parts = []
        for c in range(FFN_DOWN_CHUNKS):
            cols = slice(c * down_w, (c + 1) * down_w)
            parts.append(jnp.dot(act, wd_bf[:, cols], preferred_element_type=F32)
                         + bd_ref[0][:, cols])
            after_chunk()
        y = jnp.concatenate(parts, axis=1)

        @pl.when(r >= 2)
        def _():
            rows_done(cur).wait()

        for j in range(d // LANES):
            ybuf[cur, pl.ds(j, bm, stride=ROW_TILE), :] = y[:, j * LANES:(j + 1) * LANES]
        meta = xs_ref[pl.ds(META_SUBROW, bm, stride=ROW_TILE), :].astype(I32)
        row = lax.broadcasted_iota(I32, (bm, 1), 0)
        slot = n_slots + cur * bm + row
        for k in range(TOP_K):
            slot = jnp.where(meta[:, 1 + k:2 + k] == be_ref[r], k * n_tok + meta[:, 0:1], slot)
        slots = jnp.broadcast_to(slot.astype(F32), (bm, LANES)).T
        slot_v[...] = slots[0:8, :].astype(I32)
        slots_copy(cur).start()

    @pl.when(jnp.logical_and(valid, prev_valid))
    def _():
        ffn(True)

    @pl.when(jnp.logical_and(valid, jnp.logical_not(prev_valid)))
    def _():
        ffn(False)

    @pl.when(jnp.logical_and(jnp.logical_not(valid), prev_valid))
    def _():
        scatter_rows(0, bm)

        @pl.when(r >= 2)
        def _():
            rows_done(cur).wait()

        rows_done(prev).wait()


def _expert_ffn(blk_expert, blk_valid, blk_src, xs, w_gu, b_gu, w_down, b_down, *, bm, n_tok):
    n_grid = blk_expert.shape[0]
    d_ff, d = w_down.shape[1:]
    assert d == ROW_TILE * LANES
    block_rows = bm * ROW_TILE
    grid_spec = pltpu.PrefetchScalarGridSpec(
        num_scalar_prefetch=3,
        grid=(n_grid,),
        in_specs=[
            pl.BlockSpec((block_rows, LANES), lambda r, be, valid, src: (src[r], 0)),
            pl.BlockSpec((1, d, 2 * d_ff), lambda r, be, *_: (be[r], 0, 0)),
            pl.BlockSpec((1, 1, 2 * d_ff), lambda r, be, *_: (be[r], 0, 0)),
            pl.BlockSpec((1, d_ff, d), lambda r, be, *_: (be[r], 0, 0)),
            pl.BlockSpec((1, 1, d), lambda r, be, *_: (be[r], 0, 0)),
        ],
        out_specs=pl.BlockSpec(memory_space=pl.ANY),
        scratch_shapes=[pltpu.VMEM((d, 2 * d_ff), BF16), pltpu.VMEM((d_ff, d), BF16),
                        pltpu.VMEM((2, block_rows, LANES), F32), pltpu.VMEM((8, bm), I32),
                        pltpu.SMEM((2, bm), I32), pltpu.SemaphoreType.DMA((2,)),
                        pltpu.SemaphoreType.DMA((2,)), pltpu.SemaphoreType.DMA(())],
    )
    return pl.pallas_call(
        functools.partial(_expert_kernel, d_ff=d_ff, bm=bm, n_tok=n_tok),
        grid_spec=grid_spec,
        out_shape=jax.ShapeDtypeStruct(((TOP_K * n_tok + 2 * bm) * ROW_TILE, LANES), F32),
        compiler_params=_params(("arbitrary",)),
        name="expert_ffn",
    )(blk_expert, blk_valid, blk_src, xs, w_gu, b_gu, w_down, b_down)


def _combine_kernel(x_ref, gate_ref, g_ref, y0_ref, y1_ref, y2_ref, y3_ref, o_ref):
    tm, d = x_ref.shape
    gates = gate_ref[...]
    acc = x_ref[...]
    for k, y_ref in enumerate((y0_ref, y1_ref, y2_ref, y3_ref)):
        y = jnp.concatenate([y_ref[pl.ds(j, tm, stride=ROW_TILE), :]
                             for j in range(d // LANES)], axis=1)
        acc = acc + gates[:, k:k + 1] * y
    o_ref[...] = _rms(acc, g_ref[...])


def _combine(x2, gates, g_final, ys, *, tm):
    n_tok, d = x2.shape
    tiles = n_tok // tm
    slot_spec = lambda k: pl.BlockSpec((tm * ROW_TILE, LANES), lambda i: (k * tiles + i, 0))
    return pl.pallas_call(
        _combine_kernel,
        grid=(tiles,),
        in_specs=[
            pl.BlockSpec((tm, d), lambda i: (i, 0)),
            pl.BlockSpec((tm, LANES), lambda i: (i, 0)),
            pl.BlockSpec((1, d), lambda i: (0, 0)),
        ] + [slot_spec(k) for k in range(TOP_K)],
        out_specs=pl.BlockSpec((tm, d), lambda i: (i, 0)),
        out_shape=jax.ShapeDtypeStruct((n_tok, d), F32),
        compiler_params=_params(("arbitrary",)),
        name="combine",
    )(x2, gates, g_final, ys, ys, ys, ys)


def _routing_tables(route, cum, *, bm, n_grid):
    expert = route[:, 0:TOP_K]
    pos = route[:, TOP_K:2 * TOP_K]
    experts = jnp.arange(N_EXPERTS, dtype=I32)
    counts = cum[-1, 0, :N_EXPERTS].astype(I32)
    padded = (counts + bm - 1) // bm * bm
    pend = jnp.cumsum(padded)
    pstart = pend - padded
    dest = pos + jnp.sum(jnp.where(expert[:, :, None] == experts, pstart, 0), axis=2)

    blk = jnp.arange(n_grid, dtype=I32)
    blk_expert = jnp.minimum(jnp.sum(pend[None, :] <= (blk * bm)[:, None], axis=1),
                             N_EXPERTS - 1).astype(I32)
    n_used = pend[-1] // bm
    blk_valid = (blk < n_used).astype(I32)
    blk_src = jnp.minimum(blk, jnp.maximum(n_used - 1, 0)).astype(I32)
    zflag = (padded > 0).astype(I32)
    zstart = jnp.maximum(pend - bm, 0).astype(I32)
    return (dest.astype(I32).reshape(-1), blk_expert, blk_valid, blk_src, zstart, zflag,
            n_used.astype(I32).reshape(1))


def kernel(x, mem, positions, norm_mix, w_in, lambda_q1, lambda_k1, lambda_q2, lambda_k2,
           diff_subln, w_up_sb, w_up_diff, w_out, norm_mem_q, norm_mem_kv, w_mem_q, w_mem_kv,
           w_mem_o, norm_ffn, w_router, b_router, w_gate_up, b_gate_up, w_down, b_down,
           norm_final):
    b, s, d = x.shape
    n_tok = b * s
    depth = norm_mix.shape[0]
    sb_width = w_up_sb.shape[1]
    diff_width = w_up_diff.shape[1]
    n_in = w_in.shape[2]
    chunk = 512
    assert sb_width == chunk and diff_width == chunk and d == 2 * chunk
    sbq, sbk, sbv, dq, dk, dv = range(6)
    blocks_per_chunk = chunk // LANES
    scale = 1.0 / math.sqrt(SB_HEAD_DIM)
    assert SB_HEAD_DIM == DIFF_HEAD_DIM

    tm_in = min(512, n_tok)
    tq_sb = min(256, s)
    tq_diff = min(512, s)
    tm_post = min(256, s)
    bm = 256
    n_grid = (n_tok * TOP_K) // bm + N_EXPERTS

    cos_t, sin_t = _rope_tables(positions)
    x2d = x.reshape(n_tok, d)
    for l in range(depth):
        lambda_init = 0.8 - 0.6 * math.exp(-0.3 * l)
        proj = _in_proj(x2d, norm_mix[l].reshape(1, d), w_in[l].astype(BF16), cos_t, sin_t,
                        tm=tm_in, chunk=chunk, rope_chunks=(dq, dk), scale_chunks=(sbq, dq),
                        scale=scale)
        proj3 = proj.reshape(b, s, n_in)
        o_sb = _sb_attention(proj3, q_col=sbq * blocks_per_chunk, k_col=sbk * blocks_per_chunk,
                             v_col=sbv * blocks_per_chunk, n_pairs=sb_width // LANES, tq=tq_sb)
        o_diff = _diff_attention(
            proj3, lambda_q1[l].reshape(1, -1), lambda_k1[l].reshape(1, -1),
            lambda_q2[l].reshape(1, -1), lambda_k2[l].reshape(1, -1),
            diff_subln[l].reshape(1, -1), q_col=dq * blocks_per_chunk,
            k_col=dk * blocks_per_chunk, v_col=dv * blocks_per_chunk,
            n_heads=diff_width // DIFF_V_DIM, tq=tq_diff, lambda_init=lambda_init)
        kv = _mem_kv(mem, norm_mem_kv[l].reshape(1, d), w_mem_kv[l].astype(BF16))
        w_router_pad = jnp.zeros((d, LANES), F32).at[:, :N_EXPERTS].set(w_router[l])
        b_router_pad = jnp.full((1, LANES), NEG_BIG, F32).at[0, :N_EXPERTS].set(b_router[l])
        x_res, hf, route, gates, cum = _post_mix(
            x2d, proj, o_sb.reshape(n_tok, sb_width), o_diff.reshape(n_tok, diff_width),
            w_up_sb[l].astype(BF16), w_up_diff[l].astype(BF16), w_out[l].astype(BF16),
            norm_mem_q[l].reshape(1, d), w_mem_q[l].astype(BF16), kv, w_mem_o[l].astype(BF16),
            norm_ffn[l].reshape(1, d), w_router_pad, b_router_pad,
            tm=tm_post, seq=s, gate_col=6 * chunk // d)
        dest, blk_expert, blk_valid, blk_src, zstart, zflag, n_used = _routing_tables(
            route, cum, bm=bm, n_grid=n_grid + 1)
        xs = _dispatch(zstart, zflag, n_used, dest, hf, tm=tm_post, bm=bm, n_rows=n_grid * bm)
        ys = _expert_ffn(blk_expert, blk_valid, blk_src, xs, w_gate_up[l],
                         b_gate_up[l][:, None, :], w_down[l], b_down[l][:, None, :], bm=bm,
                         n_tok=n_tok)
        g_next = norm_final if l == depth - 1 else None
        assert g_next is not None, "only depth 1 is wired: the final norm is fused into combine"
        x2d = _combine(x_res, gates, g_next.reshape(1, d), ys, tm=tm_post)
    return x2d.reshape(b, s, d)
```

```python
import functools
import math

import jax
import jax.numpy as jnp
from jax import lax
from jax.experimental import pallas as pl
from jax.experimental.pallas import tpu as pltpu

F32 = jnp.float32
BF16 = jnp.bfloat16
I32 = jnp.int32
U32 = jnp.uint32

LANES = 128
VMEM_LIMIT_BYTES = 56 * 1024 * 1024
ROW_TILE = 8
PACKED_SUBROWS = 4
META_SUBROW = 4

NORM_EPS = 1e-6
ROPE_THETA = 10000.0
CHUNK = 64
SB_HEAD_DIM = 64
DIFF_HEAD_DIM = 64
DIFF_V_DIM = 128
MEM_HEADS = 4
N_EXPERTS = 32
TOP_K = 4
SWIGLU_LIMIT = 7.0
SWIGLU_ALPHA = 1.702

SB_DEAD_LOG = -105.0
NEG_BIG = -1e30


def _params(semantics):
    return pltpu.CompilerParams(dimension_semantics=semantics,
                                vmem_limit_bytes=VMEM_LIMIT_BYTES)


def _const_spec(shape):
    nd = len(shape)
    return pl.BlockSpec(shape, lambda *_: (0,) * nd)


def _rms(x, g):
    return x * lax.rsqrt(jnp.mean(x * x, axis=-1, keepdims=True) + NORM_EPS) * g


def _rope_table_kernel(pos_ref, inv_ref, cos_ref, sin_ref):
    ang = pos_ref[...] * inv_ref[...]
    cos_ref[...] = jnp.cos(ang)
    sin_ref[...] = jnp.sin(ang)


def _rope_tables(positions):
    n_tok = positions.size
    half = DIFF_HEAD_DIM // 2
    per_row = LANES // half
    rows = n_tok // per_row
    pos = jnp.repeat(positions.reshape(rows, per_row).astype(F32), half, axis=1)
    inv = ROPE_THETA ** (-jnp.arange(half, dtype=F32) / half)
    inv = jnp.tile(inv, per_row).reshape(1, LANES)
    tr = min(rows, 512)
    cos, sin = pl.pallas_call(
        _rope_table_kernel,
        grid=(rows // tr,),
        in_specs=[pl.BlockSpec((tr, LANES), lambda i: (i, 0)), _const_spec((1, LANES))],
        out_specs=[pl.BlockSpec((tr, LANES), lambda i: (i, 0))] * 2,
        out_shape=[jax.ShapeDtypeStruct((rows, LANES), F32)] * 2,
        compiler_params=_params(("arbitrary",)),
        name="rope_table",
    )(pos, inv)
    cos = cos.reshape(n_tok, half)
    sin = sin.reshape(n_tok, half)
    cos_t = jnp.tile(cos, (1, LANES // half))
    sin_t = jnp.tile(jnp.concatenate([-sin, sin], axis=1), (1, LANES // (2 * half)))
    return cos_t, sin_t


def _inproj_kernel(x_ref, g_ref, w_ref, cos_ref, sin_ref, o_ref, *, chunk, rope_chunks,
                   scale_chunks, scale):
    h = _rms(x_ref[...], g_ref[...]).astype(BF16)
    lane = lax.broadcasted_iota(I32, (1, chunk), 1)
    first_half = (lane % DIFF_HEAD_DIM) < (DIFF_HEAD_DIM // 2)
    for c in range(w_ref.shape[1] // chunk):
        cols = slice(c * chunk, (c + 1) * chunk)
        acc = jnp.dot(h, w_ref[:, cols], preferred_element_type=F32)
        if c in rope_chunks:
            cos = jnp.tile(cos_ref[...], (1, chunk // LANES))
            sin = jnp.tile(sin_ref[...], (1, chunk // LANES))
            partner = jnp.where(first_half,
                                pltpu.roll(acc, chunk - DIFF_HEAD_DIM // 2, 1),
                                pltpu.roll(acc, DIFF_HEAD_DIM // 2, 1))
            acc = acc * cos + partner * sin
        if c in scale_chunks:
            acc = acc * scale
        o_ref[:, cols] = acc.astype(BF16)


def _in_proj(x2d, g, w_bf16, cos_t, sin_t, *, tm, chunk, rope_chunks, scale_chunks, scale):
    n_tok, d = x2d.shape
    n_in = w_bf16.shape[1]
    kern = functools.partial(_inproj_kernel, chunk=chunk, rope_chunks=rope_chunks,
                             scale_chunks=scale_chunks, scale=scale)
    return pl.pallas_call(
        kern,
        grid=(n_tok // tm,),
        in_specs=[pl.BlockSpec((tm, d), lambda i: (i, 0)),
                  _const_spec((1, d)),
                  _const_spec((d, n_in)),
                  pl.BlockSpec((tm, LANES), lambda i: (i, 0)),
                  pl.BlockSpec((tm, LANES), lambda i: (i, 0))],
        out_specs=pl.BlockSpec((tm, n_in), lambda i: (i, 0)),
        out_shape=jax.ShapeDtypeStruct((n_tok, n_in), BF16),
        compiler_params=_params(("arbitrary",)),
        name="in_proj",
    )(x2d, g, w_bf16, cos_t, sin_t)


def _sb_kernel(q_ref, k_ref, v_ref, o_ref, acc_ref, carry_ref, *, tq):
    i = pl.program_id(2)
    q = q_ref[0]
    lane = lax.broadcasted_iota(I32, (1, LANES), 1)
    zero = jnp.zeros_like(q)
    q2 = jnp.concatenate([jnp.where(lane < SB_HEAD_DIM, q, zero),
                          jnp.where(lane >= SB_HEAD_DIM, q, zero)], axis=0)
    uj = lax.broadcasted_iota(I32, (2 * tq, tq), 0) % tq
    us = lax.broadcasted_iota(I32, (2 * tq, tq), 1)
    suffix = jnp.where(uj > us, 1.0, 0.0).astype(BF16)

    def block(j, masked):
        start = pl.multiple_of(j * tq, tq)
        kj = k_ref[0, pl.ds(start, tq), :]
        vj = v_ref[0, pl.ds(start, tq), :]
        z = lax.dot_general(q2, kj, (((1,), (1,)), ((), ())), preferred_element_type=F32)
        sp = jnp.maximum(z, 0.0) + jnp.log(1.0 + jnp.exp(-jnp.abs(z)))
        log_rem = -sp
        if masked:
            causal = (lax.broadcasted_iota(I32, (2 * tq, tq), 1)
                      < lax.broadcasted_iota(I32, (2 * tq, tq), 0) % tq)
            log_rem = jnp.where(causal, log_rem, 0.0)
        hi = log_rem.astype(BF16)
        lo = (log_rem - hi.astype(F32)).astype(BF16)
        after = jnp.dot(jnp.concatenate([hi, lo], axis=1), suffix, preferred_element_type=F32)
        carry = carry_ref[...]
        w = jnp.exp(z - sp + after + carry)
        if masked:
            w = jnp.where(causal, w, 0.0)
        acc_ref[...] += jnp.dot(w.astype(BF16), vj, preferred_element_type=F32)
        carry = carry + after[:, 0:1] + log_rem[:, 0:1]
        carry_ref[...] = carry
        return jnp.max(carry)

    acc_ref[...] = jnp.zeros_like(acc_ref)
    carry_ref[...] = jnp.zeros_like(carry_ref)
    alive = block(i, True)

    def cond(state):
        j, alive = state
        return jnp.logical_and(j >= 0, alive > SB_DEAD_LOG)

    def body(state):
        j, _ = state
        return j - 1, block(j, False)

    lax.while_loop(cond, body, (i - 1, alive))
    o_ref[0] = jnp.where(lane < SB_HEAD_DIM, acc_ref[0:tq, :], acc_ref[tq:2 * tq, :]).astype(BF16)


def _sb_attention(proj3, *, q_col, k_col, v_col, n_pairs, tq):
    b, s, _ = proj3.shape
    return pl.pallas_call(
        functools.partial(_sb_kernel, tq=tq),
        grid=(b, n_pairs, s // tq),
        in_specs=[pl.BlockSpec((1, tq, LANES), lambda bi, p, i: (bi, i, q_col + p)),
                  pl.BlockSpec((1, s, LANES), lambda bi, p, i: (bi, 0, k_col + p)),
                  pl.BlockSpec((1, s, LANES), lambda bi, p, i: (bi, 0, v_col + p))],
        out_specs=pl.BlockSpec((1, tq, LANES), lambda bi, p, i: (bi, i, p)),
        out_shape=jax.ShapeDtypeStruct((b, s, n_pairs * LANES), BF16),
        scratch_shapes=[pltpu.VMEM((2 * tq, LANES), F32), pltpu.VMEM((2 * tq, 1), F32)],
        compiler_params=_params(("arbitrary", "arbitrary", "arbitrary")),
        name="sb_attention",
    )(proj3, proj3, proj3)


def _diff_kernel(lq1_ref, lk1_ref, lq2_ref, lk2_ref, q_ref, k_ref, v_ref, g_ref, o_ref, vt_ref,
                 acc_ref, *, tq, lambda_init):
    i = pl.program_id(2)
    s_len = v_ref.shape[1]

    @pl.when(i == 0)
    def _():
        for c in range(s_len // tq):
            cols = slice(c * tq, (c + 1) * tq)
            vt_ref[:, cols] = v_ref[0, cols, :].astype(F32).T.astype(BF16)

    q = q_ref[0]
    lane = lax.broadcasted_iota(I32, (1, LANES), 1)
    zero = jnp.zeros_like(q)
    q2 = jnp.concatenate([jnp.where(lane < DIFF_HEAD_DIM, q, zero),
                          jnp.where(lane >= DIFF_HEAD_DIM, q, zero)], axis=0)

    def step(j, stats, masked):
        m, l = stats
        start = pl.multiple_of(j * tq, tq)
        kj = k_ref[0, pl.ds(start, tq), :]
        vtj = vt_ref[:, pl.ds(start, tq)]
        zt = lax.dot_general(kj, q2, (((1,), (1,)), ((), ())), preferred_element_type=F32)
        if masked:
            visible = (lax.broadcasted_iota(I32, (tq, 2 * tq), 0) // CHUNK
                       <= (lax.broadcasted_iota(I32, (tq, 2 * tq), 1) % tq) // CHUNK)
            zt = jnp.where(visible, zt, -jnp.inf)
        m_new = jnp.maximum(m, jnp.max(zt, axis=0, keepdims=True))
        alpha = jnp.exp(m - m_new)
        pt = jnp.exp(zt - m_new)
        l = alpha * l + jnp.sum(pt, axis=0, keepdims=True)
        acc_ref[...] = alpha * acc_ref[...] + jnp.dot(vtj, pt.astype(BF16),
                                                      preferred_element_type=F32)
        return m_new, l

    acc_ref[...] = jnp.zeros_like(acc_ref)
    init = (jnp.full((1, 2 * tq), -jnp.inf, F32), jnp.zeros((1, 2 * tq), F32))
    stats = lax.fori_loop(0, i, lambda j, st: step(j, st, False), init)
    _, l = step(i, stats, True)
    lam = (jnp.exp(jnp.sum(lq1_ref[...] * lk1_ref[...], axis=1, keepdims=True))
           - jnp.exp(jnp.sum(lq2_ref[...] * lk2_ref[...], axis=1, keepdims=True))
           + lambda_init)
    ot = acc_ref[...] / l
    ot = ot[:, 0:tq] - lam * ot[:, tq:2 * tq]
    o_ref[0] = (_rms(ot.T, g_ref[...]) * (1.0 - lambda_init)).astype(BF16)


def _diff_attention(proj3, lq1, lk1, lq2, lk2, subln, *, q_col, k_col, v_col, n_heads, tq,
                    lambda_init):
    b, s, _ = proj3.shape
    lam_spec = _const_spec((1, DIFF_HEAD_DIM))
    return pl.pallas_call(
        functools.partial(_diff_kernel, tq=tq, lambda_init=lambda_init),
        grid=(b, n_heads, s // tq),
        in_specs=[lam_spec, lam_spec, lam_spec, lam_spec,
                  pl.BlockSpec((1, tq, LANES), lambda bi, h, i: (bi, i, q_col + h)),
                  pl.BlockSpec((1, s, LANES), lambda bi, h, i: (bi, 0, k_col + h)),
                  pl.BlockSpec((1, s, LANES), lambda bi, h, i: (bi, 0, v_col + h)),
                  _const_spec((1, DIFF_V_DIM))],
        out_specs=pl.BlockSpec((1, tq, LANES), lambda bi, h, i: (bi, i, h)),
        out_shape=jax.ShapeDtypeStruct((b, s, n_heads * DIFF_V_DIM), BF16),
        scratch_shapes=[pltpu.VMEM((DIFF_V_DIM, s), BF16), pltpu.VMEM((DIFF_V_DIM, 2 * tq), F32)],
        compiler_params=_params(("arbitrary", "arbitrary", "arbitrary")),
        name="diff_attention",
    )(lq1, lk1, lq2, lk2, proj3, proj3, proj3, subln)


def _mem_kv_kernel(mem_ref, g_ref, w_ref, o_ref):
    h = _rms(mem_ref[0], g_ref[...]).astype(BF16)
    o_ref[0] = jnp.dot(h, w_ref[...], preferred_element_type=F32).astype(BF16)


def _mem_kv(mem, g, w_bf16):
    b, m, d = mem.shape
    n = w_bf16.shape[1]
    return pl.pallas_call(
        _mem_kv_kernel,
        grid=(b,),
        in_specs=[pl.BlockSpec((1, m, d), lambda i: (i, 0, 0)), _const_spec((1, d)),
                  _const_spec((d, n))],
        out_specs=pl.BlockSpec((1, m, n), lambda i: (i, 0, 0)),
        out_shape=jax.ShapeDtypeStruct((b, m, n), BF16),
        compiler_params=_params(("arbitrary",)),
        name="mem_kv",
    )(mem, g, w_bf16)


def _split_bf16(v):
    hi = v.astype(BF16)
    return hi, (v - hi.astype(F32)).astype(BF16)


def _pack_bf16_pairs(v):
    half = v.shape[1] // 2
    bits = lax.bitcast_convert_type(v.astype(BF16).astype(F32), U32)
    return (bits[:, :half] >> 16) | (bits[:, half:] & jnp.uint32(0xFFFF0000))


def _unpack_bf16_pairs(w):
    lo = lax.bitcast_convert_type(w << 16, F32)
    hi = lax.bitcast_convert_type(w & jnp.uint32(0xFFFF0000), F32)
    return jnp.concatenate([lo, hi], axis=1).astype(BF16)


def _postmix_kernel(x_ref, osb_ref, odf_ref, gsb_ref, gdf_ref, wus_ref, wud_ref, wout_ref,
                    gq_ref, wq_ref, kv_ref, wo_ref, gf_ref, wr_ref, br_ref,
                    x2_ref, hf_ref, route_ref, gate_ref, cum_ref, count_ref, *, tm, d_model):
    step = pl.program_id(0)

    @pl.when(step == 0)
    def _():
        count_ref[...] = jnp.zeros_like(count_ref)

    y_sb = jnp.dot(osb_ref[...], wus_ref[...], preferred_element_type=F32)
    y_df = jnp.dot(odf_ref[...], wud_ref[...], preferred_element_type=F32)
    mixed = (jax.nn.sigmoid(gsb_ref[...].astype(F32)) * y_sb
             + jax.nn.sigmoid(gdf_ref[...].astype(F32)) * y_df)
    x1 = x_ref[...] + jnp.dot(mixed.astype(BF16), wout_ref[...], preferred_element_type=F32)

    hq = _rms(x1, gq_ref[...]).astype(BF16)
    hd = d_model // MEM_HEADS
    q = jnp.dot(hq, wq_ref[...], preferred_element_type=F32) * (1.0 / math.sqrt(hd))
    q = q.astype(BF16)
    heads = []
    for h in range(MEM_HEADS):
        kh = kv_ref[0, :, h * hd:(h + 1) * hd]
        vh = kv_ref[0, :, d_model + h * hd:d_model + (h + 1) * hd]
        z = lax.dot_general(q[:, h * hd:(h + 1) * hd], kh, (((1,), (1,)), ((), ())),
                            preferred_element_type=F32)
        p = jnp.exp(z - jnp.max(z, axis=1, keepdims=True))
        l = jnp.sum(p, axis=1, keepdims=True)
        heads.append((jnp.dot(p.astype(BF16), vh, preferred_element_type=F32) / l).astype(BF16))
    x2 = x1 + jnp.dot(jnp.concatenate(heads, axis=1), wo_ref[...], preferred_element_type=F32)
    x2_ref[...] = x2

    hf = _rms(x2, gf_ref[...])
    packed = _pack_bf16_pairs(hf)
    for j in range(PACKED_SUBROWS):
        hf_ref[pl.ds(j, tm, stride=ROW_TILE), :] = packed[:, j * LANES:(j + 1) * LANES]
    h_hi, h_lo = _split_bf16(hf)
    w_hi, w_lo = _split_bf16(wr_ref[...])
    logits = (jnp.dot(h_hi, w_hi, preferred_element_type=F32)
              + jnp.dot(h_hi, w_lo, preferred_element_type=F32)
              + jnp.dot(h_lo, w_hi, preferred_element_type=F32)) + br_ref[...]
    lane = lax.broadcasted_iota(I32, (tm, LANES), 1)
    work = logits
    vals, idxs, hots = [], [], []
    for _ in range(TOP_K):
        mx = jnp.max(work, axis=1, keepdims=True)
        idx = jnp.min(jnp.where(work == mx, lane, LANES), axis=1, keepdims=True)
        hot = lane == idx
        work = jnp.where(hot, NEG_BIG, work)
        vals.append(mx)
        idxs.append(idx)
        hots.append(hot)
    exps = [jnp.exp(v - vals[0]) for v in vals]
    denom = exps[0] + exps[1] + exps[2] + exps[3]

    onehot_sum = jnp.zeros((tm, LANES), F32)
    for hot in hots:
        onehot_sum = onehot_sum + jnp.where(hot, 1.0, 0.0)
    r = lax.broadcasted_iota(I32, (tm, tm), 0)
    c = lax.broadcasted_iota(I32, (tm, tm), 1)
    lower = jnp.where(c < r, 1.0, 0.0).astype(BF16)
    rank = jnp.dot(lower, onehot_sum.astype(BF16), preferred_element_type=F32) + count_ref[...]
    route = jnp.zeros((tm, LANES), I32)
    gates = jnp.zeros((tm, LANES), F32)
    for k in range(TOP_K):
        pos = jnp.sum(jnp.where(hots[k], rank, 0.0), axis=1, keepdims=True).astype(I32)
        route = jnp.where(lane == k, idxs[k], route)
        route = jnp.where(lane == TOP_K + k, pos, route)
        gates = jnp.where(lane == k, exps[k] / denom, gates)
    route_ref[...] = route
    gate_ref[...] = gates
    meta = jnp.where(lane == 0, step * tm + lax.broadcasted_iota(I32, (tm, LANES), 0), 0)
    for k in range(TOP_K):
        meta = jnp.where(lane == 1 + k, idxs[k], meta)
    hf_ref[pl.ds(META_SUBROW, tm, stride=ROW_TILE), :] = meta.astype(U32)
    for j in range(META_SUBROW + 1, ROW_TILE):
        hf_ref[pl.ds(j, tm, stride=ROW_TILE), :] = jnp.zeros((tm, LANES), U32)
    count_ref[...] = count_ref[...] + jnp.sum(onehot_sum, axis=0, keepdims=True)
    cum_ref[0] = jnp.broadcast_to(count_ref[...], (8, LANES))


def _post_mix(x2d, proj2, o_sb, o_diff, w_up_sb, w_up_diff, w_out, g_memq, w_memq, kv, w_memo,
              g_ffn, w_router_pad, b_router_pad, *, tm, seq, gate_col):
    n_tok, d = x2d.shape
    n_tiles = n_tok // tm
    tiles_per_batch = seq // tm
    row = lambda i: (i, 0)
    in_specs = [
        pl.BlockSpec((tm, d), row),
        pl.BlockSpec((tm, o_sb.shape[1]), row),
        pl.BlockSpec((tm, o_diff.shape[1]), row),
        pl.BlockSpec((tm, d), lambda i: (i, gate_col)),
        pl.BlockSpec((tm, d), lambda i: (i, gate_col + 1)),
        _const_spec(w_up_sb.shape), _const_spec(w_up_diff.shape), _const_spec(w_out.shape),
        _const_spec((1, d)), _const_spec(w_memq.shape),
        pl.BlockSpec((1,) + kv.shape[1:], lambda i: (i // tiles_per_batch, 0, 0)),
        _const_spec(w_memo.shape), _const_spec((1, d)),
        _const_spec(w_router_pad.shape), _const_spec((1, LANES)),
    ]
    out_specs = [
        pl.BlockSpec((tm, d), row),
        pl.BlockSpec((tm * ROW_TILE, LANES), row),
        pl.BlockSpec((tm, LANES), row),
        pl.BlockSpec((tm, LANES), row),
        pl.BlockSpec((1, 8, LANES), lambda i: (i, 0, 0)),
    ]
    assert d // 2 == PACKED_SUBROWS * LANES
    out_shape = [
        jax.ShapeDtypeStruct((n_tok, d), F32),
        jax.ShapeDtypeStruct((n_tok * ROW_TILE, LANES), U32),
        jax.ShapeDtypeStruct((n_tok, LANES), I32),
        jax.ShapeDtypeStruct((n_tok, LANES), F32),
        jax.ShapeDtypeStruct((n_tiles, 8, LANES), F32),
    ]
    return pl.pallas_call(
        functools.partial(_postmix_kernel, tm=tm, d_model=d),
        grid=(n_tiles,),
        in_specs=in_specs,
        out_specs=out_specs,
        out_shape=out_shape,
        scratch_shapes=[pltpu.VMEM((1, LANES), F32)],
        compiler_params=_params(("arbitrary",)),
        name="post_mix",
    )(x2d, o_sb, o_diff, proj2, proj2, w_up_sb, w_up_diff, w_out, g_memq, w_memq, kv, w_memo,
      g_ffn, w_router_pad, b_router_pad)


def _dispatch_kernel(zstart_ref, zflag_ref, nused_ref, dest_ref, hf_ref, xs_hbm, stage0, stage1,
                     zbuf, sem, zsem, *, tm, bm):
    s = pl.program_id(0)

    @pl.when(s == 0)
    def _():
        lane = lax.broadcasted_iota(I32, (bm, LANES), 1)
        zbuf[...] = jnp.zeros_like(zbuf)
        zbuf[pl.ds(META_SUBROW, bm, stride=ROW_TILE), :] = jnp.where(
            jnp.logical_and(lane >= 1, lane <= TOP_K), -1, 0).astype(U32)

        def fill_copy(start):
            start = pl.multiple_of(start * ROW_TILE, bm * ROW_TILE)
            return pltpu.make_async_copy(zbuf, xs_hbm.at[pl.ds(start, bm * ROW_TILE), :], zsem)

        n_blocks = xs_hbm.shape[0] // (bm * ROW_TILE)
        for action in ("start", "wait"):
            for e in range(N_EXPERTS):
                @pl.when(zflag_ref[e] != 0)
                def _(e=e, action=action):
                    getattr(fill_copy(zstart_ref[e]), action)()

            def trailing(blk, _, action=action):
                getattr(fill_copy(blk * bm), action)()
                return 0

            lax.fori_loop(nused_ref[0], n_blocks, trailing, 0)

    def tile_done(stage, parity):
        for _ in range(TOP_K):
            pltpu.make_async_copy(stage, xs_hbm.at[pl.ds(0, tm * ROW_TILE), :],
                                  sem.at[parity]).wait()

    def run(stage, other, parity):
        stage[...] = hf_ref[...]

        def issue(t, _):
            src = stage.at[pl.ds(pl.multiple_of(t * ROW_TILE, ROW_TILE), ROW_TILE), :]
            for k in range(TOP_K):
                row = pl.multiple_of(dest_ref[t * TOP_K + k] * ROW_TILE, ROW_TILE)
                pltpu.make_async_copy(src, xs_hbm.at[pl.ds(row, ROW_TILE), :],
                                      sem.at[parity]).start(priority=k % 2)
            return 0

        lax.fori_loop(0, tm, issue, 0)

        @pl.when(s > 0)
        def _():
            tile_done(other, 1 - parity)

        @pl.when(s == pl.num_programs(0) - 1)
        def _():
            tile_done(stage, parity)

    @pl.when(s % 2 == 0)
    def _():
        run(stage0, stage1, 0)

    @pl.when(s % 2 == 1)
    def _():
        run(stage1, stage0, 1)


def _dispatch(zstart, zflag, n_used, dest_flat, hf_rows, *, tm, bm, n_rows):
    n_tok = hf_rows.shape[0] // ROW_TILE
    grid_spec = pltpu.PrefetchScalarGridSpec(
        num_scalar_prefetch=3,
        grid=(n_tok // tm,),
        in_specs=[
            pl.BlockSpec((tm * TOP_K,), lambda i, *_: (i,), memory_space=pltpu.SMEM),
            pl.BlockSpec((tm * ROW_TILE, LANES), lambda i, *_: (i, 0)),
        ],
        out_specs=pl.BlockSpec(memory_space=pl.ANY),
        scratch_shapes=[pltpu.VMEM((tm * ROW_TILE, LANES), U32),
                        pltpu.VMEM((tm * ROW_TILE, LANES), U32),
                        pltpu.VMEM((bm * ROW_TILE, LANES), U32),
                        pltpu.SemaphoreType.DMA((2,)), pltpu.SemaphoreType.DMA(())],
    )
    return pl.pallas_call(
        functools.partial(_dispatch_kernel, tm=tm, bm=bm),
        grid_spec=grid_spec,
        out_shape=jax.ShapeDtypeStruct((n_rows * ROW_TILE, LANES), U32),
        compiler_params=_params(("arbitrary",)),
        name="dispatch",
    )(zstart, zflag, n_used, dest_flat, hf_rows)


FFN_UP_CHUNKS = 8
FFN_DOWN_CHUNKS = 4


def _expert_kernel(be_ref, valid_ref, src_ref, xs_ref, wgu_ref, bgu_ref, wd_ref, bd_ref, ys_hbm,
                   wgu_bf, wd_bf, ybuf0, ybuf1, slot_v, slot_s, sem_y, sem_s, sem_z, *, d_ff, bm,
                   n_tok):
    r = pl.program_id(0)
    valid = valid_ref[r] != 0
    prev_valid = jnp.logical_and(r > 0, valid_ref[jnp.maximum(r - 1, 0)] != 0)
    fresh = jnp.logical_or(r == 0, be_ref[r] != be_ref[jnp.maximum(r - 1, 0)])
    ybufs = (ybuf0, ybuf1)
    d = wd_bf.shape[1]
    n_slots = TOP_K * n_tok
    block_rows = bm * ROW_TILE

    def rows_done(parity):
        return pltpu.make_async_copy(ybufs[parity], ys_hbm.at[pl.ds(0, block_rows), :],
                                     sem_y.at[parity])

    def slots_copy(parity):
        return pltpu.make_async_copy(slot_v.at[0:1, :], slot_s.at[parity:parity + 1, :],
                                     sem_s.at[parity])

    def scatter_rows(prev, lo, hi):
        for i in range(lo, hi):
            dst = pl.multiple_of(slot_s[prev, i] * ROW_TILE, ROW_TILE)
            pltpu.make_async_copy(ybufs[prev].at[pl.ds(i * ROW_TILE, ROW_TILE), :],
                                  ys_hbm.at[pl.ds(dst, ROW_TILE), :],
                                  sem_y.at[prev]).start(priority=i % 2)

    @pl.when(r == 0)
    def _():
        for action in ("start", "wait"):
            for parity in range(2):
                if action == "start":
                    ybufs[parity][...] = jnp.zeros_like(ybufs[parity])
                trash = (n_slots + parity * bm) * ROW_TILE
                getattr(pltpu.make_async_copy(ybufs[parity],
                                              ys_hbm.at[pl.ds(trash, block_rows), :], sem_z),
                        action)()

    for parity in range(2):
        @pl.when(jnp.logical_and(prev_valid, r % 2 == parity))
        def _(parity=parity):
            slots_copy(1 - parity).wait()

    @pl.when(jnp.logical_and(valid, fresh))
    def _():
        wgu_bf[...] = wgu_ref[0].astype(BF16)
        wd_bf[...] = wd_ref[0].astype(BF16)

    def ffn(cur, interleave):
        prev = 1 - cur
        ybuf = ybufs[cur]
        bounds = [bm * c // FFN_UP_CHUNKS for c in range(FFN_UP_CHUNKS + 1)]

        @pl.when(r >= 2)
        def _():
            rows_done(cur).wait()

        meta = xs_ref[pl.ds(META_SUBROW, bm, stride=ROW_TILE), :].astype(I32)
        row = lax.broadcasted_iota(I32, (bm, 1), 0)
        slot = n_slots + cur * bm + row
        for k in range(TOP_K):
            slot = jnp.where(meta[:, 1 + k:2 + k] == be_ref[r], k * n_tok + meta[:, 0:1], slot)
        slots = jnp.broadcast_to(slot.astype(F32), (bm, LANES)).T
        slot_v[...] = slots[0:8, :].astype(I32)
        slots_copy(cur).start()

        words = jnp.concatenate([xs_ref[pl.ds(j, bm, stride=ROW_TILE), :]
                                 for j in range(PACKED_SUBROWS)], axis=1)
        xb = _unpack_bf16_pairs(words)
        up_w = 2 * d_ff // FFN_UP_CHUNKS
        parts = []
        for c in range(FFN_UP_CHUNKS):
            cols = slice(c * up_w, (c + 1) * up_w)
            parts.append(jnp.dot(xb, wgu_bf[:, cols], preferred_element_type=F32)
                         + bgu_ref[0][:, cols])
            if interleave:
                scatter_rows(prev, bounds[c], bounds[c + 1])
        hgu = jnp.concatenate(parts, axis=1)
        glu = jnp.minimum(hgu[:, :d_ff], SWIGLU_LIMIT)
        lin = jnp.clip(hgu[:, d_ff:], -SWIGLU_LIMIT, SWIGLU_LIMIT)
        act = (glu * jax.nn.sigmoid(SWIGLU_ALPHA * glu) * (lin + 1.0)).astype(BF16)
        down_w = d // FFN_DOWN_CHUNKS
        for c in range(FFN_DOWN_CHUNKS):
            cols = slice(c * down_w, (c + 1) * down_w)
            y = jnp.dot(act, wd_bf[:, cols], preferred_element_type=F32) + bd_ref[0][:, cols]
            for j in range(down_w // LANES):
                sub = c * (down_w // LANES) + j
                ybuf[pl.ds(sub, bm, stride=ROW_TILE), :] = y[:, j * LANES:(j + 1) * LANES]

    for parity in range(2):
        on_parity = r % 2 == parity

        @pl.when(jnp.logical_and(on_parity, jnp.logical_and(valid, prev_valid)))
        def _(parity=parity):
            ffn(parity, True)

        @pl.when(jnp.logical_and(on_parity, jnp.logical_and(jnp.logical_not(valid), prev_valid)))
        def _(parity=parity):
            scatter_rows(1 - parity, 0, bm)

            @pl.when(r >= 2)
            def _():
                rows_done(parity).wait()

            rows_done(1 - parity).wait()

    @pl.when(r == 0)
    def _():
        ffn(0, False)


def _expert_ffn(blk_expert, blk_valid, blk_src, xs, w_gu, b_gu, w_down, b_down, *, bm, n_tok):
    n_grid = blk_expert.shape[0]
    d_ff, d = w_down.shape[1:]
    assert d == ROW_TILE * LANES
    block_rows = bm * ROW_TILE
    grid_spec = pltpu.PrefetchScalarGridSpec(
        num_scalar_prefetch=3,
        grid=(n_grid,),
        in_specs=[
            pl.BlockSpec((block_rows, LANES), lambda r, be, valid, src: (src[r], 0)),
            pl.BlockSpec((1, d, 2 * d_ff), lambda r, be, *_: (be[r], 0, 0)),
            pl.BlockSpec((1, 1, 2 * d_ff), lambda r, be, *_: (be[r], 0, 0)),
            pl.BlockSpec((1, d_ff, d), lambda r, be, *_: (be[r], 0, 0)),
            pl.BlockSpec((1, 1, d), lambda r, be, *_: (be[r], 0, 0)),
        ],
        out_specs=pl.BlockSpec(memory_space=pl.ANY),
        scratch_shapes=[pltpu.VMEM((d, 2 * d_ff), BF16), pltpu.VMEM((d_ff, d), BF16),
                        pltpu.VMEM((block_rows, LANES), F32),
                        pltpu.VMEM((block_rows, LANES), F32), pltpu.VMEM((8, bm), I32),
                        pltpu.SMEM((2, bm), I32), pltpu.SemaphoreType.DMA((2,)),
                        pltpu.SemaphoreType.DMA((2,)), pltpu.SemaphoreType.DMA(())],
    )
    return pl.pallas_call(
        functools.partial(_expert_kernel, d_ff=d_ff, bm=bm, n_tok=n_tok),
        grid_spec=grid_spec,
        out_shape=jax.ShapeDtypeStruct(((TOP_K * n_tok + 2 * bm) * ROW_TILE, LANES), F32),
        compiler_params=_params(("arbitrary",)),
        name="expert_ffn",
    )(blk_expert, blk_valid, blk_src, xs, w_gu, b_gu, w_down, b_down)


def _combine_kernel(x_ref, gate_ref, g_ref, y0_ref, y1_ref, y2_ref, y3_ref, o_ref):
    tm, d = x_ref.shape
    gates = gate_ref[...]
    acc = x_ref[...]
    for k, y_ref in enumerate((y0_ref, y1_ref, y2_ref, y3_ref)):
        y = jnp.concatenate([y_ref[pl.ds(j, tm, stride=ROW_TILE), :]
                             for j in range(d // LANES)], axis=1)
        acc = acc + gates[:, k:k + 1] * y
    o_ref[...] = _rms(acc, g_ref[...])


def _combine(x2, gates, g_final, ys, *, tm):
    n_tok, d = x2.shape
    tiles = n_tok // tm
    slot_spec = lambda k: pl.BlockSpec((tm * ROW_TILE, LANES), lambda i: (k * tiles + i, 0))
    return pl.pallas_call(
        _combine_kernel,
        grid=(tiles,),
        in_specs=[
            pl.BlockSpec((tm, d), lambda i: (i, 0)),
            pl.BlockSpec((tm, LANES), lambda i: (i, 0)),
            pl.BlockSpec((1, d), lambda i: (0, 0)),
        ] + [slot_spec(k) for k in range(TOP_K)],
        out_specs=pl.BlockSpec((tm, d), lambda i: (i, 0)),
        out_shape=jax.ShapeDtypeStruct((n_tok, d), F32),
        compiler_params=_params(("arbitrary",)),
        name="combine",
    )(x2, gates, g_final, ys, ys, ys, ys)


def _routing_tables(route, cum, *, bm, n_grid):
    expert = route[:, 0:TOP_K]
    pos = route[:, TOP_K:2 * TOP_K]
    experts = jnp.arange(N_EXPERTS, dtype=I32)
    counts = cum[-1, 0, :N_EXPERTS].astype(I32)
    padded = (counts + bm - 1) // bm * bm
    pend = jnp.cumsum(padded)
    pstart = pend - padded
    dest = pos + jnp.sum(jnp.where(expert[:, :, None] == experts, pstart, 0), axis=2)

    blk = jnp.arange(n_grid, dtype=I32)
    blk_expert = jnp.minimum(jnp.sum(pend[None, :] <= (blk * bm)[:, None], axis=1),
                             N_EXPERTS - 1).astype(I32)
    n_used = pend[-1] // bm
    blk_valid = (blk < n_used).astype(I32)
    blk_src = jnp.minimum(blk, jnp.maximum(n_used - 1, 0)).astype(I32)
    zflag = (padded > 0).astype(I32)
    zstart = jnp.maximum(pend - bm, 0).astype(I32)
    return (dest.astype(I32).reshape(-1), blk_expert, blk_valid, blk_src, zstart, zflag,
            n_used.astype(I32).reshape(1))


def kernel(x, mem, positions, norm_mix, w_in, lambda_q1, lambda_k1, lambda_q2, lambda_k2,
           diff_subln, w_up_sb, w_up_diff, w_out, norm_mem_q, norm_mem_kv, w_mem_q, w_mem_kv,
           w_mem_o, norm_ffn, w_router, b_router, w_gate_up, b_gate_up, w_down, b_down,
           norm_final):
    b, s, d = x.shape
    n_tok = b * s
    depth = norm_mix.shape[0]
    sb_width = w_up_sb.shape[1]
    diff_width = w_up_diff.shape[1]
    n_in = w_in.shape[2]
    chunk = 512
    assert sb_width == chunk and diff_width == chunk and d == 2 * chunk
    sbq, sbk, sbv, dq, dk, dv = range(6)
    blocks_per_chunk = chunk // LANES
    scale = 1.0 / math.sqrt(SB_HEAD_DIM)
    assert SB_HEAD_DIM == DIFF_HEAD_DIM

    tm_in = min(512, n_tok)
    tq_sb = min(256, s)
    tq_diff = min(512, s)
    tm_post = min(512, s)
    tm_tok = min(256, n_tok)
    bm = 256
    n_grid = (n_tok * TOP_K) // bm + N_EXPERTS

    cos_t, sin_t = _rope_tables(positions)
    x2d = x.reshape(n_tok, d)
    for l in range(depth):
        lambda_init = 0.8 - 0.6 * math.exp(-0.3 * l)
        proj = _in_proj(x2d, norm_mix[l].reshape(1, d), w_in[l].astype(BF16), cos_t, sin_t,
                        tm=tm_in, chunk=chunk, rope_chunks=(dq, dk), scale_chunks=(sbq, dq),
                        scale=scale)
        proj3 = proj.reshape(b, s, n_in)
        o_sb = _sb_attention(proj3, q_col=sbq * blocks_per_chunk, k_col=sbk * blocks_per_chunk,
                             v_col=sbv * blocks_per_chunk, n_pairs=sb_width // LANES, tq=tq_sb)
        o_diff = _diff_attention(
            proj3, lambda_q1[l].reshape(1, -1), lambda_k1[l].reshape(1, -1),
            lambda_q2[l].reshape(1, -1), lambda_k2[l].reshape(1, -1),
            diff_subln[l].reshape(1, -1), q_col=dq * blocks_per_chunk,
            k_col=dk * blocks_per_chunk, v_col=dv * blocks_per_chunk,
            n_heads=diff_width // DIFF_V_DIM, tq=tq_diff, lambda_init=lambda_init)
        kv = _mem_kv(mem, norm_mem_kv[l].reshape(1, d), w_mem_kv[l].astype(BF16))
        w_router_pad = jnp.zeros((d, LANES), F32).at[:, :N_EXPERTS].set(w_router[l])
        b_router_pad = jnp.full((1, LANES), NEG_BIG, F32).at[0, :N_EXPERTS].set(b_router[l])
        x_res, hf, route, gates, cum = _post_mix(
            x2d, proj, o_sb.reshape(n_tok, sb_width), o_diff.reshape(n_tok, diff_width),
            w_up_sb[l].astype(BF16), w_up_diff[l].astype(BF16), w_out[l].astype(BF16),
            norm_mem_q[l].reshape(1, d), w_mem_q[l].astype(BF16), kv, w_mem_o[l].astype(BF16),
            norm_ffn[l].reshape(1, d), w_router_pad, b_router_pad,
            tm=tm_post, seq=s, gate_col=6 * chunk // d)
        dest, blk_expert, blk_valid, blk_src, zstart, zflag, n_used = _routing_tables(
            route, cum, bm=bm, n_grid=n_grid + 1)
        xs = _dispatch(zstart, zflag, n_used, dest, hf, tm=tm_tok, bm=bm, n_rows=n_grid * bm)
        ys = _expert_ffn(blk_expert, blk_valid, blk_src, xs, w_gate_up[l],
                         b_gate_up[l][:, None, :], w_down[l], b_down[l][:, None, :], bm=bm,
                         n_tok=n_tok)
        g_next = norm_final if l == depth - 1 else None
        assert g_next is not None, "only depth 1 is wired: the final norm is fused into combine"
        x2d = _combine(x_res, gates, g_next.reshape(1, d), ys, tm=tm_tok)
    return x2d.reshape(b, s, d)
```

```python
import functools
import math

import jax
import jax.numpy as jnp
from jax import lax
from jax.experimental import pallas as pl
from jax.experimental.pallas import tpu as pltpu

F32 = jnp.float32
BF16 = jnp.bfloat16
I32 = jnp.int32
U32 = jnp.uint32

LANES = 128
VMEM_LIMIT_BYTES = 56 * 1024 * 1024
ROW_TILE = 8
PACKED_SUBROWS = 4
META_SUBROW = 4

NORM_EPS = 1e-6
ROPE_THETA = 10000.0
CHUNK = 64
SB_HEAD_DIM = 64
DIFF_HEAD_DIM = 64
DIFF_V_DIM = 128
MEM_HEADS = 4
N_EXPERTS = 32
TOP_K = 4
SWIGLU_LIMIT = 7.0
SWIGLU_ALPHA = 1.702

SB_DEAD_LOG = -105.0
NEG_BIG = -1e30


def _params(semantics):
    return pltpu.CompilerParams(dimension_semantics=semantics,
                                vmem_limit_bytes=VMEM_LIMIT_BYTES)


def _const_spec(shape):
    nd = len(shape)
    return pl.BlockSpec(shape, lambda *_: (0,) * nd)


def _rms(x, g):
    return x * lax.rsqrt(jnp.mean(x * x, axis=-1, keepdims=True) + NORM_EPS) * g


def _rope_table_kernel(pos_ref, inv_ref, cos_ref, sin_ref):
    ang = pos_ref[...] * inv_ref[...]
    cos_ref[...] = jnp.cos(ang)
    sin_ref[...] = jnp.sin(ang)


def _rope_tables(positions):
    n_tok = positions.size
    half = DIFF_HEAD_DIM // 2
    per_row = LANES // half
    rows = n_tok // per_row
    pos = jnp.repeat(positions.reshape(rows, per_row).astype(F32), half, axis=1)
    inv = ROPE_THETA ** (-jnp.arange(half, dtype=F32) / half)
    inv = jnp.tile(inv, per_row).reshape(1, LANES)
    tr = min(rows, 512)
    cos, sin = pl.pallas_call(
        _rope_table_kernel,
        grid=(rows // tr,),
        in_specs=[pl.BlockSpec((tr, LANES), lambda i: (i, 0)), _const_spec((1, LANES))],
        out_specs=[pl.BlockSpec((tr, LANES), lambda i: (i, 0))] * 2,
        out_shape=[jax.ShapeDtypeStruct((rows, LANES), F32)] * 2,
        compiler_params=_params(("arbitrary",)),
        name="rope_table",
    )(pos, inv)
    cos = cos.reshape(n_tok, half)
    sin = sin.reshape(n_tok, half)
    cos_t = jnp.tile(cos, (1, LANES // half))
    sin_t = jnp.tile(jnp.concatenate([-sin, sin], axis=1), (1, LANES // (2 * half)))
    return cos_t, sin_t


def _inproj_kernel(x_ref, g_ref, w_ref, cos_ref, sin_ref, o_ref, *, chunk, rope_chunks,
                   scale_chunks, scale):
    h = _rms(x_ref[...], g_ref[...]).astype(BF16)
    lane = lax.broadcasted_iota(I32, (1, chunk), 1)
    first_half = (lane % DIFF_HEAD_DIM) < (DIFF_HEAD_DIM // 2)
    for c in range(w_ref.shape[1] // chunk):
        cols = slice(c * chunk, (c + 1) * chunk)
        acc = jnp.dot(h, w_ref[:, cols], preferred_element_type=F32)
        if c in rope_chunks:
            cos = jnp.tile(cos_ref[...], (1, chunk // LANES))
            sin = jnp.tile(sin_ref[...], (1, chunk // LANES))
            partner = jnp.where(first_half,
                                pltpu.roll(acc, chunk - DIFF_HEAD_DIM // 2, 1),
                                pltpu.roll(acc, DIFF_HEAD_DIM // 2, 1))
            acc = acc * cos + partner * sin
        if c in scale_chunks:
            acc = acc * scale
        o_ref[:, cols] = acc.astype(BF16)


def _in_proj(x2d, g, w_bf16, cos_t, sin_t, *, tm, chunk, rope_chunks, scale_chunks, scale):
    n_tok, d = x2d.shape
    n_in = w_bf16.shape[1]
    kern = functools.partial(_inproj_kernel, chunk=chunk, rope_chunks=rope_chunks,
                             scale_chunks=scale_chunks, scale=scale)
    return pl.pallas_call(
        kern,
        grid=(n_tok // tm,),
        in_specs=[pl.BlockSpec((tm, d), lambda i: (i, 0)),
                  _const_spec((1, d)),
                  _const_spec((d, n_in)),
                  pl.BlockSpec((tm, LANES), lambda i: (i, 0)),
                  pl.BlockSpec((tm, LANES), lambda i: (i, 0))],
        out_specs=pl.BlockSpec((tm, n_in), lambda i: (i, 0)),
        out_shape=jax.ShapeDtypeStruct((n_tok, n_in), BF16),
        compiler_params=_params(("arbitrary",)),
        name="in_proj",
    )(x2d, g, w_bf16, cos_t, sin_t)


def _sb_kernel(q_ref, k_ref, v_ref, o_ref, acc_ref, carry_ref, *, tq):
    i = pl.program_id(2)
    q = q_ref[0]
    lane = lax.broadcasted_iota(I32, (1, LANES), 1)
    zero = jnp.zeros_like(q)
    q2 = jnp.concatenate([jnp.where(lane < SB_HEAD_DIM, q, zero),
                          jnp.where(lane >= SB_HEAD_DIM, q, zero)], axis=0)
    uj = lax.broadcasted_iota(I32, (2 * tq, tq), 0) % tq
    us = lax.broadcasted_iota(I32, (2 * tq, tq), 1)
    suffix = jnp.where(uj > us, 1.0, 0.0).astype(BF16)

    def block(j, masked):
        start = pl.multiple_of(j * tq, tq)
        kj = k_ref[0, pl.ds(start, tq), :]
        vj = v_ref[0, pl.ds(start, tq), :]
        z = lax.dot_general(q2, kj, (((1,), (1,)), ((), ())), preferred_element_type=F32)
        sp = jnp.maximum(z, 0.0) + jnp.log(1.0 + jnp.exp(-jnp.abs(z)))
        log_rem = -sp
        if masked:
            causal = (lax.broadcasted_iota(I32, (2 * tq, tq), 1)
                      < lax.broadcasted_iota(I32, (2 * tq, tq), 0) % tq)
            log_rem = jnp.where(causal, log_rem, 0.0)
        hi = log_rem.astype(BF16)
        lo = (log_rem - hi.astype(F32)).astype(BF16)
        after = jnp.dot(jnp.concatenate([hi, lo], axis=1), suffix, preferred_element_type=F32)
        carry = carry_ref[...]
        w = jnp.exp(z - sp + after + carry)
        if masked:
            w = jnp.where(causal, w, 0.0)
        acc_ref[...] += jnp.dot(w.astype(BF16), vj, preferred_element_type=F32)
        carry = carry + after[:, 0:1] + log_rem[:, 0:1]
        carry_ref[...] = carry
        return jnp.max(carry)

    acc_ref[...] = jnp.zeros_like(acc_ref)
    carry_ref[...] = jnp.zeros_like(carry_ref)
    alive = block(i, True)

    def cond(state):
        j, alive = state
        return jnp.logical_and(j >= 0, alive > SB_DEAD_LOG)

    def body(state):
        j, _ = state
        return j - 1, block(j, False)

    lax.while_loop(cond, body, (i - 1, alive))
    o_ref[0] = jnp.where(lane < SB_HEAD_DIM, acc_ref[0:tq, :], acc_ref[tq:2 * tq, :]).astype(BF16)


def _sb_attention(proj3, *, q_col, k_col, v_col, n_pairs, tq):
    b, s, _ = proj3.shape
    return pl.pallas_call(
        functools.partial(_sb_kernel, tq=tq),
        grid=(b, n_pairs, s // tq),
        in_specs=[pl.BlockSpec((1, tq, LANES), lambda bi, p, i: (bi, i, q_col + p)),
                  pl.BlockSpec((1, s, LANES), lambda bi, p, i: (bi, 0, k_col + p)),
                  pl.BlockSpec((1, s, LANES), lambda bi, p, i: (bi, 0, v_col + p))],
        out_specs=pl.BlockSpec((1, tq, LANES), lambda bi, p, i: (bi, i, p)),
        out_shape=jax.ShapeDtypeStruct((b, s, n_pairs * LANES), BF16),
        scratch_shapes=[pltpu.VMEM((2 * tq, LANES), F32), pltpu.VMEM((2 * tq, 1), F32)],
        compiler_params=_params(("arbitrary", "arbitrary", "arbitrary")),
        name="sb_attention",
    )(proj3, proj3, proj3)


def _diff_kernel(lq1_ref, lk1_ref, lq2_ref, lk2_ref, q_ref, k_ref, v_ref, g_ref, o_ref, vt_ref,
                 acc_ref, *, tq, lambda_init):
    i = pl.program_id(2)
    s_len = v_ref.shape[1]

    @pl.when(i == 0)
    def _():
        for c in range(s_len // tq):
            cols = slice(c * tq, (c + 1) * tq)
            vt_ref[:, cols] = v_ref[0, cols, :].astype(F32).T.astype(BF16)

    q = q_ref[0]
    lane = lax.broadcasted_iota(I32, (1, LANES), 1)
    zero = jnp.zeros_like(q)
    q2 = jnp.concatenate([jnp.where(lane < DIFF_HEAD_DIM, q, zero),
                          jnp.where(lane >= DIFF_HEAD_DIM, q, zero)], axis=0)

    def step(j, stats, masked):
        m, l = stats
        start = pl.multiple_of(j * tq, tq)
        kj = k_ref[0, pl.ds(start, tq), :]
        vtj = vt_ref[:, pl.ds(start, tq)]
        zt = lax.dot_general(kj, q2, (((1,), (1,)), ((), ())), preferred_element_type=F32)
        if masked:
            visible = (lax.broadcasted_iota(I32, (tq, 2 * tq), 0) // CHUNK
                       <= (lax.broadcasted_iota(I32, (tq, 2 * tq), 1) % tq) // CHUNK)
            zt = jnp.where(visible, zt, -jnp.inf)
        m_new = jnp.maximum(m, jnp.max(zt, axis=0, keepdims=True))
        alpha = jnp.exp(m - m_new)
        pt = jnp.exp(zt - m_new)
        l = alpha * l + jnp.sum(pt, axis=0, keepdims=True)
        acc_ref[...] = alpha * acc_ref[...] + jnp.dot(vtj, pt.astype(BF16),
                                                      preferred_element_type=F32)
        return m_new, l

    acc_ref[...] = jnp.zeros_like(acc_ref)
    init = (jnp.full((1, 2 * tq), -jnp.inf, F32), jnp.zeros((1, 2 * tq), F32))
    stats = lax.fori_loop(0, i, lambda j, st: step(j, st, False), init)
    _, l = step(i, stats, True)
    lam = (jnp.exp(jnp.sum(lq1_ref[...] * lk1_ref[...], axis=1, keepdims=True))
           - jnp.exp(jnp.sum(lq2_ref[...] * lk2_ref[...], axis=1, keepdims=True))
           + lambda_init)
    ot = acc_ref[...] / l
    ot = ot[:, 0:tq] - lam * ot[:, tq:2 * tq]
    o_ref[0] = (_rms(ot.T, g_ref[...]) * (1.0 - lambda_init)).astype(BF16)


def _diff_attention(proj3, lq1, lk1, lq2, lk2, subln, *, q_col, k_col, v_col, n_heads, tq,
                    lambda_init):
    b, s, _ = proj3.shape
    lam_spec = _const_spec((1, DIFF_HEAD_DIM))
    return pl.pallas_call(
        functools.partial(_diff_kernel, tq=tq, lambda_init=lambda_init),
        grid=(b, n_heads, s // tq),
        in_specs=[lam_spec, lam_spec, lam_spec, lam_spec,
                  pl.BlockSpec((1, tq, LANES), lambda bi, h, i: (bi, i, q_col + h)),
                  pl.BlockSpec((1, s, LANES), lambda bi, h, i: (bi, 0, k_col + h)),
                  pl.BlockSpec((1, s, LANES), lambda bi, h, i: (bi, 0, v_col + h)),
                  _const_spec((1, DIFF_V_DIM))],
        out_specs=pl.BlockSpec((1, tq, LANES), lambda bi, h, i: (bi, i, h)),
        out_shape=jax.ShapeDtypeStruct((b, s, n_heads * DIFF_V_DIM), BF16),
        scratch_shapes=[pltpu.VMEM((DIFF_V_DIM, s), BF16), pltpu.VMEM((DIFF_V_DIM, 2 * tq), F32)],
        compiler_params=_params(("arbitrary", "arbitrary", "arbitrary")),
        name="diff_attention",
    )(lq1, lk1, lq2, lk2, proj3, proj3, proj3, subln)


def _mem_kv_kernel(mem_ref, g_ref, w_ref, o_ref):
    h = _rms(mem_ref[0], g_ref[...]).astype(BF16)
    o_ref[0] = jnp.dot(h, w_ref[...], preferred_element_type=F32).astype(BF16)


def _mem_kv(mem, g, w_bf16):
    b, m, d = mem.shape
    n = w_bf16.shape[1]
    return pl.pallas_call(
        _mem_kv_kernel,
        grid=(b,),
        in_specs=[pl.BlockSpec((1, m, d), lambda i: (i, 0, 0)), _const_spec((1, d)),
                  _const_spec((d, n))],
        out_specs=pl.BlockSpec((1, m, n), lambda i: (i, 0, 0)),
        out_shape=jax.ShapeDtypeStruct((b, m, n), BF16),
        compiler_params=_params(("arbitrary",)),
        name="mem_kv",
    )(mem, g, w_bf16)


def _split_bf16(v):
    hi = v.astype(BF16)
    return hi, (v - hi.astype(F32)).astype(BF16)


def _pack_bf16_pairs(v):
    half = v.shape[1] // 2
    bits = lax.bitcast_convert_type(v.astype(BF16).astype(F32), U32)
    return (bits[:, :half] >> 16) | (bits[:, half:] & jnp.uint32(0xFFFF0000))


def _unpack_bf16_pairs(w):
    lo = lax.bitcast_convert_type(w << 16, F32)
    hi = lax.bitcast_convert_type(w & jnp.uint32(0xFFFF0000), F32)
    return jnp.concatenate([lo, hi], axis=1).astype(BF16)


def _postmix_kernel(x_ref, osb_ref, odf_ref, gsb_ref, gdf_ref, wus_ref, wud_ref, wout_ref,
                    gq_ref, wq_ref, kv_ref, wo_ref, gf_ref, wr_ref, br_ref,
                    x2_ref, hf_ref, route_ref, gate_ref, cum_ref, count_ref, *, tm, d_model):
    step = pl.program_id(0)

    @pl.when(step == 0)
    def _():
        count_ref[...] = jnp.zeros_like(count_ref)

    y_sb = jnp.dot(osb_ref[...], wus_ref[...], preferred_element_type=F32)
    y_df = jnp.dot(odf_ref[...], wud_ref[...], preferred_element_type=F32)
    mixed = (jax.nn.sigmoid(gsb_ref[...].astype(F32)) * y_sb
             + jax.nn.sigmoid(gdf_ref[...].astype(F32)) * y_df)
    x1 = x_ref[...] + jnp.dot(mixed.astype(BF16), wout_ref[...], preferred_element_type=F32)

    hq = _rms(x1, gq_ref[...]).astype(BF16)
    hd = d_model // MEM_HEADS
    q = jnp.dot(hq, wq_ref[...], preferred_element_type=F32) * (1.0 / math.sqrt(hd))
    q = q.astype(BF16)
    heads = []
    for h in range(MEM_HEADS):
        kh = kv_ref[0, :, h * hd:(h + 1) * hd]
        vh = kv_ref[0, :, d_model + h * hd:d_model + (h + 1) * hd]
        z = lax.dot_general(q[:, h * hd:(h + 1) * hd], kh, (((1,), (1,)), ((), ())),
                            preferred_element_type=F32)
        p = jnp.exp(z - jnp.max(z, axis=1, keepdims=True))
        l = jnp.sum(p, axis=1, keepdims=True)
        heads.append((jnp.dot(p.astype(BF16), vh, preferred_element_type=F32) / l).astype(BF16))
    x2 = x1 + jnp.dot(jnp.concatenate(heads, axis=1), wo_ref[...], preferred_element_type=F32)
    x2_ref[...] = x2

    hf = _rms(x2, gf_ref[...])
    packed = _pack_bf16_pairs(hf)
    for j in range(PACKED_SUBROWS):
        hf_ref[pl.ds(j, tm, stride=ROW_TILE), :] = packed[:, j * LANES:(j + 1) * LANES]
    h_hi, h_lo = _split_bf16(hf)
    w_hi, w_lo = _split_bf16(wr_ref[...])
    logits = (jnp.dot(h_hi, w_hi, preferred_element_type=F32)
              + jnp.dot(h_hi, w_lo, preferred_element_type=F32)
              + jnp.dot(h_lo, w_hi, preferred_element_type=F32)) + br_ref[...]
    lane = lax.broadcasted_iota(I32, (tm, LANES), 1)
    work = logits
    vals, idxs, hots = [], [], []
    for _ in range(TOP_K):
        mx = jnp.max(work, axis=1, keepdims=True)
        idx = jnp.min(jnp.where(work == mx, lane, LANES), axis=1, keepdims=True)
        hot = lane == idx
        work = jnp.where(hot, NEG_BIG, work)
        vals.append(mx)
        idxs.append(idx)
        hots.append(hot)
    exps = [jnp.exp(v - vals[0]) for v in vals]
    denom = exps[0] + exps[1] + exps[2] + exps[3]

    onehot_sum = jnp.zeros((tm, LANES), F32)
    for hot in hots:
        onehot_sum = onehot_sum + jnp.where(hot, 1.0, 0.0)
    r = lax.broadcasted_iota(I32, (tm, tm), 0)
    c = lax.broadcasted_iota(I32, (tm, tm), 1)
    lower = jnp.where(c < r, 1.0, 0.0).astype(BF16)
    rank = jnp.dot(lower, onehot_sum.astype(BF16), preferred_element_type=F32) + count_ref[...]
    route = jnp.zeros((tm, LANES), I32)
    gates = jnp.zeros((tm, LANES), F32)
    for k in range(TOP_K):
        pos = jnp.sum(jnp.where(hots[k], rank, 0.0), axis=1, keepdims=True).astype(I32)
        route = jnp.where(lane == k, idxs[k], route)
        route = jnp.where(lane == TOP_K + k, pos, route)
        gates = jnp.where(lane == k, exps[k] / denom, gates)
    route_ref[...] = route
    gate_ref[...] = gates
    meta = jnp.where(lane == 0, step * tm + lax.broadcasted_iota(I32, (tm, LANES), 0), 0)
    for k in range(TOP_K):
        meta = jnp.where(lane == 1 + k, idxs[k], meta)
    hf_ref[pl.ds(META_SUBROW, tm, stride=ROW_TILE), :] = meta.astype(U32)
    for j in range(META_SUBROW + 1, ROW_TILE):
        hf_ref[pl.ds(j, tm, stride=ROW_TILE), :] = jnp.zeros((tm, LANES), U32)
    count_ref[...] = count_ref[...] + jnp.sum(onehot_sum, axis=0, keepdims=True)
    cum_ref[0] = jnp.broadcast_to(count_ref[...], (8, LANES))


def _post_mix(x2d, proj2, o_sb, o_diff, w_up_sb, w_up_diff, w_out, g_memq, w_memq, kv, w_memo,
              g_ffn, w_router_pad, b_router_pad, *, tm, seq, gate_col):
    n_tok, d = x2d.shape
    n_tiles = n_tok // tm
    tiles_per_batch = seq // tm
    row = lambda i: (i, 0)
    in_specs = [
        pl.BlockSpec((tm, d), row),
        pl.BlockSpec((tm, o_sb.shape[1]), row),
        pl.BlockSpec((tm, o_diff.shape[1]), row),
        pl.BlockSpec((tm, d), lambda i: (i, gate_col)),
        pl.BlockSpec((tm, d), lambda i: (i, gate_col + 1)),
        _const_spec(w_up_sb.shape), _const_spec(w_up_diff.shape), _const_spec(w_out.shape),
        _const_spec((1, d)), _const_spec(w_memq.shape),
        pl.BlockSpec((1,) + kv.shape[1:], lambda i: (i // tiles_per_batch, 0, 0)),
        _const_spec(w_memo.shape), _const_spec((1, d)),
        _const_spec(w_router_pad.shape), _const_spec((1, LANES)),
    ]
    out_specs = [
        pl.BlockSpec((tm, d), row),
        pl.BlockSpec((tm * ROW_TILE, LANES), row),
        pl.BlockSpec((tm, LANES), row),
        pl.BlockSpec((tm, LANES), row),
        pl.BlockSpec((1, 8, LANES), lambda i: (i, 0, 0)),
    ]
    assert d // 2 == PACKED_SUBROWS * LANES
    out_shape = [
        jax.ShapeDtypeStruct((n_tok, d), F32),
        jax.ShapeDtypeStruct((n_tok * ROW_TILE, LANES), U32),
        jax.ShapeDtypeStruct((n_tok, LANES), I32),
        jax.ShapeDtypeStruct((n_tok, LANES), F32),
        jax.ShapeDtypeStruct((n_tiles, 8, LANES), F32),
    ]
    return pl.pallas_call(
        functools.partial(_postmix_kernel, tm=tm, d_model=d),
        grid=(n_tiles,),
        in_specs=in_specs,
        out_specs=out_specs,
        out_shape=out_shape,
        scratch_shapes=[pltpu.VMEM((1, LANES), F32)],
        compiler_params=_params(("arbitrary",)),
        name="post_mix",
    )(x2d, o_sb, o_diff, proj2, proj2, w_up_sb, w_up_diff, w_out, g_memq, w_memq, kv, w_memo,
      g_ffn, w_router_pad, b_router_pad)


def _dispatch_kernel(zstart_ref, zflag_ref, nused_ref, dest_ref, hf_ref, xs_hbm, stage0, stage1,
                     zbuf, sem, zsem, *, tm, bm):
    s = pl.program_id(0)

    @pl.when(s == 0)
    def _():
        lane = lax.broadcasted_iota(I32, (bm, LANES), 1)
        zbuf[...] = jnp.zeros_like(zbuf)
        zbuf[pl.ds(META_SUBROW, bm, stride=ROW_TILE), :] = jnp.where(
            jnp.logical_and(lane >= 1, lane <= TOP_K), -1, 0).astype(U32)

        def fill_copy(start):
            start = pl.multiple_of(start * ROW_TILE, bm * ROW_TILE)
            return pltpu.make_async_copy(zbuf, xs_hbm.at[pl.ds(start, bm * ROW_TILE), :], zsem)

        n_blocks = xs_hbm.shape[0] // (bm * ROW_TILE)
        for action in ("start", "wait"):
            for e in range(N_EXPERTS):
                @pl.when(zflag_ref[e] != 0)
                def _(e=e, action=action):
                    getattr(fill_copy(zstart_ref[e]), action)()

            def trailing(blk, _, action=action):
                getattr(fill_copy(blk * bm), action)()
                return 0

            lax.fori_loop(nused_ref[0], n_blocks, trailing, 0)

    def tile_done(stage, parity):
        for _ in range(TOP_K):
            pltpu.make_async_copy(stage, xs_hbm.at[pl.ds(0, tm * ROW_TILE), :],
                                  sem.at[parity]).wait()

    def run(stage, other, parity):
        stage[...] = hf_ref[...]

        def issue(t, _):
            src = stage.at[pl.ds(pl.multiple_of(t * ROW_TILE, ROW_TILE), ROW_TILE), :]
            for k in range(TOP_K):
                row = pl.multiple_of(dest_ref[t * TOP_K + k] * ROW_TILE, ROW_TILE)
                pltpu.make_async_copy(src, xs_hbm.at[pl.ds(row, ROW_TILE), :],
                                      sem.at[parity]).start(priority=k % 2)
            return 0

        lax.fori_loop(0, tm, issue, 0)

        @pl.when(s > 0)
        def _():
            tile_done(other, 1 - parity)

        @pl.when(s == pl.num_programs(0) - 1)
        def _():
            tile_done(stage, parity)

    @pl.when(s % 2 == 0)
    def _():
        run(stage0, stage1, 0)

    @pl.when(s % 2 == 1)
    def _():
        run(stage1, stage0, 1)


def _dispatch(zstart, zflag, n_used, dest_flat, hf_rows, *, tm, bm, n_rows):
    n_tok = hf_rows.shape[0] // ROW_TILE
    grid_spec = pltpu.PrefetchScalarGridSpec(
        num_scalar_prefetch=3,
        grid=(n_tok // tm,),
        in_specs=[
            pl.BlockSpec((tm * TOP_K,), lambda i, *_: (i,), memory_space=pltpu.SMEM),
            pl.BlockSpec((tm * ROW_TILE, LANES), lambda i, *_: (i, 0)),
        ],
        out_specs=pl.BlockSpec(memory_space=pl.ANY),
        scratch_shapes=[pltpu.VMEM((tm * ROW_TILE, LANES), U32),
                        pltpu.VMEM((tm * ROW_TILE, LANES), U32),
                        pltpu.VMEM((bm * ROW_TILE, LANES), U32),
                        pltpu.SemaphoreType.DMA((2,)), pltpu.SemaphoreType.DMA(())],
    )
    return pl.pallas_call(
        functools.partial(_dispatch_kernel, tm=tm, bm=bm),
        grid_spec=grid_spec,
        out_shape=jax.ShapeDtypeStruct((n_rows * ROW_TILE, LANES), U32),
        compiler_params=_params(("arbitrary",)),
        name="dispatch",
    )(zstart, zflag, n_used, dest_flat, hf_rows)


FFN_UP_CHUNKS = 8
FFN_DOWN_CHUNKS = 4


def _expert_kernel(be_ref, valid_ref, src_ref, wslot_ref, next_ref, xs_ref, bgu_ref, bd_ref,
                   wgu_hbm, wd_hbm, ys_hbm, wgu_f32, wd_f32, wgu_bf, wd_bf, ybuf0, ybuf1, slot_v,
                   slot_s, sem_w, sem_y, sem_s, sem_z, *, d_ff, bm, n_tok):
    r = pl.program_id(0)
    valid = valid_ref[r] != 0
    prev_valid = jnp.logical_and(r > 0, valid_ref[jnp.maximum(r - 1, 0)] != 0)
    fresh = jnp.logical_or(r == 0, be_ref[r] != be_ref[jnp.maximum(r - 1, 0)])
    expert = be_ref[r]
    ybufs = (ybuf0, ybuf1)

    def weight_copies(e, wslot):
        return (pltpu.make_async_copy(wgu_hbm.at[e], wgu_f32.at[wslot], sem_w.at[wslot]),
                pltpu.make_async_copy(wd_hbm.at[e], wd_f32.at[wslot], sem_w.at[wslot]))
    d = wd_bf.shape[1]
    n_slots = TOP_K * n_tok
    block_rows = bm * ROW_TILE

    def rows_done(parity):
        return pltpu.make_async_copy(ybufs[parity], ys_hbm.at[pl.ds(0, block_rows), :],
                                     sem_y.at[parity])

    def slots_copy(parity):
        return pltpu.make_async_copy(slot_v.at[0:1, :], slot_s.at[parity:parity + 1, :],
                                     sem_s.at[parity])

    def scatter_rows(prev, lo, hi):
        for i in range(lo, hi):
            dst = pl.multiple_of(slot_s[prev, i] * ROW_TILE, ROW_TILE)
            pltpu.make_async_copy(ybufs[prev].at[pl.ds(i * ROW_TILE, ROW_TILE), :],
                                  ys_hbm.at[pl.ds(dst, ROW_TILE), :],
                                  sem_y.at[prev]).start(priority=i % 2)

    @pl.when(r == 0)
    def _():
        for action in ("start", "wait"):
            for parity in range(2):
                if action == "start":
                    ybufs[parity][...] = jnp.zeros_like(ybufs[parity])
                trash = (n_slots + parity * bm) * ROW_TILE
                getattr(pltpu.make_async_copy(ybufs[parity],
                                              ys_hbm.at[pl.ds(trash, block_rows), :], sem_z),
                        action)()

    for parity in range(2):
        @pl.when(jnp.logical_and(prev_valid, r % 2 == parity))
        def _(parity=parity):
            slots_copy(1 - parity).wait()

    @pl.when(jnp.logical_and(valid, fresh))
    def _():
        wslot = wslot_ref[r]

        @pl.when(r == 0)
        def _():
            for c in weight_copies(expert, wslot):
                c.start()

        for c in weight_copies(expert, wslot):
            c.wait()
        wgu_bf[...] = wgu_f32[wslot].astype(BF16)
        wd_bf[...] = wd_f32[wslot].astype(BF16)

        @pl.when(next_ref[r] >= 0)
        def _():
            for c in weight_copies(next_ref[r], 1 - wslot):
                c.start()

    def ffn(cur, interleave):
        prev = 1 - cur
        ybuf = ybufs[cur]
        bounds = [bm * c // FFN_UP_CHUNKS for c in range(FFN_UP_CHUNKS + 1)]

        @pl.when(r >= 2)
        def _():
            rows_done(cur).wait()

        meta = xs_ref[pl.ds(META_SUBROW, bm, stride=ROW_TILE), :].astype(I32)
        row = lax.broadcasted_iota(I32, (bm, 1), 0)
        slot = n_slots + cur * bm + row
        for k in range(TOP_K):
            slot = jnp.where(meta[:, 1 + k:2 + k] == be_ref[r], k * n_tok + meta[:, 0:1], slot)
        slots = jnp.broadcast_to(slot.astype(F32), (bm, LANES)).T
        slot_v[...] = slots[0:8, :].astype(I32)
        slots_copy(cur).start()

        words = jnp.concatenate([xs_ref[pl.ds(j, bm, stride=ROW_TILE), :]
                                 for j in range(PACKED_SUBROWS)], axis=1)
        xb = _unpack_bf16_pairs(words)
        up_w = 2 * d_ff // FFN_UP_CHUNKS
        pairs = FFN_UP_CHUNKS // 2
        acts = []
        for c in range(pairs):
            halves = []
            for half in range(2):
                cols = slice(half * d_ff + c * up_w, half * d_ff + (c + 1) * up_w)
                halves.append(jnp.dot(xb, wgu_bf[:, cols], preferred_element_type=F32)
                              + bgu_ref[pl.ds(expert, 1), cols])
                if interleave:
                    step_no = 2 * c + half
                    scatter_rows(prev, bounds[step_no], bounds[step_no + 1])
            glu = jnp.minimum(halves[0], SWIGLU_LIMIT)
            lin = jnp.clip(halves[1], -SWIGLU_LIMIT, SWIGLU_LIMIT)
            acts.append((glu * jax.nn.sigmoid(SWIGLU_ALPHA * glu) * (lin + 1.0)).astype(BF16))
        act = jnp.concatenate(acts, axis=1)
        down_w = d // FFN_DOWN_CHUNKS
        for c in range(FFN_DOWN_CHUNKS):
            cols = slice(c * down_w, (c + 1) * down_w)
            y = (jnp.dot(act, wd_bf[:, cols], preferred_element_type=F32)
                 + bd_ref[pl.ds(expert, 1), cols])
            for j in range(down_w // LANES):
                sub = c * (down_w // LANES) + j
                ybuf[pl.ds(sub, bm, stride=ROW_TILE), :] = y[:, j * LANES:(j + 1) * LANES]

    for parity in range(2):
        on_parity = r % 2 == parity

        @pl.when(jnp.logical_and(on_parity, jnp.logical_and(valid, prev_valid)))
        def _(parity=parity):
            ffn(parity, True)

        @pl.when(jnp.logical_and(on_parity, jnp.logical_and(jnp.logical_not(valid), prev_valid)))
        def _(parity=parity):
            scatter_rows(1 - parity, 0, bm)

            @pl.when(r >= 2)
            def _():
                rows_done(parity).wait()

            rows_done(1 - parity).wait()

    @pl.when(r == 0)
    def _():
        ffn(0, False)


def _expert_ffn(blk_expert, blk_valid, blk_src, blk_wslot, blk_next, xs, w_gu, b_gu, w_down,
                b_down, *, bm, n_tok):
    n_grid = blk_expert.shape[0]
    n_exp, d_ff, d = w_down.shape
    assert d == ROW_TILE * LANES
    block_rows = bm * ROW_TILE
    grid_spec = pltpu.PrefetchScalarGridSpec(
        num_scalar_prefetch=5,
        grid=(n_grid,),
        in_specs=[
            pl.BlockSpec((block_rows, LANES), lambda r, be, valid, src, *_: (src[r], 0)),
            pl.BlockSpec((n_exp, 2 * d_ff), lambda r, *_: (0, 0)),
            pl.BlockSpec((n_exp, d), lambda r, *_: (0, 0)),
            pl.BlockSpec(memory_space=pl.ANY),
            pl.BlockSpec(memory_space=pl.ANY),
        ],
        out_specs=pl.BlockSpec(memory_space=pl.ANY),
        scratch_shapes=[pltpu.VMEM((2, d, 2 * d_ff), F32), pltpu.VMEM((2, d_ff, d), F32),
                        pltpu.VMEM((d, 2 * d_ff), BF16), pltpu.VMEM((d_ff, d), BF16),
                        pltpu.VMEM((block_rows, LANES), F32),
                        pltpu.VMEM((block_rows, LANES), F32), pltpu.VMEM((8, bm), I32),
                        pltpu.SMEM((2, bm), I32), pltpu.SemaphoreType.DMA((2,)),
                        pltpu.SemaphoreType.DMA((2,)), pltpu.SemaphoreType.DMA((2,)),
                        pltpu.SemaphoreType.DMA(())],
    )
    return pl.pallas_call(
        functools.partial(_expert_kernel, d_ff=d_ff, bm=bm, n_tok=n_tok),
        grid_spec=grid_spec,
        out_shape=jax.ShapeDtypeStruct(((TOP_K * n_tok + 2 * bm) * ROW_TILE, LANES), F32),
        compiler_params=_params(("arbitrary",)),
        name="expert_ffn",
    )(blk_expert, blk_valid, blk_src, blk_wslot, blk_next, xs, b_gu, b_down, w_gu, w_down)


def _combine_kernel(x_ref, gate_ref, g_ref, y0_ref, y1_ref, y2_ref, y3_ref, o_ref):
    tm, d = x_ref.shape
    gates = gate_ref[...]
    acc = x_ref[...]
    for k, y_ref in enumerate((y0_ref, y1_ref, y2_ref, y3_ref)):
        y = jnp.concatenate([y_ref[pl.ds(j, tm, stride=ROW_TILE), :]
                             for j in range(d // LANES)], axis=1)
        acc = acc + gates[:, k:k + 1] * y
    o_ref[...] = _rms(acc, g_ref[...])


def _combine(x2, gates, g_final, ys, *, tm):
    n_tok, d = x2.shape
    tiles = n_tok // tm
    slot_spec = lambda k: pl.BlockSpec((tm * ROW_TILE, LANES), lambda i: (k * tiles + i, 0))
    return pl.pallas_call(
        _combine_kernel,
        grid=(tiles,),
        in_specs=[
            pl.BlockSpec((tm, d), lambda i: (i, 0)),
            pl.BlockSpec((tm, LANES), lambda i: (i, 0)),
            pl.BlockSpec((1, d), lambda i: (0, 0)),
        ] + [slot_spec(k) for k in range(TOP_K)],
        out_specs=pl.BlockSpec((tm, d), lambda i: (i, 0)),
        out_shape=jax.ShapeDtypeStruct((n_tok, d), F32),
        compiler_params=_params(("arbitrary",)),
        name="combine",
    )(x2, gates, g_final, ys, ys, ys, ys)


def _routing_tables(route, cum, *, bm, n_grid):
    expert = route[:, 0:TOP_K]
    pos = route[:, TOP_K:2 * TOP_K]
    experts = jnp.arange(N_EXPERTS, dtype=I32)
    counts = cum[-1, 0, :N_EXPERTS].astype(I32)
    padded = (counts + bm - 1) // bm * bm
    pend = jnp.cumsum(padded)
    pstart = pend - padded
    dest = pos + jnp.sum(jnp.where(expert[:, :, None] == experts, pstart, 0), axis=2)

    blk = jnp.arange(n_grid, dtype=I32)
    blk_expert = jnp.minimum(jnp.sum(pend[None, :] <= (blk * bm)[:, None], axis=1),
                             N_EXPERTS - 1).astype(I32)
    n_used = pend[-1] // bm
    blk_valid = (blk < n_used).astype(I32)
    blk_src = jnp.minimum(blk, jnp.maximum(n_used - 1, 0)).astype(I32)
    zflag = (padded > 0).astype(I32)
    zstart = jnp.maximum(pend - bm, 0).astype(I32)
    wslot = (jnp.cumsum(zflag) - 1) % 2
    later = jnp.logical_and(experts[None, :] > experts[:, None], zflag[None, :] > 0)
    nxt = jnp.min(jnp.where(later, experts[None, :], N_EXPERTS), axis=1)
    nxt = jnp.where(nxt < N_EXPERTS, nxt, -1)
    owner = blk_expert[:, None] == experts[None, :]
    of_block = lambda v: jnp.sum(jnp.where(owner, v[None, :], 0), axis=1).astype(I32)
    return (dest.astype(I32).reshape(-1), blk_expert, blk_valid, blk_src, of_block(wslot),
            of_block(nxt), zstart, zflag, n_used.astype(I32).reshape(1))


def kernel(x, mem, positions, norm_mix, w_in, lambda_q1, lambda_k1, lambda_q2, lambda_k2,
           diff_subln, w_up_sb, w_up_diff, w_out, norm_mem_q, norm_mem_kv, w_mem_q, w_mem_kv,
           w_mem_o, norm_ffn, w_router, b_router, w_gate_up, b_gate_up, w_down, b_down,
           norm_final):
    b, s, d = x.shape
    n_tok = b * s
    depth = norm_mix.shape[0]
    sb_width = w_up_sb.shape[1]
    diff_width = w_up_diff.shape[1]
    n_in = w_in.shape[2]
    chunk = 512
    assert sb_width == chunk and diff_width == chunk and d == 2 * chunk
    sbq, sbk, sbv, dq, dk, dv = range(6)
    blocks_per_chunk = chunk // LANES
    scale = 1.0 / math.sqrt(SB_HEAD_DIM)
    assert SB_HEAD_DIM == DIFF_HEAD_DIM

    tm_in = min(512, n_tok)
    tq_sb = min(256, s)
    tq_diff = min(512, s)
    tm_post = min(512, s)
    tm_tok = min(256, n_tok)
    bm = 256
    n_grid = (n_tok * TOP_K) // bm + N_EXPERTS

    cos_t, sin_t = _rope_tables(positions)
    x2d = x.reshape(n_tok, d)
    for l in range(depth):
        lambda_init = 0.8 - 0.6 * math.exp(-0.3 * l)
        proj = _in_proj(x2d, norm_mix[l].reshape(1, d), w_in[l].astype(BF16), cos_t, sin_t,
                        tm=tm_in, chunk=chunk, rope_chunks=(dq, dk), scale_chunks=(sbq, dq),
                        scale=scale)
        proj3 = proj.reshape(b, s, n_in)
        o_sb = _sb_attention(proj3, q_col=sbq * blocks_per_chunk, k_col=sbk * blocks_per_chunk,
                             v_col=sbv * blocks_per_chunk, n_pairs=sb_width // LANES, tq=tq_sb)
        o_diff = _diff_attention(
            proj3, lambda_q1[l].reshape(1, -1), lambda_k1[l].reshape(1, -1),
            lambda_q2[l].reshape(1, -1), lambda_k2[l].reshape(1, -1),
            diff_subln[l].reshape(1, -1), q_col=dq * blocks_per_chunk,
            k_col=dk * blocks_per_chunk, v_col=dv * blocks_per_chunk,
            n_heads=diff_width // DIFF_V_DIM, tq=tq_diff, lambda_init=lambda_init)
        kv = _mem_kv(mem, norm_mem_kv[l].reshape(1, d), w_mem_kv[l].astype(BF16))
        w_router_pad = jnp.zeros((d, LANES), F32).at[:, :N_EXPERTS].set(w_router[l])
        b_router_pad = jnp.full((1, LANES), NEG_BIG, F32).at[0, :N_EXPERTS].set(b_router[l])
        x_res, hf, route, gates, cum = _post_mix(
            x2d, proj, o_sb.reshape(n_tok, sb_width), o_diff.reshape(n_tok, diff_width),
            w_up_sb[l].astype(BF16), w_up_diff[l].astype(BF16), w_out[l].astype(BF16),
            norm_mem_q[l].reshape(1, d), w_mem_q[l].astype(BF16), kv, w_mem_o[l].astype(BF16),
            norm_ffn[l].reshape(1, d), w_router_pad, b_router_pad,
            tm=tm_post, seq=s, gate_col=6 * chunk // d)
        (dest, blk_expert, blk_valid, blk_src, blk_wslot, blk_next, zstart, zflag,
         n_used) = _routing_tables(route, cum, bm=bm, n_grid=n_grid + 1)
        xs = _dispatch(zstart, zflag, n_used, dest, hf, tm=tm_tok, bm=bm, n_rows=n_grid * bm)
        ys = _expert_ffn(blk_expert, blk_valid, blk_src, blk_wslot, blk_next, xs, w_gate_up[l],
                         b_gate_up[l], w_down[l], b_down[l], bm=bm, n_tok=n_tok)
        g_next = norm_final if l == depth - 1 else None
        assert g_next is not None, "only depth 1 is wired: the final norm is fused into combine"
        x2d = _combine(x_res, gates, g_next.reshape(1, d), ys, tm=tm_tok)
    return x2d.reshape(b, s, d)
```

```python
import functools
import math

import jax
import jax.numpy as jnp
from jax import lax
from jax.experimental import pallas as pl
from jax.experimental.pallas import tpu as pltpu

F32 = jnp.float32
BF16 = jnp.bfloat16
I32 = jnp.int32
U32 = jnp.uint32

LANES = 128
VMEM_LIMIT_BYTES = 56 * 1024 * 1024
ROW_TILE = 8
PACKED_SUBROWS = 4
META_SUBROW = 4

NORM_EPS = 1e-6
ROPE_THETA = 10000.0
CHUNK = 64
SB_HEAD_DIM = 64
DIFF_HEAD_DIM = 64
DIFF_V_DIM = 128
MEM_HEADS = 4
N_EXPERTS = 32
TOP_K = 4
SWIGLU_LIMIT = 7.0
SWIGLU_ALPHA = 1.702

SB_DEAD_LOG = -105.0
NEG_BIG = -1e30


def _params(semantics):
    return pltpu.CompilerParams(dimension_semantics=semantics,
                                vmem_limit_bytes=VMEM_LIMIT_BYTES)


def _const_spec(shape):
    nd = len(shape)
    return pl.BlockSpec(shape, lambda *_: (0,) * nd)


def _rms(x, g):
    return x * lax.rsqrt(jnp.mean(x * x, axis=-1, keepdims=True) + NORM_EPS) * g


def _rope_table_kernel(pos_ref, inv_ref, cos_ref, sin_ref):
    ang = pos_ref[...] * inv_ref[...]
    cos_ref[...] = jnp.cos(ang)
    sin_ref[...] = jnp.sin(ang)


def _rope_tables(positions):
    n_tok = positions.size
    half = DIFF_HEAD_DIM // 2
    per_row = LANES // half
    rows = n_tok // per_row
    pos = jnp.repeat(positions.reshape(rows, per_row).astype(F32), half, axis=1)
    inv = ROPE_THETA ** (-jnp.arange(half, dtype=F32) / half)
    inv = jnp.tile(inv, per_row).reshape(1, LANES)
    tr = min(rows, 512)
    cos, sin = pl.pallas_call(
        _rope_table_kernel,
        grid=(rows // tr,),
        in_specs=[pl.BlockSpec((tr, LANES), lambda i: (i, 0)), _const_spec((1, LANES))],
        out_specs=[pl.BlockSpec((tr, LANES), lambda i: (i, 0))] * 2,
        out_shape=[jax.ShapeDtypeStruct((rows, LANES), F32)] * 2,
        compiler_params=_params(("arbitrary",)),
        name="rope_table",
    )(pos, inv)
    cos = cos.reshape(n_tok, half)
    sin = sin.reshape(n_tok, half)
    cos_t = jnp.tile(cos, (1, LANES // half))
    sin_t = jnp.tile(jnp.concatenate([-sin, sin], axis=1), (1, LANES // (2 * half)))
    return cos_t, sin_t


def _inproj_kernel(x_ref, g_ref, w_ref, cos_ref, sin_ref, o_ref, *, chunk, rope_chunks,
                   scale_chunks, scale):
    h = _rms(x_ref[...], g_ref[...]).astype(BF16)
    lane = lax.broadcasted_iota(I32, (1, chunk), 1)
    first_half = (lane % DIFF_HEAD_DIM) < (DIFF_HEAD_DIM // 2)
    for c in range(w_ref.shape[1] // chunk):
        cols = slice(c * chunk, (c + 1) * chunk)
        acc = jnp.dot(h, w_ref[:, cols], preferred_element_type=F32)
        if c in rope_chunks:
            cos = jnp.tile(cos_ref[...], (1, chunk // LANES))
            sin = jnp.tile(sin_ref[...], (1, chunk // LANES))
            partner = jnp.where(first_half,
                                pltpu.roll(acc, chunk - DIFF_HEAD_DIM // 2, 1),
                                pltpu.roll(acc, DIFF_HEAD_DIM // 2, 1))
            acc = acc * cos + partner * sin
        if c in scale_chunks:
            acc = acc * scale
        o_ref[:, cols] = acc.astype(BF16)


def _in_proj(x2d, g, w_bf16, cos_t, sin_t, *, tm, chunk, rope_chunks, scale_chunks, scale):
    n_tok, d = x2d.shape
    n_in = w_bf16.shape[1]
    kern = functools.partial(_inproj_kernel, chunk=chunk, rope_chunks=rope_chunks,
                             scale_chunks=scale_chunks, scale=scale)
    return pl.pallas_call(
        kern,
        grid=(n_tok // tm,),
        in_specs=[pl.BlockSpec((tm, d), lambda i: (i, 0)),
                  _const_spec((1, d)),
                  _const_spec((d, n_in)),
                  pl.BlockSpec((tm, LANES), lambda i: (i, 0)),
                  pl.BlockSpec((tm, LANES), lambda i: (i, 0))],
        out_specs=pl.BlockSpec((tm, n_in), lambda i: (i, 0)),
        out_shape=jax.ShapeDtypeStruct((n_tok, n_in), BF16),
        compiler_params=_params(("arbitrary",)),
        name="in_proj",
    )(x2d, g, w_bf16, cos_t, sin_t)


def _sb_kernel(q_ref, k_ref, v_ref, o_ref, acc_ref, carry_ref, *, tq):
    i = pl.program_id(2)
    q = q_ref[0]
    lane = lax.broadcasted_iota(I32, (1, LANES), 1)
    zero = jnp.zeros_like(q)
    q2 = jnp.concatenate([jnp.where(lane < SB_HEAD_DIM, q, zero),
                          jnp.where(lane >= SB_HEAD_DIM, q, zero)], axis=0)
    uj = lax.broadcasted_iota(I32, (2 * tq, tq), 0) % tq
    us = lax.broadcasted_iota(I32, (2 * tq, tq), 1)
    suffix = jnp.where(uj > us, 1.0, 0.0).astype(BF16)

    def block(j, masked):
        start = pl.multiple_of(j * tq, tq)
        kj = k_ref[0, pl.ds(start, tq), :]
        vj = v_ref[0, pl.ds(start, tq), :]
        z = lax.dot_general(q2, kj, (((1,), (1,)), ((), ())), preferred_element_type=F32)
        sp = jnp.maximum(z, 0.0) + jnp.log(1.0 + jnp.exp(-jnp.abs(z)))
        log_rem = -sp
        if masked:
            causal = (lax.broadcasted_iota(I32, (2 * tq, tq), 1)
                      < lax.broadcasted_iota(I32, (2 * tq, tq), 0) % tq)
            log_rem = jnp.where(causal, log_rem, 0.0)
        hi = log_rem.astype(BF16)
        lo = (log_rem - hi.astype(F32)).astype(BF16)
        after = jnp.dot(jnp.concatenate([hi, lo], axis=1), suffix, preferred_element_type=F32)
        carry = carry_ref[...]
        w = jnp.exp(z - sp + after + carry)
        if masked:
            w = jnp.where(causal, w, 0.0)
        acc_ref[...] += jnp.dot(w.astype(BF16), vj, preferred_element_type=F32)
        carry = carry + after[:, 0:1] + log_rem[:, 0:1]
        carry_ref[...] = carry
        return jnp.max(carry)

    acc_ref[...] = jnp.zeros_like(acc_ref)
    carry_ref[...] = jnp.zeros_like(carry_ref)
    alive = block(i, True)

    def cond(state):
        j, alive = state
        return jnp.logical_and(j >= 0, alive > SB_DEAD_LOG)

    def body(state):
        j, _ = state
        return j - 1, block(j, False)

    lax.while_loop(cond, body, (i - 1, alive))
    o_ref[0] = jnp.where(lane < SB_HEAD_DIM, acc_ref[0:tq, :], acc_ref[tq:2 * tq, :]).astype(BF16)


def _sb_attention(proj3, *, q_col, k_col, v_col, n_pairs, tq):
    b, s, _ = proj3.shape
    return pl.pallas_call(
        functools.partial(_sb_kernel, tq=tq),
        grid=(b, n_pairs, s // tq),
        in_specs=[pl.BlockSpec((1, tq, LANES), lambda bi, p, i: (bi, i, q_col + p)),
                  pl.BlockSpec((1, s, LANES), lambda bi, p, i: (bi, 0, k_col + p)),
                  pl.BlockSpec((1, s, LANES), lambda bi, p, i: (bi, 0, v_col + p))],
        out_specs=pl.BlockSpec((1, tq, LANES), lambda bi, p, i: (bi, i, p)),
        out_shape=jax.ShapeDtypeStruct((b, s, n_pairs * LANES), BF16),
        scratch_shapes=[pltpu.VMEM((2 * tq, LANES), F32), pltpu.VMEM((2 * tq, 1), F32)],
        compiler_params=_params(("arbitrary", "arbitrary", "arbitrary")),
        name="sb_attention",
    )(proj3, proj3, proj3)


DIFF_ONES_ROWS = 16


def _diff_kernel(lq1_ref, lk1_ref, lq2_ref, lk2_ref, q_ref, k_ref, v_ref, g_ref, o_ref, vt_ref,
                 acc_ref, z0_ref, z1_ref, *, tq, lambda_init):
    i = pl.program_id(2)
    s_len = v_ref.shape[1]
    vd = v_ref.shape[2]
    half = tq // 2
    zs = (z0_ref, z1_ref)

    @pl.when(i == 0)
    def _():
        for c in range(s_len // tq):
            cols = slice(c * tq, (c + 1) * tq)
            vt_ref[0:vd, cols] = v_ref[0, cols, :].astype(F32).T.astype(BF16)
        rid = lax.broadcasted_iota(I32, (DIFF_ONES_ROWS, s_len), 0)
        vt_ref[vd:vd + DIFF_ONES_ROWS, :] = jnp.where(rid == 0, 1.0, 0.0).astype(BF16)

    q = q_ref[0]
    lane = lax.broadcasted_iota(I32, (1, LANES), 1)
    zero = jnp.zeros_like(q)
    q2 = jnp.concatenate([jnp.where(lane < DIFF_HEAD_DIM, q, zero),
                          jnp.where(lane >= DIFF_HEAD_DIM, q, zero)], axis=0)

    def scores(j, h, masked):
        start = pl.multiple_of(j * tq + h * half, half)
        kj = k_ref[0, pl.ds(start, half), :]
        zt = lax.dot_general(kj, q2, (((1,), (1,)), ((), ())), preferred_element_type=F32)
        if masked:
            qidx = lax.broadcasted_iota(I32, (half, 2 * tq), 1) % tq
            kidx = lax.broadcasted_iota(I32, (half, 2 * tq), 0) + h * half
            zt = jnp.where(kidx // CHUNK <= qidx // CHUNK, zt, -jnp.inf)
        zs[h][...] = zt
        return jnp.max(zt, axis=0, keepdims=True)

    def values(j, h, m, mblk):
        start = pl.multiple_of(j * tq + h * half, half)
        vtj = vt_ref[:, pl.ds(start, half)]
        m_new = jnp.maximum(m, mblk)
        pt = jnp.exp(zs[h][...] - m_new).astype(BF16)
        acc_ref[...] = jnp.exp(m - m_new) * acc_ref[...] + jnp.dot(vtj, pt,
                                                                   preferred_element_type=F32)
        return m_new

    def block(j, m, mb0, masked, next_kind):
        mb1 = scores(j, 1, masked)
        m = values(j, 0, m, mb0)
        nb0 = mb0 if next_kind is None else scores(j + 1, 0, next_kind == "masked")
        m = values(j, 1, m, mb1)
        return m, nb0

    acc_ref[...] = jnp.zeros_like(acc_ref)
    m0 = jnp.full((1, 2 * tq), -jnp.inf, F32)

    @pl.when(i == 0)
    def _():
        block(0, m0, scores(0, 0, True), True, None)

    @pl.when(i > 0)
    def _():
        state = (m0, scores(0, 0, False))
        state = lax.fori_loop(0, i - 1, lambda j, st: block(j, st[0], st[1], False, "plain"),
                              state)
        state = block(i - 1, state[0], state[1], False, "masked")
        block(i, state[0], state[1], True, None)

    lam = (jnp.exp(jnp.sum(lq1_ref[...] * lk1_ref[...], axis=1, keepdims=True))
           - jnp.exp(jnp.sum(lq2_ref[...] * lk2_ref[...], axis=1, keepdims=True))
           + lambda_init)
    ot = acc_ref[0:vd, :] / acc_ref[vd:vd + 1, :]
    ot = ot[:, 0:tq] - lam * ot[:, tq:2 * tq]
    o_ref[0] = (_rms(ot.T, g_ref[...]) * (1.0 - lambda_init)).astype(BF16)


def _diff_attention(proj3, lq1, lk1, lq2, lk2, subln, *, q_col, k_col, v_col, n_heads, tq,
                    lambda_init):
    b, s, _ = proj3.shape
    lam_spec = _const_spec((1, DIFF_HEAD_DIM))
    return pl.pallas_call(
        functools.partial(_diff_kernel, tq=tq, lambda_init=lambda_init),
        grid=(b, n_heads, s // tq),
        in_specs=[lam_spec, lam_spec, lam_spec, lam_spec,
                  pl.BlockSpec((1, tq, LANES), lambda bi, h, i: (bi, i, q_col + h)),
                  pl.BlockSpec((1, s, LANES), lambda bi, h, i: (bi, 0, k_col + h)),
                  pl.BlockSpec((1, s, LANES), lambda bi, h, i: (bi, 0, v_col + h)),
                  _const_spec((1, DIFF_V_DIM))],
        out_specs=pl.BlockSpec((1, tq, LANES), lambda bi, h, i: (bi, i, h)),
        out_shape=jax.ShapeDtypeStruct((b, s, n_heads * DIFF_V_DIM), BF16),
        scratch_shapes=[pltpu.VMEM((DIFF_V_DIM + DIFF_ONES_ROWS, s), BF16),
                        pltpu.VMEM((DIFF_V_DIM + DIFF_ONES_ROWS, 2 * tq), F32),
                        pltpu.VMEM((tq // 2, 2 * tq), F32), pltpu.VMEM((tq // 2, 2 * tq), F32)],
        compiler_params=_params(("arbitrary", "arbitrary", "arbitrary")),
        name="diff_attention",
    )(lq1, lk1, lq2, lk2, proj3, proj3, proj3, subln)


def _mem_kv_kernel(mem_ref, g_ref, w_ref, o_ref):
    h = _rms(mem_ref[0], g_ref[...]).astype(BF16)
    o_ref[0] = jnp.dot(h, w_ref[...], preferred_element_type=F32).astype(BF16)


def _mem_kv(mem, g, w_bf16):
    b, m, d = mem.shape
    n = w_bf16.shape[1]
    return pl.pallas_call(
        _mem_kv_kernel,
        grid=(b,),
        in_specs=[pl.BlockSpec((1, m, d), lambda i: (i, 0, 0)), _const_spec((1, d)),
                  _const_spec((d, n))],
        out_specs=pl.BlockSpec((1, m, n), lambda i: (i, 0, 0)),
        out_shape=jax.ShapeDtypeStruct((b, m, n), BF16),
        compiler_params=_params(("arbitrary",)),
        name="mem_kv",
    )(mem, g, w_bf16)


def _split_bf16(v):
    hi = v.astype(BF16)
    return hi, (v - hi.astype(F32)).astype(BF16)


def _pack_bf16_pairs(v):
    half = v.shape[1] // 2
    bits = lax.bitcast_convert_type(v.astype(BF16).astype(F32), U32)
    return (bits[:, :half] >> 16) | (bits[:, half:] & jnp.uint32(0xFFFF0000))


def _unpack_bf16_pairs(w):
    lo = lax.bitcast_convert_type(w << 16, F32)
    hi = lax.bitcast_convert_type(w & jnp.uint32(0xFFFF0000), F32)
    return jnp.concatenate([lo, hi], axis=1).astype(BF16)


def _postmix_kernel(x_ref, osb_ref, odf_ref, gsb_ref, gdf_ref, wus_ref, wud_ref, wout_ref,
                    gq_ref, wq_ref, kv_ref, wo_ref, gf_ref, wr_ref, br_ref,
                    x2_ref, hf_ref, route_ref, gate_ref, cum_ref, count_ref, *, tm, d_model):
    step = pl.program_id(0)

    @pl.when(step == 0)
    def _():
        count_ref[...] = jnp.zeros_like(count_ref)

    y_sb = jnp.dot(osb_ref[...], wus_ref[...], preferred_element_type=F32)
    y_df = jnp.dot(odf_ref[...], wud_ref[...], preferred_element_type=F32)
    mixed = (jax.nn.sigmoid(gsb_ref[...].astype(F32)) * y_sb
             + jax.nn.sigmoid(gdf_ref[...].astype(F32)) * y_df)
    x1 = x_ref[...] + jnp.dot(mixed.astype(BF16), wout_ref[...], preferred_element_type=F32)

    hq = _rms(x1, gq_ref[...]).astype(BF16)
    hd = d_model // MEM_HEADS
    q = jnp.dot(hq, wq_ref[...], preferred_element_type=F32) * (1.0 / math.sqrt(hd))
    q = q.astype(BF16)
    heads = []
    for h in range(MEM_HEADS):
        kh = kv_ref[0, :, h * hd:(h + 1) * hd]
        vh = kv_ref[0, :, d_model + h * hd:d_model + (h + 1) * hd]
        z = lax.dot_general(q[:, h * hd:(h + 1) * hd], kh, (((1,), (1,)), ((), ())),
                            preferred_element_type=F32)
        p = jnp.exp(z - jnp.max(z, axis=1, keepdims=True))
        l = jnp.sum(p, axis=1, keepdims=True)
        heads.append((jnp.dot(p.astype(BF16), vh, preferred_element_type=F32) / l).astype(BF16))
    x2 = x1 + jnp.dot(jnp.concatenate(heads, axis=1), wo_ref[...], preferred_element_type=F32)
    x2_ref[...] = x2

    hf = _rms(x2, gf_ref[...])
    packed = _pack_bf16_pairs(hf)
    for j in range(PACKED_SUBROWS):
        hf_ref[pl.ds(j, tm, stride=ROW_TILE), :] = packed[:, j * LANES:(j + 1) * LANES]
    h_hi, h_lo = _split_bf16(hf)
    w_hi, w_lo = _split_bf16(wr_ref[...])
    logits = (jnp.dot(h_hi, w_hi, preferred_element_type=F32)
              + jnp.dot(h_hi, w_lo, preferred_element_type=F32)
              + jnp.dot(h_lo, w_hi, preferred_element_type=F32)) + br_ref[...]
    lane = lax.broadcasted_iota(I32, (tm, LANES), 1)
    work = logits
    vals, idxs, hots = [], [], []
    for _ in range(TOP_K):
        mx = jnp.max(work, axis=1, keepdims=True)
        idx = jnp.min(jnp.where(work == mx, lane, LANES), axis=1, keepdims=True)
        hot = lane == idx
        work = jnp.where(hot, NEG_BIG, work)
        vals.append(mx)
        idxs.append(idx)
        hots.append(hot)
    exps = [jnp.exp(v - vals[0]) for v in vals]
    denom = exps[0] + exps[1] + exps[2] + exps[3]

    onehot_sum = jnp.zeros((tm, LANES), F32)
    for hot in hots:
        onehot_sum = onehot_sum + jnp.where(hot, 1.0, 0.0)
    r = lax.broadcasted_iota(I32, (tm, tm), 0)
    c = lax.broadcasted_iota(I32, (tm, tm), 1)
    lower = jnp.where(c < r, 1.0, 0.0).astype(BF16)
    rank = jnp.dot(lower, onehot_sum.astype(BF16), preferred_element_type=F32) + count_ref[...]
    route = jnp.zeros((tm, LANES), I32)
    gates = jnp.zeros((tm, LANES), F32)
    for k in range(TOP_K):
        pos = jnp.sum(jnp.where(hots[k], rank, 0.0), axis=1, keepdims=True).astype(I32)
        route = jnp.where(lane == k, idxs[k], route)
        route = jnp.where(lane == TOP_K + k, pos, route)
        gates = jnp.where(lane == k, exps[k] / denom, gates)
    route_ref[...] = route
    gate_ref[...] = gates
    meta = jnp.where(lane == 0, step * tm + lax.broadcasted_iota(I32, (tm, LANES), 0), 0)
    for k in range(TOP_K):
        meta = jnp.where(lane == 1 + k, idxs[k], meta)
    hf_ref[pl.ds(META_SUBROW, tm, stride=ROW_TILE), :] = meta.astype(U32)
    for j in range(META_SUBROW + 1, ROW_TILE):
        hf_ref[pl.ds(j, tm, stride=ROW_TILE), :] = jnp.zeros((tm, LANES), U32)
    count_ref[...] = count_ref[...] + jnp.sum(onehot_sum, axis=0, keepdims=True)
    cum_ref[0] = jnp.broadcast_to(count_ref[...], (8, LANES))


def _post_mix(x2d, proj2, o_sb, o_diff, w_up_sb, w_up_diff, w_out, g_memq, w_memq, kv, w_memo,
              g_ffn, w_router_pad, b_router_pad, *, tm, seq, gate_col):
    n_tok, d = x2d.shape
    n_tiles = n_tok // tm
    tiles_per_batch = seq // tm
    row = lambda i: (i, 0)
    in_specs = [
        pl.BlockSpec((tm, d), row),
        pl.BlockSpec((tm, o_sb.shape[1]), row),
        pl.BlockSpec((tm, o_diff.shape[1]), row),
        pl.BlockSpec((tm, d), lambda i: (i, gate_col)),
        pl.BlockSpec((tm, d), lambda i: (i, gate_col + 1)),
        _const_spec(w_up_sb.shape), _const_spec(w_up_diff.shape), _const_spec(w_out.shape),
        _const_spec((1, d)), _const_spec(w_memq.shape),
        pl.BlockSpec((1,) + kv.shape[1:], lambda i: (i // tiles_per_batch, 0, 0)),
        _const_spec(w_memo.shape), _const_spec((1, d)),
        _const_spec(w_router_pad.shape), _const_spec((1, LANES)),
    ]
    out_specs = [
        pl.BlockSpec((tm, d), row),
        pl.BlockSpec((tm * ROW_TILE, LANES), row),
        pl.BlockSpec((tm, LANES), row),
        pl.BlockSpec((tm, LANES), row),
        pl.BlockSpec((1, 8, LANES), lambda i: (i, 0, 0)),
    ]
    assert d // 2 == PACKED_SUBROWS * LANES
    out_shape = [
        jax.ShapeDtypeStruct((n_tok, d), F32),
        jax.ShapeDtypeStruct((n_tok * ROW_TILE, LANES), U32),
        jax.ShapeDtypeStruct((n_tok, LANES), I32),
        jax.ShapeDtypeStruct((n_tok, LANES), F32),
        jax.ShapeDtypeStruct((n_tiles, 8, LANES), F32),
    ]
    return pl.pallas_call(
        functools.partial(_postmix_kernel, tm=tm, d_model=d),
        grid=(n_tiles,),
        in_specs=in_specs,
        out_specs=out_specs,
        out_shape=out_shape,
        scratch_shapes=[pltpu.VMEM((1, LANES), F32)],
        compiler_params=_params(("arbitrary",)),
        name="post_mix",
    )(x2d, o_sb, o_diff, proj2, proj2, w_up_sb, w_up_diff, w_out, g_memq, w_memq, kv, w_memo,
      g_ffn, w_router_pad, b_router_pad)


def _dispatch_kernel(zstart_ref, zflag_ref, nused_ref, dest_ref, hf_ref, xs_hbm, stage0, stage1,
                     zbuf, sem, zsem, *, tm, bm):
    s = pl.program_id(0)

    @pl.when(s == 0)
    def _():
        lane = lax.broadcasted_iota(I32, (bm, LANES), 1)
        zbuf[...] = jnp.zeros_like(zbuf)
        zbuf[pl.ds(META_SUBROW, bm, stride=ROW_TILE), :] = jnp.where(
            jnp.logical_and(lane >= 1, lane <= TOP_K), -1, 0).astype(U32)

        def fill_copy(start):
            start = pl.multiple_of(start * ROW_TILE, bm * ROW_TILE)
            return pltpu.make_async_copy(zbuf, xs_hbm.at[pl.ds(start, bm * ROW_TILE), :], zsem)

        n_blocks = xs_hbm.shape[0] // (bm * ROW_TILE)
        for action in ("start", "wait"):
            for e in range(N_EXPERTS):
                @pl.when(zflag_ref[e] != 0)
                def _(e=e, action=action):
                    getattr(fill_copy(zstart_ref[e]), action)()

            def trailing(blk, _, action=action):
                getattr(fill_copy(blk * bm), action)()
                return 0

            lax.fori_loop(nused_ref[0], n_blocks, trailing, 0)

    def tile_done(stage, parity):
        for _ in range(TOP_K):
            pltpu.make_async_copy(stage, xs_hbm.at[pl.ds(0, tm * ROW_TILE), :],
                                  sem.at[parity]).wait()

    def run(stage, other, parity):
        stage[...] = hf_ref[...]

        def issue(t, _):
            src = stage.at[pl.ds(pl.multiple_of(t * ROW_TILE, ROW_TILE), ROW_TILE), :]
            for k in range(TOP_K):
                row = pl.multiple_of(dest_ref[t * TOP_K + k] * ROW_TILE, ROW_TILE)
                pltpu.make_async_copy(src, xs_hbm.at[pl.ds(row, ROW_TILE), :],
                                      sem.at[parity]).start(priority=k % 2)
            return 0

        lax.fori_loop(0, tm, issue, 0)

        @pl.when(s > 0)
        def _():
            tile_done(other, 1 - parity)

        @pl.when(s == pl.num_programs(0) - 1)
        def _():
            tile_done(stage, parity)

    @pl.when(s % 2 == 0)
    def _():
        run(stage0, stage1, 0)

    @pl.when(s % 2 == 1)
    def _():
        run(stage1, stage0, 1)


def _dispatch(zstart, zflag, n_used, dest_flat, hf_rows, *, tm, bm, n_rows):
    n_tok = hf_rows.shape[0] // ROW_TILE
    grid_spec = pltpu.PrefetchScalarGridSpec(
        num_scalar_prefetch=3,
        grid=(n_tok // tm,),
        in_specs=[
            pl.BlockSpec((tm * TOP_K,), lambda i, *_: (i,), memory_space=pltpu.SMEM),
            pl.BlockSpec((tm * ROW_TILE, LANES), lambda i, *_: (i, 0)),
        ],
        out_specs=pl.BlockSpec(memory_space=pl.ANY),
        scratch_shapes=[pltpu.VMEM((tm * ROW_TILE, LANES), U32),
                        pltpu.VMEM((tm * ROW_TILE, LANES), U32),
                        pltpu.VMEM((bm * ROW_TILE, LANES), U32),
                        pltpu.SemaphoreType.DMA((2,)), pltpu.SemaphoreType.DMA(())],
    )
    return pl.pallas_call(
        functools.partial(_dispatch_kernel, tm=tm, bm=bm),
        grid_spec=grid_spec,
        out_shape=jax.ShapeDtypeStruct((n_rows * ROW_TILE, LANES), U32),
        compiler_params=_params(("arbitrary",)),
        name="dispatch",
    )(zstart, zflag, n_used, dest_flat, hf_rows)


FFN_UP_CHUNKS = 8
FFN_DOWN_CHUNKS = 4


def _expert_kernel(be_ref, valid_ref, src_ref, wslot_ref, next_ref, xs_ref, bgu_ref, bd_ref,
                   wgu_hbm, wd_hbm, ys_hbm, wgu_f32, wd_f32, wgu_bf, wd_bf, ybuf0, ybuf1, slot_v,
                   slot_s, sem_w, sem_y, sem_s, sem_z, *, d_ff, bm, n_tok):
    r = pl.program_id(0)
    valid = valid_ref[r] != 0
    prev_valid = jnp.logical_and(r > 0, valid_ref[jnp.maximum(r - 1, 0)] != 0)
    fresh = jnp.logical_or(r == 0, be_ref[r] != be_ref[jnp.maximum(r - 1, 0)])
    expert = be_ref[r]
    ybufs = (ybuf0, ybuf1)

    def weight_copies(e, wslot):
        return (pltpu.make_async_copy(wgu_hbm.at[e], wgu_f32.at[wslot], sem_w.at[wslot]),
                pltpu.make_async_copy(wd_hbm.at[e], wd_f32.at[wslot], sem_w.at[wslot]))
    d = wd_bf.shape[1]
    n_slots = TOP_K * n_tok
    block_rows = bm * ROW_TILE

    def rows_done(parity):
        return pltpu.make_async_copy(ybufs[parity], ys_hbm.at[pl.ds(0, block_rows), :],
                                     sem_y.at[parity])

    def slots_copy(parity):
        return pltpu.make_async_copy(slot_v.at[0:1, :], slot_s.at[parity:parity + 1, :],
                                     sem_s.at[parity])

    def scatter_rows(prev, lo, hi):
        for i in range(lo, hi):
            dst = pl.multiple_of(slot_s[prev, i] * ROW_TILE, ROW_TILE)
            pltpu.make_async_copy(ybufs[prev].at[pl.ds(i * ROW_TILE, ROW_TILE), :],
                                  ys_hbm.at[pl.ds(dst, ROW_TILE), :],
                                  sem_y.at[prev]).start(priority=i % 2)

    @pl.when(r == 0)
    def _():
        for action in ("start", "wait"):
            for parity in range(2):
                if action == "start":
                    ybufs[parity][...] = jnp.zeros_like(ybufs[parity])
                trash = (n_slots + parity * bm) * ROW_TILE
                getattr(pltpu.make_async_copy(ybufs[parity],
                                              ys_hbm.at[pl.ds(trash, block_rows), :], sem_z),
                        action)()

    for parity in range(2):
        @pl.when(jnp.logical_and(prev_valid, r % 2 == parity))
        def _(parity=parity):
            slots_copy(1 - parity).wait()

    @pl.when(jnp.logical_and(valid, fresh))
    def _():
        wslot = wslot_ref[r]

        @pl.when(r == 0)
        def _():
            for c in weight_copies(expert, wslot):
                c.start()

        for c in weight_copies(expert, wslot):
            c.wait()
        wgu_bf[...] = wgu_f32[wslot].astype(BF16)
        wd_bf[...] = wd_f32[wslot].astype(BF16)

        @pl.when(next_ref[r] >= 0)
        def _():
            for c in weight_copies(next_ref[r], 1 - wslot):
                c.start()

    def ffn(cur, interleave):
        prev = 1 - cur
        ybuf = ybufs[cur]
        bounds = [bm * c // FFN_UP_CHUNKS for c in range(FFN_UP_CHUNKS + 1)]

        @pl.when(r >= 2)
        def _():
            rows_done(cur).wait()

        meta = xs_ref[pl.ds(META_SUBROW, bm, stride=ROW_TILE), :].astype(I32)
        row = lax.broadcasted_iota(I32, (bm, 1), 0)
        slot = n_slots + cur * bm + row
        for k in range(TOP_K):
            slot = jnp.where(meta[:, 1 + k:2 + k] == be_ref[r], k * n_tok + meta[:, 0:1], slot)
        slots = jnp.broadcast_to(slot.astype(F32), (bm, LANES)).T
        slot_v[...] = slots[0:8, :].astype(I32)
        slots_copy(cur).start()

        words = jnp.concatenate([xs_ref[pl.ds(j, bm, stride=ROW_TILE), :]
                                 for j in range(PACKED_SUBROWS)], axis=1)
        xb = _unpack_bf16_pairs(words)
        up_w = 2 * d_ff // FFN_UP_CHUNKS
        pairs = FFN_UP_CHUNKS // 2
        acts = []
        for c in range(pairs):
            halves = []
            for half in range(2):
                cols = slice(half * d_ff + c * up_w, half * d_ff + (c + 1) * up_w)
                halves.append(jnp.dot(xb, wgu_bf[:, cols], preferred_element_type=F32)
                              + bgu_ref[pl.ds(expert, 1), cols])
                if interleave:
                    step_no = 2 * c + half
                    scatter_rows(prev, bounds[step_no], bounds[step_no + 1])
            glu = jnp.minimum(halves[0], SWIGLU_LIMIT)
            lin = jnp.clip(halves[1], -SWIGLU_LIMIT, SWIGLU_LIMIT)
            acts.append((glu * jax.nn.sigmoid(SWIGLU_ALPHA * glu) * (lin + 1.0)).astype(BF16))
        act = jnp.concatenate(acts, axis=1)
        down_w = d // FFN_DOWN_CHUNKS
        for c in range(FFN_DOWN_CHUNKS):
            cols = slice(c * down_w, (c + 1) * down_w)
            y = (jnp.dot(act, wd_bf[:, cols], preferred_element_type=F32)
                 + bd_ref[pl.ds(expert, 1), cols])
            for j in range(down_w // LANES):
                sub = c * (down_w // LANES) + j
                ybuf[pl.ds(sub, bm, stride=ROW_TILE), :] = y[:, j * LANES:(j + 1) * LANES]

    for parity in range(2):
        on_parity = r % 2 == parity

        @pl.when(jnp.logical_and(on_parity, jnp.logical_and(valid, prev_valid)))
        def _(parity=parity):
            ffn(parity, True)

        @pl.when(jnp.logical_and(on_parity, jnp.logical_and(jnp.logical_not(valid), prev_valid)))
        def _(parity=parity):
            scatter_rows(1 - parity, 0, bm)

            @pl.when(r >= 2)
            def _():
                rows_done(parity).wait()

            rows_done(1 - parity).wait()

    @pl.when(r == 0)
    def _():
        ffn(0, False)


def _expert_ffn(blk_expert, blk_valid, blk_src, blk_wslot, blk_next, xs, w_gu, b_gu, w_down,
                b_down, *, bm, n_tok):
    n_grid = blk_expert.shape[0]
    n_exp, d_ff, d = w_down.shape
    assert d == ROW_TILE * LANES
    block_rows = bm * ROW_TILE
    grid_spec = pltpu.PrefetchScalarGridSpec(
        num_scalar_prefetch=5,
        grid=(n_grid,),
        in_specs=[
            pl.BlockSpec((block_rows, LANES), lambda r, be, valid, src, *_: (src[r], 0)),
            pl.BlockSpec((n_exp, 2 * d_ff), lambda r, *_: (0, 0)),
            pl.BlockSpec((n_exp, d), lambda r, *_: (0, 0)),
            pl.BlockSpec(memory_space=pl.ANY),
            pl.BlockSpec(memory_space=pl.ANY),
        ],
        out_specs=pl.BlockSpec(memory_space=pl.ANY),
        scratch_shapes=[pltpu.VMEM((2, d, 2 * d_ff), F32), pltpu.VMEM((2, d_ff, d), F32),
                        pltpu.VMEM((d, 2 * d_ff), BF16), pltpu.VMEM((d_ff, d), BF16),
                        pltpu.VMEM((block_rows, LANES), F32),
                        pltpu.VMEM((block_rows, LANES), F32), pltpu.VMEM((8, bm), I32),
                        pltpu.SMEM((2, bm), I32), pltpu.SemaphoreType.DMA((2,)),
                        pltpu.SemaphoreType.DMA((2,)), pltpu.SemaphoreType.DMA((2,)),
                        pltpu.SemaphoreType.DMA(())],
    )
    return pl.pallas_call(
        functools.partial(_expert_kernel, d_ff=d_ff, bm=bm, n_tok=n_tok),
        grid_spec=grid_spec,
        out_shape=jax.ShapeDtypeStruct(((TOP_K * n_tok + 2 * bm) * ROW_TILE, LANES), F32),
        compiler_params=_params(("arbitrary",)),
        name="expert_ffn",
    )(blk_expert, blk_valid, blk_src, blk_wslot, blk_next, xs, b_gu, b_down, w_gu, w_down)


def _combine_kernel(x_ref, gate_ref, g_ref, y0_ref, y1_ref, y2_ref, y3_ref, o_ref):
    tm, d = x_ref.shape
    gates = gate_ref[...]
    acc = x_ref[...]
    for k, y_ref in enumerate((y0_ref, y1_ref, y2_ref, y3_ref)):
        y = jnp.concatenate([y_ref[pl.ds(j, tm, stride=ROW_TILE), :]
                             for j in range(d // LANES)], axis=1)
        acc = acc + gates[:, k:k + 1] * y
    o_ref[...] = _rms(acc, g_ref[...])


def _combine(x2, gates, g_final, ys, *, tm):
    n_tok, d = x2.shape
    tiles = n_tok // tm
    slot_spec = lambda k: pl.BlockSpec((tm * ROW_TILE, LANES), lambda i: (k * tiles + i, 0))
    return pl.pallas_call(
        _combine_kernel,
        grid=(tiles,),
        in_specs=[
            pl.BlockSpec((tm, d), lambda i: (i, 0)),
            pl.BlockSpec((tm, LANES), lambda i: (i, 0)),
            pl.BlockSpec((1, d), lambda i: (0, 0)),
        ] + [slot_spec(k) for k in range(TOP_K)],
        out_specs=pl.BlockSpec((tm, d), lambda i: (i, 0)),
        out_shape=jax.ShapeDtypeStruct((n_tok, d), F32),
        compiler_params=_params(("arbitrary",)),
        name="combine",
    )(x2, gates, g_final, ys, ys, ys, ys)


def _routing_tables(route, cum, *, bm, n_grid):
    expert = route[:, 0:TOP_K]
    pos = route[:, TOP_K:2 * TOP_K]
    experts = jnp.arange(N_EXPERTS, dtype=I32)
    counts = cum[-1, 0, :N_EXPERTS].astype(I32)
    padded = (counts + bm - 1) // bm * bm
    pend = jnp.cumsum(padded)
    pstart = pend - padded
    dest = pos + jnp.sum(jnp.where(expert[:, :, None] == experts, pstart, 0), axis=2)

    blk = jnp.arange(n_grid, dtype=I32)
    blk_expert = jnp.minimum(jnp.sum(pend[None, :] <= (blk * bm)[:, None], axis=1),
                             N_EXPERTS - 1).astype(I32)
    n_used = pend[-1] // bm
    blk_valid = (blk < n_used).astype(I32)
    blk_src = jnp.minimum(blk, jnp.maximum(n_used - 1, 0)).astype(I32)
    zflag = (padded > 0).astype(I32)
    zstart = jnp.maximum(pend - bm, 0).astype(I32)
    wslot = (jnp.cumsum(zflag) - 1) % 2
    later = jnp.logical_and(experts[None, :] > experts[:, None], zflag[None, :] > 0)
    nxt = jnp.min(jnp.where(later, experts[None, :], N_EXPERTS), axis=1)
    nxt = jnp.where(nxt < N_EXPERTS, nxt, -1)
    owner = blk_expert[:, None] == experts[None, :]
    of_block = lambda v: jnp.sum(jnp.where(owner, v[None, :], 0), axis=1).astype(I32)
    return (dest.astype(I32).reshape(-1), blk_expert, blk_valid, blk_src, of_block(wslot),
            of_block(nxt), zstart, zflag, n_used.astype(I32).reshape(1))


def kernel(x, mem, positions, norm_mix, w_in, lambda_q1, lambda_k1, lambda_q2, lambda_k2,
           diff_subln, w_up_sb, w_up_diff, w_out, norm_mem_q, norm_mem_kv, w_mem_q, w_mem_kv,
           w_mem_o, norm_ffn, w_router, b_router, w_gate_up, b_gate_up, w_down, b_down,
           norm_final):
    b, s, d = x.shape
    n_tok = b * s
    depth = norm_mix.shape[0]
    sb_width = w_up_sb.shape[1]
    diff_width = w_up_diff.shape[1]
    n_in = w_in.shape[2]
    chunk = 512
    assert sb_width == chunk and diff_width == chunk and d == 2 * chunk
    sbq, sbk, sbv, dq, dk, dv = range(6)
    blocks_per_chunk = chunk // LANES
    scale = 1.0 / math.sqrt(SB_HEAD_DIM)
    assert SB_HEAD_DIM == DIFF_HEAD_DIM

    tm_in = min(512, n_tok)
    tq_sb = min(256, s)
    tq_diff = min(512, s)
    tm_post = min(512, s)
    tm_tok = min(256, n_tok)
    bm = 256
    n_grid = (n_tok * TOP_K) // bm + N_EXPERTS

    cos_t, sin_t = _rope_tables(positions)
    x2d = x.reshape(n_tok, d)
    for l in range(depth):
        lambda_init = 0.8 - 0.6 * math.exp(-0.3 * l)
        proj = _in_proj(x2d, norm_mix[l].reshape(1, d), w_in[l].astype(BF16), cos_t, sin_t,
                        tm=tm_in, chunk=chunk, rope_chunks=(dq, dk), scale_chunks=(sbq, dq),
                        scale=scale)
        proj3 = proj.reshape(b, s, n_in)
        o_sb = _sb_attention(proj3, q_col=sbq * blocks_per_chunk, k_col=sbk * blocks_per_chunk,
                             v_col=sbv * blocks_per_chunk, n_pairs=sb_width // LANES, tq=tq_sb)
        o_diff = _diff_attention(
            proj3, lambda_q1[l].reshape(1, -1), lambda_k1[l].reshape(1, -1),
            lambda_q2[l].reshape(1, -1), lambda_k2[l].reshape(1, -1),
            diff_subln[l].reshape(1, -1), q_col=dq * blocks_per_chunk,
            k_col=dk * blocks_per_chunk, v_col=dv * blocks_per_chunk,
            n_heads=diff_width // DIFF_V_DIM, tq=tq_diff, lambda_init=lambda_init)
        kv = _mem_kv(mem, norm_mem_kv[l].reshape(1, d), w_mem_kv[l].astype(BF16))
        w_router_pad = jnp.zeros((d, LANES), F32).at[:, :N_EXPERTS].set(w_router[l])
        b_router_pad = jnp.full((1, LANES), NEG_BIG, F32).at[0, :N_EXPERTS].set(b_router[l])
        x_res, hf, route, gates, cum = _post_mix(
            x2d, proj, o_sb.reshape(n_tok, sb_width), o_diff.reshape(n_tok, diff_width),
            w_up_sb[l].astype(BF16), w_up_diff[l].astype(BF16), w_out[l].astype(BF16),
            norm_mem_q[l].reshape(1, d), w_mem_q[l].astype(BF16), kv, w_mem_o[l].astype(BF16),
            norm_ffn[l].reshape(1, d), w_router_pad, b_router_pad,
            tm=tm_post, seq=s, gate_col=6 * chunk // d)
        (dest, blk_expert, blk_valid, blk_src, blk_wslot, blk_next, zstart, zflag,
         n_used) = _routing_tables(route, cum, bm=bm, n_grid=n_grid + 1)
        xs = _dispatch(zstart, zflag, n_used, dest, hf, tm=tm_tok, bm=bm, n_rows=n_grid * bm)
        ys = _expert_ffn(blk_expert, blk_valid, blk_src, blk_wslot, blk_next, xs, w_gate_up[l],
                         b_gate_up[l], w_down[l], b_down[l], bm=bm, n_tok=n_tok)
        g_next = norm_final if l == depth - 1 else None
        assert g_next is not None, "only depth 1 is wired: the final norm is fused into combine"
        x2d = _combine(x_res, gates, g_next.reshape(1, d), ys, tm=tm_tok)
    return x2d.reshape(b, s, d)
```

```python
import functools
import math

import jax
import jax.numpy as jnp
from jax import lax
from jax.experimental import pallas as pl
from jax.experimental.pallas import tpu as pltpu

F32 = jnp.float32
BF16 = jnp.bfloat16
I32 = jnp.int32
U32 = jnp.uint32

LANES = 128
VMEM_LIMIT_BYTES = 56 * 1024 * 1024
ROW_TILE = 8
PACKED_SUBROWS = 4
META_SUBROW = 4

NORM_EPS = 1e-6
ROPE_THETA = 10000.0
CHUNK = 64
SB_HEAD_DIM = 64
DIFF_HEAD_DIM = 64
DIFF_V_DIM = 128
MEM_HEADS = 4
N_EXPERTS = 32
TOP_K = 4
SWIGLU_LIMIT = 7.0
SWIGLU_ALPHA = 1.702

SB_DEAD_LOG = -105.0
NEG_BIG = -1e30


def _params(semantics):
    return pltpu.CompilerParams(dimension_semantics=semantics,
                                vmem_limit_bytes=VMEM_LIMIT_BYTES)


def _const_spec(shape):
    nd = len(shape)
    return pl.BlockSpec(shape, lambda *_: (0,) * nd)


def _rms(x, g):
    return x * lax.rsqrt(jnp.mean(x * x, axis=-1, keepdims=True) + NORM_EPS) * g


def _rope_table_kernel(pos_ref, inv_ref, cos_ref, sin_ref):
    ang = pos_ref[...] * inv_ref[...]
    cos_ref[...] = jnp.cos(ang)
    sin_ref[...] = jnp.sin(ang)


def _rope_tables(positions):
    n_tok = positions.size
    half = DIFF_HEAD_DIM // 2
    per_row = LANES // half
    rows = n_tok // per_row
    pos = jnp.repeat(positions.reshape(rows, per_row).astype(F32), half, axis=1)
    inv = ROPE_THETA ** (-jnp.arange(half, dtype=F32) / half)
    inv = jnp.tile(inv, per_row).reshape(1, LANES)
    tr = min(rows, 512)
    cos, sin = pl.pallas_call(
        _rope_table_kernel,
        grid=(rows // tr,),
        in_specs=[pl.BlockSpec((tr, LANES), lambda i: (i, 0)), _const_spec((1, LANES))],
        out_specs=[pl.BlockSpec((tr, LANES), lambda i: (i, 0))] * 2,
        out_shape=[jax.ShapeDtypeStruct((rows, LANES), F32)] * 2,
        compiler_params=_params(("arbitrary",)),
        name="rope_table",
    )(pos, inv)
    cos = cos.reshape(n_tok, half)
    sin = sin.reshape(n_tok, half)
    cos_t = jnp.tile(cos, (1, LANES // half))
    sin_t = jnp.tile(jnp.concatenate([-sin, sin], axis=1), (1, LANES // (2 * half)))
    return cos_t, sin_t


def _inproj_kernel(x_ref, g_ref, w_ref, cos_ref, sin_ref, o_ref, *, chunk, rope_chunks,
                   scale_chunks, scale):
    h = _rms(x_ref[...], g_ref[...]).astype(BF16)
    lane = lax.broadcasted_iota(I32, (1, chunk), 1)
    first_half = (lane % DIFF_HEAD_DIM) < (DIFF_HEAD_DIM // 2)
    for c in range(w_ref.shape[1] // chunk):
        cols = slice(c * chunk, (c + 1) * chunk)
        acc = jnp.dot(h, w_ref[:, cols], preferred_element_type=F32)
        if c in rope_chunks:
            cos = jnp.tile(cos_ref[...], (1, chunk // LANES))
            sin = jnp.tile(sin_ref[...], (1, chunk // LANES))
            partner = jnp.where(first_half,
                                pltpu.roll(acc, chunk - DIFF_HEAD_DIM // 2, 1),
                                pltpu.roll(acc, DIFF_HEAD_DIM // 2, 1))
            acc = acc * cos + partner * sin
        if c in scale_chunks:
            acc = acc * scale
        o_ref[:, cols] = acc.astype(BF16)


def _in_proj(x2d, g, w_bf16, cos_t, sin_t, *, tm, chunk, rope_chunks, scale_chunks, scale):
    n_tok, d = x2d.shape
    n_in = w_bf16.shape[1]
    kern = functools.partial(_inproj_kernel, chunk=chunk, rope_chunks=rope_chunks,
                             scale_chunks=scale_chunks, scale=scale)
    return pl.pallas_call(
        kern,
        grid=(n_tok // tm,),
        in_specs=[pl.BlockSpec((tm, d), lambda i: (i, 0)),
                  _const_spec((1, d)),
                  _const_spec((d, n_in)),
                  pl.BlockSpec((tm, LANES), lambda i: (i, 0)),
                  pl.BlockSpec((tm, LANES), lambda i: (i, 0))],
        out_specs=pl.BlockSpec((tm, n_in), lambda i: (i, 0)),
        out_shape=jax.ShapeDtypeStruct((n_tok, n_in), BF16),
        compiler_params=_params(("arbitrary",)),
        name="in_proj",
    )(x2d, g, w_bf16, cos_t, sin_t)


def _sb_kernel(q_ref, k_ref, v_ref, o_ref, acc_ref, lw0, lw1, cat0, cat1, *, tq):
    i = pl.program_id(2)
    q = q_ref[0]
    lane = lax.broadcasted_iota(I32, (1, LANES), 1)
    zero = jnp.zeros_like(q)
    q2 = jnp.concatenate([jnp.where(lane < SB_HEAD_DIM, q, zero),
                          jnp.where(lane >= SB_HEAD_DIM, q, zero)], axis=0)
    uj = lax.broadcasted_iota(I32, (2 * tq, tq), 0) % tq
    us = lax.broadcasted_iota(I32, (2 * tq, tq), 1)
    suffix = jnp.where(uj > us, 1.0, 0.0).astype(BF16)
    lws = (lw0, lw1)
    cats = (cat0, cat1)

    def causal_mask():
        return (lax.broadcasted_iota(I32, (2 * tq, tq), 1)
                < lax.broadcasted_iota(I32, (2 * tq, tq), 0) % tq)

    def logits(j, buf, masked):
        start = pl.multiple_of(j * tq, tq)
        kj = k_ref[0, pl.ds(start, tq), :]
        z = lax.dot_general(q2, kj, (((1,), (1,)), ((), ())), preferred_element_type=F32)
        sp = jnp.maximum(z, 0.0) + jnp.log(1.0 + jnp.exp(-jnp.abs(z)))
        log_rem = -sp
        if masked:
            log_rem = jnp.where(causal_mask(), log_rem, 0.0)
        lws[buf][...] = z - sp
        hi = log_rem.astype(BF16)
        cats[buf][:, 0:tq] = hi
        cats[buf][:, tq:2 * tq] = (log_rem - hi.astype(F32)).astype(BF16)
        return log_rem[:, 0:1]

    def suffix_sums(buf, first):
        after = jnp.dot(cats[buf][...], suffix, preferred_element_type=F32)
        lws[buf][...] += after
        return after[:, 0:1] + first

    def weigh(j, buf, carry, masked):
        start = pl.multiple_of(j * tq, tq)
        vj = v_ref[0, pl.ds(start, tq), :]
        w = jnp.exp(lws[buf][...] + carry)
        if masked:
            w = jnp.where(causal_mask(), w, 0.0)
        acc_ref[...] += jnp.dot(w.astype(BF16), vj, preferred_element_type=F32)

    def earlier_blocks(first_block, carry):
        def cond(state):
            j, _, alive = state
            return jnp.logical_and(j >= 0, alive > SB_DEAD_LOG)

        def body(state):
            j, carry, _ = state
            total = suffix_sums(0, logits(j, 0, False))
            weigh(j, 0, carry, False)
            carry = carry + total
            return j - 1, carry, jnp.max(carry)

        lax.while_loop(cond, body, (first_block, carry, jnp.max(carry)))

    acc_ref[...] = jnp.zeros_like(acc_ref)
    no_carry = jnp.zeros((2 * tq, 1), F32)

    @pl.when(i == 0)
    def _():
        suffix_sums(0, logits(0, 0, True))
        weigh(0, 0, no_carry, True)

    @pl.when(i > 0)
    def _():
        first_diag = logits(i, 0, True)
        first_prev = logits(i - 1, 1, False)
        total_diag = suffix_sums(0, first_diag)
        total_prev = suffix_sums(1, first_prev)
        weigh(i, 0, no_carry, True)
        weigh(i - 1, 1, total_diag, False)
        earlier_blocks(i - 2, total_diag + total_prev)

    o_ref[0] = jnp.where(lane < SB_HEAD_DIM, acc_ref[0:tq, :], acc_ref[tq:2 * tq, :]).astype(BF16)


def _sb_attention(proj3, *, q_col, k_col, v_col, n_pairs, tq):
    b, s, _ = proj3.shape
    return pl.pallas_call(
        functools.partial(_sb_kernel, tq=tq),
        grid=(b, n_pairs, s // tq),
        in_specs=[pl.BlockSpec((1, tq, LANES), lambda bi, p, i: (bi, i, q_col + p)),
                  pl.BlockSpec((1, s, LANES), lambda bi, p, i: (bi, 0, k_col + p)),
                  pl.BlockSpec((1, s, LANES), lambda bi, p, i: (bi, 0, v_col + p))],
        out_specs=pl.BlockSpec((1, tq, LANES), lambda bi, p, i: (bi, i, p)),
        out_shape=jax.ShapeDtypeStruct((b, s, n_pairs * LANES), BF16),
        scratch_shapes=[pltpu.VMEM((2 * tq, LANES), F32),
                        pltpu.VMEM((2 * tq, tq), F32), pltpu.VMEM((2 * tq, tq), F32),
                        pltpu.VMEM((2 * tq, 2 * tq), BF16), pltpu.VMEM((2 * tq, 2 * tq), BF16)],
        compiler_params=_params(("arbitrary", "arbitrary", "arbitrary")),
        name="sb_attention",
    )(proj3, proj3, proj3)


DIFF_ONES_ROWS = 16


def _diff_kernel(lq1_ref, lk1_ref, lq2_ref, lk2_ref, q_ref, k_ref, v_ref, g_ref, o_ref, vt_ref,
                 acc_ref, z0_ref, z1_ref, *, tq, lambda_init):
    i = pl.program_id(2)
    s_len = v_ref.shape[1]
    vd = v_ref.shape[2]
    half = tq // 2
    zs = (z0_ref, z1_ref)

    @pl.when(i == 0)
    def _():
        for c in range(s_len // tq):
            cols = slice(c * tq, (c + 1) * tq)
            vt_ref[0:vd, cols] = v_ref[0, cols, :].astype(F32).T.astype(BF16)
        rid = lax.broadcasted_iota(I32, (DIFF_ONES_ROWS, s_len), 0)
        vt_ref[vd:vd + DIFF_ONES_ROWS, :] = jnp.where(rid == 0, 1.0, 0.0).astype(BF16)

    q = q_ref[0]
    lane = lax.broadcasted_iota(I32, (1, LANES), 1)
    zero = jnp.zeros_like(q)
    q2 = jnp.concatenate([jnp.where(lane < DIFF_HEAD_DIM, q, zero),
                          jnp.where(lane >= DIFF_HEAD_DIM, q, zero)], axis=0)

    def scores(j, h, masked):
        start = pl.multiple_of(j * tq + h * half, half)
        kj = k_ref[0, pl.ds(start, half), :]
        zt = lax.dot_general(kj, q2, (((1,), (1,)), ((), ())), preferred_element_type=F32)
        if masked:
            qidx = lax.broadcasted_iota(I32, (half, 2 * tq), 1) % tq
            kidx = lax.broadcasted_iota(I32, (half, 2 * tq), 0) + h * half
            zt = jnp.where(kidx // CHUNK <= qidx // CHUNK, zt, -jnp.inf)
        zs[h][...] = zt
        return jnp.max(zt, axis=0, keepdims=True)

    def values(j, h, m, mblk):
        start = pl.multiple_of(j * tq + h * half, half)
        vtj = vt_ref[:, pl.ds(start, half)]
        m_new = jnp.maximum(m, mblk)
        pt = jnp.exp(zs[h][...] - m_new).astype(BF16)
        acc_ref[...] = jnp.exp(m - m_new) * acc_ref[...] + jnp.dot(vtj, pt,
                                                                   preferred_element_type=F32)
        return m_new

    def block(j, m, mb0, masked, next_kind):
        mb1 = scores(j, 1, masked)
        m = values(j, 0, m, mb0)
        nb0 = mb0 if next_kind is None else scores(j + 1, 0, next_kind == "masked")
        m = values(j, 1, m, mb1)
        return m, nb0

    acc_ref[...] = jnp.zeros_like(acc_ref)
    m0 = jnp.full((1, 2 * tq), -jnp.inf, F32)

    @pl.when(i == 0)
    def _():
        block(0, m0, scores(0, 0, True), True, None)

    @pl.when(i > 0)
    def _():
        state = (m0, scores(0, 0, False))
        state = lax.fori_loop(0, i - 1, lambda j, st: block(j, st[0], st[1], False, "plain"),
                              state)
        state = block(i - 1, state[0], state[1], False, "masked")
        block(i, state[0], state[1], True, None)

    lam = (jnp.exp(jnp.sum(lq1_ref[...] * lk1_ref[...], axis=1, keepdims=True))
           - jnp.exp(jnp.sum(lq2_ref[...] * lk2_ref[...], axis=1, keepdims=True))
           + lambda_init)
    ot = acc_ref[0:vd, :] / acc_ref[vd:vd + 1, :]
    ot = ot[:, 0:tq] - lam * ot[:, tq:2 * tq]
    o_ref[0] = (_rms(ot.T, g_ref[...]) * (1.0 - lambda_init)).astype(BF16)


def _diff_attention(proj3, lq1, lk1, lq2, lk2, subln, *, q_col, k_col, v_col, n_heads, tq,
                    lambda_init):
    b, s, _ = proj3.shape
    lam_spec = _const_spec((1, DIFF_HEAD_DIM))
    return pl.pallas_call(
        functools.partial(_diff_kernel, tq=tq, lambda_init=lambda_init),
        grid=(b, n_heads, s // tq),
        in_specs=[lam_spec, lam_spec, lam_spec, lam_spec,
                  pl.BlockSpec((1, tq, LANES), lambda bi, h, i: (bi, i, q_col + h)),
                  pl.BlockSpec((1, s, LANES), lambda bi, h, i: (bi, 0, k_col + h)),
                  pl.BlockSpec((1, s, LANES), lambda bi, h, i: (bi, 0, v_col + h)),
                  _const_spec((1, DIFF_V_DIM))],
        out_specs=pl.BlockSpec((1, tq, LANES), lambda bi, h, i: (bi, i, h)),
        out_shape=jax.ShapeDtypeStruct((b, s, n_heads * DIFF_V_DIM), BF16),
        scratch_shapes=[pltpu.VMEM((DIFF_V_DIM + DIFF_ONES_ROWS, s), BF16),
                        pltpu.VMEM((DIFF_V_DIM + DIFF_ONES_ROWS, 2 * tq), F32),
                        pltpu.VMEM((tq // 2, 2 * tq), F32), pltpu.VMEM((tq // 2, 2 * tq), F32)],
        compiler_params=_params(("arbitrary", "arbitrary", "arbitrary")),
        name="diff_attention",
    )(lq1, lk1, lq2, lk2, proj3, proj3, proj3, subln)


def _mem_kv_kernel(mem_ref, g_ref, w_ref, o_ref):
    h = _rms(mem_ref[0], g_ref[...]).astype(BF16)
    o_ref[0] = jnp.dot(h, w_ref[...], preferred_element_type=F32).astype(BF16)


def _mem_kv(mem, g, w_bf16):
    b, m, d = mem.shape
    n = w_bf16.shape[1]
    return pl.pallas_call(
        _mem_kv_kernel,
        grid=(b,),
        in_specs=[pl.BlockSpec((1, m, d), lambda i: (i, 0, 0)), _const_spec((1, d)),
                  _const_spec((d, n))],
        out_specs=pl.BlockSpec((1, m, n), lambda i: (i, 0, 0)),
        out_shape=jax.ShapeDtypeStruct((b, m, n), BF16),
        compiler_params=_params(("arbitrary",)),
        name="mem_kv",
    )(mem, g, w_bf16)


def _split_bf16(v):
    hi = v.astype(BF16)
    return hi, (v - hi.astype(F32)).astype(BF16)


def _pack_bf16_pairs(v):
    half = v.shape[1] // 2
    bits = lax.bitcast_convert_type(v.astype(BF16).astype(F32), U32)
    return (bits[:, :half] >> 16) | (bits[:, half:] & jnp.uint32(0xFFFF0000))


def _unpack_bf16_pairs(w):
    lo = lax.bitcast_convert_type(w << 16, F32)
    hi = lax.bitcast_convert_type(w & jnp.uint32(0xFFFF0000), F32)
    return jnp.concatenate([lo, hi], axis=1).astype(BF16)


def _postmix_kernel(x_ref, osb_ref, odf_ref, gsb_ref, gdf_ref, wus_ref, wud_ref, wout_ref,
                    gq_ref, wq_ref, kv_ref, wo_ref, gf_ref, wr_ref, br_ref,
                    x2_ref, hf_ref, route_ref, gate_ref, cum_ref, count_ref, *, tm, d_model):
    step = pl.program_id(0)

    @pl.when(step == 0)
    def _():
        count_ref[...] = jnp.zeros_like(count_ref)

    y_sb = jnp.dot(osb_ref[...], wus_ref[...], preferred_element_type=F32)
    y_df = jnp.dot(odf_ref[...], wud_ref[...], preferred_element_type=F32)
    mixed = (jax.nn.sigmoid(gsb_ref[...].astype(F32)) * y_sb
             + jax.nn.sigmoid(gdf_ref[...].astype(F32)) * y_df)
    x1 = x_ref[...] + jnp.dot(mixed.astype(BF16), wout_ref[...], preferred_element_type=F32)

    hq = _rms(x1, gq_ref[...]).astype(BF16)
    hd = d_model // MEM_HEADS
    q = jnp.dot(hq, wq_ref[...], preferred_element_type=F32) * (1.0 / math.sqrt(hd))
    q = q.astype(BF16)
    heads = []
    for h in range(MEM_HEADS):
        kh = kv_ref[0, :, h * hd:(h + 1) * hd]
        vh = kv_ref[0, :, d_model + h * hd:d_model + (h + 1) * hd]
        z = lax.dot_general(q[:, h * hd:(h + 1) * hd], kh, (((1,), (1,)), ((), ())),
                            preferred_element_type=F32)
        p = jnp.exp(z - jnp.max(z, axis=1, keepdims=True))
        l = jnp.sum(p, axis=1, keepdims=True)
        heads.append((jnp.dot(p.astype(BF16), vh, preferred_element_type=F32) / l).astype(BF16))
    x2 = x1 + jnp.dot(jnp.concatenate(heads, axis=1), wo_ref[...], preferred_element_type=F32)
    x2_ref[...] = x2

    hf = _rms(x2, gf_ref[...])
    packed = _pack_bf16_pairs(hf)
    for j in range(PACKED_SUBROWS):
        hf_ref[pl.ds(j, tm, stride=ROW_TILE), :] = packed[:, j * LANES:(j + 1) * LANES]
    h_hi, h_lo = _split_bf16(hf)
    w_hi, w_lo = _split_bf16(wr_ref[...])
    logits = (jnp.dot(h_hi, w_hi, preferred_element_type=F32)
              + jnp.dot(h_hi, w_lo, preferred_element_type=F32)
              + jnp.dot(h_lo, w_hi, preferred_element_type=F32)) + br_ref[...]
    lane = lax.broadcasted_iota(I32, (tm, LANES), 1)
    work = logits
    vals, idxs, hots = [], [], []
    for _ in range(TOP_K):
        mx = jnp.max(work, axis=1, keepdims=True)
        idx = jnp.min(jnp.where(work == mx, lane, LANES), axis=1, keepdims=True)
        hot = lane == idx
        work = jnp.where(hot, NEG_BIG, work)
        vals.append(mx)
        idxs.append(idx)
        hots.append(hot)
    exps = [jnp.exp(v - vals[0]) for v in vals]
    denom = exps[0] + exps[1] + exps[2] + exps[3]

    onehot_sum = jnp.zeros((tm, LANES), F32)
    for hot in hots:
        onehot_sum = onehot_sum + jnp.where(hot, 1.0, 0.0)
    r = lax.broadcasted_iota(I32, (tm, tm), 0)
    c = lax.broadcasted_iota(I32, (tm, tm), 1)
    lower = jnp.where(c < r, 1.0, 0.0).astype(BF16)
    rank = jnp.dot(lower, onehot_sum.astype(BF16), preferred_element_type=F32) + count_ref[...]
    route = jnp.zeros((tm, LANES), I32)
    gates = jnp.zeros((tm, LANES), F32)
    for k in range(TOP_K):
        pos = jnp.sum(jnp.where(hots[k], rank, 0.0), axis=1, keepdims=True).astype(I32)
        route = jnp.where(lane == k, idxs[k], route)
        route = jnp.where(lane == TOP_K + k, pos, route)
        gates = jnp.where(lane == k, exps[k] / denom, gates)
    route_ref[...] = route
    gate_ref[...] = gates
    meta = jnp.where(lane == 0, step * tm + lax.broadcasted_iota(I32, (tm, LANES), 0), 0)
    for k in range(TOP_K):
        meta = jnp.where(lane == 1 + k, idxs[k], meta)
    hf_ref[pl.ds(META_SUBROW, tm, stride=ROW_TILE), :] = meta.astype(U32)
    for j in range(META_SUBROW + 1, ROW_TILE):
        hf_ref[pl.ds(j, tm, stride=ROW_TILE), :] = jnp.zeros((tm, LANES), U32)
    count_ref[...] = count_ref[...] + jnp.sum(onehot_sum, axis=0, keepdims=True)
    cum_ref[0] = jnp.broadcast_to(count_ref[...], (8, LANES))


def _post_mix(x2d, proj2, o_sb, o_diff, w_up_sb, w_up_diff, w_out, g_memq, w_memq, kv, w_memo,
              g_ffn, w_router_pad, b_router_pad, *, tm, seq, gate_col):
    n_tok, d = x2d.shape
    n_tiles = n_tok // tm
    tiles_per_batch = seq // tm
    row = lambda i: (i, 0)
    in_specs = [
        pl.BlockSpec((tm, d), row),
        pl.BlockSpec((tm, o_sb.shape[1]), row),
        pl.BlockSpec((tm, o_diff.shape[1]), row),
        pl.BlockSpec((tm, d), lambda i: (i, gate_col)),
        pl.BlockSpec((tm, d), lambda i: (i, gate_col + 1)),
        _const_spec(w_up_sb.shape), _const_spec(w_up_diff.shape), _const_spec(w_out.shape),
        _const_spec((1, d)), _const_spec(w_memq.shape),
        pl.BlockSpec((1,) + kv.shape[1:], lambda i: (i // tiles_per_batch, 0, 0)),
        _const_spec(w_memo.shape), _const_spec((1, d)),
        _const_spec(w_router_pad.shape), _const_spec((1, LANES)),
    ]
    out_specs = [
        pl.BlockSpec((tm, d), row),
        pl.BlockSpec((tm * ROW_TILE, LANES), row),
        pl.BlockSpec((tm, LANES), row),
        pl.BlockSpec((tm, LANES), row),
        pl.BlockSpec((1, 8, LANES), lambda i: (i, 0, 0)),
    ]
    assert d // 2 == PACKED_SUBROWS * LANES
    out_shape = [
        jax.ShapeDtypeStruct((n_tok, d), F32),
        jax.ShapeDtypeStruct((n_tok * ROW_TILE, LANES), U32),
        jax.ShapeDtypeStruct((n_tok, LANES), I32),
        jax.ShapeDtypeStruct((n_tok, LANES), F32),
        jax.ShapeDtypeStruct((n_tiles, 8, LANES), F32),
    ]
    return pl.pallas_call(
        functools.partial(_postmix_kernel, tm=tm, d_model=d),
        grid=(n_tiles,),
        in_specs=in_specs,
        out_specs=out_specs,
        out_shape=out_shape,
        scratch_shapes=[pltpu.VMEM((1, LANES), F32)],
        compiler_params=_params(("arbitrary",)),
        name="post_mix",
    )(x2d, o_sb, o_diff, proj2, proj2, w_up_sb, w_up_diff, w_out, g_memq, w_memq, kv, w_memo,
      g_ffn, w_router_pad, b_router_pad)


def _dispatch_kernel(zstart_ref, zflag_ref, nused_ref, dest_ref, hf_ref, xs_hbm, stage0, stage1,
                     zbuf, sem, zsem, *, tm, bm):
    s = pl.program_id(0)

    @pl.when(s == 0)
    def _():
        lane = lax.broadcasted_iota(I32, (bm, LANES), 1)
        zbuf[...] = jnp.zeros_like(zbuf)
        zbuf[pl.ds(META_SUBROW, bm, stride=ROW_TILE), :] = jnp.where(
            jnp.logical_and(lane >= 1, lane <= TOP_K), -1, 0).astype(U32)

        def fill_copy(start):
            start = pl.multiple_of(start * ROW_TILE, bm * ROW_TILE)
            return pltpu.make_async_copy(zbuf, xs_hbm.at[pl.ds(start, bm * ROW_TILE), :], zsem)

        n_blocks = xs_hbm.shape[0] // (bm * ROW_TILE)
        for action in ("start", "wait"):
            for e in range(N_EXPERTS):
                @pl.when(zflag_ref[e] != 0)
                def _(e=e, action=action):
                    getattr(fill_copy(zstart_ref[e]), action)()

            def trailing(blk, _, action=action):
                getattr(fill_copy(blk * bm), action)()
                return 0

            lax.fori_loop(nused_ref[0], n_blocks, trailing, 0)

    def tile_done(stage, parity):
        for _ in range(TOP_K):
            pltpu.make_async_copy(stage, xs_hbm.at[pl.ds(0, tm * ROW_TILE), :],
                                  sem.at[parity]).wait()

    def run(stage, other, parity):
        stage[...] = hf_ref[...]

        def issue(t, _):
            src = stage.at[pl.ds(pl.multiple_of(t * ROW_TILE, ROW_TILE), ROW_TILE), :]
            for k in range(TOP_K):
                row = pl.multiple_of(dest_ref[t * TOP_K + k] * ROW_TILE, ROW_TILE)
                pltpu.make_async_copy(src, xs_hbm.at[pl.ds(row, ROW_TILE), :],
                                      sem.at[parity]).start(priority=k % 2)
            return 0

        lax.fori_loop(0, tm, issue, 0)

        @pl.when(s > 0)
        def _():
            tile_done(other, 1 - parity)

        @pl.when(s == pl.num_programs(0) - 1)
        def _():
            tile_done(stage, parity)

    @pl.when(s % 2 == 0)
    def _():
        run(stage0, stage1, 0)

    @pl.when(s % 2 == 1)
    def _():
        run(stage1, stage0, 1)


def _dispatch(zstart, zflag, n_used, dest_flat, hf_rows, *, tm, bm, n_rows):
    n_tok = hf_rows.shape[0] // ROW_TILE
    grid_spec = pltpu.PrefetchScalarGridSpec(
        num_scalar_prefetch=3,
        grid=(n_tok // tm,),
        in_specs=[
            pl.BlockSpec((tm * TOP_K,), lambda i, *_: (i,), memory_space=pltpu.SMEM),
            pl.BlockSpec((tm * ROW_TILE, LANES), lambda i, *_: (i, 0)),
        ],
        out_specs=pl.BlockSpec(memory_space=pl.ANY),
        scratch_shapes=[pltpu.VMEM((tm * ROW_TILE, LANES), U32),
                        pltpu.VMEM((tm * ROW_TILE, LANES), U32),
                        pltpu.VMEM((bm * ROW_TILE, LANES), U32),
                        pltpu.SemaphoreType.DMA((2,)), pltpu.SemaphoreType.DMA(())],
    )
    return pl.pallas_call(
        functools.partial(_dispatch_kernel, tm=tm, bm=bm),
        grid_spec=grid_spec,
        out_shape=jax.ShapeDtypeStruct((n_rows * ROW_TILE, LANES), U32),
        compiler_params=_params(("arbitrary",)),
        name="dispatch",
    )(zstart, zflag, n_used, dest_flat, hf_rows)


FFN_UP_CHUNKS = 8
FFN_DOWN_CHUNKS = 4


def _expert_kernel(be_ref, valid_ref, src_ref, wslot_ref, next_ref, xs_ref, bgu_ref, bd_ref,
                   wgu_hbm, wd_hbm, ys_hbm, wgu_f32, wd_f32, wgu_bf, wd_bf, ybuf0, ybuf1, slot_v,
                   slot_s, sem_w, sem_y, sem_s, sem_z, *, d_ff, bm, n_tok):
    r = pl.program_id(0)
    valid = valid_ref[r] != 0
    prev_valid = jnp.logical_and(r > 0, valid_ref[jnp.maximum(r - 1, 0)] != 0)
    fresh = jnp.logical_or(r == 0, be_ref[r] != be_ref[jnp.maximum(r - 1, 0)])
    expert = be_ref[r]
    ybufs = (ybuf0, ybuf1)

    def weight_copies(e, wslot):
        return (pltpu.make_async_copy(wgu_hbm.at[e], wgu_f32.at[wslot], sem_w.at[wslot]),
                pltpu.make_async_copy(wd_hbm.at[e], wd_f32.at[wslot], sem_w.at[wslot]))
    d = wd_bf.shape[1]
    n_slots = TOP_K * n_tok
    block_rows = bm * ROW_TILE

    def rows_done(parity):
        return pltpu.make_async_copy(ybufs[parity], ys_hbm.at[pl.ds(0, block_rows), :],
                                     sem_y.at[parity])

    def slots_copy(parity):
        return pltpu.make_async_copy(slot_v.at[0:1, :], slot_s.at[parity:parity + 1, :],
                                     sem_s.at[parity])

    def scatter_rows(prev, lo, hi):
        for i in range(lo, hi):
            dst = pl.multiple_of(slot_s[prev, i] * ROW_TILE, ROW_TILE)
            pltpu.make_async_copy(ybufs[prev].at[pl.ds(i * ROW_TILE, ROW_TILE), :],
                                  ys_hbm.at[pl.ds(dst, ROW_TILE), :],
                                  sem_y.at[prev]).start(priority=i % 2)

    @pl.when(r == 0)
    def _():
        for action in ("start", "wait"):
            for parity in range(2):
                if action == "start":
                    ybufs[parity][...] = jnp.zeros_like(ybufs[parity])
                trash = (n_slots + parity * bm) * ROW_TILE
                getattr(pltpu.make_async_copy(ybufs[parity],
                                              ys_hbm.at[pl.ds(trash, block_rows), :], sem_z),
                        action)()

    for parity in range(2):
        @pl.when(jnp.logical_and(prev_valid, r % 2 == parity))
        def _(parity=parity):
            slots_copy(1 - parity).wait()

    @pl.when(jnp.logical_and(valid, fresh))
    def _():
        wslot = wslot_ref[r]

        @pl.when(r == 0)
        def _():
            for c in weight_copies(expert, wslot):
                c.start()

        for c in weight_copies(expert, wslot):
            c.wait()
        wgu_bf[...] = wgu_f32[wslot].astype(BF16)
        wd_bf[...] = wd_f32[wslot].astype(BF16)

        @pl.when(next_ref[r] >= 0)
        def _():
            for c in weight_copies(next_ref[r], 1 - wslot):
                c.start()

    def ffn(cur, interleave):
        prev = 1 - cur
        ybuf = ybufs[cur]
        bounds = [bm * c // FFN_UP_CHUNKS for c in range(FFN_UP_CHUNKS + 1)]

        @pl.when(r >= 2)
        def _():
            rows_done(cur).wait()

        meta = xs_ref[pl.ds(META_SUBROW, bm, stride=ROW_TILE), :].astype(I32)
        row = lax.broadcasted_iota(I32, (bm, 1), 0)
        slot = n_slots + cur * bm + row
        for k in range(TOP_K):
            slot = jnp.where(meta[:, 1 + k:2 + k] == be_ref[r], k * n_tok + meta[:, 0:1], slot)
        slots = jnp.broadcast_to(slot.astype(F32), (bm, LANES)).T
        slot_v[...] = slots[0:8, :].astype(I32)
        slots_copy(cur).start()

        words = jnp.concatenate([xs_ref[pl.ds(j, bm, stride=ROW_TILE), :]
                                 for j in range(PACKED_SUBROWS)], axis=1)
        xb = _unpack_bf16_pairs(words)
        up_w = 2 * d_ff // FFN_UP_CHUNKS
        pairs = FFN_UP_CHUNKS // 2
        acts = []
        for c in range(pairs):
            halves = []
            for half in range(2):
                cols = slice(half * d_ff + c * up_w, half * d_ff + (c + 1) * up_w)
                halves.append(jnp.dot(xb, wgu_bf[:, cols], preferred_element_type=F32)
                              + bgu_ref[pl.ds(expert, 1), cols])
                if interleave:
                    step_no = 2 * c + half
                    scatter_rows(prev, bounds[step_no], bounds[step_no + 1])
            glu = jnp.minimum(halves[0], SWIGLU_LIMIT)
            lin = jnp.clip(halves[1], -SWIGLU_LIMIT, SWIGLU_LIMIT)
            acts.append((glu * jax.nn.sigmoid(SWIGLU_ALPHA * glu) * (lin + 1.0)).astype(BF16))
        act = jnp.concatenate(acts, axis=1)
        down_w = d // FFN_DOWN_CHUNKS
        for c in range(FFN_DOWN_CHUNKS):
            cols = slice(c * down_w, (c + 1) * down_w)
            y = (jnp.dot(act, wd_bf[:, cols], preferred_element_type=F32)
                 + bd_ref[pl.ds(expert, 1), cols])
            for j in range(down_w // LANES):
                sub = c * (down_w // LANES) + j
                ybuf[pl.ds(sub, bm, stride=ROW_TILE), :] = y[:, j * LANES:(j + 1) * LANES]

    for parity in range(2):
        on_parity = r % 2 == parity

        @pl.when(jnp.logical_and(on_parity, jnp.logical_and(valid, prev_valid)))
        def _(parity=parity):
            ffn(parity, True)

        @pl.when(jnp.logical_and(on_parity, jnp.logical_and(jnp.logical_not(valid), prev_valid)))
        def _(parity=parity):
            scatter_rows(1 - parity, 0, bm)

            @pl.when(r >= 2)
            def _():
                rows_done(parity).wait()

            rows_done(1 - parity).wait()

    @pl.when(r == 0)
    def _():
        ffn(0, False)


def _expert_ffn(blk_expert, blk_valid, blk_src, blk_wslot, blk_next, xs, w_gu, b_gu, w_down,
                b_down, *, bm, n_tok):
    n_grid = blk_expert.shape[0]
    n_exp, d_ff, d = w_down.shape
    assert d == ROW_TILE * LANES
    block_rows = bm * ROW_TILE
    grid_spec = pltpu.PrefetchScalarGridSpec(
        num_scalar_prefetch=5,
        grid=(n_grid,),
        in_specs=[
            pl.BlockSpec((block_rows, LANES), lambda r, be, valid, src, *_: (src[r], 0)),
            pl.BlockSpec((n_exp, 2 * d_ff), lambda r, *_: (0, 0)),
            pl.BlockSpec((n_exp, d), lambda r, *_: (0, 0)),
            pl.BlockSpec(memory_space=pl.ANY),
            pl.BlockSpec(memory_space=pl.ANY),
        ],
        out_specs=pl.BlockSpec(memory_space=pl.ANY),
        scratch_shapes=[pltpu.VMEM((2, d, 2 * d_ff), F32), pltpu.VMEM((2, d_ff, d), F32),
                        pltpu.VMEM((d, 2 * d_ff), BF16), pltpu.VMEM((d_ff, d), BF16),
                        pltpu.VMEM((block_rows, LANES), F32),
                        pltpu.VMEM((block_rows, LANES), F32), pltpu.VMEM((8, bm), I32),
                        pltpu.SMEM((2, bm), I32), pltpu.SemaphoreType.DMA((2,)),
                        pltpu.SemaphoreType.DMA((2,)), pltpu.SemaphoreType.DMA((2,)),
                        pltpu.SemaphoreType.DMA(())],
    )
    return pl.pallas_call(
        functools.partial(_expert_kernel, d_ff=d_ff, bm=bm, n_tok=n_tok),
        grid_spec=grid_spec,
        out_shape=jax.ShapeDtypeStruct(((TOP_K * n_tok + 2 * bm) * ROW_TILE, LANES), F32),
        compiler_params=_params(("arbitrary",)),
        name="expert_ffn",
    )(blk_expert, blk_valid, blk_src, blk_wslot, blk_next, xs, b_gu, b_down, w_gu, w_down)


def _combine_kernel(x_ref, gate_ref, g_ref, y0_ref, y1_ref, y2_ref, y3_ref, o_ref):
    tm, d = x_ref.shape
    gates = gate_ref[...]
    acc = x_ref[...]
    for k, y_ref in enumerate((y0_ref, y1_ref, y2_ref, y3_ref)):
        y = jnp.concatenate([y_ref[pl.ds(j, tm, stride=ROW_TILE), :]
                             for j in range(d // LANES)], axis=1)
        acc = acc + gates[:, k:k + 1] * y
    o_ref[...] = _rms(acc, g_ref[...])


def _combine(x2, gates, g_final, ys, *, tm):
    n_tok, d = x2.shape
    tiles = n_tok // tm
    slot_spec = lambda k: pl.BlockSpec((tm * ROW_TILE, LANES), lambda i: (k * tiles + i, 0))
    return pl.pallas_call(
        _combine_kernel,
        grid=(tiles,),
        in_specs=[
            pl.BlockSpec((tm, d), lambda i: (i, 0)),
            pl.BlockSpec((tm, LANES), lambda i: (i, 0)),
            pl.BlockSpec((1, d), lambda i: (0, 0)),
        ] + [slot_spec(k) for k in range(TOP_K)],
        out_specs=pl.BlockSpec((tm, d), lambda i: (i, 0)),
        out_shape=jax.ShapeDtypeStruct((n_tok, d), F32),
        compiler_params=_params(("arbitrary",)),
        name="combine",
    )(x2, gates, g_final, ys, ys, ys, ys)


def _routing_tables(route, cum, *, bm, n_grid):
    expert = route[:, 0:TOP_K]
    pos = route[:, TOP_K:2 * TOP_K]
    experts = jnp.arange(N_EXPERTS, dtype=I32)
    counts = cum[-1, 0, :N_EXPERTS].astype(I32)
    padded = (counts + bm - 1) // bm * bm
    pend = jnp.cumsum(padded)
    pstart = pend - padded
    dest = pos + jnp.sum(jnp.where(expert[:, :, None] == experts, pstart, 0), axis=2)

    blk = jnp.arange(n_grid, dtype=I32)
    blk_expert = jnp.minimum(jnp.sum(pend[None, :] <= (blk * bm)[:, None], axis=1),
                             N_EXPERTS - 1).astype(I32)
    n_used = pend[-1] // bm
    blk_valid = (blk < n_used).astype(I32)
    blk_src = jnp.minimum(blk, jnp.maximum(n_used - 1, 0)).astype(I32)
    zflag = (padded > 0).astype(I32)
    zstart = jnp.maximum(pend - bm, 0).astype(I32)
    wslot = (jnp.cumsum(zflag) - 1) % 2
    later = jnp.logical_and(experts[None, :] > experts[:, None], zflag[None, :] > 0)
    nxt = jnp.min(jnp.where(later, experts[None, :], N_EXPERTS), axis=1)
    nxt = jnp.where(nxt < N_EXPERTS, nxt, -1)
    owner = blk_expert[:, None] == experts[None, :]
    of_block = lambda v: jnp.sum(jnp.where(owner, v[None, :], 0), axis=1).astype(I32)
    return (dest.astype(I32).reshape(-1), blk_expert, blk_valid, blk_src, of_block(wslot),
            of_block(nxt), zstart, zflag, n_used.astype(I32).reshape(1))


def kernel(x, mem, positions, norm_mix, w_in, lambda_q1, lambda_k1, lambda_q2, lambda_k2,
           diff_subln, w_up_sb, w_up_diff, w_out, norm_mem_q, norm_mem_kv, w_mem_q, w_mem_kv,
           w_mem_o, norm_ffn, w_router, b_router, w_gate_up, b_gate_up, w_down, b_down,
           norm_final):
    b, s, d = x.shape
    n_tok = b * s
    depth = norm_mix.shape[0]
    sb_width = w_up_sb.shape[1]
    diff_width = w_up_diff.shape[1]
    n_in = w_in.shape[2]
    chunk = 512
    assert sb_width == chunk and diff_width == chunk and d == 2 * chunk
    sbq, sbk, sbv, dq, dk, dv = range(6)
    blocks_per_chunk = chunk // LANES
    scale = 1.0 / math.sqrt(SB_HEAD_DIM)
    assert SB_HEAD_DIM == DIFF_HEAD_DIM

    tm_in = min(512, n_tok)
    tq_sb = min(256, s)
    tq_diff = min(512, s)
    tm_post = min(512, s)
    tm_tok = min(256, n_tok)
    bm = 256
    n_grid = (n_tok * TOP_K) // bm + N_EXPERTS

    cos_t, sin_t = _rope_tables(positions)
    x2d = x.reshape(n_tok, d)
    for l in range(depth):
        lambda_init = 0.8 - 0.6 * math.exp(-0.3 * l)
        proj = _in_proj(x2d, norm_mix[l].reshape(1, d), w_in[l].astype(BF16), cos_t, sin_t,
                        tm=tm_in, chunk=chunk, rope_chunks=(dq, dk), scale_chunks=(sbq, dq),
                        scale=scale)
        proj3 = proj.reshape(b, s, n_in)
        o_sb = _sb_attention(proj3, q_col=sbq * blocks_per_chunk, k_col=sbk * blocks_per_chunk,
                             v_col=sbv * blocks_per_chunk, n_pairs=sb_width // LANES, tq=tq_sb)
        o_diff = _diff_attention(
            proj3, lambda_q1[l].reshape(1, -1), lambda_k1[l].reshape(1, -1),
            lambda_q2[l].reshape(1, -1), lambda_k2[l].reshape(1, -1),
            diff_subln[l].reshape(1, -1), q_col=dq * blocks_per_chunk,
            k_col=dk * blocks_per_chunk, v_col=dv * blocks_per_chunk,
            n_heads=diff_width // DIFF_V_DIM, tq=tq_diff, lambda_init=lambda_init)
        kv = _mem_kv(mem, norm_mem_kv[l].reshape(1, d), w_mem_kv[l].astype(BF16))
        w_router_pad = jnp.zeros((d, LANES), F32).at[:, :N_EXPERTS].set(w_router[l])
        b_router_pad = jnp.full((1, LANES), NEG_BIG, F32).at[0, :N_EXPERTS].set(b_router[l])
        x_res, hf, route, gates, cum = _post_mix(
            x2d, proj, o_sb.reshape(n_tok, sb_width), o_diff.reshape(n_tok, diff_width),
            w_up_sb[l].astype(BF16), w_up_diff[l].astype(BF16), w_out[l].astype(BF16),
            norm_mem_q[l].reshape(1, d), w_mem_q[l].astype(BF16), kv, w_mem_o[l].astype(BF16),
            norm_ffn[l].reshape(1, d), w_router_pad, b_router_pad,
            tm=tm_post, seq=s, gate_col=6 * chunk // d)
        (dest, blk_expert, blk_valid, blk_src, blk_wslot, blk_next, zstart, zflag,
         n_used) = _routing_tables(route, cum, bm=bm, n_grid=n_grid + 1)
        xs = _dispatch(zstart, zflag, n_used, dest, hf, tm=tm_tok, bm=bm, n_rows=n_grid * bm)
        ys = _expert_ffn(blk_expert, blk_valid, blk_src, blk_wslot, blk_next, xs, w_gate_up[l],
                         b_gate_up[l], w_down[l], b_down[l], bm=bm, n_tok=n_tok)
        g_next = norm_final if l == depth - 1 else None
        assert g_next is not None, "only depth 1 is wired: the final norm is fused into combine"
        x2d = _combine(x_res, gates, g_next.reshape(1, d), ys, tm=tm_tok)
    return x2d.reshape(b, s, d)
```

```python
import functools
import math

import jax
import jax.numpy as jnp
from jax import lax
from jax.experimental import pallas as pl
from jax.experimental.pallas import tpu as pltpu

F32 = jnp.float32
BF16 = jnp.bfloat16
I32 = jnp.int32
U32 = jnp.uint32

LANES = 128
VMEM_LIMIT_BYTES = 56 * 1024 * 1024
ROW_TILE = 8
PACKED_SUBROWS = 4
META_SUBROW = 4

NORM_EPS = 1e-6
ROPE_THETA = 10000.0
CHUNK = 64
SB_HEAD_DIM = 64
DIFF_HEAD_DIM = 64
DIFF_V_DIM = 128
MEM_HEADS = 4
N_EXPERTS = 32
TOP_K = 4
SWIGLU_LIMIT = 7.0
SWIGLU_ALPHA = 1.702

SB_DEAD_LOG = -105.0
NEG_BIG = -1e30


def _params(semantics):
    return pltpu.CompilerParams(dimension_semantics=semantics,
                                vmem_limit_bytes=VMEM_LIMIT_BYTES)


def _const_spec(shape):
    nd = len(shape)
    return pl.BlockSpec(shape, lambda *_: (0,) * nd)


def _rms(x, g):
    return x * lax.rsqrt(jnp.mean(x * x, axis=-1, keepdims=True) + NORM_EPS) * g


def _rope_table_kernel(pos_ref, inv_ref, cos_ref, sin_ref):
    ang = pos_ref[...] * inv_ref[...]
    cos_ref[...] = jnp.cos(ang)
    sin_ref[...] = jnp.sin(ang)


def _rope_tables(positions):
    n_tok = positions.size
    half = DIFF_HEAD_DIM // 2
    per_row = LANES // half
    rows = n_tok // per_row
    pos = jnp.repeat(positions.reshape(rows, per_row).astype(F32), half, axis=1)
    inv = ROPE_THETA ** (-jnp.arange(half, dtype=F32) / half)
    inv = jnp.tile(inv, per_row).reshape(1, LANES)
    tr = min(rows, 512)
    cos, sin = pl.pallas_call(
        _rope_table_kernel,
        grid=(rows // tr,),
        in_specs=[pl.BlockSpec((tr, LANES), lambda i: (i, 0)), _const_spec((1, LANES))],
        out_specs=[pl.BlockSpec((tr, LANES), lambda i: (i, 0))] * 2,
        out_shape=[jax.ShapeDtypeStruct((rows, LANES), F32)] * 2,
        compiler_params=_params(("arbitrary",)),
        name="rope_table",
    )(pos, inv)
    cos = cos.reshape(n_tok, half)
    sin = sin.reshape(n_tok, half)
    cos_t = jnp.tile(cos, (1, LANES // half))
    sin_t = jnp.tile(jnp.concatenate([-sin, sin], axis=1), (1, LANES // (2 * half)))
    return cos_t, sin_t


def _inproj_kernel(x_ref, g_ref, w_ref, cos_ref, sin_ref, o_ref, *, chunk, rope_chunks,
                   scale_chunks, scale):
    h = _rms(x_ref[...], g_ref[...]).astype(BF16)
    lane = lax.broadcasted_iota(I32, (1, chunk), 1)
    first_half = (lane % DIFF_HEAD_DIM) < (DIFF_HEAD_DIM // 2)
    for c in range(w_ref.shape[1] // chunk):
        cols = slice(c * chunk, (c + 1) * chunk)
        acc = jnp.dot(h, w_ref[:, cols], preferred_element_type=F32)
        if c in rope_chunks:
            cos = jnp.tile(cos_ref[...], (1, chunk // LANES))
            sin = jnp.tile(sin_ref[...], (1, chunk // LANES))
            partner = jnp.where(first_half,
                                pltpu.roll(acc, chunk - DIFF_HEAD_DIM // 2, 1),
                                pltpu.roll(acc, DIFF_HEAD_DIM // 2, 1))
            acc = acc * cos + partner * sin
        if c in scale_chunks:
            acc = acc * scale
        o_ref[:, cols] = acc.astype(BF16)


def _in_proj(x2d, g, w_bf16, cos_t, sin_t, *, tm, chunk, rope_chunks, scale_chunks, scale):
    n_tok, d = x2d.shape
    n_in = w_bf16.shape[1]
    kern = functools.partial(_inproj_kernel, chunk=chunk, rope_chunks=rope_chunks,
                             scale_chunks=scale_chunks, scale=scale)
    return pl.pallas_call(
        kern,
        grid=(n_tok // tm,),
        in_specs=[pl.BlockSpec((tm, d), lambda i: (i, 0)),
                  _const_spec((1, d)),
                  _const_spec((d, n_in)),
                  pl.BlockSpec((tm, LANES), lambda i: (i, 0)),
                  pl.BlockSpec((tm, LANES), lambda i: (i, 0))],
        out_specs=pl.BlockSpec((tm, n_in), lambda i: (i, 0)),
        out_shape=jax.ShapeDtypeStruct((n_tok, n_in), BF16),
        compiler_params=_params(("arbitrary",)),
        name="in_proj",
    )(x2d, g, w_bf16, cos_t, sin_t)


def _sb_kernel(q_ref, k_ref, v_ref, o_ref, acc_ref, lw0, lw1, cat0, cat1, *, tq, pairs):
    for p in range(pairs):
        _sb_head_pair(q_ref, k_ref, v_ref, o_ref, acc_ref, lw0, lw1, cat0, cat1, tq=tq,
                      cols=slice(p * LANES, (p + 1) * LANES))


def _sb_head_pair(q_ref, k_ref, v_ref, o_ref, acc_ref, lw0, lw1, cat0, cat1, *, tq, cols):
    i = pl.program_id(2)
    q = q_ref[0, :, cols]
    lane = lax.broadcasted_iota(I32, (1, LANES), 1)
    zero = jnp.zeros_like(q)
    q2 = jnp.concatenate([jnp.where(lane < SB_HEAD_DIM, q, zero),
                          jnp.where(lane >= SB_HEAD_DIM, q, zero)], axis=0)
    uj = lax.broadcasted_iota(I32, (2 * tq, tq), 0) % tq
    us = lax.broadcasted_iota(I32, (2 * tq, tq), 1)
    suffix = jnp.where(uj > us, 1.0, 0.0).astype(BF16)
    lws = (lw0, lw1)
    cats = (cat0, cat1)

    def causal_mask():
        return (lax.broadcasted_iota(I32, (2 * tq, tq), 1)
                < lax.broadcasted_iota(I32, (2 * tq, tq), 0) % tq)

    def logits(j, buf, masked):
        start = pl.multiple_of(j * tq, tq)
        kj = k_ref[0, pl.ds(start, tq), cols]
        z = lax.dot_general(q2, kj, (((1,), (1,)), ((), ())), preferred_element_type=F32)
        sp = jnp.maximum(z, 0.0) + jnp.log(1.0 + jnp.exp(-jnp.abs(z)))
        log_rem = -sp
        if masked:
            log_rem = jnp.where(causal_mask(), log_rem, 0.0)
        lws[buf][...] = z - sp
        hi = log_rem.astype(BF16)
        cats[buf][:, 0:tq] = hi
        cats[buf][:, tq:2 * tq] = (log_rem - hi.astype(F32)).astype(BF16)
        return log_rem[:, 0:1]

    def suffix_sums(buf, first):
        after = jnp.dot(cats[buf][...], suffix, preferred_element_type=F32)
        lws[buf][...] += after
        return after[:, 0:1] + first

    def weigh(j, buf, carry, masked):
        start = pl.multiple_of(j * tq, tq)
        vj = v_ref[0, pl.ds(start, tq), cols]
        w = jnp.exp(lws[buf][...] + carry)
        if masked:
            w = jnp.where(causal_mask(), w, 0.0)
        acc_ref[...] += jnp.dot(w.astype(BF16), vj, preferred_element_type=F32)

    def earlier_blocks(first_block, carry):
        def cond(state):
            j, _, alive = state
            return jnp.logical_and(j >= 0, alive > SB_DEAD_LOG)

        def body(state):
            j, carry, _ = state
            total = suffix_sums(0, logits(j, 0, False))
            weigh(j, 0, carry, False)
            carry = carry + total
            return j - 1, carry, jnp.max(carry)

        lax.while_loop(cond, body, (first_block, carry, jnp.max(carry)))

    acc_ref[...] = jnp.zeros_like(acc_ref)
    no_carry = jnp.zeros((2 * tq, 1), F32)

    @pl.when(i == 0)
    def _():
        suffix_sums(0, logits(0, 0, True))
        weigh(0, 0, no_carry, True)

    @pl.when(i > 0)
    def _():
        first_diag = logits(i, 0, True)
        first_prev = logits(i - 1, 1, False)
        total_diag = suffix_sums(0, first_diag)
        total_prev = suffix_sums(1, first_prev)
        weigh(i, 0, no_carry, True)
        weigh(i - 1, 1, total_diag, False)
        earlier_blocks(i - 2, total_diag + total_prev)

    o_ref[0, :, cols] = jnp.where(lane < SB_HEAD_DIM, acc_ref[0:tq, :],
                                  acc_ref[tq:2 * tq, :]).astype(BF16)


def _sb_attention(proj3, *, q_col, k_col, v_col, n_pairs, tq, pairs_per_step=2):
    b, s, _ = proj3.shape
    pp = pairs_per_step
    assert n_pairs % pp == 0 and q_col % pp == 0 and k_col % pp == 0 and v_col % pp == 0
    width = pp * LANES
    return pl.pallas_call(
        functools.partial(_sb_kernel, tq=tq, pairs=pp),
        grid=(b, n_pairs // pp, s // tq),
        in_specs=[pl.BlockSpec((1, tq, width), lambda bi, p, i: (bi, i, q_col // pp + p)),
                  pl.BlockSpec((1, s, width), lambda bi, p, i: (bi, 0, k_col // pp + p)),
                  pl.BlockSpec((1, s, width), lambda bi, p, i: (bi, 0, v_col // pp + p))],
        out_specs=pl.BlockSpec((1, tq, width), lambda bi, p, i: (bi, i, p)),
        out_shape=jax.ShapeDtypeStruct((b, s, n_pairs * LANES), BF16),
        scratch_shapes=[pltpu.VMEM((2 * tq, LANES), F32),
                        pltpu.VMEM((2 * tq, tq), F32), pltpu.VMEM((2 * tq, tq), F32),
                        pltpu.VMEM((2 * tq, 2 * tq), BF16), pltpu.VMEM((2 * tq, 2 * tq), BF16)],
        compiler_params=_params(("arbitrary", "arbitrary", "arbitrary")),
        name="sb_attention",
    )(proj3, proj3, proj3)


DIFF_ONES_ROWS = 16


def _diff_kernel(lq1_ref, lk1_ref, lq2_ref, lk2_ref, q_ref, k_ref, v_ref, g_ref, o_ref, vt_ref,
                 acc_ref, z0_ref, z1_ref, *, tq, lambda_init):
    i = pl.program_id(2)
    s_len = v_ref.shape[1]
    vd = v_ref.shape[2]
    half = tq // 2
    zs = (z0_ref, z1_ref)

    @pl.when(i == 0)
    def _():
        for c in range(s_len // tq):
            cols = slice(c * tq, (c + 1) * tq)
            vt_ref[0:vd, cols] = v_ref[0, cols, :].astype(F32).T.astype(BF16)
        rid = lax.broadcasted_iota(I32, (DIFF_ONES_ROWS, s_len), 0)
        vt_ref[vd:vd + DIFF_ONES_ROWS, :] = jnp.where(rid == 0, 1.0, 0.0).astype(BF16)

    q = q_ref[0]
    lane = lax.broadcasted_iota(I32, (1, LANES), 1)
    zero = jnp.zeros_like(q)
    q2 = jnp.concatenate([jnp.where(lane < DIFF_HEAD_DIM, q, zero),
                          jnp.where(lane >= DIFF_HEAD_DIM, q, zero)], axis=0)

    def scores(j, h, masked):
        start = pl.multiple_of(j * tq + h * half, half)
        kj = k_ref[0, pl.ds(start, half), :]
        zt = lax.dot_general(kj, q2, (((1,), (1,)), ((), ())), preferred_element_type=F32)
        if masked:
            qidx = lax.broadcasted_iota(I32, (half, 2 * tq), 1) % tq
            kidx = lax.broadcasted_iota(I32, (half, 2 * tq), 0) + h * half
            zt = jnp.where(kidx // CHUNK <= qidx // CHUNK, zt, -jnp.inf)
        zs[h][...] = zt
        return jnp.max(zt, axis=0, keepdims=True)

    def values(j, h, m, mblk):
        start = pl.multiple_of(j * tq + h * half, half)
        vtj = vt_ref[:, pl.ds(start, half)]
        m_new = jnp.maximum(m, mblk)
        pt = jnp.exp(zs[h][...] - m_new).astype(BF16)
        acc_ref[...] = jnp.exp(m - m_new) * acc_ref[...] + jnp.dot(vtj, pt,
                                                                   preferred_element_type=F32)
        return m_new

    def block(j, m, mb0, masked, next_kind):
        mb1 = scores(j, 1, masked)
        m = values(j, 0, m, mb0)
        nb0 = mb0 if next_kind is None else scores(j + 1, 0, next_kind == "masked")
        m = values(j, 1, m, mb1)
        return m, nb0

    acc_ref[...] = jnp.zeros_like(acc_ref)
    m0 = jnp.full((1, 2 * tq), -jnp.inf, F32)

    @pl.when(i == 0)
    def _():
        block(0, m0, scores(0, 0, True), True, None)

    @pl.when(i > 0)
    def _():
        state = (m0, scores(0, 0, False))
        state = lax.fori_loop(0, i - 1, lambda j, st: block(j, st[0], st[1], False, "plain"),
                              state)
        state = block(i - 1, state[0], state[1], False, "masked")
        block(i, state[0], state[1], True, None)

    lam = (jnp.exp(jnp.sum(lq1_ref[...] * lk1_ref[...], axis=1, keepdims=True))
           - jnp.exp(jnp.sum(lq2_ref[...] * lk2_ref[...], axis=1, keepdims=True))
           + lambda_init)
    ot = acc_ref[0:vd, :] / acc_ref[vd:vd + 1, :]
    ot = ot[:, 0:tq] - lam * ot[:, tq:2 * tq]
    o_ref[0] = (_rms(ot.T, g_ref[...]) * (1.0 - lambda_init)).astype(BF16)


def _diff_attention(proj3, lq1, lk1, lq2, lk2, subln, *, q_col, k_col, v_col, n_heads, tq,
                    lambda_init):
    b, s, _ = proj3.shape
    lam_spec = _const_spec((1, DIFF_HEAD_DIM))
    return pl.pallas_call(
        functools.partial(_diff_kernel, tq=tq, lambda_init=lambda_init),
        grid=(b, n_heads, s // tq),
        in_specs=[lam_spec, lam_spec, lam_spec, lam_spec,
                  pl.BlockSpec((1, tq, LANES), lambda bi, h, i: (bi, i, q_col + h)),
                  pl.BlockSpec((1, s, LANES), lambda bi, h, i: (bi, 0, k_col + h)),
                  pl.BlockSpec((1, s, LANES), lambda bi, h, i: (bi, 0, v_col + h)),
                  _const_spec((1, DIFF_V_DIM))],
        out_specs=pl.BlockSpec((1, tq, LANES), lambda bi, h, i: (bi, i, h)),
        out_shape=jax.ShapeDtypeStruct((b, s, n_heads * DIFF_V_DIM), BF16),
        scratch_shapes=[pltpu.VMEM((DIFF_V_DIM + DIFF_ONES_ROWS, s), BF16),
                        pltpu.VMEM((DIFF_V_DIM + DIFF_ONES_ROWS, 2 * tq), F32),
                        pltpu.VMEM((tq // 2, 2 * tq), F32), pltpu.VMEM((tq // 2, 2 * tq), F32)],
        compiler_params=_params(("arbitrary", "arbitrary", "arbitrary")),
        name="diff_attention",
    )(lq1, lk1, lq2, lk2, proj3, proj3, proj3, subln)


def _mem_kv_kernel(mem_ref, g_ref, w_ref, o_ref):
    h = _rms(mem_ref[0], g_ref[...]).astype(BF16)
    o_ref[0] = jnp.dot(h, w_ref[...], preferred_element_type=F32).astype(BF16)


def _mem_kv(mem, g, w_bf16):
    b, m, d = mem.shape
    n = w_bf16.shape[1]
    return pl.pallas_call(
        _mem_kv_kernel,
        grid=(b,),
        in_specs=[pl.BlockSpec((1, m, d), lambda i: (i, 0, 0)), _const_spec((1, d)),
                  _const_spec((d, n))],
        out_specs=pl.BlockSpec((1, m, n), lambda i: (i, 0, 0)),
        out_shape=jax.ShapeDtypeStruct((b, m, n), BF16),
        compiler_params=_params(("arbitrary",)),
        name="mem_kv",
    )(mem, g, w_bf16)


def _split_bf16(v):
    hi = v.astype(BF16)
    return hi, (v - hi.astype(F32)).astype(BF16)


def _pack_bf16_pairs(v):
    half = v.shape[1] // 2
    bits = lax.bitcast_convert_type(v.astype(BF16).astype(F32), U32)
    return (bits[:, :half] >> 16) | (bits[:, half:] & jnp.uint32(0xFFFF0000))


def _unpack_bf16_pairs(w):
    lo = lax.bitcast_convert_type(w << 16, F32)
    hi = lax.bitcast_convert_type(w & jnp.uint32(0xFFFF0000), F32)
    return jnp.concatenate([lo, hi], axis=1).astype(BF16)


def _postmix_kernel(x_ref, osb_ref, odf_ref, gsb_ref, gdf_ref, wus_ref, wud_ref, wout_ref,
                    gq_ref, wq_ref, kv_ref, wo_ref, gf_ref, wr_ref, br_ref,
                    x2_ref, hf_ref, route_ref, gate_ref, cum_ref, count_ref, *, tm, d_model):
    step = pl.program_id(0)

    @pl.when(step == 0)
    def _():
        count_ref[...] = jnp.zeros_like(count_ref)

    y_sb = jnp.dot(osb_ref[...], wus_ref[...], preferred_element_type=F32)
    y_df = jnp.dot(odf_ref[...], wud_ref[...], preferred_element_type=F32)
    mixed = (jax.nn.sigmoid(gsb_ref[...].astype(F32)) * y_sb
             + jax.nn.sigmoid(gdf_ref[...].astype(F32)) * y_df)
    x1 = x_ref[...] + jnp.dot(mixed.astype(BF16), wout_ref[...], preferred_element_type=F32)

    hq = _rms(x1, gq_ref[...]).astype(BF16)
    hd = d_model // MEM_HEADS
    q = jnp.dot(hq, wq_ref[...], preferred_element_type=F32) * (1.0 / math.sqrt(hd))
    q = q.astype(BF16)
    heads = []
    for h in range(MEM_HEADS):
        kh = kv_ref[0, :, h * hd:(h + 1) * hd]
        vh = kv_ref[0, :, d_model + h * hd:d_model + (h + 1) * hd]
        z = lax.dot_general(q[:, h * hd:(h + 1) * hd], kh, (((1,), (1,)), ((), ())),
                            preferred_element_type=F32)
        p = jnp.exp(z - jnp.max(z, axis=1, keepdims=True))
        l = jnp.sum(p, axis=1, keepdims=True)
        heads.append((jnp.dot(p.astype(BF16), vh, preferred_element_type=F32) / l).astype(BF16))
    x2 = x1 + jnp.dot(jnp.concatenate(heads, axis=1), wo_ref[...], preferred_element_type=F32)
    x2_ref[...] = x2

    hf = _rms(x2, gf_ref[...])
    packed = _pack_bf16_pairs(hf)
    for j in range(PACKED_SUBROWS):
        hf_ref[pl.ds(j, tm, stride=ROW_TILE), :] = packed[:, j * LANES:(j + 1) * LANES]
    h_hi, h_lo = _split_bf16(hf)
    w_hi, w_lo = _split_bf16(wr_ref[...])
    logits = (jnp.dot(h_hi, w_hi, preferred_element_type=F32)
              + jnp.dot(h_hi, w_lo, preferred_element_type=F32)
              + jnp.dot(h_lo, w_hi, preferred_element_type=F32)) + br_ref[...]
    lane = lax.broadcasted_iota(I32, (tm, LANES), 1)
    work = logits
    vals, idxs, hots = [], [], []
    for _ in range(TOP_K):
        mx = jnp.max(work, axis=1, keepdims=True)
        idx = jnp.min(jnp.where(work == mx, lane, LANES), axis=1, keepdims=True)
        hot = lane == idx
        work = jnp.where(hot, NEG_BIG, work)
        vals.append(mx)
        idxs.append(idx)
        hots.append(hot)
    exps = [jnp.exp(v - vals[0]) for v in vals]
    denom = exps[0] + exps[1] + exps[2] + exps[3]

    onehot_sum = jnp.zeros((tm, LANES), F32)
    for hot in hots:
        onehot_sum = onehot_sum + jnp.where(hot, 1.0, 0.0)
    r = lax.broadcasted_iota(I32, (tm, tm), 0)
    c = lax.broadcasted_iota(I32, (tm, tm), 1)
    lower = jnp.where(c < r, 1.0, 0.0).astype(BF16)
    rank = jnp.dot(lower, onehot_sum.astype(BF16), preferred_element_type=F32) + count_ref[...]
    route = jnp.zeros((tm, LANES), I32)
    gates = jnp.zeros((tm, LANES), F32)
    for k in range(TOP_K):
        pos = jnp.sum(jnp.where(hots[k], rank, 0.0), axis=1, keepdims=True).astype(I32)
        route = jnp.where(lane == k, idxs[k], route)
        route = jnp.where(lane == TOP_K + k, pos, route)
        gates = jnp.where(lane == k, exps[k] / denom, gates)
    route_ref[...] = route
    gate_ref[...] = gates
    meta = jnp.where(lane == 0, step * tm + lax.broadcasted_iota(I32, (tm, LANES), 0), 0)
    for k in range(TOP_K):
        meta = jnp.where(lane == 1 + k, idxs[k], meta)
    hf_ref[pl.ds(META_SUBROW, tm, stride=ROW_TILE), :] = meta.astype(U32)
    for j in range(META_SUBROW + 1, ROW_TILE):
        hf_ref[pl.ds(j, tm, stride=ROW_TILE), :] = jnp.zeros((tm, LANES), U32)
    count_ref[...] = count_ref[...] + jnp.sum(onehot_sum, axis=0, keepdims=True)
    cum_ref[0] = jnp.broadcast_to(count_ref[...], (8, LANES))


def _post_mix(x2d, proj2, o_sb, o_diff, w_up_sb, w_up_diff, w_out, g_memq, w_memq, kv, w_memo,
              g_ffn, w_router_pad, b_router_pad, *, tm, seq, gate_col):
    n_tok, d = x2d.shape
    n_tiles = n_tok // tm
    tiles_per_batch = seq // tm
    row = lambda i: (i, 0)
    in_specs = [
        pl.BlockSpec((tm, d), row),
        pl.BlockSpec((tm, o_sb.shape[1]), row),
        pl.BlockSpec((tm, o_diff.shape[1]), row),
        pl.BlockSpec((tm, d), lambda i: (i, gate_col)),
        pl.BlockSpec((tm, d), lambda i: (i, gate_col + 1)),
        _const_spec(w_up_sb.shape), _const_spec(w_up_diff.shape), _const_spec(w_out.shape),
        _const_spec((1, d)), _const_spec(w_memq.shape),
        pl.BlockSpec((1,) + kv.shape[1:], lambda i: (i // tiles_per_batch, 0, 0)),
        _const_spec(w_memo.shape), _const_spec((1, d)),
        _const_spec(w_router_pad.shape), _const_spec((1, LANES)),
    ]
    out_specs = [
        pl.BlockSpec((tm, d), row),
        pl.BlockSpec((tm * ROW_TILE, LANES), row),
        pl.BlockSpec((tm, LANES), row),
        pl.BlockSpec((tm, LANES), row),
        pl.BlockSpec((1, 8, LANES), lambda i: (i, 0, 0)),
    ]
    assert d // 2 == PACKED_SUBROWS * LANES
    out_shape = [
        jax.ShapeDtypeStruct((n_tok, d), F32),
        jax.ShapeDtypeStruct((n_tok * ROW_TILE, LANES), U32),
        jax.ShapeDtypeStruct((n_tok, LANES), I32),
        jax.ShapeDtypeStruct((n_tok, LANES), F32),
        jax.ShapeDtypeStruct((n_tiles, 8, LANES), F32),
    ]
    return pl.pallas_call(
        functools.partial(_postmix_kernel, tm=tm, d_model=d),
        grid=(n_tiles,),
        in_specs=in_specs,
        out_specs=out_specs,
        out_shape=out_shape,
        scratch_shapes=[pltpu.VMEM((1, LANES), F32)],
        compiler_params=_params(("arbitrary",)),
        name="post_mix",
    )(x2d, o_sb, o_diff, proj2, proj2, w_up_sb, w_up_diff, w_out, g_memq, w_memq, kv, w_memo,
      g_ffn, w_router_pad, b_router_pad)


def _dispatch_kernel(zstart_ref, zflag_ref, nused_ref, dest_ref, hf_ref, xs_hbm, stage0, stage1,
                     zbuf, sem, zsem, *, tm, bm):
    s = pl.program_id(0)

    @pl.when(s == 0)
    def _():
        lane = lax.broadcasted_iota(I32, (bm, LANES), 1)
        zbuf[...] = jnp.zeros_like(zbuf)
        zbuf[pl.ds(META_SUBROW, bm, stride=ROW_TILE), :] = jnp.where(
            jnp.logical_and(lane >= 1, lane <= TOP_K), -1, 0).astype(U32)

        def fill_copy(start):
            start = pl.multiple_of(start * ROW_TILE, bm * ROW_TILE)
            return pltpu.make_async_copy(zbuf, xs_hbm.at[pl.ds(start, bm * ROW_TILE), :], zsem)

        n_blocks = xs_hbm.shape[0] // (bm * ROW_TILE)
        for action in ("start", "wait"):
            for e in range(N_EXPERTS):
                @pl.when(zflag_ref[e] != 0)
                def _(e=e, action=action):
                    getattr(fill_copy(zstart_ref[e]), action)()

            def trailing(blk, _, action=action):
                getattr(fill_copy(blk * bm), action)()
                return 0

            lax.fori_loop(nused_ref[0], n_blocks, trailing, 0)

    def tile_done(stage, parity):
        for _ in range(TOP_K):
            pltpu.make_async_copy(stage, xs_hbm.at[pl.ds(0, tm * ROW_TILE), :],
                                  sem.at[parity]).wait()

    def run(stage, other, parity):
        stage[...] = hf_ref[...]

        def issue(t, _):
            src = stage.at[pl.ds(pl.multiple_of(t * ROW_TILE, ROW_TILE), ROW_TILE), :]
            for k in range(TOP_K):
                row = pl.multiple_of(dest_ref[t * TOP_K + k] * ROW_TILE, ROW_TILE)
                pltpu.make_async_copy(src, xs_hbm.at[pl.ds(row, ROW_TILE), :],
                                      sem.at[parity]).start(priority=k % 2)
            return 0

        lax.fori_loop(0, tm, issue, 0)

        @pl.when(s > 0)
        def _():
            tile_done(other, 1 - parity)

        @pl.when(s == pl.num_programs(0) - 1)
        def _():
            tile_done(stage, parity)

    @pl.when(s % 2 == 0)
    def _():
        run(stage0, stage1, 0)

    @pl.when(s % 2 == 1)
    def _():
        run(stage1, stage0, 1)


def _dispatch(zstart, zflag, n_used, dest_flat, hf_rows, *, tm, bm, n_rows):
    n_tok = hf_rows.shape[0] // ROW_TILE
    grid_spec = pltpu.PrefetchScalarGridSpec(
        num_scalar_prefetch=3,
        grid=(n_tok // tm,),
        in_specs=[
            pl.BlockSpec((tm * TOP_K,), lambda i, *_: (i,), memory_space=pltpu.SMEM),
            pl.BlockSpec((tm * ROW_TILE, LANES), lambda i, *_: (i, 0)),
        ],
        out_specs=pl.BlockSpec(memory_space=pl.ANY),
        scratch_shapes=[pltpu.VMEM((tm * ROW_TILE, LANES), U32),
                        pltpu.VMEM((tm * ROW_TILE, LANES), U32),
                        pltpu.VMEM((bm * ROW_TILE, LANES), U32),
                        pltpu.SemaphoreType.DMA((2,)), pltpu.SemaphoreType.DMA(())],
    )
    return pl.pallas_call(
        functools.partial(_dispatch_kernel, tm=tm, bm=bm),
        grid_spec=grid_spec,
        out_shape=jax.ShapeDtypeStruct((n_rows * ROW_TILE, LANES), U32),
        compiler_params=_params(("arbitrary",)),
        name="dispatch",
    )(zstart, zflag, n_used, dest_flat, hf_rows)


FFN_UP_CHUNKS = 8
FFN_DOWN_CHUNKS = 4


def _expert_kernel(be_ref, valid_ref, src_ref, wslot_ref, next_ref, xs_ref, bgu_ref, bd_ref,
                   wgu_hbm, wd_hbm, ys_hbm, wgu_f32, wd_f32, wgu_bf, wd_bf, ybuf0, ybuf1, slot_v,
                   slot_s, sem_w, sem_y, sem_s, sem_z, *, d_ff, bm, n_tok):
    r = pl.program_id(0)
    valid = valid_ref[r] != 0
    prev_valid = jnp.logical_and(r > 0, valid_ref[jnp.maximum(r - 1, 0)] != 0)
    fresh = jnp.logical_or(r == 0, be_ref[r] != be_ref[jnp.maximum(r - 1, 0)])
    expert = be_ref[r]
    ybufs = (ybuf0, ybuf1)

    def weight_copies(e, wslot):
        return (pltpu.make_async_copy(wgu_hbm.at[e], wgu_f32.at[wslot], sem_w.at[wslot]),
                pltpu.make_async_copy(wd_hbm.at[e], wd_f32.at[wslot], sem_w.at[wslot]))
    d = wd_bf.shape[1]
    n_slots = TOP_K * n_tok
    block_rows = bm * ROW_TILE

    def rows_done(parity):
        return pltpu.make_async_copy(ybufs[parity], ys_hbm.at[pl.ds(0, block_rows), :],
                                     sem_y.at[parity])

    def slots_copy(parity):
        return pltpu.make_async_copy(slot_v.at[0:1, :], slot_s.at[parity:parity + 1, :],
                                     sem_s.at[parity])

    def scatter_rows(prev, lo, hi):
        for i in range(lo, hi):
            dst = pl.multiple_of(slot_s[prev, i] * ROW_TILE, ROW_TILE)
            pltpu.make_async_copy(ybufs[prev].at[pl.ds(i * ROW_TILE, ROW_TILE), :],
                                  ys_hbm.at[pl.ds(dst, ROW_TILE), :],
                                  sem_y.at[prev]).start(priority=i % 2)

    @pl.when(r == 0)
    def _():
        for action in ("start", "wait"):
            for parity in range(2):
                if action == "start":
                    ybufs[parity][...] = jnp.zeros_like(ybufs[parity])
                trash = (n_slots + parity * bm) * ROW_TILE
                getattr(pltpu.make_async_copy(ybufs[parity],
                                              ys_hbm.at[pl.ds(trash, block_rows), :], sem_z),
                        action)()

    for parity in range(2):
        @pl.when(jnp.logical_and(prev_valid, r % 2 == parity))
        def _(parity=parity):
            slots_copy(1 - parity).wait()

    @pl.when(jnp.logical_and(valid, fresh))
    def _():
        wslot = wslot_ref[r]

        @pl.when(r == 0)
        def _():
            for c in weight_copies(expert, wslot):
                c.start()

        for c in weight_copies(expert, wslot):
            c.wait()
        wgu_bf[...] = wgu_f32[wslot].astype(BF16)
        wd_bf[...] = wd_f32[wslot].astype(BF16)

        @pl.when(next_ref[r] >= 0)
        def _():
            for c in weight_copies(next_ref[r], 1 - wslot):
                c.start()

    def ffn(cur, interleave):
        prev = 1 - cur
        ybuf = ybufs[cur]
        bounds = [bm * c // FFN_UP_CHUNKS for c in range(FFN_UP_CHUNKS + 1)]

        @pl.when(r >= 2)
        def _():
            rows_done(cur).wait()

        meta = xs_ref[pl.ds(META_SUBROW, bm, stride=ROW_TILE), :].astype(I32)
        row = lax.broadcasted_iota(I32, (bm, 1), 0)
        slot = n_slots + cur * bm + row
        for k in range(TOP_K):
            slot = jnp.where(meta[:, 1 + k:2 + k] == be_ref[r], k * n_tok + meta[:, 0:1], slot)
        slots = jnp.broadcast_to(slot.astype(F32), (bm, LANES)).T
        slot_v[...] = slots[0:8, :].astype(I32)
        slots_copy(cur).start()

        words = jnp.concatenate([xs_ref[pl.ds(j, bm, stride=ROW_TILE), :]
                                 for j in range(PACKED_SUBROWS)], axis=1)
        xb = _unpack_bf16_pairs(words)
        up_w = 2 * d_ff // FFN_UP_CHUNKS
        pairs = FFN_UP_CHUNKS // 2
        acts = []
        for c in range(pairs):
            halves = []
            for half in range(2):
                cols = slice(half * d_ff + c * up_w, half * d_ff + (c + 1) * up_w)
                halves.append(jnp.dot(xb, wgu_bf[:, cols], preferred_element_type=F32)
                              + bgu_ref[pl.ds(expert, 1), cols])
                if interleave:
                    step_no = 2 * c + half
                    scatter_rows(prev, bounds[step_no], bounds[step_no + 1])
            glu = jnp.minimum(halves[0], SWIGLU_LIMIT)
            lin = jnp.clip(halves[1], -SWIGLU_LIMIT, SWIGLU_LIMIT)
            acts.append((glu * jax.nn.sigmoid(SWIGLU_ALPHA * glu) * (lin + 1.0)).astype(BF16))
        act = jnp.concatenate(acts, axis=1)
        down_w = d // FFN_DOWN_CHUNKS
        for c in range(FFN_DOWN_CHUNKS):
            cols = slice(c * down_w, (c + 1) * down_w)
            y = (jnp.dot(act, wd_bf[:, cols], preferred_element_type=F32)
                 + bd_ref[pl.ds(expert, 1), cols])
            for j in range(down_w // LANES):
                sub = c * (down_w // LANES) + j
                ybuf[pl.ds(sub, bm, stride=ROW_TILE), :] = y[:, j * LANES:(j + 1) * LANES]

    for parity in range(2):
        on_parity = r % 2 == parity

        @pl.when(jnp.logical_and(on_parity, jnp.logical_and(valid, prev_valid)))
        def _(parity=parity):
            ffn(parity, True)

        @pl.when(jnp.logical_and(on_parity, jnp.logical_and(jnp.logical_not(valid), prev_valid)))
        def _(parity=parity):
            scatter_rows(1 - parity, 0, bm)

            @pl.when(r >= 2)
            def _():
                rows_done(parity).wait()

            rows_done(1 - parity).wait()

    @pl.when(r == 0)
    def _():
        ffn(0, False)


def _expert_ffn(blk_expert, blk_valid, blk_src, blk_wslot, blk_next, xs, w_gu, b_gu, w_down,
                b_down, *, bm, n_tok):
    n_grid = blk_expert.shape[0]
    n_exp, d_ff, d = w_down.shape
    assert d == ROW_TILE * LANES
    block_rows = bm * ROW_TILE
    grid_spec = pltpu.PrefetchScalarGridSpec(
        num_scalar_prefetch=5,
        grid=(n_grid,),
        in_specs=[
            pl.BlockSpec((block_rows, LANES), lambda r, be, valid, src, *_: (src[r], 0)),
            pl.BlockSpec((n_exp, 2 * d_ff), lambda r, *_: (0, 0)),
            pl.BlockSpec((n_exp, d), lambda r, *_: (0, 0)),
            pl.BlockSpec(memory_space=pl.ANY),
            pl.BlockSpec(memory_space=pl.ANY),
        ],
        out_specs=pl.BlockSpec(memory_space=pl.ANY),
        scratch_shapes=[pltpu.VMEM((2, d, 2 * d_ff), F32), pltpu.VMEM((2, d_ff, d), F32),
                        pltpu.VMEM((d, 2 * d_ff), BF16), pltpu.VMEM((d_ff, d), BF16),
                        pltpu.VMEM((block_rows, LANES), F32),
                        pltpu.VMEM((block_rows, LANES), F32), pltpu.VMEM((8, bm), I32),
                        pltpu.SMEM((2, bm), I32), pltpu.SemaphoreType.DMA((2,)),
                        pltpu.SemaphoreType.DMA((2,)), pltpu.SemaphoreType.DMA((2,)),
                        pltpu.SemaphoreType.DMA(())],
    )
    return pl.pallas_call(
        functools.partial(_expert_kernel, d_ff=d_ff, bm=bm, n_tok=n_tok),
        grid_spec=grid_spec,
        out_shape=jax.ShapeDtypeStruct(((TOP_K * n_tok + 2 * bm) * ROW_TILE, LANES), F32),
        compiler_params=_params(("arbitrary",)),
        name="expert_ffn",
    )(blk_expert, blk_valid, blk_src, blk_wslot, blk_next, xs, b_gu, b_down, w_gu, w_down)


def _combine_kernel(x_ref, gate_ref, g_ref, y0_ref, y1_ref, y2_ref, y3_ref, o_ref):
    tm, d = x_ref.shape
    gates = gate_ref[...]
    acc = x_ref[...]
    for k, y_ref in enumerate((y0_ref, y1_ref, y2_ref, y3_ref)):
        y = jnp.concatenate([y_ref[pl.ds(j, tm, stride=ROW_TILE), :]
                             for j in range(d // LANES)], axis=1)
        acc = acc + gates[:, k:k + 1] * y
    o_ref[...] = _rms(acc, g_ref[...])


def _combine(x2, gates, g_final, ys, *, tm):
    n_tok, d = x2.shape
    tiles = n_tok // tm
    slot_spec = lambda k: pl.BlockSpec((tm * ROW_TILE, LANES), lambda i: (k * tiles + i, 0))
    return pl.pallas_call(
        _combine_kernel,
        grid=(tiles,),
        in_specs=[
            pl.BlockSpec((tm, d), lambda i: (i, 0)),
            pl.BlockSpec((tm, LANES), lambda i: (i, 0)),
            pl.BlockSpec((1, d), lambda i: (0, 0)),
        ] + [slot_spec(k) for k in range(TOP_K)],
        out_specs=pl.BlockSpec((tm, d), lambda i: (i, 0)),
        out_shape=jax.ShapeDtypeStruct((n_tok, d), F32),
        compiler_params=_params(("arbitrary",)),
        name="combine",
    )(x2, gates, g_final, ys, ys, ys, ys)


def _routing_tables(route, cum, *, bm, n_grid):
    expert = route[:, 0:TOP_K]
    pos = route[:, TOP_K:2 * TOP_K]
    experts = jnp.arange(N_EXPERTS, dtype=I32)
    counts = cum[-1, 0, :N_EXPERTS].astype(I32)
    padded = (counts + bm - 1) // bm * bm
    pend = jnp.cumsum(padded)
    pstart = pend - padded
    dest = pos + jnp.sum(jnp.where(expert[:, :, None] == experts, pstart, 0), axis=2)

    blk = jnp.arange(n_grid, dtype=I32)
    blk_expert = jnp.minimum(jnp.sum(pend[None, :] <= (blk * bm)[:, None], axis=1),
                             N_EXPERTS - 1).astype(I32)
    n_used = pend[-1] // bm
    blk_valid = (blk < n_used).astype(I32)
    blk_src = jnp.minimum(blk, jnp.maximum(n_used - 1, 0)).astype(I32)
    zflag = (padded > 0).astype(I32)
    zstart = jnp.maximum(pend - bm, 0).astype(I32)
    wslot = (jnp.cumsum(zflag) - 1) % 2
    later = jnp.logical_and(experts[None, :] > experts[:, None], zflag[None, :] > 0)
    nxt = jnp.min(jnp.where(later, experts[None, :], N_EXPERTS), axis=1)
    nxt = jnp.where(nxt < N_EXPERTS, nxt, -1)
    owner = blk_expert[:, None] == experts[None, :]
    of_block = lambda v: jnp.sum(jnp.where(owner, v[None, :], 0), axis=1).astype(I32)
    return (dest.astype(I32).reshape(-1), blk_expert, blk_valid, blk_src, of_block(wslot),
            of_block(nxt), zstart, zflag, n_used.astype(I32).reshape(1))


def kernel(x, mem, positions, norm_mix, w_in, lambda_q1, lambda_k1, lambda_q2, lambda_k2,
           diff_subln, w_up_sb, w_up_diff, w_out, norm_mem_q, norm_mem_kv, w_mem_q, w_mem_kv,
           w_mem_o, norm_ffn, w_router, b_router, w_gate_up, b_gate_up, w_down, b_down,
           norm_final):
    b, s, d = x.shape
    n_tok = b * s
    depth = norm_mix.shape[0]
    sb_width = w_up_sb.shape[1]
    diff_width = w_up_diff.shape[1]
    n_in = w_in.shape[2]
    chunk = 512
    assert sb_width == chunk and diff_width == chunk and d == 2 * chunk
    sbq, sbk, sbv, dq, dk, dv = range(6)
    blocks_per_chunk = chunk // LANES
    scale = 1.0 / math.sqrt(SB_HEAD_DIM)
    assert SB_HEAD_DIM == DIFF_HEAD_DIM

    tm_in = min(512, n_tok)
    tq_sb = min(256, s)
    tq_diff = min(512, s)
    tm_post = min(512, s)
    tm_tok = min(256, n_tok)
    bm = 256
    n_grid = (n_tok * TOP_K) // bm + N_EXPERTS

    cos_t, sin_t = _rope_tables(positions)
    x2d = x.reshape(n_tok, d)
    for l in range(depth):
        lambda_init = 0.8 - 0.6 * math.exp(-0.3 * l)
        proj = _in_proj(x2d, norm_mix[l].reshape(1, d), w_in[l].astype(BF16), cos_t, sin_t,
                        tm=tm_in, chunk=chunk, rope_chunks=(dq, dk), scale_chunks=(sbq, dq),
                        scale=scale)
        proj3 = proj.reshape(b, s, n_in)
        o_sb = _sb_attention(proj3, q_col=sbq * blocks_per_chunk, k_col=sbk * blocks_per_chunk,
                             v_col=sbv * blocks_per_chunk, n_pairs=sb_width // LANES, tq=tq_sb)
        o_diff = _diff_attention(
            proj3, lambda_q1[l].reshape(1, -1), lambda_k1[l].reshape(1, -1),
            lambda_q2[l].reshape(1, -1), lambda_k2[l].reshape(1, -1),
            diff_subln[l].reshape(1, -1), q_col=dq * blocks_per_chunk,
            k_col=dk * blocks_per_chunk, v_col=dv * blocks_per_chunk,
            n_heads=diff_width // DIFF_V_DIM, tq=tq_diff, lambda_init=lambda_init)
        kv = _mem_kv(mem, norm_mem_kv[l].reshape(1, d), w_mem_kv[l].astype(BF16))
        w_router_pad = jnp.zeros((d, LANES), F32).at[:, :N_EXPERTS].set(w_router[l])
        b_router_pad = jnp.full((1, LANES), NEG_BIG, F32).at[0, :N_EXPERTS].set(b_router[l])
        x_res, hf, route, gates, cum = _post_mix(
            x2d, proj, o_sb.reshape(n_tok, sb_width), o_diff.reshape(n_tok, diff_width),
            w_up_sb[l].astype(BF16), w_up_diff[l].astype(BF16), w_out[l].astype(BF16),
            norm_mem_q[l].reshape(1, d), w_mem_q[l].astype(BF16), kv, w_mem_o[l].astype(BF16),
            norm_ffn[l].reshape(1, d), w_router_pad, b_router_pad,
            tm=tm_post, seq=s, gate_col=6 * chunk // d)
        (dest, blk_expert, blk_valid, blk_src, blk_wslot, blk_next, zstart, zflag,
         n_used) = _routing_tables(route, cum, bm=bm, n_grid=n_grid + 1)
        xs = _dispatch(zstart, zflag, n_used, dest, hf, tm=tm_tok, bm=bm, n_rows=n_grid * bm)
        ys = _expert_ffn(blk_expert, blk_valid, blk_src, blk_wslot, blk_next, xs, w_gate_up[l],
                         b_gate_up[l], w_down[l], b_down[l], bm=bm, n_tok=n_tok)
        g_next = norm_final if l == depth - 1 else None
        assert g_next is not None, "only depth 1 is wired: the final norm is fused into combine"
        x2d = _combine(x_res, gates, g_next.reshape(1, d), ys, tm=tm_tok)
    return x2d.reshape(b, s, d)
```

```python
import functools
import math

import jax
import jax.numpy as jnp
from jax import lax
from jax.experimental import pallas as pl
from jax.experimental.pallas import tpu as pltpu

F32 = jnp.float32
BF16 = jnp.bfloat16
I32 = jnp.int32
U32 = jnp.uint32

LANES = 128
VMEM_LIMIT_BYTES = 56 * 1024 * 1024
ROW_TILE = 8
PACKED_SUBROWS = 4
META_SUBROW = 4

NORM_EPS = 1e-6
ROPE_THETA = 10000.0
CHUNK = 64
SB_HEAD_DIM = 64
DIFF_HEAD_DIM = 64
DIFF_V_DIM = 128
MEM_HEADS = 4
N_EXPERTS = 32
TOP_K = 4
SWIGLU_LIMIT = 7.0
SWIGLU_ALPHA = 1.702

SB_DEAD_LOG = -105.0
NEG_BIG = -1e30


def _params(semantics):
    return pltpu.CompilerParams(dimension_semantics=semantics,
                                vmem_limit_bytes=VMEM_LIMIT_BYTES)


def _const_spec(shape):
    nd = len(shape)
    return pl.BlockSpec(shape, lambda *_: (0,) * nd)


def _rms(x, g):
    return x * lax.rsqrt(jnp.mean(x * x, axis=-1, keepdims=True) + NORM_EPS) * g


def _rope_table_kernel(pos_ref, inv_ref, cos_ref, sin_ref):
    ang = pos_ref[...] * inv_ref[...]
    cos_ref[...] = jnp.cos(ang)
    sin_ref[...] = jnp.sin(ang)


def _rope_tables(positions):
    n_tok = positions.size
    half = DIFF_HEAD_DIM // 2
    per_row = LANES // half
    rows = n_tok // per_row
    pos = jnp.repeat(positions.reshape(rows, per_row).astype(F32), half, axis=1)
    inv = ROPE_THETA ** (-jnp.arange(half, dtype=F32) / half)
    inv = jnp.tile(inv, per_row).reshape(1, LANES)
    tr = min(rows, 512)
    cos, sin = pl.pallas_call(
        _rope_table_kernel,
        grid=(rows // tr,),
        in_specs=[pl.BlockSpec((tr, LANES), lambda i: (i, 0)), _const_spec((1, LANES))],
        out_specs=[pl.BlockSpec((tr, LANES), lambda i: (i, 0))] * 2,
        out_shape=[jax.ShapeDtypeStruct((rows, LANES), F32)] * 2,
        compiler_params=_params(("arbitrary",)),
        name="rope_table",
    )(pos, inv)
    cos = cos.reshape(n_tok, half)
    sin = sin.reshape(n_tok, half)
    cos_t = jnp.tile(cos, (1, LANES // half))
    sin_t = jnp.tile(jnp.concatenate([-sin, sin], axis=1), (1, LANES // (2 * half)))
    return cos_t, sin_t


def _inproj_kernel(x_ref, g_ref, w_ref, cos_ref, sin_ref, o_ref, *, chunk, rope_chunks,
                   scale_chunks, scale):
    h = _rms(x_ref[...], g_ref[...]).astype(BF16)
    lane = lax.broadcasted_iota(I32, (1, chunk), 1)
    first_half = (lane % DIFF_HEAD_DIM) < (DIFF_HEAD_DIM // 2)
    for c in range(w_ref.shape[1] // chunk):
        cols = slice(c * chunk, (c + 1) * chunk)
        acc = jnp.dot(h, w_ref[:, cols], preferred_element_type=F32)
        if c in rope_chunks:
            cos = jnp.tile(cos_ref[...], (1, chunk // LANES))
            sin = jnp.tile(sin_ref[...], (1, chunk // LANES))
            partner = jnp.where(first_half,
                                pltpu.roll(acc, chunk - DIFF_HEAD_DIM // 2, 1),
                                pltpu.roll(acc, DIFF_HEAD_DIM // 2, 1))
            acc = acc * cos + partner * sin
        if c in scale_chunks:
            acc = acc * scale
        o_ref[:, cols] = acc.astype(BF16)


def _in_proj(x2d, g, w_bf16, cos_t, sin_t, *, tm, chunk, rope_chunks, scale_chunks, scale):
    n_tok, d = x2d.shape
    n_in = w_bf16.shape[1]
    kern = functools.partial(_inproj_kernel, chunk=chunk, rope_chunks=rope_chunks,
                             scale_chunks=scale_chunks, scale=scale)
    return pl.pallas_call(
        kern,
        grid=(n_tok // tm,),
        in_specs=[pl.BlockSpec((tm, d), lambda i: (i, 0)),
                  _const_spec((1, d)),
                  _const_spec((d, n_in)),
                  pl.BlockSpec((tm, LANES), lambda i: (i, 0)),
                  pl.BlockSpec((tm, LANES), lambda i: (i, 0))],
        out_specs=pl.BlockSpec((tm, n_in), lambda i: (i, 0)),
        out_shape=jax.ShapeDtypeStruct((n_tok, n_in), BF16),
        compiler_params=_params(("arbitrary",)),
        name="in_proj",
    )(x2d, g, w_bf16, cos_t, sin_t)


def _sb_kernel(q_ref, k_ref, v_ref, o_ref, acc_ref, lw0, lw1, cat0, cat1, *, tq):
    i = pl.program_id(2)
    q = q_ref[0]
    lane = lax.broadcasted_iota(I32, (1, LANES), 1)
    zero = jnp.zeros_like(q)
    q2 = jnp.concatenate([jnp.where(lane < SB_HEAD_DIM, q, zero),
                          jnp.where(lane >= SB_HEAD_DIM, q, zero)], axis=0)
    uj = lax.broadcasted_iota(I32, (2 * tq, tq), 0) % tq
    us = lax.broadcasted_iota(I32, (2 * tq, tq), 1)
    suffix = jnp.where(uj > us, 1.0, 0.0).astype(BF16)
    lws = (lw0, lw1)
    cats = (cat0, cat1)

    def causal_mask():
        return (lax.broadcasted_iota(I32, (2 * tq, tq), 1)
                < lax.broadcasted_iota(I32, (2 * tq, tq), 0) % tq)

    def logits(j, buf, masked):
        start = pl.multiple_of(j * tq, tq)
        kj = k_ref[0, pl.ds(start, tq), :]
        z = lax.dot_general(q2, kj, (((1,), (1,)), ((), ())), preferred_element_type=F32)
        sp = jnp.maximum(z, 0.0) + jnp.log(1.0 + jnp.exp(-jnp.abs(z)))
        log_rem = -sp
        if masked:
            log_rem = jnp.where(causal_mask(), log_rem, 0.0)
        lws[buf][...] = z - sp
        hi = log_rem.astype(BF16)
        cats[buf][:, 0:tq] = hi
        cats[buf][:, tq:2 * tq] = (log_rem - hi.astype(F32)).astype(BF16)
        return log_rem[:, 0:1]

    def suffix_sums(buf, first):
        after = jnp.dot(cats[buf][...], suffix, preferred_element_type=F32)
        lws[buf][...] += after
        return after[:, 0:1] + first

    def weigh(j, buf, carry, masked):
        start = pl.multiple_of(j * tq, tq)
        vj = v_ref[0, pl.ds(start, tq), :]
        w = jnp.exp(lws[buf][...] + carry)
        if masked:
            w = jnp.where(causal_mask(), w, 0.0)
        acc_ref[...] += jnp.dot(w.astype(BF16), vj, preferred_element_type=F32)

    def earlier_blocks(first_block, carry):
        def cond(state):
            j, _, alive = state
            return jnp.logical_and(j >= 0, alive > SB_DEAD_LOG)

        def body(state):
            j, carry, _ = state
            total = suffix_sums(0, logits(j, 0, False))
            weigh(j, 0, carry, False)
            carry = carry + total
            return j - 1, carry, jnp.max(carry)

        lax.while_loop(cond, body, (first_block, carry, jnp.max(carry)))

    acc_ref[...] = jnp.zeros_like(acc_ref)
    no_carry = jnp.zeros((2 * tq, 1), F32)

    @pl.when(i == 0)
    def _():
        suffix_sums(0, logits(0, 0, True))
        weigh(0, 0, no_carry, True)

    @pl.when(i > 0)
    def _():
        first_diag = logits(i, 0, True)
        first_prev = logits(i - 1, 1, False)
        total_diag = suffix_sums(0, first_diag)
        total_prev = suffix_sums(1, first_prev)
        weigh(i, 0, no_carry, True)
        weigh(i - 1, 1, total_diag, False)
        earlier_blocks(i - 2, total_diag + total_prev)

    o_ref[0] = jnp.where(lane < SB_HEAD_DIM, acc_ref[0:tq, :], acc_ref[tq:2 * tq, :]).astype(BF16)


def _sb_attention(proj3, *, q_col, k_col, v_col, n_pairs, tq):
    b, s, _ = proj3.shape
    return pl.pallas_call(
        functools.partial(_sb_kernel, tq=tq),
        grid=(b, n_pairs, s // tq),
        in_specs=[pl.BlockSpec((1, tq, LANES), lambda bi, p, i: (bi, i, q_col + p)),
                  pl.BlockSpec((1, s, LANES), lambda bi, p, i: (bi, 0, k_col + p)),
                  pl.BlockSpec((1, s, LANES), lambda bi, p, i: (bi, 0, v_col + p))],
        out_specs=pl.BlockSpec((1, tq, LANES), lambda bi, p, i: (bi, i, p)),
        out_shape=jax.ShapeDtypeStruct((b, s, n_pairs * LANES), BF16),
        scratch_shapes=[pltpu.VMEM((2 * tq, LANES), F32),
                        pltpu.VMEM((2 * tq, tq), F32), pltpu.VMEM((2 * tq, tq), F32),
                        pltpu.VMEM((2 * tq, 2 * tq), BF16), pltpu.VMEM((2 * tq, 2 * tq), BF16)],
        compiler_params=_params(("arbitrary", "arbitrary", "arbitrary")),
        name="sb_attention",
    )(proj3, proj3, proj3)


DIFF_ONES_ROWS = 16


def _diff_kernel(lq1_ref, lk1_ref, lq2_ref, lk2_ref, q_ref, k_ref, v_ref, g_ref, o_ref, vt_ref,
                 acc_ref, z0_ref, z1_ref, *, tq, lambda_init):
    i = pl.program_id(2)
    s_len = v_ref.shape[1]
    vd = v_ref.shape[2]
    half = tq // 2
    zs = (z0_ref, z1_ref)

    @pl.when(i == 0)
    def _():
        for c in range(s_len // tq):
            cols = slice(c * tq, (c + 1) * tq)
            vt_ref[0:vd, cols] = v_ref[0, cols, :].astype(F32).T.astype(BF16)
        rid = lax.broadcasted_iota(I32, (DIFF_ONES_ROWS, s_len), 0)
        vt_ref[vd:vd + DIFF_ONES_ROWS, :] = jnp.where(rid == 0, 1.0, 0.0).astype(BF16)

    q = q_ref[0]
    lane = lax.broadcasted_iota(I32, (1, LANES), 1)
    zero = jnp.zeros_like(q)
    q2 = jnp.concatenate([jnp.where(lane < DIFF_HEAD_DIM, q, zero),
                          jnp.where(lane >= DIFF_HEAD_DIM, q, zero)], axis=0)

    def scores(j, h, masked):
        start = pl.multiple_of(j * tq + h * half, half)
        kj = k_ref[0, pl.ds(start, half), :]
        zt = lax.dot_general(kj, q2, (((1,), (1,)), ((), ())), preferred_element_type=F32)
        if masked:
            qidx = lax.broadcasted_iota(I32, (half, 2 * tq), 1) % tq
            kidx = lax.broadcasted_iota(I32, (half, 2 * tq), 0) + h * half
            zt = jnp.where(kidx // CHUNK <= qidx // CHUNK, zt, -jnp.inf)
        zs[h][...] = zt
        return jnp.max(zt, axis=0, keepdims=True)

    def values(j, h, m, mblk):
        start = pl.multiple_of(j * tq + h * half, half)
        vtj = vt_ref[:, pl.ds(start, half)]
        m_new = jnp.maximum(m, mblk)
        pt = jnp.exp(zs[h][...] - m_new).astype(BF16)
        acc_ref[...] = jnp.exp(m - m_new) * acc_ref[...] + jnp.dot(vtj, pt,
                                                                   preferred_element_type=F32)
        return m_new

    def block(j, m, mb0, masked, next_kind):
        mb1 = scores(j, 1, masked)
        m = values(j, 0, m, mb0)
        nb0 = mb0 if next_kind is None else scores(j + 1, 0, next_kind == "masked")
        m = values(j, 1, m, mb1)
        return m, nb0

    acc_ref[...] = jnp.zeros_like(acc_ref)
    m0 = jnp.full((1, 2 * tq), -jnp.inf, F32)

    @pl.when(i == 0)
    def _():
        block(0, m0, scores(0, 0, True), True, None)

    @pl.when(i > 0)
    def _():
        state = (m0, scores(0, 0, False))
        state = lax.fori_loop(0, i - 1, lambda j, st: block(j, st[0], st[1], False, "plain"),
                              state)
        state = block(i - 1, state[0], state[1], False, "masked")
        block(i, state[0], state[1], True, None)

    lam = (jnp.exp(jnp.sum(lq1_ref[...] * lk1_ref[...], axis=1, keepdims=True))
           - jnp.exp(jnp.sum(lq2_ref[...] * lk2_ref[...], axis=1, keepdims=True))
           + lambda_init)
    ot = acc_ref[0:vd, :] / acc_ref[vd:vd + 1, :]
    ot = ot[:, 0:tq] - lam * ot[:, tq:2 * tq]
    o_ref[0] = (_rms(ot.T, g_ref[...]) * (1.0 - lambda_init)).astype(BF16)


def _diff_attention(proj3, lq1, lk1, lq2, lk2, subln, *, q_col, k_col, v_col, n_heads, tq,
                    lambda_init):
    b, s, _ = proj3.shape
    lam_spec = _const_spec((1, DIFF_HEAD_DIM))
    return pl.pallas_call(
        functools.partial(_diff_kernel, tq=tq, lambda_init=lambda_init),
        grid=(b, n_heads, s // tq),
        in_specs=[lam_spec, lam_spec, lam_spec, lam_spec,
                  pl.BlockSpec((1, tq, LANES), lambda bi, h, i: (bi, i, q_col + h)),
                  pl.BlockSpec((1, s, LANES), lambda bi, h, i: (bi, 0, k_col + h)),
                  pl.BlockSpec((1, s, LANES), lambda bi, h, i: (bi, 0, v_col + h)),
                  _const_spec((1, DIFF_V_DIM))],
        out_specs=pl.BlockSpec((1, tq, LANES), lambda bi, h, i: (bi, i, h)),
        out_shape=jax.ShapeDtypeStruct((b, s, n_heads * DIFF_V_DIM), BF16),
        scratch_shapes=[pltpu.VMEM((DIFF_V_DIM + DIFF_ONES_ROWS, s), BF16),
                        pltpu.VMEM((DIFF_V_DIM + DIFF_ONES_ROWS, 2 * tq), F32),
                        pltpu.VMEM((tq // 2, 2 * tq), F32), pltpu.VMEM((tq // 2, 2 * tq), F32)],
        compiler_params=_params(("arbitrary", "arbitrary", "arbitrary")),
        name="diff_attention",
    )(lq1, lk1, lq2, lk2, proj3, proj3, proj3, subln)


def _mem_kv_kernel(mem_ref, g_ref, w_ref, o_ref):
    h = _rms(mem_ref[0], g_ref[...]).astype(BF16)
    o_ref[0] = jnp.dot(h, w_ref[...], preferred_element_type=F32).astype(BF16)


def _mem_kv(mem, g, w_bf16):
    b, m, d = mem.shape
    n = w_bf16.shape[1]
    return pl.pallas_call(
        _mem_kv_kernel,
        grid=(b,),
        in_specs=[pl.BlockSpec((1, m, d), lambda i: (i, 0, 0)), _const_spec((1, d)),
                  _const_spec((d, n))],
        out_specs=pl.BlockSpec((1, m, n), lambda i: (i, 0, 0)),
        out_shape=jax.ShapeDtypeStruct((b, m, n), BF16),
        compiler_params=_params(("arbitrary",)),
        name="mem_kv",
    )(mem, g, w_bf16)


def _split_bf16(v):
    hi = v.astype(BF16)
    return hi, (v - hi.astype(F32)).astype(BF16)


def _pack_bf16_pairs(v):
    half = v.shape[1] // 2
    bits = lax.bitcast_convert_type(v.astype(BF16).astype(F32), U32)
    return (bits[:, :half] >> 16) | (bits[:, half:] & jnp.uint32(0xFFFF0000))


def _unpack_bf16_pairs(w):
    lo = lax.bitcast_convert_type(w << 16, F32)
    hi = lax.bitcast_convert_type(w & jnp.uint32(0xFFFF0000), F32)
    return jnp.concatenate([lo, hi], axis=1).astype(BF16)


def _postmix_kernel(x_ref, osb_ref, odf_ref, gsb_ref, gdf_ref, wus_ref, wud_ref, wout_ref,
                    gq_ref, wq_ref, kv_ref, wo_ref, gf_ref, wr_ref, br_ref,
                    x2_ref, hf_ref, route_ref, gate_ref, cum_ref, count_ref, *, tm, d_model):
    step = pl.program_id(0)

    @pl.when(step == 0)
    def _():
        count_ref[...] = jnp.zeros_like(count_ref)

    y_sb = jnp.dot(osb_ref[...], wus_ref[...], preferred_element_type=F32)
    y_df = jnp.dot(odf_ref[...], wud_ref[...], preferred_element_type=F32)
    mixed = (jax.nn.sigmoid(gsb_ref[...].astype(F32)) * y_sb
             + jax.nn.sigmoid(gdf_ref[...].astype(F32)) * y_df)
    x1 = x_ref[...] + jnp.dot(mixed.astype(BF16), wout_ref[...], preferred_element_type=F32)

    hq = _rms(x1, gq_ref[...]).astype(BF16)
    hd = d_model // MEM_HEADS
    q = jnp.dot(hq, wq_ref[...], preferred_element_type=F32) * (1.0 / math.sqrt(hd))
    q = q.astype(BF16)
    heads = []
    for h in range(MEM_HEADS):
        kh = kv_ref[0, :, h * hd:(h + 1) * hd]
        vh = kv_ref[0, :, d_model + h * hd:d_model + (h + 1) * hd]
        z = lax.dot_general(q[:, h * hd:(h + 1) * hd], kh, (((1,), (1,)), ((), ())),
                            preferred_element_type=F32)
        p = jnp.exp(z - jnp.max(z, axis=1, keepdims=True))
        l = jnp.sum(p, axis=1, keepdims=True)
        heads.append((jnp.dot(p.astype(BF16), vh, preferred_element_type=F32) / l).astype(BF16))
    x2 = x1 + jnp.dot(jnp.concatenate(heads, axis=1), wo_ref[...], preferred_element_type=F32)
    x2_ref[...] = x2

    hf = _rms(x2, gf_ref[...])
    packed = _pack_bf16_pairs(hf)
    for j in range(PACKED_SUBROWS):
        hf_ref[pl.ds(j, tm, stride=ROW_TILE), :] = packed[:, j * LANES:(j + 1) * LANES]
    h_hi, h_lo = _split_bf16(hf)
    w_hi, w_lo = _split_bf16(wr_ref[...])
    logits = (jnp.dot(h_hi, w_hi, preferred_element_type=F32)
              + jnp.dot(h_hi, w_lo, preferred_element_type=F32)
              + jnp.dot(h_lo, w_hi, preferred_element_type=F32)) + br_ref[...]
    lane = lax.broadcasted_iota(I32, (tm, LANES), 1)
    work = logits
    vals, idxs, hots = [], [], []
    for _ in range(TOP_K):
        mx = jnp.max(work, axis=1, keepdims=True)
        idx = jnp.min(jnp.where(work == mx, lane, LANES), axis=1, keepdims=True)
        hot = lane == idx
        work = jnp.where(hot, NEG_BIG, work)
        vals.append(mx)
        idxs.append(idx)
        hots.append(hot)
    exps = [jnp.exp(v - vals[0]) for v in vals]
    denom = exps[0] + exps[1] + exps[2] + exps[3]

    onehot_sum = jnp.zeros((tm, LANES), F32)
    for hot in hots:
        onehot_sum = onehot_sum + jnp.where(hot, 1.0, 0.0)
    r = lax.broadcasted_iota(I32, (tm, tm), 0)
    c = lax.broadcasted_iota(I32, (tm, tm), 1)
    lower = jnp.where(c < r, 1.0, 0.0).astype(BF16)
    rank = jnp.dot(lower, onehot_sum.astype(BF16), preferred_element_type=F32) + count_ref[...]
    route = jnp.zeros((tm, LANES), I32)
    gates = jnp.zeros((tm, LANES), F32)
    for k in range(TOP_K):
        pos = jnp.sum(jnp.where(hots[k], rank, 0.0), axis=1, keepdims=True).astype(I32)
        route = jnp.where(lane == k, idxs[k], route)
        route = jnp.where(lane == TOP_K + k, pos, route)
        gates = jnp.where(lane == k, exps[k] / denom, gates)
    route_ref[...] = route
    gate_ref[...] = gates
    meta = jnp.where(lane == 0, step * tm + lax.broadcasted_iota(I32, (tm, LANES), 0), 0)
    for k in range(TOP_K):
        meta = jnp.where(lane == 1 + k, idxs[k], meta)
    hf_ref[pl.ds(META_SUBROW, tm, stride=ROW_TILE), :] = meta.astype(U32)
    for j in range(META_SUBROW + 1, ROW_TILE):
        hf_ref[pl.ds(j, tm, stride=ROW_TILE), :] = jnp.zeros((tm, LANES), U32)
    count_ref[...] = count_ref[...] + jnp.sum(onehot_sum, axis=0, keepdims=True)
    cum_ref[0] = jnp.broadcast_to(count_ref[...], (8, LANES))


def _post_mix(x2d, proj2, o_sb, o_diff, w_up_sb, w_up_diff, w_out, g_memq, w_memq, kv, w_memo,
              g_ffn, w_router_pad, b_router_pad, *, tm, seq, gate_col):
    n_tok, d = x2d.shape
    n_tiles = n_tok // tm
    tiles_per_batch = seq // tm
    row = lambda i: (i, 0)
    in_specs = [
        pl.BlockSpec((tm, d), row),
        pl.BlockSpec((tm, o_sb.shape[1]), row),
        pl.BlockSpec((tm, o_diff.shape[1]), row),
        pl.BlockSpec((tm, d), lambda i: (i, gate_col)),
        pl.BlockSpec((tm, d), lambda i: (i, gate_col + 1)),
        _const_spec(w_up_sb.shape), _const_spec(w_up_diff.shape), _const_spec(w_out.shape),
        _const_spec((1, d)), _const_spec(w_memq.shape),
        pl.BlockSpec((1,) + kv.shape[1:], lambda i: (i // tiles_per_batch, 0, 0)),
        _const_spec(w_memo.shape), _const_spec((1, d)),
        _const_spec(w_router_pad.shape), _const_spec((1, LANES)),
    ]
    out_specs = [
        pl.BlockSpec((tm, d), row),
        pl.BlockSpec((tm * ROW_TILE, LANES), row),
        pl.BlockSpec((tm, LANES), row),
        pl.BlockSpec((tm, LANES), row),
        pl.BlockSpec((1, 8, LANES), lambda i: (i, 0, 0)),
    ]
    assert d // 2 == PACKED_SUBROWS * LANES
    out_shape = [
        jax.ShapeDtypeStruct((n_tok, d), F32),
        jax.ShapeDtypeStruct((n_tok * ROW_TILE, LANES), U32),
        jax.ShapeDtypeStruct((n_tok, LANES), I32),
        jax.ShapeDtypeStruct((n_tok, LANES), F32),
        jax.ShapeDtypeStruct((n_tiles, 8, LANES), F32),
    ]
    return pl.pallas_call(
        functools.partial(_postmix_kernel, tm=tm, d_model=d),
        grid=(n_tiles,),
        in_specs=in_specs,
        out_specs=out_specs,
        out_shape=out_shape,
        scratch_shapes=[pltpu.VMEM((1, LANES), F32)],
        compiler_params=_params(("arbitrary",)),
        name="post_mix",
    )(x2d, o_sb, o_diff, proj2, proj2, w_up_sb, w_up_diff, w_out, g_memq, w_memq, kv, w_memo,
      g_ffn, w_router_pad, b_router_pad)


def _dispatch_kernel(zstart_ref, zflag_ref, nused_ref, dest_ref, hf_ref, xs_hbm, stage0, stage1,
                     zbuf, sem, zsem, *, tm, bm):
    s = pl.program_id(0)

    @pl.when(s == 0)
    def _():
        lane = lax.broadcasted_iota(I32, (bm, LANES), 1)
        zbuf[...] = jnp.zeros_like(zbuf)
        zbuf[pl.ds(META_SUBROW, bm, stride=ROW_TILE), :] = jnp.where(
            jnp.logical_and(lane >= 1, lane <= TOP_K), -1, 0).astype(U32)

        def fill_copy(start):
            start = pl.multiple_of(start * ROW_TILE, bm * ROW_TILE)
            return pltpu.make_async_copy(zbuf, xs_hbm.at[pl.ds(start, bm * ROW_TILE), :], zsem)

        n_blocks = xs_hbm.shape[0] // (bm * ROW_TILE)
        for action in ("start", "wait"):
            for e in range(N_EXPERTS):
                @pl.when(zflag_ref[e] != 0)
                def _(e=e, action=action):
                    getattr(fill_copy(zstart_ref[e]), action)()

            def trailing(blk, _, action=action):
                getattr(fill_copy(blk * bm), action)()
                return 0

            lax.fori_loop(nused_ref[0], n_blocks, trailing, 0)

    def tile_done(stage, parity):
        for _ in range(TOP_K):
            pltpu.make_async_copy(stage, xs_hbm.at[pl.ds(0, tm * ROW_TILE), :],
                                  sem.at[parity]).wait()

    def run(stage, other, parity):
        stage[...] = hf_ref[...]

        def issue(t, _):
            src = stage.at[pl.ds(pl.multiple_of(t * ROW_TILE, ROW_TILE), ROW_TILE), :]
            for k in range(TOP_K):
                row = pl.multiple_of(dest_ref[t * TOP_K + k] * ROW_TILE, ROW_TILE)
                pltpu.make_async_copy(src, xs_hbm.at[pl.ds(row, ROW_TILE), :],
                                      sem.at[parity]).start(priority=k % 2)
            return 0

        lax.fori_loop(0, tm, issue, 0)

        @pl.when(s > 0)
        def _():
            tile_done(other, 1 - parity)

        @pl.when(s == pl.num_programs(0) - 1)
        def _():
            tile_done(stage, parity)

    @pl.when(s % 2 == 0)
    def _():
        run(stage0, stage1, 0)

    @pl.when(s % 2 == 1)
    def _():
        run(stage1, stage0, 1)


def _dispatch(zstart, zflag, n_used, dest_flat, hf_rows, *, tm, bm, n_rows):
    n_tok = hf_rows.shape[0] // ROW_TILE
    grid_spec = pltpu.PrefetchScalarGridSpec(
        num_scalar_prefetch=3,
        grid=(n_tok // tm,),
        in_specs=[
            pl.BlockSpec((tm * TOP_K,), lambda i, *_: (i,), memory_space=pltpu.SMEM),
            pl.BlockSpec((tm * ROW_TILE, LANES), lambda i, *_: (i, 0)),
        ],
        out_specs=pl.BlockSpec(memory_space=pl.ANY),
        scratch_shapes=[pltpu.VMEM((tm * ROW_TILE, LANES), U32),
                        pltpu.VMEM((tm * ROW_TILE, LANES), U32),
                        pltpu.VMEM((bm * ROW_TILE, LANES), U32),
                        pltpu.SemaphoreType.DMA((2,)), pltpu.SemaphoreType.DMA(())],
    )
    return pl.pallas_call(
        functools.partial(_dispatch_kernel, tm=tm, bm=bm),
        grid_spec=grid_spec,
        out_shape=jax.ShapeDtypeStruct((n_rows * ROW_TILE, LANES), U32),
        compiler_params=_params(("arbitrary",)),
        name="dispatch",
    )(zstart, zflag, n_used, dest_flat, hf_rows)


FFN_UP_CHUNKS = 8
FFN_DOWN_CHUNKS = 4


def _expert_kernel(be_ref, valid_ref, src_ref, wslot_ref, next_ref, xs_ref, bgu_ref, bd_ref,
                   wgu_hbm, wd_hbm, ys_hbm, wgu_f32, wd_f32, wgu_bf, wd_bf, ybuf0, ybuf1, slot_v,
                   slot_s, sem_w, sem_y, sem_s, sem_z, *, d_ff, bm, n_tok):
    r = pl.program_id(0)
    valid = valid_ref[r] != 0
    prev_valid = jnp.logical_and(r > 0, valid_ref[jnp.maximum(r - 1, 0)] != 0)
    fresh = jnp.logical_or(r == 0, be_ref[r] != be_ref[jnp.maximum(r - 1, 0)])
    expert = be_ref[r]
    ybufs = (ybuf0, ybuf1)

    def weight_copies(e, wslot):
        return (pltpu.make_async_copy(wgu_hbm.at[e], wgu_f32.at[wslot], sem_w.at[wslot]),
                pltpu.make_async_copy(wd_hbm.at[e], wd_f32.at[wslot], sem_w.at[wslot]))
    d = wd_bf.shape[1]
    n_slots = TOP_K * n_tok
    block_rows = bm * ROW_TILE

    def rows_done(parity):
        return pltpu.make_async_copy(ybufs[parity], ys_hbm.at[pl.ds(0, block_rows), :],
                                     sem_y.at[parity])

    def slots_copy(parity):
        return pltpu.make_async_copy(slot_v.at[0:1, :], slot_s.at[parity:parity + 1, :],
                                     sem_s.at[parity])

    def scatter_rows(prev, lo, hi):
        for i in range(lo, hi):
            dst = pl.multiple_of(slot_s[prev, i] * ROW_TILE, ROW_TILE)
            pltpu.make_async_copy(ybufs[prev].at[pl.ds(i * ROW_TILE, ROW_TILE), :],
                                  ys_hbm.at[pl.ds(dst, ROW_TILE), :],
                                  sem_y.at[prev]).start(priority=i % 2)

    @pl.when(r == 0)
    def _():
        for action in ("start", "wait"):
            for parity in range(2):
                if action == "start":
                    ybufs[parity][...] = jnp.zeros_like(ybufs[parity])
                trash = (n_slots + parity * bm) * ROW_TILE
                getattr(pltpu.make_async_copy(ybufs[parity],
                                              ys_hbm.at[pl.ds(trash, block_rows), :], sem_z),
                        action)()

    for parity in range(2):
        @pl.when(jnp.logical_and(prev_valid, r % 2 == parity))
        def _(parity=parity):
            slots_copy(1 - parity).wait()

    @pl.when(jnp.logical_and(valid, fresh))
    def _():
        wslot = wslot_ref[r]

        @pl.when(r == 0)
        def _():
            for c in weight_copies(expert, wslot):
                c.start()

        for c in weight_copies(expert, wslot):
            c.wait()
        wgu_bf[...] = wgu_f32[wslot].astype(BF16)
        wd_bf[...] = wd_f32[wslot].astype(BF16)

        @pl.when(next_ref[r] >= 0)
        def _():
            for c in weight_copies(next_ref[r], 1 - wslot):
                c.start()

    def ffn(cur, interleave):
        prev = 1 - cur
        ybuf = ybufs[cur]
        bounds = [bm * c // FFN_UP_CHUNKS for c in range(FFN_UP_CHUNKS + 1)]

        meta = xs_ref[pl.ds(META_SUBROW, bm, stride=ROW_TILE), :].astype(I32)
        row = lax.broadcasted_iota(I32, (bm, 1), 0)
        slot = n_slots + cur * bm + row
        for k in range(TOP_K):
            slot = jnp.where(meta[:, 1 + k:2 + k] == be_ref[r], k * n_tok + meta[:, 0:1], slot)
        slots = jnp.broadcast_to(slot.astype(F32), (bm, LANES)).T
        slot_v[...] = slots[0:8, :].astype(I32)
        slots_copy(cur).start()

        words = jnp.concatenate([xs_ref[pl.ds(j, bm, stride=ROW_TILE), :]
                                 for j in range(PACKED_SUBROWS)], axis=1)
        xb = _unpack_bf16_pairs(words)
        up_w = 2 * d_ff // FFN_UP_CHUNKS
        pairs = FFN_UP_CHUNKS // 2
        acts = []
        for c in range(pairs):
            halves = []
            for half in range(2):
                cols = slice(half * d_ff + c * up_w, half * d_ff + (c + 1) * up_w)
                halves.append(jnp.dot(xb, wgu_bf[:, cols], preferred_element_type=F32)
                              + bgu_ref[pl.ds(expert, 1), cols])
                if interleave:
                    step_no = 2 * c + half
                    scatter_rows(prev, bounds[step_no], bounds[step_no + 1])
            glu = jnp.minimum(halves[0], SWIGLU_LIMIT)
            lin = jnp.clip(halves[1], -SWIGLU_LIMIT, SWIGLU_LIMIT)
            acts.append((glu * jax.nn.sigmoid(SWIGLU_ALPHA * glu) * (lin + 1.0)).astype(BF16))
        act = jnp.concatenate(acts, axis=1)

        @pl.when(r >= 2)
        def _():
            rows_done(cur).wait()

        down_w = d // FFN_DOWN_CHUNKS
        for c in range(FFN_DOWN_CHUNKS):
            cols = slice(c * down_w, (c + 1) * down_w)
            y = (jnp.dot(act, wd_bf[:, cols], preferred_element_type=F32)
                 + bd_ref[pl.ds(expert, 1), cols])
            for j in range(down_w // LANES):
                sub = c * (down_w // LANES) + j
                ybuf[pl.ds(sub, bm, stride=ROW_TILE), :] = y[:, j * LANES:(j + 1) * LANES]

    for parity in range(2):
        on_parity = r % 2 == parity

        @pl.when(jnp.logical_and(on_parity, jnp.logical_and(valid, prev_valid)))
        def _(parity=parity):
            ffn(parity, True)

        @pl.when(jnp.logical_and(on_parity, jnp.logical_and(jnp.logical_not(valid), prev_valid)))
        def _(parity=parity):
            scatter_rows(1 - parity, 0, bm)

            @pl.when(r >= 2)
            def _():
                rows_done(parity).wait()

            rows_done(1 - parity).wait()

    @pl.when(r == 0)
    def _():
        ffn(0, False)


def _expert_ffn(blk_expert, blk_valid, blk_src, blk_wslot, blk_next, xs, w_gu, b_gu, w_down,
                b_down, *, bm, n_tok):
    n_grid = blk_expert.shape[0]
    n_exp, d_ff, d = w_down.shape
    assert d == ROW_TILE * LANES
    block_rows = bm * ROW_TILE
    grid_spec = pltpu.PrefetchScalarGridSpec(
        num_scalar_prefetch=5,
        grid=(n_grid,),
        in_specs=[
            pl.BlockSpec((block_rows, LANES), lambda r, be, valid, src, *_: (src[r], 0)),
            pl.BlockSpec((n_exp, 2 * d_ff), lambda r, *_: (0, 0)),
            pl.BlockSpec((n_exp, d), lambda r, *_: (0, 0)),
            pl.BlockSpec(memory_space=pl.ANY),
            pl.BlockSpec(memory_space=pl.ANY),
        ],
        out_specs=pl.BlockSpec(memory_space=pl.ANY),
        scratch_shapes=[pltpu.VMEM((2, d, 2 * d_ff), F32), pltpu.VMEM((2, d_ff, d), F32),
                        pltpu.VMEM((d, 2 * d_ff), BF16), pltpu.VMEM((d_ff, d), BF16),
                        pltpu.VMEM((block_rows, LANES), F32),
                        pltpu.VMEM((block_rows, LANES), F32), pltpu.VMEM((8, bm), I32),
                        pltpu.SMEM((2, bm), I32), pltpu.SemaphoreType.DMA((2,)),
                        pltpu.SemaphoreType.DMA((2,)), pltpu.SemaphoreType.DMA((2,)),
                        pltpu.SemaphoreType.DMA(())],
    )
    return pl.pallas_call(
        functools.partial(_expert_kernel, d_ff=d_ff, bm=bm, n_tok=n_tok),
        grid_spec=grid_spec,
        out_shape=jax.ShapeDtypeStruct(((TOP_K * n_tok + 2 * bm) * ROW_TILE, LANES), F32),
        compiler_params=_params(("arbitrary",)),
        name="expert_ffn",
    )(blk_expert, blk_valid, blk_src, blk_wslot, blk_next, xs, b_gu, b_down, w_gu, w_down)


def _combine_kernel(x_ref, gate_ref, g_ref, y0_ref, y1_ref, y2_ref, y3_ref, o_ref):
    tm, d = x_ref.shape
    gates = gate_ref[...]
    acc = x_ref[...]
    for k, y_ref in enumerate((y0_ref, y1_ref, y2_ref, y3_ref)):
        y = jnp.concatenate([y_ref[pl.ds(j, tm, stride=ROW_TILE), :]
                             for j in range(d // LANES)], axis=1)
        acc = acc + gates[:, k:k + 1] * y
    o_ref[...] = _rms(acc, g_ref[...])


def _combine(x2, gates, g_final, ys, *, tm):
    n_tok, d = x2.shape
    tiles = n_tok // tm
    slot_spec = lambda k: pl.BlockSpec((tm * ROW_TILE, LANES), lambda i: (k * tiles + i, 0))
    return pl.pallas_call(
        _combine_kernel,
        grid=(tiles,),
        in_specs=[
            pl.BlockSpec((tm, d), lambda i: (i, 0)),
            pl.BlockSpec((tm, LANES), lambda i: (i, 0)),
            pl.BlockSpec((1, d), lambda i: (0, 0)),
        ] + [slot_spec(k) for k in range(TOP_K)],
        out_specs=pl.BlockSpec((tm, d), lambda i: (i, 0)),
        out_shape=jax.ShapeDtypeStruct((n_tok, d), F32),
        compiler_params=_params(("arbitrary",)),
        name="combine",
    )(x2, gates, g_final, ys, ys, ys, ys)


def _routing_tables(route, cum, *, bm, n_grid):
    expert = route[:, 0:TOP_K]
    pos = route[:, TOP_K:2 * TOP_K]
    experts = jnp.arange(N_EXPERTS, dtype=I32)
    counts = cum[-1, 0, :N_EXPERTS].astype(I32)
    padded = (counts + bm - 1) // bm * bm
    pend = jnp.cumsum(padded)
    pstart = pend - padded
    dest = pos + jnp.sum(jnp.where(expert[:, :, None] == experts, pstart, 0), axis=2)

    blk = jnp.arange(n_grid, dtype=I32)
    blk_expert = jnp.minimum(jnp.sum(pend[None, :] <= (blk * bm)[:, None], axis=1),
                             N_EXPERTS - 1).astype(I32)
    n_used = pend[-1] // bm
    blk_valid = (blk < n_used).astype(I32)
    blk_src = jnp.minimum(blk, jnp.maximum(n_used - 1, 0)).astype(I32)
    zflag = (padded > 0).astype(I32)
    zstart = jnp.maximum(pend - bm, 0).astype(I32)
    wslot = (jnp.cumsum(zflag) - 1) % 2
    later = jnp.logical_and(experts[None, :] > experts[:, None], zflag[None, :] > 0)
    nxt = jnp.min(jnp.where(later, experts[None, :], N_EXPERTS), axis=1)
    nxt = jnp.where(nxt < N_EXPERTS, nxt, -1)
    owner = blk_expert[:, None] == experts[None, :]
    of_block = lambda v: jnp.sum(jnp.where(owner, v[None, :], 0), axis=1).astype(I32)
    return (dest.astype(I32).reshape(-1), blk_expert, blk_valid, blk_src, of_block(wslot),
            of_block(nxt), zstart, zflag, n_used.astype(I32).reshape(1))


def kernel(x, mem, positions, norm_mix, w_in, lambda_q1, lambda_k1, lambda_q2, lambda_k2,
           diff_subln, w_up_sb, w_up_diff, w_out, norm_mem_q, norm_mem_kv, w_mem_q, w_mem_kv,
           w_mem_o, norm_ffn, w_router, b_router, w_gate_up, b_gate_up, w_down, b_down,
           norm_final):
    b, s, d = x.shape
    n_tok = b * s
    depth = norm_mix.shape[0]
    sb_width = w_up_sb.shape[1]
    diff_width = w_up_diff.shape[1]
    n_in = w_in.shape[2]
    chunk = 512
    assert sb_width == chunk and diff_width == chunk and d == 2 * chunk
    sbq, sbk, sbv, dq, dk, dv = range(6)
    blocks_per_chunk = chunk // LANES
    scale = 1.0 / math.sqrt(SB_HEAD_DIM)
    assert SB_HEAD_DIM == DIFF_HEAD_DIM

    tm_in = min(512, n_tok)
    tq_sb = min(256, s)
    tq_diff = min(512, s)
    tm_post = min(512, s)
    tm_tok = min(256, n_tok)
    bm = 256
    n_grid = (n_tok * TOP_K) // bm + N_EXPERTS

    cos_t, sin_t = _rope_tables(positions)
    x2d = x.reshape(n_tok, d)
    for l in range(depth):
        lambda_init = 0.8 - 0.6 * math.exp(-0.3 * l)
        proj = _in_proj(x2d, norm_mix[l].reshape(1, d), w_in[l].astype(BF16), cos_t, sin_t,
                        tm=tm_in, chunk=chunk, rope_chunks=(dq, dk), scale_chunks=(sbq, dq),
                        scale=scale)
        proj3 = proj.reshape(b, s, n_in)
        o_sb = _sb_attention(proj3, q_col=sbq * blocks_per_chunk, k_col=sbk * blocks_per_chunk,
                             v_col=sbv * blocks_per_chunk, n_pairs=sb_width // LANES, tq=tq_sb)
        o_diff = _diff_attention(
            proj3, lambda_q1[l].reshape(1, -1), lambda_k1[l].reshape(1, -1),
            lambda_q2[l].reshape(1, -1), lambda_k2[l].reshape(1, -1),
            diff_subln[l].reshape(1, -1), q_col=dq * blocks_per_chunk,
            k_col=dk * blocks_per_chunk, v_col=dv * blocks_per_chunk,
            n_heads=diff_width // DIFF_V_DIM, tq=tq_diff, lambda_init=lambda_init)
        kv = _mem_kv(mem, norm_mem_kv[l].reshape(1, d), w_mem_kv[l].astype(BF16))
        w_router_pad = jnp.zeros((d, LANES), F32).at[:, :N_EXPERTS].set(w_router[l])
        b_router_pad = jnp.full((1, LANES), NEG_BIG, F32).at[0, :N_EXPERTS].set(b_router[l])
        x_res, hf, route, gates, cum = _post_mix(
            x2d, proj, o_sb.reshape(n_tok, sb_width), o_diff.reshape(n_tok, diff_width),
            w_up_sb[l].astype(BF16), w_up_diff[l].astype(BF16), w_out[l].astype(BF16),
            norm_mem_q[l].reshape(1, d), w_mem_q[l].astype(BF16), kv, w_mem_o[l].astype(BF16),
            norm_ffn[l].reshape(1, d), w_router_pad, b_router_pad,
            tm=tm_post, seq=s, gate_col=6 * chunk // d)
        (dest, blk_expert, blk_valid, blk_src, blk_wslot, blk_next, zstart, zflag,
         n_used) = _routing_tables(route, cum, bm=bm, n_grid=n_grid + 1)
        xs = _dispatch(zstart, zflag, n_used, dest, hf, tm=tm_tok, bm=bm, n_rows=n_grid * bm)
        ys = _expert_ffn(blk_expert, blk_valid, blk_src, blk_wslot, blk_next, xs, w_gate_up[l],
                         b_gate_up[l], w_down[l], b_down[l], bm=bm, n_tok=n_tok)
        g_next = norm_final if l == depth - 1 else None
        assert g_next is not None, "only depth 1 is wired: the final norm is fused into combine"
        x2d = _combine(x_res, gates, g_next.reshape(1, d), ys, tm=tm_tok)
    return x2d.reshape(b, s, d)
```

```python
import functools
import math

import jax
import jax.numpy as jnp
from jax import lax
from jax.experimental import pallas as pl
from jax.experimental.pallas import tpu as pltpu

F32 = jnp.float32
BF16 = jnp.bfloat16
I32 = jnp.int32
U32 = jnp.uint32

LANES = 128
VMEM_LIMIT_BYTES = 56 * 1024 * 1024
ROW_TILE = 8
PACKED_SUBROWS = 4
META_SUBROW = 4

NORM_EPS = 1e-6
ROPE_THETA = 10000.0
CHUNK = 64
SB_HEAD_DIM = 64
DIFF_HEAD_DIM = 64
DIFF_V_DIM = 128
MEM_HEADS = 4
N_EXPERTS = 32
TOP_K = 4
SWIGLU_LIMIT = 7.0
SWIGLU_ALPHA = 1.702

SB_DEAD_LOG = -105.0
NEG_BIG = -1e30


def _params(semantics):
    return pltpu.CompilerParams(dimension_semantics=semantics,
                                vmem_limit_bytes=VMEM_LIMIT_BYTES)


def _const_spec(shape):
    nd = len(shape)
    return pl.BlockSpec(shape, lambda *_: (0,) * nd)


def _rms(x, g):
    return x * lax.rsqrt(jnp.mean(x * x, axis=-1, keepdims=True) + NORM_EPS) * g


def _rope_table_kernel(pos_ref, inv_ref, cos_ref, sin_ref):
    ang = pos_ref[...] * inv_ref[...]
    cos_ref[...] = jnp.cos(ang)
    sin_ref[...] = jnp.sin(ang)


def _rope_tables(positions):
    n_tok = positions.size
    half = DIFF_HEAD_DIM // 2
    per_row = LANES // half
    rows = n_tok // per_row
    pos = jnp.repeat(positions.reshape(rows, per_row).astype(F32), half, axis=1)
    inv = ROPE_THETA ** (-jnp.arange(half, dtype=F32) / half)
    inv = jnp.tile(inv, per_row).reshape(1, LANES)
    tr = min(rows, 512)
    cos, sin = pl.pallas_call(
        _rope_table_kernel,
        grid=(rows // tr,),
        in_specs=[pl.BlockSpec((tr, LANES), lambda i: (i, 0)), _const_spec((1, LANES))],
        out_specs=[pl.BlockSpec((tr, LANES), lambda i: (i, 0))] * 2,
        out_shape=[jax.ShapeDtypeStruct((rows, LANES), F32)] * 2,
        compiler_params=_params(("arbitrary",)),
        name="rope_table",
    )(pos, inv)
    cos = cos.reshape(n_tok, half)
    sin = sin.reshape(n_tok, half)
    cos_t = jnp.tile(cos, (1, LANES // half))
    sin_t = jnp.tile(jnp.concatenate([-sin, sin], axis=1), (1, LANES // (2 * half)))
    return cos_t, sin_t


def _inproj_kernel(x_ref, g_ref, w_ref, cos_ref, sin_ref, o_ref, *, chunk, rope_chunks,
                   scale_chunks, scale):
    h = _rms(x_ref[...], g_ref[...]).astype(BF16)
    lane = lax.broadcasted_iota(I32, (1, chunk), 1)
    first_half = (lane % DIFF_HEAD_DIM) < (DIFF_HEAD_DIM // 2)
    for c in range(w_ref.shape[1] // chunk):
        cols = slice(c * chunk, (c + 1) * chunk)
        acc = jnp.dot(h, w_ref[:, cols], preferred_element_type=F32)
        if c in rope_chunks:
            cos = jnp.tile(cos_ref[...], (1, chunk // LANES))
            sin = jnp.tile(sin_ref[...], (1, chunk // LANES))
            partner = jnp.where(first_half,
                                pltpu.roll(acc, chunk - DIFF_HEAD_DIM // 2, 1),
                                pltpu.roll(acc, DIFF_HEAD_DIM // 2, 1))
            acc = acc * cos + partner * sin
        if c in scale_chunks:
            acc = acc * scale
        o_ref[:, cols] = acc.astype(BF16)


def _in_proj(x2d, g, w_bf16, cos_t, sin_t, *, tm, chunk, rope_chunks, scale_chunks, scale):
    n_tok, d = x2d.shape
    n_in = w_bf16.shape[1]
    kern = functools.partial(_inproj_kernel, chunk=chunk, rope_chunks=rope_chunks,
                             scale_chunks=scale_chunks, scale=scale)
    return pl.pallas_call(
        kern,
        grid=(n_tok // tm,),
        in_specs=[pl.BlockSpec((tm, d), lambda i: (i, 0)),
                  _const_spec((1, d)),
                  _const_spec((d, n_in)),
                  pl.BlockSpec((tm, LANES), lambda i: (i, 0)),
                  pl.BlockSpec((tm, LANES), lambda i: (i, 0))],
        out_specs=pl.BlockSpec((tm, n_in), lambda i: (i, 0)),
        out_shape=jax.ShapeDtypeStruct((n_tok, n_in), BF16),
        compiler_params=_params(("arbitrary",)),
        name="in_proj",
    )(x2d, g, w_bf16, cos_t, sin_t)


def _sb_kernel(q_ref, k_ref, v_ref, o_ref, acc_ref, lw0, lw1, cat0, cat1, *, tq):
    i = pl.program_id(2)
    q = q_ref[0]
    lane = lax.broadcasted_iota(I32, (1, LANES), 1)
    zero = jnp.zeros_like(q)
    q2 = jnp.concatenate([jnp.where(lane < SB_HEAD_DIM, q, zero),
                          jnp.where(lane >= SB_HEAD_DIM, q, zero)], axis=0)
    uj = lax.broadcasted_iota(I32, (2 * tq, tq), 0) % tq
    us = lax.broadcasted_iota(I32, (2 * tq, tq), 1)
    suffix = jnp.where(uj > us, 1.0, 0.0).astype(BF16)
    lws = (lw0, lw1)
    cats = (cat0, cat1)

    def causal_mask():
        return (lax.broadcasted_iota(I32, (2 * tq, tq), 1)
                < lax.broadcasted_iota(I32, (2 * tq, tq), 0) % tq)

    def logits(j, buf, masked):
        start = pl.multiple_of(j * tq, tq)
        kj = k_ref[0, pl.ds(start, tq), :]
        z = lax.dot_general(q2, kj, (((1,), (1,)), ((), ())), preferred_element_type=F32)
        sp = jnp.maximum(z, 0.0) + jnp.log(1.0 + jnp.exp(-jnp.abs(z)))
        log_rem = -sp
        if masked:
            log_rem = jnp.where(causal_mask(), log_rem, 0.0)
        lws[buf][...] = z - sp
        hi = log_rem.astype(BF16)
        cats[buf][:, 0:tq] = hi
        cats[buf][:, tq:2 * tq] = (log_rem - hi.astype(F32)).astype(BF16)
        return log_rem[:, 0:1]

    def suffix_sums(buf, first):
        after = jnp.dot(cats[buf][...], suffix, preferred_element_type=F32)
        lws[buf][...] += after
        return after[:, 0:1] + first

    def weigh(j, buf, carry, masked):
        start = pl.multiple_of(j * tq, tq)
        vj = v_ref[0, pl.ds(start, tq), :]
        w = jnp.exp(lws[buf][...] + carry)
        if masked:
            w = jnp.where(causal_mask(), w, 0.0)
        acc_ref[...] += jnp.dot(w.astype(BF16), vj, preferred_element_type=F32)

    def earlier_blocks(first_block, carry):
        def cond(state):
            j, _, alive = state
            return jnp.logical_and(j >= 0, alive > SB_DEAD_LOG)

        def body(state):
            j, carry, _ = state
            total = suffix_sums(0, logits(j, 0, False))
            weigh(j, 0, carry, False)
            carry = carry + total
            return j - 1, carry, jnp.max(carry)

        lax.while_loop(cond, body, (first_block, carry, jnp.max(carry)))

    acc_ref[...] = jnp.zeros_like(acc_ref)
    no_carry = jnp.zeros((2 * tq, 1), F32)

    @pl.when(i == 0)
    def _():
        suffix_sums(0, logits(0, 0, True))
        weigh(0, 0, no_carry, True)

    @pl.when(i > 0)
    def _():
        first_diag = logits(i, 0, True)
        first_prev = logits(i - 1, 1, False)
        total_diag = suffix_sums(0, first_diag)
        total_prev = suffix_sums(1, first_prev)
        weigh(i, 0, no_carry, True)
        weigh(i - 1, 1, total_diag, False)
        earlier_blocks(i - 2, total_diag + total_prev)

    o_ref[0] = jnp.where(lane < SB_HEAD_DIM, acc_ref[0:tq, :], acc_ref[tq:2 * tq, :]).astype(BF16)


def _sb_attention(proj3, *, q_col, k_col, v_col, n_pairs, tq):
    b, s, _ = proj3.shape
    return pl.pallas_call(
        functools.partial(_sb_kernel, tq=tq),
        grid=(b, n_pairs, s // tq),
        in_specs=[pl.BlockSpec((1, tq, LANES), lambda bi, p, i: (bi, i, q_col + p)),
                  pl.BlockSpec((1, s, LANES), lambda bi, p, i: (bi, 0, k_col + p)),
                  pl.BlockSpec((1, s, LANES), lambda bi, p, i: (bi, 0, v_col + p))],
        out_specs=pl.BlockSpec((1, tq, LANES), lambda bi, p, i: (bi, i, p)),
        out_shape=jax.ShapeDtypeStruct((b, s, n_pairs * LANES), BF16),
        scratch_shapes=[pltpu.VMEM((2 * tq, LANES), F32),
                        pltpu.VMEM((2 * tq, tq), F32), pltpu.VMEM((2 * tq, tq), F32),
                        pltpu.VMEM((2 * tq, 2 * tq), BF16), pltpu.VMEM((2 * tq, 2 * tq), BF16)],
        compiler_params=_params(("arbitrary", "arbitrary", "arbitrary")),
        name="sb_attention",
    )(proj3, proj3, proj3)


DIFF_ONES_ROWS = 16


def _diff_kernel(lq1_ref, lk1_ref, lq2_ref, lk2_ref, q_ref, k_ref, v_ref, g_ref, o_ref, vt_ref,
                 acc_ref, z0_ref, z1_ref, *, tq, lambda_init):
    i = pl.program_id(2)
    s_len = v_ref.shape[1]
    vd = v_ref.shape[2]
    half = tq // 2
    zs = (z0_ref, z1_ref)

    @pl.when(i == 0)
    def _():
        for c in range(s_len // tq):
            cols = slice(c * tq, (c + 1) * tq)
            vt_ref[0:vd, cols] = v_ref[0, cols, :].astype(F32).T.astype(BF16)
        rid = lax.broadcasted_iota(I32, (DIFF_ONES_ROWS, s_len), 0)
        vt_ref[vd:vd + DIFF_ONES_ROWS, :] = jnp.where(rid == 0, 1.0, 0.0).astype(BF16)

    q = q_ref[0]
    lane = lax.broadcasted_iota(I32, (1, LANES), 1)
    zero = jnp.zeros_like(q)
    q2 = jnp.concatenate([jnp.where(lane < DIFF_HEAD_DIM, q, zero),
                          jnp.where(lane >= DIFF_HEAD_DIM, q, zero)], axis=0)

    def scores(j, h, masked):
        start = pl.multiple_of(j * tq + h * half, half)
        kj = k_ref[0, pl.ds(start, half), :]
        zt = lax.dot_general(kj, q2, (((1,), (1,)), ((), ())), preferred_element_type=F32)
        if masked:
            qidx = lax.broadcasted_iota(I32, (half, 2 * tq), 1) % tq
            kidx = lax.broadcasted_iota(I32, (half, 2 * tq), 0) + h * half
            zt = jnp.where(kidx // CHUNK <= qidx // CHUNK, zt, -jnp.inf)
        zs[h][...] = zt
        return jnp.max(zt, axis=0, keepdims=True)

    def values(j, h, m, mblk):
        start = pl.multiple_of(j * tq + h * half, half)
        vtj = vt_ref[:, pl.ds(start, half)]
        m_new = jnp.maximum(m, mblk)
        pt = jnp.exp(zs[h][...] - m_new).astype(BF16)
        acc_ref[...] = jnp.exp(m - m_new) * acc_ref[...] + jnp.dot(vtj, pt,
                                                                   preferred_element_type=F32)
        return m_new

    def block(j, m, mb0, masked, next_kind):
        mb1 = scores(j, 1, masked)
        m = values(j, 0, m, mb0)
        nb0 = mb0 if next_kind is None else scores(j + 1, 0, next_kind == "masked")
        m = values(j, 1, m, mb1)
        return m, nb0

    acc_ref[...] = jnp.zeros_like(acc_ref)
    m0 = jnp.full((1, 2 * tq), -jnp.inf, F32)

    @pl.when(i == 0)
    def _():
        block(0, m0, scores(0, 0, True), True, None)

    @pl.when(i > 0)
    def _():
        state = (m0, scores(0, 0, False))
        state = lax.fori_loop(0, i - 1, lambda j, st: block(j, st[0], st[1], False, "plain"),
                              state)
        state = block(i - 1, state[0], state[1], False, "masked")
        block(i, state[0], state[1], True, None)

    lam = (jnp.exp(jnp.sum(lq1_ref[...] * lk1_ref[...], axis=1, keepdims=True))
           - jnp.exp(jnp.sum(lq2_ref[...] * lk2_ref[...], axis=1, keepdims=True))
           + lambda_init)
    ot = acc_ref[0:vd, :] / acc_ref[vd:vd + 1, :]
    ot = ot[:, 0:tq] - lam * ot[:, tq:2 * tq]
    o_ref[0] = (_rms(ot.T, g_ref[...]) * (1.0 - lambda_init)).astype(BF16)


def _diff_attention(proj3, lq1, lk1, lq2, lk2, subln, *, q_col, k_col, v_col, n_heads, tq,
                    lambda_init):
    b, s, _ = proj3.shape
    lam_spec = _const_spec((1, DIFF_HEAD_DIM))
    return pl.pallas_call(
        functools.partial(_diff_kernel, tq=tq, lambda_init=lambda_init),
        grid=(b, n_heads, s // tq),
        in_specs=[lam_spec, lam_spec, lam_spec, lam_spec,
                  pl.BlockSpec((1, tq, LANES), lambda bi, h, i: (bi, i, q_col + h)),
                  pl.BlockSpec((1, s, LANES), lambda bi, h, i: (bi, 0, k_col + h)),
                  pl.BlockSpec((1, s, LANES), lambda bi, h, i: (bi, 0, v_col + h)),
                  _const_spec((1, DIFF_V_DIM))],
        out_specs=pl.BlockSpec((1, tq, LANES), lambda bi, h, i: (bi, i, h)),
        out_shape=jax.ShapeDtypeStruct((b, s, n_heads * DIFF_V_DIM), BF16),
        scratch_shapes=[pltpu.VMEM((DIFF_V_DIM + DIFF_ONES_ROWS, s), BF16),
                        pltpu.VMEM((DIFF_V_DIM + DIFF_ONES_ROWS, 2 * tq), F32),
                        pltpu.VMEM((tq // 2, 2 * tq), F32), pltpu.VMEM((tq // 2, 2 * tq), F32)],
        compiler_params=_params(("arbitrary", "arbitrary", "arbitrary")),
        name="diff_attention",
    )(lq1, lk1, lq2, lk2, proj3, proj3, proj3, subln)


def _mem_kv_kernel(mem_ref, g_ref, w_ref, o_ref):
    h = _rms(mem_ref[0], g_ref[...]).astype(BF16)
    o_ref[0] = jnp.dot(h, w_ref[...], preferred_element_type=F32).astype(BF16)


def _mem_kv(mem, g, w_bf16):
    b, m, d = mem.shape
    n = w_bf16.shape[1]
    return pl.pallas_call(
        _mem_kv_kernel,
        grid=(b,),
        in_specs=[pl.BlockSpec((1, m, d), lambda i: (i, 0, 0)), _const_spec((1, d)),
                  _const_spec((d, n))],
        out_specs=pl.BlockSpec((1, m, n), lambda i: (i, 0, 0)),
        out_shape=jax.ShapeDtypeStruct((b, m, n), BF16),
        compiler_params=_params(("arbitrary",)),
        name="mem_kv",
    )(mem, g, w_bf16)


def _split_bf16(v):
    hi = v.astype(BF16)
    return hi, (v - hi.astype(F32)).astype(BF16)


def _pack_bf16_pairs(v):
    half = v.shape[1] // 2
    bits = lax.bitcast_convert_type(v.astype(BF16).astype(F32), U32)
    return (bits[:, :half] >> 16) | (bits[:, half:] & jnp.uint32(0xFFFF0000))


def _unpack_bf16_pairs(w):
    lo = lax.bitcast_convert_type(w << 16, F32)
    hi = lax.bitcast_convert_type(w & jnp.uint32(0xFFFF0000), F32)
    return jnp.concatenate([lo, hi], axis=1).astype(BF16)


def _postmix_kernel(x_ref, osb_ref, odf_ref, gsb_ref, gdf_ref, wus_ref, wud_ref, wout_ref,
                    gq_ref, wq_ref, kv_ref, wo_ref, gf_ref, wr_ref, br_ref,
                    x2_ref, hf_ref, route_ref, gate_ref, cum_ref, count_ref, *, tm, d_model):
    step = pl.program_id(0)

    @pl.when(step == 0)
    def _():
        count_ref[...] = jnp.zeros_like(count_ref)

    y_sb = jnp.dot(osb_ref[...], wus_ref[...], preferred_element_type=F32)
    y_df = jnp.dot(odf_ref[...], wud_ref[...], preferred_element_type=F32)
    mixed = (jax.nn.sigmoid(gsb_ref[...].astype(F32)) * y_sb
             + jax.nn.sigmoid(gdf_ref[...].astype(F32)) * y_df)
    x1 = x_ref[...] + jnp.dot(mixed.astype(BF16), wout_ref[...], preferred_element_type=F32)

    hq = _rms(x1, gq_ref[...]).astype(BF16)
    hd = d_model // MEM_HEADS
    q = jnp.dot(hq, wq_ref[...], preferred_element_type=F32) * (1.0 / math.sqrt(hd))
    q = q.astype(BF16)
    heads = []
    for h in range(MEM_HEADS):
        kh = kv_ref[0, :, h * hd:(h + 1) * hd]
        vh = kv_ref[0, :, d_model + h * hd:d_model + (h + 1) * hd]
        z = lax.dot_general(q[:, h * hd:(h + 1) * hd], kh, (((1,), (1,)), ((), ())),
                            preferred_element_type=F32)
        p = jnp.exp(z - jnp.max(z, axis=1, keepdims=True))
        l = jnp.sum(p, axis=1, keepdims=True)
        heads.append((jnp.dot(p.astype(BF16), vh, preferred_element_type=F32) / l).astype(BF16))
    x2 = x1 + jnp.dot(jnp.concatenate(heads, axis=1), wo_ref[...], preferred_element_type=F32)
    x2_ref[...] = x2

    hf = _rms(x2, gf_ref[...])
    packed = _pack_bf16_pairs(hf)
    for j in range(PACKED_SUBROWS):
        hf_ref[pl.ds(j, tm, stride=ROW_TILE), :] = packed[:, j * LANES:(j + 1) * LANES]
    h_hi, h_lo = _split_bf16(hf)
    w_hi, w_lo = _split_bf16(wr_ref[...])
    logits = (jnp.dot(h_hi, w_hi, preferred_element_type=F32)
              + jnp.dot(h_hi, w_lo, preferred_element_type=F32)
              + jnp.dot(h_lo, w_hi, preferred_element_type=F32)) + br_ref[...]
    lane = lax.broadcasted_iota(I32, (tm, LANES), 1)
    work = logits
    vals, idxs, hots = [], [], []
    for _ in range(TOP_K):
        mx = jnp.max(work, axis=1, keepdims=True)
        idx = jnp.min(jnp.where(work == mx, lane, LANES), axis=1, keepdims=True)
        hot = lane == idx
        work = jnp.where(hot, NEG_BIG, work)
        vals.append(mx)
        idxs.append(idx)
        hots.append(hot)
    exps = [jnp.exp(v - vals[0]) for v in vals]
    denom = exps[0] + exps[1] + exps[2] + exps[3]

    onehot_sum = jnp.zeros((tm, LANES), F32)
    for hot in hots:
        onehot_sum = onehot_sum + jnp.where(hot, 1.0, 0.0)
    r = lax.broadcasted_iota(I32, (tm, tm), 0)
    c = lax.broadcasted_iota(I32, (tm, tm), 1)
    lower = jnp.where(c < r, 1.0, 0.0).astype(BF16)
    rank = jnp.dot(lower, onehot_sum.astype(BF16), preferred_element_type=F32) + count_ref[...]
    route = jnp.zeros((tm, LANES), I32)
    gates = jnp.zeros((tm, LANES), F32)
    for k in range(TOP_K):
        pos = jnp.sum(jnp.where(hots[k], rank, 0.0), axis=1, keepdims=True).astype(I32)
        route = jnp.where(lane == k, idxs[k], route)
        route = jnp.where(lane == TOP_K + k, pos, route)
        gates = jnp.where(lane == k, exps[k] / denom, gates)
    route_ref[...] = route
    gate_ref[...] = gates
    meta = jnp.where(lane == 0, step * tm + lax.broadcasted_iota(I32, (tm, LANES), 0), 0)
    for k in range(TOP_K):
        meta = jnp.where(lane == 1 + k, idxs[k], meta)
    hf_ref[pl.ds(META_SUBROW, tm, stride=ROW_TILE), :] = meta.astype(U32)
    for j in range(META_SUBROW + 1, ROW_TILE):
        hf_ref[pl.ds(j, tm, stride=ROW_TILE), :] = jnp.zeros((tm, LANES), U32)
    count_ref[...] = count_ref[...] + jnp.sum(onehot_sum, axis=0, keepdims=True)
    cum_ref[0] = jnp.broadcast_to(count_ref[...], (8, LANES))


def _post_mix(x2d, proj2, o_sb, o_diff, w_up_sb, w_up_diff, w_out, g_memq, w_memq, kv, w_memo,
              g_ffn, w_router_pad, b_router_pad, *, tm, seq, gate_col):
    n_tok, d = x2d.shape
    n_tiles = n_tok // tm
    tiles_per_batch = seq // tm
    row = lambda i: (i, 0)
    in_specs = [
        pl.BlockSpec((tm, d), row),
        pl.BlockSpec((tm, o_sb.shape[1]), row),
        pl.BlockSpec((tm, o_diff.shape[1]), row),
        pl.BlockSpec((tm, d), lambda i: (i, gate_col)),
        pl.BlockSpec((tm, d), lambda i: (i, gate_col + 1)),
        _const_spec(w_up_sb.shape), _const_spec(w_up_diff.shape), _const_spec(w_out.shape),
        _const_spec((1, d)), _const_spec(w_memq.shape),
        pl.BlockSpec((1,) + kv.shape[1:], lambda i: (i // tiles_per_batch, 0, 0)),
        _const_spec(w_memo.shape), _const_spec((1, d)),
        _const_spec(w_router_pad.shape), _const_spec((1, LANES)),
    ]
    out_specs = [
        pl.BlockSpec((tm, d), row),
        pl.BlockSpec((tm * ROW_TILE, LANES), row),
        pl.BlockSpec((tm, LANES), row),
        pl.BlockSpec((tm, LANES), row),
        pl.BlockSpec((1, 8, LANES), lambda i: (i, 0, 0)),
    ]
    assert d // 2 == PACKED_SUBROWS * LANES
    out_shape = [
        jax.ShapeDtypeStruct((n_tok, d), F32),
        jax.ShapeDtypeStruct((n_tok * ROW_TILE, LANES), U32),
        jax.ShapeDtypeStruct((n_tok, LANES), I32),
        jax.ShapeDtypeStruct((n_tok, LANES), F32),
        jax.ShapeDtypeStruct((n_tiles, 8, LANES), F32),
    ]
    return pl.pallas_call(
        functools.partial(_postmix_kernel, tm=tm, d_model=d),
        grid=(n_tiles,),
        in_specs=in_specs,
        out_specs=out_specs,
        out_shape=out_shape,
        scratch_shapes=[pltpu.VMEM((1, LANES), F32)],
        compiler_params=_params(("arbitrary",)),
        name="post_mix",
    )(x2d, o_sb, o_diff, proj2, proj2, w_up_sb, w_up_diff, w_out, g_memq, w_memq, kv, w_memo,
      g_ffn, w_router_pad, b_router_pad)


def _dispatch_kernel(zstart_ref, zflag_ref, nused_ref, dest_ref, hf_ref, xs_hbm, stage0, stage1,
                     zbuf, sem, zsem, *, tm, bm):
    s = pl.program_id(0)

    @pl.when(s == 0)
    def _():
        lane = lax.broadcasted_iota(I32, (bm, LANES), 1)
        zbuf[...] = jnp.zeros_like(zbuf)
        zbuf[pl.ds(META_SUBROW, bm, stride=ROW_TILE), :] = jnp.where(
            jnp.logical_and(lane >= 1, lane <= TOP_K), -1, 0).astype(U32)

        def fill_copy(start):
            start = pl.multiple_of(start * ROW_TILE, bm * ROW_TILE)
            return pltpu.make_async_copy(zbuf, xs_hbm.at[pl.ds(start, bm * ROW_TILE), :], zsem)

        n_blocks = xs_hbm.shape[0] // (bm * ROW_TILE)
        for action in ("start", "wait"):
            for e in range(N_EXPERTS):
                @pl.when(zflag_ref[e] != 0)
                def _(e=e, action=action):
                    getattr(fill_copy(zstart_ref[e]), action)()

            def trailing(blk, _, action=action):
                getattr(fill_copy(blk * bm), action)()
                return 0

            lax.fori_loop(nused_ref[0], n_blocks, trailing, 0)

    def tile_done(stage, parity):
        for _ in range(TOP_K):
            pltpu.make_async_copy(stage, xs_hbm.at[pl.ds(0, tm * ROW_TILE), :],
                                  sem.at[parity]).wait()

    def run(stage, other, parity):
        stage[...] = hf_ref[...]

        def issue(t, _):
            src = stage.at[pl.ds(pl.multiple_of(t * ROW_TILE, ROW_TILE), ROW_TILE), :]
            for k in range(TOP_K):
                row = pl.multiple_of(dest_ref[t * TOP_K + k] * ROW_TILE, ROW_TILE)
                pltpu.make_async_copy(src, xs_hbm.at[pl.ds(row, ROW_TILE), :],
                                      sem.at[parity]).start(priority=k % 2)
            return 0

        lax.fori_loop(0, tm, issue, 0, unroll=2)

        @pl.when(s > 0)
        def _():
            tile_done(other, 1 - parity)

        @pl.when(s == pl.num_programs(0) - 1)
        def _():
            tile_done(stage, parity)

    @pl.when(s % 2 == 0)
    def _():
        run(stage0, stage1, 0)

    @pl.when(s % 2 == 1)
    def _():
        run(stage1, stage0, 1)


def _dispatch(zstart, zflag, n_used, dest_flat, hf_rows, *, tm, bm, n_rows):
    n_tok = hf_rows.shape[0] // ROW_TILE
    grid_spec = pltpu.PrefetchScalarGridSpec(
        num_scalar_prefetch=3,
        grid=(n_tok // tm,),
        in_specs=[
            pl.BlockSpec((tm * TOP_K,), lambda i, *_: (i,), memory_space=pltpu.SMEM),
            pl.BlockSpec((tm * ROW_TILE, LANES), lambda i, *_: (i, 0)),
        ],
        out_specs=pl.BlockSpec(memory_space=pl.ANY),
        scratch_shapes=[pltpu.VMEM((tm * ROW_TILE, LANES), U32),
                        pltpu.VMEM((tm * ROW_TILE, LANES), U32),
                        pltpu.VMEM((bm * ROW_TILE, LANES), U32),
                        pltpu.SemaphoreType.DMA((2,)), pltpu.SemaphoreType.DMA(())],
    )
    return pl.pallas_call(
        functools.partial(_dispatch_kernel, tm=tm, bm=bm),
        grid_spec=grid_spec,
        out_shape=jax.ShapeDtypeStruct((n_rows * ROW_TILE, LANES), U32),
        compiler_params=_params(("arbitrary",)),
        name="dispatch",
    )(zstart, zflag, n_used, dest_flat, hf_rows)


FFN_UP_CHUNKS = 8
FFN_DOWN_CHUNKS = 4


def _expert_kernel(be_ref, valid_ref, src_ref, wslot_ref, next_ref, xs_ref, bgu_ref, bd_ref,
                   wgu_hbm, wd_hbm, ys_hbm, wgu_f32, wd_f32, wgu_bf, wd_bf, ybuf0, ybuf1, slot_v,
                   slot_s, sem_w, sem_y, sem_s, sem_z, *, d_ff, bm, n_tok):
    r = pl.program_id(0)
    valid = valid_ref[r] != 0
    prev_valid = jnp.logical_and(r > 0, valid_ref[jnp.maximum(r - 1, 0)] != 0)
    fresh = jnp.logical_or(r == 0, be_ref[r] != be_ref[jnp.maximum(r - 1, 0)])
    expert = be_ref[r]
    ybufs = (ybuf0, ybuf1)

    def weight_copies(e, wslot):
        return (pltpu.make_async_copy(wgu_hbm.at[e], wgu_f32.at[wslot], sem_w.at[wslot]),
                pltpu.make_async_copy(wd_hbm.at[e], wd_f32.at[wslot], sem_w.at[wslot]))
    d = wd_bf.shape[1]
    n_slots = TOP_K * n_tok
    block_rows = bm * ROW_TILE

    def rows_done(parity):
        return pltpu.make_async_copy(ybufs[parity], ys_hbm.at[pl.ds(0, block_rows), :],
                                     sem_y.at[parity])

    def slots_copy(parity):
        return pltpu.make_async_copy(slot_v.at[0:1, :], slot_s.at[parity:parity + 1, :],
                                     sem_s.at[parity])

    def scatter_rows(prev, lo, hi):
        for i in range(lo, hi):
            dst = pl.multiple_of(slot_s[prev, i] * ROW_TILE, ROW_TILE)
            pltpu.make_async_copy(ybufs[prev].at[pl.ds(i * ROW_TILE, ROW_TILE), :],
                                  ys_hbm.at[pl.ds(dst, ROW_TILE), :],
                                  sem_y.at[prev]).start(priority=i % 2)

    @pl.when(r == 0)
    def _():
        for action in ("start", "wait"):
            for parity in range(2):
                if action == "start":
                    ybufs[parity][...] = jnp.zeros_like(ybufs[parity])
                trash = (n_slots + parity * bm) * ROW_TILE
                getattr(pltpu.make_async_copy(ybufs[parity],
                                              ys_hbm.at[pl.ds(trash, block_rows), :], sem_z),
                        action)()

    for parity in range(2):
        @pl.when(jnp.logical_and(prev_valid, r % 2 == parity))
        def _(parity=parity):
            slots_copy(1 - parity).wait()

    @pl.when(jnp.logical_and(valid, fresh))
    def _():
        wslot = wslot_ref[r]

        @pl.when(r == 0)
        def _():
            for c in weight_copies(expert, wslot):
                c.start()

        for c in weight_copies(expert, wslot):
            c.wait()
        wgu_bf[...] = wgu_f32[wslot].astype(BF16)
        wd_bf[...] = wd_f32[wslot].astype(BF16)

        @pl.when(next_ref[r] >= 0)
        def _():
            for c in weight_copies(next_ref[r], 1 - wslot):
                c.start()

    def ffn(cur, interleave):
        prev = 1 - cur
        ybuf = ybufs[cur]
        bounds = [bm * c // FFN_UP_CHUNKS for c in range(FFN_UP_CHUNKS + 1)]

        meta = xs_ref[pl.ds(META_SUBROW, bm, stride=ROW_TILE), :].astype(I32)
        row = lax.broadcasted_iota(I32, (bm, 1), 0)
        slot = n_slots + cur * bm + row
        for k in range(TOP_K):
            slot = jnp.where(meta[:, 1 + k:2 + k] == be_ref[r], k * n_tok + meta[:, 0:1], slot)
        slots = jnp.broadcast_to(slot.astype(F32), (bm, LANES)).T
        slot_v[...] = slots[0:8, :].astype(I32)
        slots_copy(cur).start()

        words = jnp.concatenate([xs_ref[pl.ds(j, bm, stride=ROW_TILE), :]
                                 for j in range(PACKED_SUBROWS)], axis=1)
        xb = _unpack_bf16_pairs(words)
        up_w = 2 * d_ff // FFN_UP_CHUNKS
        pairs = FFN_UP_CHUNKS // 2
        acts = []
        for c in range(pairs):
            halves = []
            for half in range(2):
                cols = slice(half * d_ff + c * up_w, half * d_ff + (c + 1) * up_w)
                halves.append(jnp.dot(xb, wgu_bf[:, cols], preferred_element_type=F32)
                              + bgu_ref[pl.ds(expert, 1), cols])
                if interleave:
                    step_no = 2 * c + half
                    scatter_rows(prev, bounds[step_no], bounds[step_no + 1])
            glu = jnp.minimum(halves[0], SWIGLU_LIMIT)
            lin = jnp.clip(halves[1], -SWIGLU_LIMIT, SWIGLU_LIMIT)
            acts.append((glu * jax.nn.sigmoid(SWIGLU_ALPHA * glu) * (lin + 1.0)).astype(BF16))
        act = jnp.concatenate(acts, axis=1)

        @pl.when(r >= 2)
        def _():
            rows_done(cur).wait()

        down_w = d // FFN_DOWN_CHUNKS
        for c in range(FFN_DOWN_CHUNKS):
            cols = slice(c * down_w, (c + 1) * down_w)
            y = (jnp.dot(act, wd_bf[:, cols], preferred_element_type=F32)
                 + bd_ref[pl.ds(expert, 1), cols])
            for j in range(down_w // LANES):
                sub = c * (down_w // LANES) + j
                ybuf[pl.ds(sub, bm, stride=ROW_TILE), :] = y[:, j * LANES:(j + 1) * LANES]

    for parity in range(2):
        on_parity = r % 2 == parity

        @pl.when(jnp.logical_and(on_parity, jnp.logical_and(valid, prev_valid)))
        def _(parity=parity):
            ffn(parity, True)

        @pl.when(jnp.logical_and(on_parity, jnp.logical_and(jnp.logical_not(valid), prev_valid)))
        def _(parity=parity):
            scatter_rows(1 - parity, 0, bm)

            @pl.when(r >= 2)
            def _():
                rows_done(parity).wait()

            rows_done(1 - parity).wait()

    @pl.when(r == 0)
    def _():
        ffn(0, False)


def _expert_ffn(blk_expert, blk_valid, blk_src, blk_wslot, blk_next, xs, w_gu, b_gu, w_down,
                b_down, *, bm, n_tok):
    n_grid = blk_expert.shape[0]
    n_exp, d_ff, d = w_down.shape
    assert d == ROW_TILE * LANES
    block_rows = bm * ROW_TILE
    grid_spec = pltpu.PrefetchScalarGridSpec(
        num_scalar_prefetch=5,
        grid=(n_grid,),
        in_specs=[
            pl.BlockSpec((block_rows, LANES), lambda r, be, valid, src, *_: (src[r], 0)),
            pl.BlockSpec((n_exp, 2 * d_ff), lambda r, *_: (0, 0)),
            pl.BlockSpec((n_exp, d), lambda r, *_: (0, 0)),
            pl.BlockSpec(memory_space=pl.ANY),
            pl.BlockSpec(memory_space=pl.ANY),
        ],
        out_specs=pl.BlockSpec(memory_space=pl.ANY),
        scratch_shapes=[pltpu.VMEM((2, d, 2 * d_ff), F32), pltpu.VMEM((2, d_ff, d), F32),
                        pltpu.VMEM((d, 2 * d_ff), BF16), pltpu.VMEM((d_ff, d), BF16),
                        pltpu.VMEM((block_rows, LANES), F32),
                        pltpu.VMEM((block_rows, LANES), F32), pltpu.VMEM((8, bm), I32),
                        pltpu.SMEM((2, bm), I32), pltpu.SemaphoreType.DMA((2,)),
                        pltpu.SemaphoreType.DMA((2,)), pltpu.SemaphoreType.DMA((2,)),
                        pltpu.SemaphoreType.DMA(())],
    )
    return pl.pallas_call(
        functools.partial(_expert_kernel, d_ff=d_ff, bm=bm, n_tok=n_tok),
        grid_spec=grid_spec,
        out_shape=jax.ShapeDtypeStruct(((TOP_K * n_tok + 2 * bm) * ROW_TILE, LANES), F32),
        compiler_params=_params(("arbitrary",)),
        name="expert_ffn",
    )(blk_expert, blk_valid, blk_src, blk_wslot, blk_next, xs, b_gu, b_down, w_gu, w_down)


def _combine_kernel(x_ref, gate_ref, g_ref, y0_ref, y1_ref, y2_ref, y3_ref, o_ref):
    tm, d = x_ref.shape
    gates = gate_ref[...]
    acc = x_ref[...]
    for k, y_ref in enumerate((y0_ref, y1_ref, y2_ref, y3_ref)):
        y = jnp.concatenate([y_ref[pl.ds(j, tm, stride=ROW_TILE), :]
                             for j in range(d // LANES)], axis=1)
        acc = acc + gates[:, k:k + 1] * y
    o_ref[...] = _rms(acc, g_ref[...])


def _combine(x2, gates, g_final, ys, *, tm):
    n_tok, d = x2.shape
    tiles = n_tok // tm
    slot_spec = lambda k: pl.BlockSpec((tm * ROW_TILE, LANES), lambda i: (k * tiles + i, 0))
    return pl.pallas_call(
        _combine_kernel,
        grid=(tiles,),
        in_specs=[
            pl.BlockSpec((tm, d), lambda i: (i, 0)),
            pl.BlockSpec((tm, LANES), lambda i: (i, 0)),
            pl.BlockSpec((1, d), lambda i: (0, 0)),
        ] + [slot_spec(k) for k in range(TOP_K)],
        out_specs=pl.BlockSpec((tm, d), lambda i: (i, 0)),
        out_shape=jax.ShapeDtypeStruct((n_tok, d), F32),
        compiler_params=_params(("arbitrary",)),
        name="combine",
    )(x2, gates, g_final, ys, ys, ys, ys)


def _routing_tables(route, cum, *, bm, n_grid):
    expert = route[:, 0:TOP_K]
    pos = route[:, TOP_K:2 * TOP_K]
    experts = jnp.arange(N_EXPERTS, dtype=I32)
    counts = cum[-1, 0, :N_EXPERTS].astype(I32)
    padded = (counts + bm - 1) // bm * bm
    pend = jnp.cumsum(padded)
    pstart = pend - padded
    dest = pos + jnp.sum(jnp.where(expert[:, :, None] == experts, pstart, 0), axis=2)

    blk = jnp.arange(n_grid, dtype=I32)
    blk_expert = jnp.minimum(jnp.sum(pend[None, :] <= (blk * bm)[:, None], axis=1),
                             N_EXPERTS - 1).astype(I32)
    n_used = pend[-1] // bm
    blk_valid = (blk < n_used).astype(I32)
    blk_src = jnp.minimum(blk, jnp.maximum(n_used - 1, 0)).astype(I32)
    zflag = (padded > 0).astype(I32)
    zstart = jnp.maximum(pend - bm, 0).astype(I32)
    wslot = (jnp.cumsum(zflag) - 1) % 2
    later = jnp.logical_and(experts[None, :] > experts[:, None], zflag[None, :] > 0)
    nxt = jnp.min(jnp.where(later, experts[None, :], N_EXPERTS), axis=1)
    nxt = jnp.where(nxt < N_EXPERTS, nxt, -1)
    owner = blk_expert[:, None] == experts[None, :]
    of_block = lambda v: jnp.sum(jnp.where(owner, v[None, :], 0), axis=1).astype(I32)
    return (dest.astype(I32).reshape(-1), blk_expert, blk_valid, blk_src, of_block(wslot),
            of_block(nxt), zstart, zflag, n_used.astype(I32).reshape(1))


def kernel(x, mem, positions, norm_mix, w_in, lambda_q1, lambda_k1, lambda_q2, lambda_k2,
           diff_subln, w_up_sb, w_up_diff, w_out, norm_mem_q, norm_mem_kv, w_mem_q, w_mem_kv,
           w_mem_o, norm_ffn, w_router, b_router, w_gate_up, b_gate_up, w_down, b_down,
           norm_final):
    b, s, d = x.shape
    n_tok = b * s
    depth = norm_mix.shape[0]
    sb_width = w_up_sb.shape[1]
    diff_width = w_up_diff.shape[1]
    n_in = w_in.shape[2]
    chunk = 512
    assert sb_width == chunk and diff_width == chunk and d == 2 * chunk
    sbq, sbk, sbv, dq, dk, dv = range(6)
    blocks_per_chunk = chunk // LANES
    scale = 1.0 / math.sqrt(SB_HEAD_DIM)
    assert SB_HEAD_DIM == DIFF_HEAD_DIM

    tm_in = min(512, n_tok)
    tq_sb = min(256, s)
    tq_diff = min(512, s)
    tm_post = min(512, s)
    tm_tok = min(256, n_tok)
    bm = 256
    n_grid = (n_tok * TOP_K) // bm + N_EXPERTS

    cos_t, sin_t = _rope_tables(positions)
    x2d = x.reshape(n_tok, d)
    for l in range(depth):
        lambda_init = 0.8 - 0.6 * math.exp(-0.3 * l)
        proj = _in_proj(x2d, norm_mix[l].reshape(1, d), w_in[l].astype(BF16), cos_t, sin_t,
                        tm=tm_in, chunk=chunk, rope_chunks=(dq, dk), scale_chunks=(sbq, dq),
                        scale=scale)
        proj3 = proj.reshape(b, s, n_in)
        o_sb = _sb_attention(proj3, q_col=sbq * blocks_per_chunk, k_col=sbk * blocks_per_chunk,
                             v_col=sbv * blocks_per_chunk, n_pairs=sb_width // LANES, tq=tq_sb)
        o_diff = _diff_attention(
            proj3, lambda_q1[l].reshape(1, -1), lambda_k1[l].reshape(1, -1),
            lambda_q2[l].reshape(1, -1), lambda_k2[l].reshape(1, -1),
            diff_subln[l].reshape(1, -1), q_col=dq * blocks_per_chunk,
            k_col=dk * blocks_per_chunk, v_col=dv * blocks_per_chunk,
            n_heads=diff_width // DIFF_V_DIM, tq=tq_diff, lambda_init=lambda_init)
        kv = _mem_kv(mem, norm_mem_kv[l].reshape(1, d), w_mem_kv[l].astype(BF16))
        w_router_pad = jnp.zeros((d, LANES), F32).at[:, :N_EXPERTS].set(w_router[l])
        b_router_pad = jnp.full((1, LANES), NEG_BIG, F32).at[0, :N_EXPERTS].set(b_router[l])
        x_res, hf, route, gates, cum = _post_mix(
            x2d, proj, o_sb.reshape(n_tok, sb_width), o_diff.reshape(n_tok, diff_width),
            w_up_sb[l].astype(BF16), w_up_diff[l].astype(BF16), w_out[l].astype(BF16),
            norm_mem_q[l].reshape(1, d), w_mem_q[l].astype(BF16), kv, w_mem_o[l].astype(BF16),
            norm_ffn[l].reshape(1, d), w_router_pad, b_router_pad,
            tm=tm_post, seq=s, gate_col=6 * chunk // d)
        (dest, blk_expert, blk_valid, blk_src, blk_wslot, blk_next, zstart, zflag,
         n_used) = _routing_tables(route, cum, bm=bm, n_grid=n_grid + 1)
        xs = _dispatch(zstart, zflag, n_used, dest, hf, tm=tm_tok, bm=bm, n_rows=n_grid * bm)
        ys = _expert_ffn(blk_expert, blk_valid, blk_src, blk_wslot, blk_next, xs, w_gate_up[l],
                         b_gate_up[l], w_down[l], b_down[l], bm=bm, n_tok=n_tok)
        g_next = norm_final if l == depth - 1 else None
        assert g_next is not None, "only depth 1 is wired: the final norm is fused into combine"
        x2d = _combine(x_res, gates, g_next.reshape(1, d), ys, tm=tm_tok)
    return x2d.reshape(b, s, d)
```

```python
import functools
import math

import jax
import jax.numpy as jnp
from jax import lax
from jax.experimental import pallas as pl
from jax.experimental.pallas import tpu as pltpu

F32 = jnp.float32
BF16 = jnp.bfloat16
I32 = jnp.int32
U32 = jnp.uint32

LANES = 128
VMEM_LIMIT_BYTES = 56 * 1024 * 1024
ROW_TILE = 8
PACKED_SUBROWS = 4
META_SUBROW = 4

NORM_EPS = 1e-6
ROPE_THETA = 10000.0
CHUNK = 64
SB_HEAD_DIM = 64
DIFF_HEAD_DIM = 64
DIFF_V_DIM = 128
MEM_HEADS = 4
N_EXPERTS = 32
TOP_K = 4
SWIGLU_LIMIT = 7.0
SWIGLU_ALPHA = 1.702

SB_DEAD_LOG = -105.0
NEG_BIG = -1e30


def _params(semantics):
    return pltpu.CompilerParams(dimension_semantics=semantics,
                                vmem_limit_bytes=VMEM_LIMIT_BYTES)


def _const_spec(shape):
    nd = len(shape)
    return pl.BlockSpec(shape, lambda *_: (0,) * nd)


def _rms(x, g):
    return x * lax.rsqrt(jnp.mean(x * x, axis=-1, keepdims=True) + NORM_EPS) * g


def _inproj_kernel(x_ref, g_ref, w_ref, pos_ref, inv_ref, o_ref, *, chunk, rope_chunks,
                   scale_chunks, scale):
    h = _rms(x_ref[...], g_ref[...]).astype(BF16)
    lane = lax.broadcasted_iota(I32, (1, chunk), 1)
    first_half = (lane % DIFF_HEAD_DIM) < (DIFF_HEAD_DIM // 2)
    ang = pos_ref[...] * inv_ref[...]
    cos_t = jnp.cos(ang)
    sin_t = jnp.where(first_half[:, :LANES], -jnp.sin(ang), jnp.sin(ang))
    for c in range(w_ref.shape[1] // chunk):
        cols = slice(c * chunk, (c + 1) * chunk)
        acc = jnp.dot(h, w_ref[:, cols], preferred_element_type=F32)
        if c in rope_chunks:
            cos = jnp.tile(cos_t, (1, chunk // LANES))
            sin = jnp.tile(sin_t, (1, chunk // LANES))
            partner = jnp.where(first_half,
                                pltpu.roll(acc, chunk - DIFF_HEAD_DIM // 2, 1),
                                pltpu.roll(acc, DIFF_HEAD_DIM // 2, 1))
            acc = acc * cos + partner * sin
        if c in scale_chunks:
            acc = acc * scale
        o_ref[:, cols] = acc.astype(BF16)


def _in_proj(x2d, g, w_bf16, positions, *, tm, chunk, rope_chunks, scale_chunks, scale):
    n_tok, d = x2d.shape
    half = DIFF_HEAD_DIM // 2
    inv_freq = ROPE_THETA ** (-jnp.arange(half, dtype=F32) / half)
    inv_freq = jnp.tile(inv_freq, LANES // half).reshape(1, LANES)
    pos = positions.reshape(n_tok, 1).astype(F32)
    n_in = w_bf16.shape[1]
    kern = functools.partial(_inproj_kernel, chunk=chunk, rope_chunks=rope_chunks,
                             scale_chunks=scale_chunks, scale=scale)
    return pl.pallas_call(
        kern,
        grid=(n_tok // tm,),
        in_specs=[pl.BlockSpec((tm, d), lambda i: (i, 0)),
                  _const_spec((1, d)),
                  _const_spec((d, n_in)),
                  pl.BlockSpec((tm, 1), lambda i: (i, 0)),
                  _const_spec((1, LANES))],
        out_specs=pl.BlockSpec((tm, n_in), lambda i: (i, 0)),
        out_shape=jax.ShapeDtypeStruct((n_tok, n_in), BF16),
        compiler_params=_params(("arbitrary",)),
        name="in_proj",
    )(x2d, g, w_bf16, pos, inv_freq)


def _sb_kernel(q_ref, k_ref, v_ref, o_ref, acc_ref, lw0, lw1, cat0, cat1, *, tq):
    i = pl.program_id(2)
    q = q_ref[0]
    lane = lax.broadcasted_iota(I32, (1, LANES), 1)
    zero = jnp.zeros_like(q)
    q2 = jnp.concatenate([jnp.where(lane < SB_HEAD_DIM, q, zero),
                          jnp.where(lane >= SB_HEAD_DIM, q, zero)], axis=0)
    uj = lax.broadcasted_iota(I32, (2 * tq, tq), 0) % tq
    us = lax.broadcasted_iota(I32, (2 * tq, tq), 1)
    suffix = jnp.where(uj > us, 1.0, 0.0).astype(BF16)
    lws = (lw0, lw1)
    cats = (cat0, cat1)

    def causal_mask():
        return (lax.broadcasted_iota(I32, (2 * tq, tq), 1)
                < lax.broadcasted_iota(I32, (2 * tq, tq), 0) % tq)

    def logits(j, buf, masked):
        start = pl.multiple_of(j * tq, tq)
        kj = k_ref[0, pl.ds(start, tq), :]
        z = lax.dot_general(q2, kj, (((1,), (1,)), ((), ())), preferred_element_type=F32)
        sp = jnp.maximum(z, 0.0) + jnp.log(1.0 + jnp.exp(-jnp.abs(z)))
        log_rem = -sp
        if masked:
            log_rem = jnp.where(causal_mask(), log_rem, 0.0)
        lws[buf][...] = z - sp
        hi = log_rem.astype(BF16)
        cats[buf][:, 0:tq] = hi
        cats[buf][:, tq:2 * tq] = (log_rem - hi.astype(F32)).astype(BF16)
        return log_rem[:, 0:1]

    def suffix_sums(buf, first):
        after = jnp.dot(cats[buf][...], suffix, preferred_element_type=F32)
        lws[buf][...] += after
        return after[:, 0:1] + first

    def weigh(j, buf, carry, masked):
        start = pl.multiple_of(j * tq, tq)
        vj = v_ref[0, pl.ds(start, tq), :]
        w = jnp.exp(lws[buf][...] + carry)
        if masked:
            w = jnp.where(causal_mask(), w, 0.0)
        acc_ref[...] += jnp.dot(w.astype(BF16), vj, preferred_element_type=F32)

    def earlier_blocks(first_block, carry):
        def cond(state):
            j, _, alive = state
            return jnp.logical_and(j >= 0, alive > SB_DEAD_LOG)

        def body(state):
            j, carry, _ = state
            total = suffix_sums(0, logits(j, 0, False))
            weigh(j, 0, carry, False)
            carry = carry + total
            return j - 1, carry, jnp.max(carry)

        lax.while_loop(cond, body, (first_block, carry, jnp.max(carry)))

    acc_ref[...] = jnp.zeros_like(acc_ref)
    no_carry = jnp.zeros((2 * tq, 1), F32)

    @pl.when(i == 0)
    def _():
        suffix_sums(0, logits(0, 0, True))
        weigh(0, 0, no_carry, True)

    @pl.when(i > 0)
    def _():
        first_diag = logits(i, 0, True)
        first_prev = logits(i - 1, 1, False)
        total_diag = suffix_sums(0, first_diag)
        total_prev = suffix_sums(1, first_prev)
        weigh(i, 0, no_carry, True)
        weigh(i - 1, 1, total_diag, False)
        earlier_blocks(i - 2, total_diag + total_prev)

    o_ref[0] = jnp.where(lane < SB_HEAD_DIM, acc_ref[0:tq, :], acc_ref[tq:2 * tq, :]).astype(BF16)


def _sb_attention(proj3, *, q_col, k_col, v_col, n_pairs, tq):
    b, s, _ = proj3.shape
    return pl.pallas_call(
        functools.partial(_sb_kernel, tq=tq),
        grid=(b, n_pairs, s // tq),
        in_specs=[pl.BlockSpec((1, tq, LANES), lambda bi, p, i: (bi, i, q_col + p)),
                  pl.BlockSpec((1, s, LANES), lambda bi, p, i: (bi, 0, k_col + p)),
                  pl.BlockSpec((1, s, LANES), lambda bi, p, i: (bi, 0, v_col + p))],
        out_specs=pl.BlockSpec((1, tq, LANES), lambda bi, p, i: (bi, i, p)),
        out_shape=jax.ShapeDtypeStruct((b, s, n_pairs * LANES), BF16),
        scratch_shapes=[pltpu.VMEM((2 * tq, LANES), F32),
                        pltpu.VMEM((2 * tq, tq), F32), pltpu.VMEM((2 * tq, tq), F32),
                        pltpu.VMEM((2 * tq, 2 * tq), BF16), pltpu.VMEM((2 * tq, 2 * tq), BF16)],
        compiler_params=_params(("arbitrary", "arbitrary", "arbitrary")),
        name="sb_attention",
    )(proj3, proj3, proj3)


DIFF_ONES_ROWS = 16


def _diff_kernel(lq1_ref, lk1_ref, lq2_ref, lk2_ref, q_ref, k_ref, v_ref, g_ref, o_ref, vt_ref,
                 acc_ref, z0_ref, z1_ref, *, tq, lambda_init):
    i = pl.program_id(2)
    s_len = v_ref.shape[1]
    vd = v_ref.shape[2]
    half = tq // 2
    zs = (z0_ref, z1_ref)

    @pl.when(i == 0)
    def _():
        for c in range(s_len // tq):
            cols = slice(c * tq, (c + 1) * tq)
            vt_ref[0:vd, cols] = v_ref[0, cols, :].astype(F32).T.astype(BF16)
        rid = lax.broadcasted_iota(I32, (DIFF_ONES_ROWS, s_len), 0)
        vt_ref[vd:vd + DIFF_ONES_ROWS, :] = jnp.where(rid == 0, 1.0, 0.0).astype(BF16)

    q = q_ref[0]
    lane = lax.broadcasted_iota(I32, (1, LANES), 1)
    zero = jnp.zeros_like(q)
    q2 = jnp.concatenate([jnp.where(lane < DIFF_HEAD_DIM, q, zero),
                          jnp.where(lane >= DIFF_HEAD_DIM, q, zero)], axis=0)

    def scores(j, h, masked):
        start = pl.multiple_of(j * tq + h * half, half)
        kj = k_ref[0, pl.ds(start, half), :]
        zt = lax.dot_general(kj, q2, (((1,), (1,)), ((), ())), preferred_element_type=F32)
        if masked:
            qidx = lax.broadcasted_iota(I32, (half, 2 * tq), 1) % tq
            kidx = lax.broadcasted_iota(I32, (half, 2 * tq), 0) + h * half
            zt = jnp.where(kidx // CHUNK <= qidx // CHUNK, zt, -jnp.inf)
        zs[h][...] = zt
        return jnp.max(zt, axis=0, keepdims=True)

    def values(j, h, m, mblk):
        start = pl.multiple_of(j * tq + h * half, half)
        vtj = vt_ref[:, pl.ds(start, half)]
        m_new = jnp.maximum(m, mblk)
        pt = jnp.exp(zs[h][...] - m_new).astype(BF16)
        acc_ref[...] = jnp.exp(m - m_new) * acc_ref[...] + jnp.dot(vtj, pt,
                                                                   preferred_element_type=F32)
        return m_new

    def block(j, m, mb0, masked, next_kind):
        mb1 = scores(j, 1, masked)
        m = values(j, 0, m, mb0)
        nb0 = mb0 if next_kind is None else scores(j + 1, 0, next_kind == "masked")
        m = values(j, 1, m, mb1)
        return m, nb0

    acc_ref[...] = jnp.zeros_like(acc_ref)
    m0 = jnp.full((1, 2 * tq), -jnp.inf, F32)

    @pl.when(i == 0)
    def _():
        block(0, m0, scores(0, 0, True), True, None)

    @pl.when(i > 0)
    def _():
        state = (m0, scores(0, 0, False))
        state = lax.fori_loop(0, i - 1, lambda j, st: block(j, st[0], st[1], False, "plain"),
                              state)
        state = block(i - 1, state[0], state[1], False, "masked")
        block(i, state[0], state[1], True, None)

    lam = (jnp.exp(jnp.sum(lq1_ref[...] * lk1_ref[...], axis=1, keepdims=True))
           - jnp.exp(jnp.sum(lq2_ref[...] * lk2_ref[...], axis=1, keepdims=True))
           + lambda_init)
    ot = acc_ref[0:vd, :] / acc_ref[vd:vd + 1, :]
    ot = ot[:, 0:tq] - lam * ot[:, tq:2 * tq]
    o_ref[0] = (_rms(ot.T, g_ref[...]) * (1.0 - lambda_init)).astype(BF16)


def _diff_attention(proj3, lq1, lk1, lq2, lk2, subln, *, q_col, k_col, v_col, n_heads, tq,
                    lambda_init):
    b, s, _ = proj3.shape
    lam_spec = _const_spec((1, DIFF_HEAD_DIM))
    return pl.pallas_call(
        functools.partial(_diff_kernel, tq=tq, lambda_init=lambda_init),
        grid=(b, n_heads, s // tq),
        in_specs=[lam_spec, lam_spec, lam_spec, lam_spec,
                  pl.BlockSpec((1, tq, LANES), lambda bi, h, i: (bi, i, q_col + h)),
                  pl.BlockSpec((1, s, LANES), lambda bi, h, i: (bi, 0, k_col + h)),
                  pl.BlockSpec((1, s, LANES), lambda bi, h, i: (bi, 0, v_col + h)),
                  _const_spec((1, DIFF_V_DIM))],
        out_specs=pl.BlockSpec((1, tq, LANES), lambda bi, h, i: (bi, i, h)),
        out_shape=jax.ShapeDtypeStruct((b, s, n_heads * DIFF_V_DIM), BF16),
        scratch_shapes=[pltpu.VMEM((DIFF_V_DIM + DIFF_ONES_ROWS, s), BF16),
                        pltpu.VMEM((DIFF_V_DIM + DIFF_ONES_ROWS, 2 * tq), F32),
                        pltpu.VMEM((tq // 2, 2 * tq), F32), pltpu.VMEM((tq // 2, 2 * tq), F32)],
        compiler_params=_params(("arbitrary", "arbitrary", "arbitrary")),
        name="diff_attention",
    )(lq1, lk1, lq2, lk2, proj3, proj3, proj3, subln)


def _mem_kv_kernel(mem_ref, g_ref, w_ref, o_ref):
    h = _rms(mem_ref[0], g_ref[...]).astype(BF16)
    o_ref[0] = jnp.dot(h, w_ref[...], preferred_element_type=F32).astype(BF16)


def _mem_kv(mem, g, w_bf16):
    b, m, d = mem.shape
    n = w_bf16.shape[1]
    return pl.pallas_call(
        _mem_kv_kernel,
        grid=(b,),
        in_specs=[pl.BlockSpec((1, m, d), lambda i: (i, 0, 0)), _const_spec((1, d)),
                  _const_spec((d, n))],
        out_specs=pl.BlockSpec((1, m, n), lambda i: (i, 0, 0)),
        out_shape=jax.ShapeDtypeStruct((b, m, n), BF16),
        compiler_params=_params(("arbitrary",)),
        name="mem_kv",
    )(mem, g, w_bf16)


def _split_bf16(v):
    hi = v.astype(BF16)
    return hi, (v - hi.astype(F32)).astype(BF16)


def _pack_bf16_pairs(v):
    half = v.shape[1] // 2
    bits = lax.bitcast_convert_type(v.astype(BF16).astype(F32), U32)
    return (bits[:, :half] >> 16) | (bits[:, half:] & jnp.uint32(0xFFFF0000))


def _unpack_bf16_pairs(w):
    lo = lax.bitcast_convert_type(w << 16, F32)
    hi = lax.bitcast_convert_type(w & jnp.uint32(0xFFFF0000), F32)
    return jnp.concatenate([lo, hi], axis=1).astype(BF16)


def _postmix_kernel(x_ref, osb_ref, odf_ref, gsb_ref, gdf_ref, wus_ref, wud_ref, wout_ref,
                    gq_ref, wq_ref, kv_ref, wo_ref, gf_ref, wr_ref, br_ref,
                    x2_ref, hf_ref, route_ref, gate_ref, cum_ref, count_ref, *, tm, d_model):
    step = pl.program_id(0)

    @pl.when(step == 0)
    def _():
        count_ref[...] = jnp.zeros_like(count_ref)

    y_sb = jnp.dot(osb_ref[...], wus_ref[...], preferred_element_type=F32)
    y_df = jnp.dot(odf_ref[...], wud_ref[...], preferred_element_type=F32)
    mixed = (jax.nn.sigmoid(gsb_ref[...].astype(F32)) * y_sb
             + jax.nn.sigmoid(gdf_ref[...].astype(F32)) * y_df)
    x1 = x_ref[...] + jnp.dot(mixed.astype(BF16), wout_ref[...], preferred_element_type=F32)

    hq = _rms(x1, gq_ref[...]).astype(BF16)
    hd = d_model // MEM_HEADS
    q = jnp.dot(hq, wq_ref[...], preferred_element_type=F32) * (1.0 / math.sqrt(hd))
    q = q.astype(BF16)
    heads = []
    for h in range(MEM_HEADS):
        kh = kv_ref[0, :, h * hd:(h + 1) * hd]
        vh = kv_ref[0, :, d_model + h * hd:d_model + (h + 1) * hd]
        z = lax.dot_general(q[:, h * hd:(h + 1) * hd], kh, (((1,), (1,)), ((), ())),
                            preferred_element_type=F32)
        p = jnp.exp(z - jnp.max(z, axis=1, keepdims=True))
        l = jnp.sum(p, axis=1, keepdims=True)
        heads.append((jnp.dot(p.astype(BF16), vh, preferred_element_type=F32) / l).astype(BF16))
    x2 = x1 + jnp.dot(jnp.concatenate(heads, axis=1), wo_ref[...], preferred_element_type=F32)
    x2_ref[...] = x2

    hf = _rms(x2, gf_ref[...])
    packed = _pack_bf16_pairs(hf)
    for j in range(PACKED_SUBROWS):
        hf_ref[pl.ds(j, tm, stride=ROW_TILE), :] = packed[:, j * LANES:(j + 1) * LANES]
    h_hi, h_lo = _split_bf16(hf)
    w_hi, w_lo = _split_bf16(wr_ref[...])
    logits = (jnp.dot(h_hi, w_hi, preferred_element_type=F32)
              + jnp.dot(h_hi, w_lo, preferred_element_type=F32)
              + jnp.dot(h_lo, w_hi, preferred_element_type=F32)) + br_ref[...]
    lane = lax.broadcasted_iota(I32, (tm, LANES), 1)
    work = logits
    vals, idxs, hots = [], [], []
    for _ in range(TOP_K):
        mx = jnp.max(work, axis=1, keepdims=True)
        idx = jnp.min(jnp.where(work == mx, lane, LANES), axis=1, keepdims=True)
        hot = lane == idx
        work = jnp.where(hot, NEG_BIG, work)
        vals.append(mx)
        idxs.append(idx)
        hots.append(hot)
    exps = [jnp.exp(v - vals[0]) for v in vals]
    denom = exps[0] + exps[1] + exps[2] + exps[3]

    onehot_sum = jnp.zeros((tm, LANES), F32)
    for hot in hots:
        onehot_sum = onehot_sum + jnp.where(hot, 1.0, 0.0)
    r = lax.broadcasted_iota(I32, (tm, tm), 0)
    c = lax.broadcasted_iota(I32, (tm, tm), 1)
    lower = jnp.where(c < r, 1.0, 0.0).astype(BF16)
    rank = jnp.dot(lower, onehot_sum.astype(BF16), preferred_element_type=F32) + count_ref[...]
    route = jnp.zeros((tm, LANES), I32)
    gates = jnp.zeros((tm, LANES), F32)
    for k in range(TOP_K):
        pos = jnp.sum(jnp.where(hots[k], rank, 0.0), axis=1, keepdims=True).astype(I32)
        route = jnp.where(lane == k, idxs[k], route)
        route = jnp.where(lane == TOP_K + k, pos, route)
        gates = jnp.where(lane == k, exps[k] / denom, gates)
    route_ref[...] = route
    gate_ref[...] = gates
    meta = jnp.where(lane == 0, step * tm + lax.broadcasted_iota(I32, (tm, LANES), 0), 0)
    for k in range(TOP_K):
        meta = jnp.where(lane == 1 + k, idxs[k], meta)
    hf_ref[pl.ds(META_SUBROW, tm, stride=ROW_TILE), :] = meta.astype(U32)
    for j in range(META_SUBROW + 1, ROW_TILE):
        hf_ref[pl.ds(j, tm, stride=ROW_TILE), :] = jnp.zeros((tm, LANES), U32)
    count_ref[...] = count_ref[...] + jnp.sum(onehot_sum, axis=0, keepdims=True)
    cum_ref[0] = jnp.broadcast_to(count_ref[...], (8, LANES))


def _post_mix(x2d, proj2, o_sb, o_diff, w_up_sb, w_up_diff, w_out, g_memq, w_memq, kv, w_memo,
              g_ffn, w_router_pad, b_router_pad, *, tm, seq, gate_col):
    n_tok, d = x2d.shape
    n_tiles = n_tok // tm
    tiles_per_batch = seq // tm
    row = lambda i: (i, 0)
    in_specs = [
        pl.BlockSpec((tm, d), row),
        pl.BlockSpec((tm, o_sb.shape[1]), row),
        pl.BlockSpec((tm, o_diff.shape[1]), row),
        pl.BlockSpec((tm, d), lambda i: (i, gate_col)),
        pl.BlockSpec((tm, d), lambda i: (i, gate_col + 1)),
        _const_spec(w_up_sb.shape), _const_spec(w_up_diff.shape), _const_spec(w_out.shape),
        _const_spec((1, d)), _const_spec(w_memq.shape),
        pl.BlockSpec((1,) + kv.shape[1:], lambda i: (i // tiles_per_batch, 0, 0)),
        _const_spec(w_memo.shape), _const_spec((1, d)),
        _const_spec(w_router_pad.shape), _const_spec((1, LANES)),
    ]
    out_specs = [
        pl.BlockSpec((tm, d), row),
        pl.BlockSpec((tm * ROW_TILE, LANES), row),
        pl.BlockSpec((tm, LANES), row),
        pl.BlockSpec((tm, LANES), row),
        pl.BlockSpec((1, 8, LANES), lambda i: (i, 0, 0)),
    ]
    assert d // 2 == PACKED_SUBROWS * LANES
    out_shape = [
        jax.ShapeDtypeStruct((n_tok, d), F32),
        jax.ShapeDtypeStruct((n_tok * ROW_TILE, LANES), U32),
        jax.ShapeDtypeStruct((n_tok, LANES), I32),
        jax.ShapeDtypeStruct((n_tok, LANES), F32),
        jax.ShapeDtypeStruct((n_tiles, 8, LANES), F32),
    ]
    return pl.pallas_call(
        functools.partial(_postmix_kernel, tm=tm, d_model=d),
        grid=(n_tiles,),
        in_specs=in_specs,
        out_specs=out_specs,
        out_shape=out_shape,
        scratch_shapes=[pltpu.VMEM((1, LANES), F32)],
        compiler_params=_params(("arbitrary",)),
        name="post_mix",
    )(x2d, o_sb, o_diff, proj2, proj2, w_up_sb, w_up_diff, w_out, g_memq, w_memq, kv, w_memo,
      g_ffn, w_router_pad, b_router_pad)


def _dispatch_kernel(zstart_ref, zflag_ref, nused_ref, dest_ref, hf_ref, xs_hbm, stage0, stage1,
                     zbuf, sem, zsem, *, tm, bm):
    s = pl.program_id(0)

    @pl.when(s == 0)
    def _():
        lane = lax.broadcasted_iota(I32, (bm, LANES), 1)
        zbuf[...] = jnp.zeros_like(zbuf)
        zbuf[pl.ds(META_SUBROW, bm, stride=ROW_TILE), :] = jnp.where(
            jnp.logical_and(lane >= 1, lane <= TOP_K), -1, 0).astype(U32)

        def fill_copy(start):
            start = pl.multiple_of(start * ROW_TILE, bm * ROW_TILE)
            return pltpu.make_async_copy(zbuf, xs_hbm.at[pl.ds(start, bm * ROW_TILE), :], zsem)

        n_blocks = xs_hbm.shape[0] // (bm * ROW_TILE)
        for action in ("start", "wait"):
            for e in range(N_EXPERTS):
                @pl.when(zflag_ref[e] != 0)
                def _(e=e, action=action):
                    getattr(fill_copy(zstart_ref[e]), action)()

            def trailing(blk, _, action=action):
                getattr(fill_copy(blk * bm), action)()
                return 0

            lax.fori_loop(nused_ref[0], n_blocks, trailing, 0)

    def tile_done(stage, parity):
        for _ in range(TOP_K):
            pltpu.make_async_copy(stage, xs_hbm.at[pl.ds(0, tm * ROW_TILE), :],
                                  sem.at[parity]).wait()

    def run(stage, other, parity):
        stage[...] = hf_ref[...]

        def issue(t, _):
            src = stage.at[pl.ds(pl.multiple_of(t * ROW_TILE, ROW_TILE), ROW_TILE), :]
            for k in range(TOP_K):
                row = pl.multiple_of(dest_ref[t * TOP_K + k] * ROW_TILE, ROW_TILE)
                pltpu.make_async_copy(src, xs_hbm.at[pl.ds(row, ROW_TILE), :],
                                      sem.at[parity]).start(priority=k % 2)
            return 0

        lax.fori_loop(0, tm, issue, 0, unroll=2)

        @pl.when(s > 0)
        def _():
            tile_done(other, 1 - parity)

        @pl.when(s == pl.num_programs(0) - 1)
        def _():
            tile_done(stage, parity)

    @pl.when(s % 2 == 0)
    def _():
        run(stage0, stage1, 0)

    @pl.when(s % 2 == 1)
    def _():
        run(stage1, stage0, 1)


def _dispatch(zstart, zflag, n_used, dest_flat, hf_rows, *, tm, bm, n_rows):
    n_tok = hf_rows.shape[0] // ROW_TILE
    grid_spec = pltpu.PrefetchScalarGridSpec(
        num_scalar_prefetch=3,
        grid=(n_tok // tm,),
        in_specs=[
            pl.BlockSpec((tm * TOP_K,), lambda i, *_: (i,), memory_space=pltpu.SMEM),
            pl.BlockSpec((tm * ROW_TILE, LANES), lambda i, *_: (i, 0)),
        ],
        out_specs=pl.BlockSpec(memory_space=pl.ANY),
        scratch_shapes=[pltpu.VMEM((tm * ROW_TILE, LANES), U32),
                        pltpu.VMEM((tm * ROW_TILE, LANES), U32),
                        pltpu.VMEM((bm * ROW_TILE, LANES), U32),
                        pltpu.SemaphoreType.DMA((2,)), pltpu.SemaphoreType.DMA(())],
    )
    return pl.pallas_call(
        functools.partial(_dispatch_kernel, tm=tm, bm=bm),
        grid_spec=grid_spec,
        out_shape=jax.ShapeDtypeStruct((n_rows * ROW_TILE, LANES), U32),
        compiler_params=_params(("arbitrary",)),
        name="dispatch",
    )(zstart, zflag, n_used, dest_flat, hf_rows)


FFN_UP_CHUNKS = 8
FFN_DOWN_CHUNKS = 4


def _expert_kernel(be_ref, valid_ref, src_ref, wslot_ref, next_ref, xs_ref, bgu_ref, bd_ref,
                   wgu_hbm, wd_hbm, ys_hbm, wgu_f32, wd_f32, wgu_bf, wd_bf, ybuf0, ybuf1, slot_v,
                   slot_s, sem_w, sem_y, sem_s, sem_z, *, d_ff, bm, n_tok):
    r = pl.program_id(0)
    valid = valid_ref[r] != 0
    prev_valid = jnp.logical_and(r > 0, valid_ref[jnp.maximum(r - 1, 0)] != 0)
    fresh = jnp.logical_or(r == 0, be_ref[r] != be_ref[jnp.maximum(r - 1, 0)])
    expert = be_ref[r]
    ybufs = (ybuf0, ybuf1)

    def weight_copies(e, wslot):
        return (pltpu.make_async_copy(wgu_hbm.at[e], wgu_f32.at[wslot], sem_w.at[wslot]),
                pltpu.make_async_copy(wd_hbm.at[e], wd_f32.at[wslot], sem_w.at[wslot]))
    d = wd_bf.shape[1]
    n_slots = TOP_K * n_tok
    block_rows = bm * ROW_TILE

    def rows_done(parity):
        return pltpu.make_async_copy(ybufs[parity], ys_hbm.at[pl.ds(0, block_rows), :],
                                     sem_y.at[parity])

    def slots_copy(parity):
        return pltpu.make_async_copy(slot_v.at[0:1, :], slot_s.at[parity:parity + 1, :],
                                     sem_s.at[parity])

    def scatter_rows(prev, lo, hi):
        for i in range(lo, hi):
            dst = pl.multiple_of(slot_s[prev, i] * ROW_TILE, ROW_TILE)
            pltpu.make_async_copy(ybufs[prev].at[pl.ds(i * ROW_TILE, ROW_TILE), :],
                                  ys_hbm.at[pl.ds(dst, ROW_TILE), :],
                                  sem_y.at[prev]).start(priority=i % 2)

    @pl.when(r == 0)
    def _():
        for action in ("start", "wait"):
            for parity in range(2):
                if action == "start":
                    ybufs[parity][...] = jnp.zeros_like(ybufs[parity])
                trash = (n_slots + parity * bm) * ROW_TILE
                getattr(pltpu.make_async_copy(ybufs[parity],
                                              ys_hbm.at[pl.ds(trash, block_rows), :], sem_z),
                        action)()

    for parity in range(2):
        @pl.when(jnp.logical_and(prev_valid, r % 2 == parity))
        def _(parity=parity):
            slots_copy(1 - parity).wait()

    @pl.when(jnp.logical_and(valid, fresh))
    def _():
        wslot = wslot_ref[r]

        @pl.when(r == 0)
        def _():
            for c in weight_copies(expert, wslot):
                c.start()

        for c in weight_copies(expert, wslot):
            c.wait()
        wgu_bf[...] = wgu_f32[wslot].astype(BF16)
        wd_bf[...] = wd_f32[wslot].astype(BF16)

        @pl.when(next_ref[r] >= 0)
        def _():
            for c in weight_copies(next_ref[r], 1 - wslot):
                c.start()

    def ffn(cur, interleave):
        prev = 1 - cur
        ybuf = ybufs[cur]
        bounds = [bm * c // FFN_UP_CHUNKS for c in range(FFN_UP_CHUNKS + 1)]

        meta = xs_ref[pl.ds(META_SUBROW, bm, stride=ROW_TILE), :].astype(I32)
        row = lax.broadcasted_iota(I32, (bm, 1), 0)
        slot = n_slots + cur * bm + row
        for k in range(TOP_K):
            slot = jnp.where(meta[:, 1 + k:2 + k] == be_ref[r], k * n_tok + meta[:, 0:1], slot)
        slots = jnp.broadcast_to(slot.astype(F32), (bm, LANES)).T
        slot_v[...] = slots[0:8, :].astype(I32)
        slots_copy(cur).start()

        words = jnp.concatenate([xs_ref[pl.ds(j, bm, stride=ROW_TILE), :]
                                 for j in range(PACKED_SUBROWS)], axis=1)
        xb = _unpack_bf16_pairs(words)
        up_w = 2 * d_ff // FFN_UP_CHUNKS
        pairs = FFN_UP_CHUNKS // 2
        acts = []
        for c in range(pairs):
            halves = []
            for half in range(2):
                cols = slice(half * d_ff + c * up_w, half * d_ff + (c + 1) * up_w)
                halves.append(jnp.dot(xb, wgu_bf[:, cols], preferred_element_type=F32)
                              + bgu_ref[pl.ds(expert, 1), cols])
                if interleave:
                    step_no = 2 * c + half
                    scatter_rows(prev, bounds[step_no], bounds[step_no + 1])
            glu = jnp.minimum(halves[0], SWIGLU_LIMIT)
            lin = jnp.clip(halves[1], -SWIGLU_LIMIT, SWIGLU_LIMIT)
            acts.append((glu * jax.nn.sigmoid(SWIGLU_ALPHA * glu) * (lin + 1.0)).astype(BF16))
        act = jnp.concatenate(acts, axis=1)

        @pl.when(r >= 2)
        def _():
            rows_done(cur).wait()

        down_w = d // FFN_DOWN_CHUNKS
        for c in range(FFN_DOWN_CHUNKS):
            cols = slice(c * down_w, (c + 1) * down_w)
            y = (jnp.dot(act, wd_bf[:, cols], preferred_element_type=F32)
                 + bd_ref[pl.ds(expert, 1), cols])
            for j in range(down_w // LANES):
                sub = c * (down_w // LANES) + j
                ybuf[pl.ds(sub, bm, stride=ROW_TILE), :] = y[:, j * LANES:(j + 1) * LANES]

    for parity in range(2):
        on_parity = r % 2 == parity

        @pl.when(jnp.logical_and(on_parity, jnp.logical_and(valid, prev_valid)))
        def _(parity=parity):
            ffn(parity, True)

        @pl.when(jnp.logical_and(on_parity, jnp.logical_and(jnp.logical_not(valid), prev_valid)))
        def _(parity=parity):
            scatter_rows(1 - parity, 0, bm)

            @pl.when(r >= 2)
            def _():
                rows_done(parity).wait()

            rows_done(1 - parity).wait()

    @pl.when(r == 0)
    def _():
        ffn(0, False)


def _expert_ffn(blk_expert, blk_valid, blk_src, blk_wslot, blk_next, xs, w_gu, b_gu, w_down,
                b_down, *, bm, n_tok):
    n_grid = blk_expert.shape[0]
    n_exp, d_ff, d = w_down.shape
    assert d == ROW_TILE * LANES
    block_rows = bm * ROW_TILE
    grid_spec = pltpu.PrefetchScalarGridSpec(
        num_scalar_prefetch=5,
        grid=(n_grid,),
        in_specs=[
            pl.BlockSpec((block_rows, LANES), lambda r, be, valid, src, *_: (src[r], 0)),
            pl.BlockSpec((n_exp, 2 * d_ff), lambda r, *_: (0, 0)),
            pl.BlockSpec((n_exp, d), lambda r, *_: (0, 0)),
            pl.BlockSpec(memory_space=pl.ANY),
            pl.BlockSpec(memory_space=pl.ANY),
        ],
        out_specs=pl.BlockSpec(memory_space=pl.ANY),
        scratch_shapes=[pltpu.VMEM((2, d, 2 * d_ff), F32), pltpu.VMEM((2, d_ff, d), F32),
                        pltpu.VMEM((d, 2 * d_ff), BF16), pltpu.VMEM((d_ff, d), BF16),
                        pltpu.VMEM((block_rows, LANES), F32),
                        pltpu.VMEM((block_rows, LANES), F32), pltpu.VMEM((8, bm), I32),
                        pltpu.SMEM((2, bm), I32), pltpu.SemaphoreType.DMA((2,)),
                        pltpu.SemaphoreType.DMA((2,)), pltpu.SemaphoreType.DMA((2,)),
                        pltpu.SemaphoreType.DMA(())],
    )
    return pl.pallas_call(
        functools.partial(_expert_kernel, d_ff=d_ff, bm=bm, n_tok=n_tok),
        grid_spec=grid_spec,
        out_shape=jax.ShapeDtypeStruct(((TOP_K * n_tok + 2 * bm) * ROW_TILE, LANES), F32),
        compiler_params=_params(("arbitrary",)),
        name="expert_ffn",
    )(blk_expert, blk_valid, blk_src, blk_wslot, blk_next, xs, b_gu, b_down, w_gu, w_down)


def _combine_kernel(x_ref, gate_ref, g_ref, y0_ref, y1_ref, y2_ref, y3_ref, o_ref):
    tm, d = x_ref.shape
    gates = gate_ref[...]
    acc = x_ref[...]
    for k, y_ref in enumerate((y0_ref, y1_ref, y2_ref, y3_ref)):
        y = jnp.concatenate([y_ref[pl.ds(j, tm, stride=ROW_TILE), :]
                             for j in range(d // LANES)], axis=1)
        acc = acc + gates[:, k:k + 1] * y
    o_ref[...] = _rms(acc, g_ref[...])


def _combine(x2, gates, g_final, ys, *, tm):
    n_tok, d = x2.shape
    tiles = n_tok // tm
    slot_spec = lambda k: pl.BlockSpec((tm * ROW_TILE, LANES), lambda i: (k * tiles + i, 0))
    return pl.pallas_call(
        _combine_kernel,
        grid=(tiles,),
        in_specs=[
            pl.BlockSpec((tm, d), lambda i: (i, 0)),
            pl.BlockSpec((tm, LANES), lambda i: (i, 0)),
            pl.BlockSpec((1, d), lambda i: (0, 0)),
        ] + [slot_spec(k) for k in range(TOP_K)],
        out_specs=pl.BlockSpec((tm, d), lambda i: (i, 0)),
        out_shape=jax.ShapeDtypeStruct((n_tok, d), F32),
        compiler_params=_params(("arbitrary",)),
        name="combine",
    )(x2, gates, g_final, ys, ys, ys, ys)


def _routing_tables(route, cum, *, bm, n_grid):
    expert = route[:, 0:TOP_K]
    pos = route[:, TOP_K:2 * TOP_K]
    experts = jnp.arange(N_EXPERTS, dtype=I32)
    counts = cum[-1, 0, :N_EXPERTS].astype(I32)
    padded = (counts + bm - 1) // bm * bm
    pend = jnp.cumsum(padded)
    pstart = pend - padded
    dest = pos + jnp.sum(jnp.where(expert[:, :, None] == experts, pstart, 0), axis=2)

    blk = jnp.arange(n_grid, dtype=I32)
    blk_expert = jnp.minimum(jnp.sum(pend[None, :] <= (blk * bm)[:, None], axis=1),
                             N_EXPERTS - 1).astype(I32)
    n_used = pend[-1] // bm
    blk_valid = (blk < n_used).astype(I32)
    blk_src = jnp.minimum(blk, jnp.maximum(n_used - 1, 0)).astype(I32)
    zflag = (padded > 0).astype(I32)
    zstart = jnp.maximum(pend - bm, 0).astype(I32)
    wslot = (jnp.cumsum(zflag) - 1) % 2
    later = jnp.logical_and(experts[None, :] > experts[:, None], zflag[None, :] > 0)
    nxt = jnp.min(jnp.where(later, experts[None, :], N_EXPERTS), axis=1)
    nxt = jnp.where(nxt < N_EXPERTS, nxt, -1)
    owner = blk_expert[:, None] == experts[None, :]
    of_block = lambda v: jnp.sum(jnp.where(owner, v[None, :], 0), axis=1).astype(I32)
    return (dest.astype(I32).reshape(-1), blk_expert, blk_valid, blk_src, of_block(wslot),
            of_block(nxt), zstart, zflag, n_used.astype(I32).reshape(1))


def kernel(x, mem, positions, norm_mix, w_in, lambda_q1, lambda_k1, lambda_q2, lambda_k2,
           diff_subln, w_up_sb, w_up_diff, w_out, norm_mem_q, norm_mem_kv, w_mem_q, w_mem_kv,
           w_mem_o, norm_ffn, w_router, b_router, w_gate_up, b_gate_up, w_down, b_down,
           norm_final):
    b, s, d = x.shape
    n_tok = b * s
    depth = norm_mix.shape[0]
    sb_width = w_up_sb.shape[1]
    diff_width = w_up_diff.shape[1]
    n_in = w_in.shape[2]
    chunk = 512
    assert sb_width == chunk and diff_width == chunk and d == 2 * chunk
    sbq, sbk, sbv, dq, dk, dv = range(6)
    blocks_per_chunk = chunk // LANES
    scale = 1.0 / math.sqrt(SB_HEAD_DIM)
    assert SB_HEAD_DIM == DIFF_HEAD_DIM

    tm_in = min(512, n_tok)
    tq_sb = min(256, s)
    tq_diff = min(512, s)
    tm_post = min(512, s)
    tm_tok = min(256, n_tok)
    bm = 256
    n_grid = (n_tok * TOP_K) // bm + N_EXPERTS

    x2d = x.reshape(n_tok, d)
    for l in range(depth):
        lambda_init = 0.8 - 0.6 * math.exp(-0.3 * l)
        proj = _in_proj(x2d, norm_mix[l].reshape(1, d), w_in[l].astype(BF16), positions,
                        tm=tm_in, chunk=chunk, rope_chunks=(dq, dk), scale_chunks=(sbq, dq),
                        scale=scale)
        proj3 = proj.reshape(b, s, n_in)
        o_sb = _sb_attention(proj3, q_col=sbq * blocks_per_chunk, k_col=sbk * blocks_per_chunk,
                             v_col=sbv * blocks_per_chunk, n_pairs=sb_width // LANES, tq=tq_sb)
        o_diff = _diff_attention(
            proj3, lambda_q1[l].reshape(1, -1), lambda_k1[l].reshape(1, -1),
            lambda_q2[l].reshape(1, -1), lambda_k2[l].reshape(1, -1),
            diff_subln[l].reshape(1, -1), q_col=dq * blocks_per_chunk,
            k_col=dk * blocks_per_chunk, v_col=dv * blocks_per_chunk,
            n_heads=diff_width // DIFF_V_DIM, tq=tq_diff, lambda_init=lambda_init)
        kv = _mem_kv(mem, norm_mem_kv[l].reshape(1, d), w_mem_kv[l].astype(BF16))
        w_router_pad = jnp.zeros((d, LANES), F32).at[:, :N_EXPERTS].set(w_router[l])
        b_router_pad = jnp.full((1, LANES), NEG_BIG, F32).at[0, :N_EXPERTS].set(b_router[l])
        x_res, hf, route, gates, cum = _post_mix(
            x2d, proj, o_sb.reshape(n_tok, sb_width), o_diff.reshape(n_tok, diff_width),
            w_up_sb[l].astype(BF16), w_up_diff[l].astype(BF16), w_out[l].astype(BF16),
            norm_mem_q[l].reshape(1, d), w_mem_q[l].astype(BF16), kv, w_mem_o[l].astype(BF16),
            norm_ffn[l].reshape(1, d), w_router_pad, b_router_pad,
            tm=tm_post, seq=s, gate_col=6 * chunk // d)
        (dest, blk_expert, blk_valid, blk_src, blk_wslot, blk_next, zstart, zflag,
         n_used) = _routing_tables(route, cum, bm=bm, n_grid=n_grid + 1)
        xs = _dispatch(zstart, zflag, n_used, dest, hf, tm=tm_tok, bm=bm, n_rows=n_grid * bm)
        ys = _expert_ffn(blk_expert, blk_valid, blk_src, blk_wslot, blk_next, xs, w_gate_up[l],
                         b_gate_up[l], w_down[l], b_down[l], bm=bm, n_tok=n_tok)
        g_next = norm_final if l == depth - 1 else None
        assert g_next is not None, "only depth 1 is wired: the final norm is fused into combine"
        x2d = _combine(x_res, gates, g_next.reshape(1, d), ys, tm=tm_tok)
    return x2d.reshape(b, s, d)
```

```python
import functools
import math

import jax
import jax.numpy as jnp
from jax import lax
from jax.experimental import pallas as pl
from jax.experimental.pallas import tpu as pltpu

F32 = jnp.float32
BF16 = jnp.bfloat16
I32 = jnp.int32
U32 = jnp.uint32

LANES = 128
VMEM_LIMIT_BYTES = 56 * 1024 * 1024
ROW_TILE = 8
PACKED_SUBROWS = 4
META_SUBROW = 4

NORM_EPS = 1e-6
ROPE_THETA = 10000.0
CHUNK = 64
SB_HEAD_DIM = 64
DIFF_HEAD_DIM = 64
DIFF_V_DIM = 128
MEM_HEADS = 4
N_EXPERTS = 32
TOP_K = 4
SWIGLU_LIMIT = 7.0
SWIGLU_ALPHA = 1.702

SB_DEAD_LOG = -105.0
NEG_BIG = -1e30


def _params(semantics):
    return pltpu.CompilerParams(dimension_semantics=semantics,
                                vmem_limit_bytes=VMEM_LIMIT_BYTES)


def _const_spec(shape):
    nd = len(shape)
    return pl.BlockSpec(shape, lambda *_: (0,) * nd)


def _rms(x, g):
    return x * lax.rsqrt(jnp.mean(x * x, axis=-1, keepdims=True) + NORM_EPS) * g


def _inproj_kernel(x_ref, g_ref, w32_ref, pos_ref, inv_ref, o_ref, w_ref, *, chunk, rope_chunks,
                   scale_chunks, scale):
    @pl.when(pl.program_id(0) == 0)
    def _():
        w_ref[...] = w32_ref[...].astype(BF16)

    tm = x_ref.shape[0]
    h = _rms(x_ref[...], g_ref[...]).astype(BF16)
    lane = lax.broadcasted_iota(I32, (1, chunk), 1)
    first_half = (lane % DIFF_HEAD_DIM) < (DIFF_HEAD_DIM // 2)
    pos = jnp.broadcast_to(pos_ref[0], (8, tm)).T[:, 0:1]
    ang = pos * inv_ref[...]
    cos_t = jnp.cos(ang)
    sin_t = jnp.where(first_half[:, :LANES], -jnp.sin(ang), jnp.sin(ang))
    for c in range(w_ref.shape[1] // chunk):
        cols = slice(c * chunk, (c + 1) * chunk)
        acc = jnp.dot(h, w_ref[:, cols], preferred_element_type=F32)
        if c in rope_chunks:
            cos = jnp.tile(cos_t, (1, chunk // LANES))
            sin = jnp.tile(sin_t, (1, chunk // LANES))
            partner = jnp.where(first_half,
                                pltpu.roll(acc, chunk - DIFF_HEAD_DIM // 2, 1),
                                pltpu.roll(acc, DIFF_HEAD_DIM // 2, 1))
            acc = acc * cos + partner * sin
        if c in scale_chunks:
            acc = acc * scale
        o_ref[:, cols] = acc.astype(BF16)


def _in_proj(x2d, g, w, positions, *, tm, chunk, rope_chunks, scale_chunks, scale):
    n_tok, d = x2d.shape
    half = DIFF_HEAD_DIM // 2
    inv_freq = ROPE_THETA ** (-jnp.arange(half, dtype=F32) / half)
    inv_freq = jnp.tile(inv_freq, LANES // half).reshape(1, LANES)
    pos = positions.reshape(n_tok // tm, 1, tm).astype(F32)
    n_in = w.shape[1]
    kern = functools.partial(_inproj_kernel, chunk=chunk, rope_chunks=rope_chunks,
                             scale_chunks=scale_chunks, scale=scale)
    return pl.pallas_call(
        kern,
        grid=(n_tok // tm,),
        in_specs=[pl.BlockSpec((tm, d), lambda i: (i, 0)),
                  _const_spec((1, d)),
                  _const_spec((d, n_in)),
                  pl.BlockSpec((1, 1, tm), lambda i: (i, 0, 0)),
                  _const_spec((1, LANES))],
        out_specs=pl.BlockSpec((tm, n_in), lambda i: (i, 0)),
        out_shape=jax.ShapeDtypeStruct((n_tok, n_in), BF16),
        scratch_shapes=[pltpu.VMEM((d, n_in), BF16)],
        compiler_params=_params(("arbitrary",)),
        name="in_proj",
    )(x2d, g, w, pos, inv_freq)


def _sb_kernel(q_ref, k_ref, v_ref, o_ref, acc_ref, lw0, lw1, cat0, cat1, *, tq):
    i = pl.program_id(2)
    q = q_ref[0]
    lane = lax.broadcasted_iota(I32, (1, LANES), 1)
    zero = jnp.zeros_like(q)
    q2 = jnp.concatenate([jnp.where(lane < SB_HEAD_DIM, q, zero),
                          jnp.where(lane >= SB_HEAD_DIM, q, zero)], axis=0)
    uj = lax.broadcasted_iota(I32, (2 * tq, tq), 0) % tq
    us = lax.broadcasted_iota(I32, (2 * tq, tq), 1)
    suffix = jnp.where(uj > us, 1.0, 0.0).astype(BF16)
    lws = (lw0, lw1)
    cats = (cat0, cat1)

    def causal_mask():
        return (lax.broadcasted_iota(I32, (2 * tq, tq), 1)
                < lax.broadcasted_iota(I32, (2 * tq, tq), 0) % tq)

    def logits(j, buf, masked):
        start = pl.multiple_of(j * tq, tq)
        kj = k_ref[0, pl.ds(start, tq), :]
        z = lax.dot_general(q2, kj, (((1,), (1,)), ((), ())), preferred_element_type=F32)
        sp = jnp.maximum(z, 0.0) + jnp.log(1.0 + jnp.exp(-jnp.abs(z)))
        log_rem = -sp
        if masked:
            log_rem = jnp.where(causal_mask(), log_rem, 0.0)
        lws[buf][...] = z - sp
        hi = log_rem.astype(BF16)
        cats[buf][:, 0:tq] = hi
        cats[buf][:, tq:2 * tq] = (log_rem - hi.astype(F32)).astype(BF16)
        return log_rem[:, 0:1]

    def suffix_sums(buf, first):
        after = jnp.dot(cats[buf][...], suffix, preferred_element_type=F32)
        lws[buf][...] += after
        return after[:, 0:1] + first

    def weigh(j, buf, carry, masked):
        start = pl.multiple_of(j * tq, tq)
        vj = v_ref[0, pl.ds(start, tq), :]
        w = jnp.exp(lws[buf][...] + carry)
        if masked:
            w = jnp.where(causal_mask(), w, 0.0)
        acc_ref[...] += jnp.dot(w.astype(BF16), vj, preferred_element_type=F32)

    def earlier_blocks(first_block, carry):
        def cond(state):
            j, _, alive = state
            return jnp.logical_and(j >= 0, alive > SB_DEAD_LOG)

        def body(state):
            j, carry, _ = state
            total = suffix_sums(0, logits(j, 0, False))
            weigh(j, 0, carry, False)
            carry = carry + total
            return j - 1, carry, jnp.max(carry)

        lax.while_loop(cond, body, (first_block, carry, jnp.max(carry)))

    acc_ref[...] = jnp.zeros_like(acc_ref)
    no_carry = jnp.zeros((2 * tq, 1), F32)

    @pl.when(i == 0)
    def _():
        suffix_sums(0, logits(0, 0, True))
        weigh(0, 0, no_carry, True)

    @pl.when(i > 0)
    def _():
        first_diag = logits(i, 0, True)
        first_prev = logits(i - 1, 1, False)
        total_diag = suffix_sums(0, first_diag)
        total_prev = suffix_sums(1, first_prev)
        weigh(i, 0, no_carry, True)
        weigh(i - 1, 1, total_diag, False)
        earlier_blocks(i - 2, total_diag + total_prev)

    o_ref[0] = jnp.where(lane < SB_HEAD_DIM, acc_ref[0:tq, :], acc_ref[tq:2 * tq, :]).astype(BF16)


def _sb_attention(proj3, *, q_col, k_col, v_col, n_pairs, tq):
    b, s, _ = proj3.shape
    return pl.pallas_call(
        functools.partial(_sb_kernel, tq=tq),
        grid=(b, n_pairs, s // tq),
        in_specs=[pl.BlockSpec((1, tq, LANES), lambda bi, p, i: (bi, i, q_col + p)),
                  pl.BlockSpec((1, s, LANES), lambda bi, p, i: (bi, 0, k_col + p)),
                  pl.BlockSpec((1, s, LANES), lambda bi, p, i: (bi, 0, v_col + p))],
        out_specs=pl.BlockSpec((1, tq, LANES), lambda bi, p, i: (bi, i, p)),
        out_shape=jax.ShapeDtypeStruct((b, s, n_pairs * LANES), BF16),
        scratch_shapes=[pltpu.VMEM((2 * tq, LANES), F32),
                        pltpu.VMEM((2 * tq, tq), F32), pltpu.VMEM((2 * tq, tq), F32),
                        pltpu.VMEM((2 * tq, 2 * tq), BF16), pltpu.VMEM((2 * tq, 2 * tq), BF16)],
        compiler_params=_params(("arbitrary", "arbitrary", "arbitrary")),
        name="sb_attention",
    )(proj3, proj3, proj3)


DIFF_ONES_ROWS = 16


def _diff_kernel(lq1_ref, lk1_ref, lq2_ref, lk2_ref, q_ref, k_ref, v_ref, g_ref, o_ref, vt_ref,
                 acc_ref, z0_ref, z1_ref, *, tq, lambda_init):
    i = pl.program_id(2)
    s_len = v_ref.shape[1]
    vd = v_ref.shape[2]
    half = tq // 2
    zs = (z0_ref, z1_ref)

    @pl.when(i == 0)
    def _():
        for c in range(s_len // tq):
            cols = slice(c * tq, (c + 1) * tq)
            vt_ref[0:vd, cols] = v_ref[0, cols, :].astype(F32).T.astype(BF16)
        rid = lax.broadcasted_iota(I32, (DIFF_ONES_ROWS, s_len), 0)
        vt_ref[vd:vd + DIFF_ONES_ROWS, :] = jnp.where(rid == 0, 1.0, 0.0).astype(BF16)

    q = q_ref[0]
    lane = lax.broadcasted_iota(I32, (1, LANES), 1)
    zero = jnp.zeros_like(q)
    q2 = jnp.concatenate([jnp.where(lane < DIFF_HEAD_DIM, q, zero),
                          jnp.where(lane >= DIFF_HEAD_DIM, q, zero)], axis=0)

    def scores(j, h, masked):
        start = pl.multiple_of(j * tq + h * half, half)
        kj = k_ref[0, pl.ds(start, half), :]
        zt = lax.dot_general(kj, q2, (((1,), (1,)), ((), ())), preferred_element_type=F32)
        if masked:
            qidx = lax.broadcasted_iota(I32, (half, 2 * tq), 1) % tq
            kidx = lax.broadcasted_iota(I32, (half, 2 * tq), 0) + h * half
            zt = jnp.where(kidx // CHUNK <= qidx // CHUNK, zt, -jnp.inf)
        zs[h][...] = zt
        return jnp.max(zt, axis=0, keepdims=True)

    def values(j, h, m, mblk):
        start = pl.multiple_of(j * tq + h * half, half)
        vtj = vt_ref[:, pl.ds(start, half)]
        m_new = jnp.maximum(m, mblk)
        pt = jnp.exp(zs[h][...] - m_new).astype(BF16)
        acc_ref[...] = jnp.exp(m - m_new) * acc_ref[...] + jnp.dot(vtj, pt,
                                                                   preferred_element_type=F32)
        return m_new

    def block(j, m, mb0, masked, next_kind):
        mb1 = scores(j, 1, masked)
        m = values(j, 0, m, mb0)
        nb0 = mb0 if next_kind is None else scores(j + 1, 0, next_kind == "masked")
        m = values(j, 1, m, mb1)
        return m, nb0

    acc_ref[...] = jnp.zeros_like(acc_ref)
    m0 = jnp.full((1, 2 * tq), -jnp.inf, F32)

    @pl.when(i == 0)
    def _():
        block(0, m0, scores(0, 0, True), True, None)

    @pl.when(i > 0)
    def _():
        state = (m0, scores(0, 0, False))
        state = lax.fori_loop(0, i - 1, lambda j, st: block(j, st[0], st[1], False, "plain"),
                              state)
        state = block(i - 1, state[0], state[1], False, "masked")
        block(i, state[0], state[1], True, None)

    lam = (jnp.exp(jnp.sum(lq1_ref[...] * lk1_ref[...], axis=1, keepdims=True))
           - jnp.exp(jnp.sum(lq2_ref[...] * lk2_ref[...], axis=1, keepdims=True))
           + lambda_init)
    ot = acc_ref[0:vd, :] / acc_ref[vd:vd + 1, :]
    ot = ot[:, 0:tq] - lam * ot[:, tq:2 * tq]
    o_ref[0] = (_rms(ot.T, g_ref[...]) * (1.0 - lambda_init)).astype(BF16)


def _diff_attention(proj3, lq1, lk1, lq2, lk2, subln, *, q_col, k_col, v_col, n_heads, tq,
                    lambda_init):
    b, s, _ = proj3.shape
    lam_spec = _const_spec((1, DIFF_HEAD_DIM))
    return pl.pallas_call(
        functools.partial(_diff_kernel, tq=tq, lambda_init=lambda_init),
        grid=(b, n_heads, s // tq),
        in_specs=[lam_spec, lam_spec, lam_spec, lam_spec,
                  pl.BlockSpec((1, tq, LANES), lambda bi, h, i: (bi, i, q_col + h)),
                  pl.BlockSpec((1, s, LANES), lambda bi, h, i: (bi, 0, k_col + h)),
                  pl.BlockSpec((1, s, LANES), lambda bi, h, i: (bi, 0, v_col + h)),
                  _const_spec((1, DIFF_V_DIM))],
        out_specs=pl.BlockSpec((1, tq, LANES), lambda bi, h, i: (bi, i, h)),
        out_shape=jax.ShapeDtypeStruct((b, s, n_heads * DIFF_V_DIM), BF16),
        scratch_shapes=[pltpu.VMEM((DIFF_V_DIM + DIFF_ONES_ROWS, s), BF16),
                        pltpu.VMEM((DIFF_V_DIM + DIFF_ONES_ROWS, 2 * tq), F32),
                        pltpu.VMEM((tq // 2, 2 * tq), F32), pltpu.VMEM((tq // 2, 2 * tq), F32)],
        compiler_params=_params(("arbitrary", "arbitrary", "arbitrary")),
        name="diff_attention",
    )(lq1, lk1, lq2, lk2, proj3, proj3, proj3, subln)


def _mem_kv_kernel(mem_ref, g_ref, w_ref, o_ref):
    h = _rms(mem_ref[0], g_ref[...]).astype(BF16)
    o_ref[0] = jnp.dot(h, w_ref[...], preferred_element_type=F32).astype(BF16)


def _mem_kv(mem, g, w_bf16):
    b, m, d = mem.shape
    n = w_bf16.shape[1]
    return pl.pallas_call(
        _mem_kv_kernel,
        grid=(b,),
        in_specs=[pl.BlockSpec((1, m, d), lambda i: (i, 0, 0)), _const_spec((1, d)),
                  _const_spec((d, n))],
        out_specs=pl.BlockSpec((1, m, n), lambda i: (i, 0, 0)),
        out_shape=jax.ShapeDtypeStruct((b, m, n), BF16),
        compiler_params=_params(("arbitrary",)),
        name="mem_kv",
    )(mem, g, w_bf16)


def _split_bf16(v):
    hi = v.astype(BF16)
    return hi, (v - hi.astype(F32)).astype(BF16)


def _pack_bf16_pairs(v):
    half = v.shape[1] // 2
    bits = lax.bitcast_convert_type(v.astype(BF16).astype(F32), U32)
    return (bits[:, :half] >> 16) | (bits[:, half:] & jnp.uint32(0xFFFF0000))


def _unpack_bf16_pairs(w):
    lo = lax.bitcast_convert_type(w << 16, F32)
    hi = lax.bitcast_convert_type(w & jnp.uint32(0xFFFF0000), F32)
    return jnp.concatenate([lo, hi], axis=1).astype(BF16)


def _postmix_kernel(x_ref, osb_ref, odf_ref, gsb_ref, gdf_ref, wus_ref, wud_ref, wout_ref,
                    gq_ref, wq_ref, kv_ref, wo_ref, gf_ref, wr_ref, br_ref,
                    x2_ref, hf_ref, route_ref, gate_ref, cum_ref, count_ref, *, tm, d_model):
    step = pl.program_id(0)

    @pl.when(step == 0)
    def _():
        count_ref[...] = jnp.zeros_like(count_ref)

    y_sb = jnp.dot(osb_ref[...], wus_ref[...], preferred_element_type=F32)
    y_df = jnp.dot(odf_ref[...], wud_ref[...], preferred_element_type=F32)
    mixed = (jax.nn.sigmoid(gsb_ref[...].astype(F32)) * y_sb
             + jax.nn.sigmoid(gdf_ref[...].astype(F32)) * y_df)
    x1 = x_ref[...] + jnp.dot(mixed.astype(BF16), wout_ref[...], preferred_element_type=F32)

    hq = _rms(x1, gq_ref[...]).astype(BF16)
    hd = d_model // MEM_HEADS
    q = jnp.dot(hq, wq_ref[...], preferred_element_type=F32) * (1.0 / math.sqrt(hd))
    q = q.astype(BF16)
    heads = []
    for h in range(MEM_HEADS):
        kh = kv_ref[0, :, h * hd:(h + 1) * hd]
        vh = kv_ref[0, :, d_model + h * hd:d_model + (h + 1) * hd]
        z = lax.dot_general(q[:, h * hd:(h + 1) * hd], kh, (((1,), (1,)), ((), ())),
                            preferred_element_type=F32)
        p = jnp.exp(z - jnp.max(z, axis=1, keepdims=True))
        l = jnp.sum(p, axis=1, keepdims=True)
        heads.append((jnp.dot(p.astype(BF16), vh, preferred_element_type=F32) / l).astype(BF16))
    x2 = x1 + jnp.dot(jnp.concatenate(heads, axis=1), wo_ref[...], preferred_element_type=F32)
    x2_ref[...] = x2

    hf = _rms(x2, gf_ref[...])
    packed = _pack_bf16_pairs(hf)
    for j in range(PACKED_SUBROWS):
        hf_ref[pl.ds(j, tm, stride=ROW_TILE), :] = packed[:, j * LANES:(j + 1) * LANES]
    h_hi, h_lo = _split_bf16(hf)
    w_hi, w_lo = _split_bf16(wr_ref[...])
    logits = (jnp.dot(h_hi, w_hi, preferred_element_type=F32)
              + jnp.dot(h_hi, w_lo, preferred_element_type=F32)
              + jnp.dot(h_lo, w_hi, preferred_element_type=F32)) + br_ref[...]
    lane = lax.broadcasted_iota(I32, (tm, LANES), 1)
    work = logits
    vals, idxs, hots = [], [], []
    for _ in range(TOP_K):
        mx = jnp.max(work, axis=1, keepdims=True)
        idx = jnp.min(jnp.where(work == mx, lane, LANES), axis=1, keepdims=True)
        hot = lane == idx
        work = jnp.where(hot, NEG_BIG, work)
        vals.append(mx)
        idxs.append(idx)
        hots.append(hot)
    exps = [jnp.exp(v - vals[0]) for v in vals]
    denom = exps[0] + exps[1] + exps[2] + exps[3]

    onehot_sum = jnp.zeros((tm, LANES), F32)
    for hot in hots:
        onehot_sum = onehot_sum + jnp.where(hot, 1.0, 0.0)
    r = lax.broadcasted_iota(I32, (tm, tm), 0)
    c = lax.broadcasted_iota(I32, (tm, tm), 1)
    lower = jnp.where(c < r, 1.0, 0.0).astype(BF16)
    rank = jnp.dot(lower, onehot_sum.astype(BF16), preferred_element_type=F32) + count_ref[...]
    route = jnp.zeros((tm, LANES), I32)
    gates = jnp.zeros((tm, LANES), F32)
    for k in range(TOP_K):
        pos = jnp.sum(jnp.where(hots[k], rank, 0.0), axis=1, keepdims=True).astype(I32)
        route = jnp.where(lane == k, idxs[k], route)
        route = jnp.where(lane == TOP_K + k, pos, route)
        gates = jnp.where(lane == k, exps[k] / denom, gates)
    route_ref[...] = route
    gate_ref[...] = gates
    meta = jnp.where(lane == 0, step * tm + lax.broadcasted_iota(I32, (tm, LANES), 0), 0)
    for k in range(TOP_K):
        meta = jnp.where(lane == 1 + k, idxs[k], meta)
    hf_ref[pl.ds(META_SUBROW, tm, stride=ROW_TILE), :] = meta.astype(U32)
    for j in range(META_SUBROW + 1, ROW_TILE):
        hf_ref[pl.ds(j, tm, stride=ROW_TILE), :] = jnp.zeros((tm, LANES), U32)
    count_ref[...] = count_ref[...] + jnp.sum(onehot_sum, axis=0, keepdims=True)
    cum_ref[0] = jnp.broadcast_to(count_ref[...], (8, LANES))


def _post_mix(x2d, proj2, o_sb, o_diff, w_up_sb, w_up_diff, w_out, g_memq, w_memq, kv, w_memo,
              g_ffn, w_router_pad, b_router_pad, *, tm, seq, gate_col):
    n_tok, d = x2d.shape
    n_tiles = n_tok // tm
    tiles_per_batch = seq // tm
    row = lambda i: (i, 0)
    in_specs = [
        pl.BlockSpec((tm, d), row),
        pl.BlockSpec((tm, o_sb.shape[1]), row),
        pl.BlockSpec((tm, o_diff.shape[1]), row),
        pl.BlockSpec((tm, d), lambda i: (i, gate_col)),
        pl.BlockSpec((tm, d), lambda i: (i, gate_col + 1)),
        _const_spec(w_up_sb.shape), _const_spec(w_up_diff.shape), _const_spec(w_out.shape),
        _const_spec((1, d)), _const_spec(w_memq.shape),
        pl.BlockSpec((1,) + kv.shape[1:], lambda i: (i // tiles_per_batch, 0, 0)),
        _const_spec(w_memo.shape), _const_spec((1, d)),
        _const_spec(w_router_pad.shape), _const_spec((1, LANES)),
    ]
    out_specs = [
        pl.BlockSpec((tm, d), row),
        pl.BlockSpec((tm * ROW_TILE, LANES), row),
        pl.BlockSpec((tm, LANES), row),
        pl.BlockSpec((tm, LANES), row),
        pl.BlockSpec((1, 8, LANES), lambda i: (i, 0, 0)),
    ]
    assert d // 2 == PACKED_SUBROWS * LANES
    out_shape = [
        jax.ShapeDtypeStruct((n_tok, d), F32),
        jax.ShapeDtypeStruct((n_tok * ROW_TILE, LANES), U32),
        jax.ShapeDtypeStruct((n_tok, LANES), I32),
        jax.ShapeDtypeStruct((n_tok, LANES), F32),
        jax.ShapeDtypeStruct((n_tiles, 8, LANES), F32),
    ]
    return pl.pallas_call(
        functools.partial(_postmix_kernel, tm=tm, d_model=d),
        grid=(n_tiles,),
        in_specs=in_specs,
        out_specs=out_specs,
        out_shape=out_shape,
        scratch_shapes=[pltpu.VMEM((1, LANES), F32)],
        compiler_params=_params(("arbitrary",)),
        name="post_mix",
    )(x2d, o_sb, o_diff, proj2, proj2, w_up_sb, w_up_diff, w_out, g_memq, w_memq, kv, w_memo,
      g_ffn, w_router_pad, b_router_pad)


def _dispatch_kernel(zstart_ref, zflag_ref, nused_ref, dest_ref, hf_ref, xs_hbm, stage0, stage1,
                     zbuf, sem, zsem, *, tm, bm):
    s = pl.program_id(0)

    @pl.when(s == 0)
    def _():
        lane = lax.broadcasted_iota(I32, (bm, LANES), 1)
        zbuf[...] = jnp.zeros_like(zbuf)
        zbuf[pl.ds(META_SUBROW, bm, stride=ROW_TILE), :] = jnp.where(
            jnp.logical_and(lane >= 1, lane <= TOP_K), -1, 0).astype(U32)

        def fill_copy(start):
            start = pl.multiple_of(start * ROW_TILE, bm * ROW_TILE)
            return pltpu.make_async_copy(zbuf, xs_hbm.at[pl.ds(start, bm * ROW_TILE), :], zsem)

        n_blocks = xs_hbm.shape[0] // (bm * ROW_TILE)
        for action in ("start", "wait"):
            for e in range(N_EXPERTS):
                @pl.when(zflag_ref[e] != 0)
                def _(e=e, action=action):
                    getattr(fill_copy(zstart_ref[e]), action)()

            def trailing(blk, _, action=action):
                getattr(fill_copy(blk * bm), action)()
                return 0

            lax.fori_loop(nused_ref[0], n_blocks, trailing, 0)

    def tile_done(stage, parity):
        for _ in range(TOP_K):
            pltpu.make_async_copy(stage, xs_hbm.at[pl.ds(0, tm * ROW_TILE), :],
                                  sem.at[parity]).wait()

    def run(stage, other, parity):
        stage[...] = hf_ref[...]

        def issue(t, _):
            src = stage.at[pl.ds(pl.multiple_of(t * ROW_TILE, ROW_TILE), ROW_TILE), :]
            for k in range(TOP_K):
                row = pl.multiple_of(dest_ref[t * TOP_K + k] * ROW_TILE, ROW_TILE)
                pltpu.make_async_copy(src, xs_hbm.at[pl.ds(row, ROW_TILE), :],
                                      sem.at[parity]).start(priority=k % 2)
            return 0

        lax.fori_loop(0, tm, issue, 0, unroll=2)

        @pl.when(s > 0)
        def _():
            tile_done(other, 1 - parity)

        @pl.when(s == pl.num_programs(0) - 1)
        def _():
            tile_done(stage, parity)

    @pl.when(s % 2 == 0)
    def _():
        run(stage0, stage1, 0)

    @pl.when(s % 2 == 1)
    def _():
        run(stage1, stage0, 1)


def _dispatch(zstart, zflag, n_used, dest_flat, hf_rows, *, tm, bm, n_rows):
    n_tok = hf_rows.shape[0] // ROW_TILE
    grid_spec = pltpu.PrefetchScalarGridSpec(
        num_scalar_prefetch=3,
        grid=(n_tok // tm,),
        in_specs=[
            pl.BlockSpec((tm * TOP_K,), lambda i, *_: (i,), memory_space=pltpu.SMEM),
            pl.BlockSpec((tm * ROW_TILE, LANES), lambda i, *_: (i, 0)),
        ],
        out_specs=pl.BlockSpec(memory_space=pl.ANY),
        scratch_shapes=[pltpu.VMEM((tm * ROW_TILE, LANES), U32),
                        pltpu.VMEM((tm * ROW_TILE, LANES), U32),
                        pltpu.VMEM((bm * ROW_TILE, LANES), U32),
                        pltpu.SemaphoreType.DMA((2,)), pltpu.SemaphoreType.DMA(())],
    )
    return pl.pallas_call(
        functools.partial(_dispatch_kernel, tm=tm, bm=bm),
        grid_spec=grid_spec,
        out_shape=jax.ShapeDtypeStruct((n_rows * ROW_TILE, LANES), U32),
        compiler_params=_params(("arbitrary",)),
        name="dispatch",
    )(zstart, zflag, n_used, dest_flat, hf_rows)


FFN_UP_CHUNKS = 8
FFN_DOWN_CHUNKS = 4


def _expert_kernel(be_ref, valid_ref, src_ref, wslot_ref, next_ref, xs_ref, bgu_ref, bd_ref,
                   wgu_hbm, wd_hbm, ys_hbm, wgu_f32, wd_f32, wgu_bf, wd_bf, ybuf0, ybuf1, slot_v,
                   slot_s, sem_w, sem_y, sem_s, sem_z, *, d_ff, bm, n_tok):
    r = pl.program_id(0)
    valid = valid_ref[r] != 0
    prev_valid = jnp.logical_and(r > 0, valid_ref[jnp.maximum(r - 1, 0)] != 0)
    fresh = jnp.logical_or(r == 0, be_ref[r] != be_ref[jnp.maximum(r - 1, 0)])
    expert = be_ref[r]
    ybufs = (ybuf0, ybuf1)

    def weight_copies(e, wslot):
        return (pltpu.make_async_copy(wgu_hbm.at[e], wgu_f32.at[wslot], sem_w.at[wslot]),
                pltpu.make_async_copy(wd_hbm.at[e], wd_f32.at[wslot], sem_w.at[wslot]))
    d = wd_bf.shape[1]
    n_slots = TOP_K * n_tok
    block_rows = bm * ROW_TILE

    def rows_done(parity):
        return pltpu.make_async_copy(ybufs[parity], ys_hbm.at[pl.ds(0, block_rows), :],
                                     sem_y.at[parity])

    def slots_copy(parity):
        return pltpu.make_async_copy(slot_v.at[0:1, :], slot_s.at[parity:parity + 1, :],
                                     sem_s.at[parity])

    def scatter_rows(prev, lo, hi):
        for i in range(lo, hi):
            dst = pl.multiple_of(slot_s[prev, i] * ROW_TILE, ROW_TILE)
            pltpu.make_async_copy(ybufs[prev].at[pl.ds(i * ROW_TILE, ROW_TILE), :],
                                  ys_hbm.at[pl.ds(dst, ROW_TILE), :],
                                  sem_y.at[prev]).start(priority=i % 2)

    @pl.when(r == 0)
    def _():
        for action in ("start", "wait"):
            for parity in range(2):
                if action == "start":
                    ybufs[parity][...] = jnp.zeros_like(ybufs[parity])
                trash = (n_slots + parity * bm) * ROW_TILE
                getattr(pltpu.make_async_copy(ybufs[parity],
                                              ys_hbm.at[pl.ds(trash, block_rows), :], sem_z),
                        action)()

    for parity in range(2):
        @pl.when(jnp.logical_and(prev_valid, r % 2 == parity))
        def _(parity=parity):
            slots_copy(1 - parity).wait()

    @pl.when(jnp.logical_and(valid, fresh))
    def _():
        wslot = wslot_ref[r]

        @pl.when(r == 0)
        def _():
            for c in weight_copies(expert, wslot):
                c.start()

        for c in weight_copies(expert, wslot):
            c.wait()
        wgu_bf[...] = wgu_f32[wslot].astype(BF16)
        wd_bf[...] = wd_f32[wslot].astype(BF16)

        @pl.when(next_ref[r] >= 0)
        def _():
            for c in weight_copies(next_ref[r], 1 - wslot):
                c.start()

    def ffn(cur, interleave):
        prev = 1 - cur
        ybuf = ybufs[cur]
        bounds = [bm * c // FFN_UP_CHUNKS for c in range(FFN_UP_CHUNKS + 1)]

        meta = xs_ref[pl.ds(META_SUBROW, bm, stride=ROW_TILE), :].astype(I32)
        row = lax.broadcasted_iota(I32, (bm, 1), 0)
        slot = n_slots + cur * bm + row
        for k in range(TOP_K):
            slot = jnp.where(meta[:, 1 + k:2 + k] == be_ref[r], k * n_tok + meta[:, 0:1], slot)
        slots = jnp.broadcast_to(slot.astype(F32), (bm, LANES)).T
        slot_v[...] = slots[0:8, :].astype(I32)
        slots_copy(cur).start()

        words = jnp.concatenate([xs_ref[pl.ds(j, bm, stride=ROW_TILE), :]
                                 for j in range(PACKED_SUBROWS)], axis=1)
        xb = _unpack_bf16_pairs(words)
        up_w = 2 * d_ff // FFN_UP_CHUNKS
        pairs = FFN_UP_CHUNKS // 2
        acts = []
        for c in range(pairs):
            halves = []
            for half in range(2):
                cols = slice(half * d_ff + c * up_w, half * d_ff + (c + 1) * up_w)
                halves.append(jnp.dot(xb, wgu_bf[:, cols], preferred_element_type=F32)
                              + bgu_ref[pl.ds(expert, 1), cols])
                if interleave:
                    step_no = 2 * c + half
                    scatter_rows(prev, bounds[step_no], bounds[step_no + 1])
            glu = jnp.minimum(halves[0], SWIGLU_LIMIT)
            lin = jnp.clip(halves[1], -SWIGLU_LIMIT, SWIGLU_LIMIT)
            acts.append((glu * jax.nn.sigmoid(SWIGLU_ALPHA * glu) * (lin + 1.0)).astype(BF16))
        act = jnp.concatenate(acts, axis=1)

        @pl.when(r >= 2)
        def _():
            rows_done(cur).wait()

        down_w = d // FFN_DOWN_CHUNKS
        for c in range(FFN_DOWN_CHUNKS):
            cols = slice(c * down_w, (c + 1) * down_w)
            y = (jnp.dot(act, wd_bf[:, cols], preferred_element_type=F32)
                 + bd_ref[pl.ds(expert, 1), cols])
            for j in range(down_w // LANES):
                sub = c * (down_w // LANES) + j
                ybuf[pl.ds(sub, bm, stride=ROW_TILE), :] = y[:, j * LANES:(j + 1) * LANES]

    for parity in range(2):
        on_parity = r % 2 == parity

        @pl.when(jnp.logical_and(on_parity, jnp.logical_and(valid, prev_valid)))
        def _(parity=parity):
            ffn(parity, True)

        @pl.when(jnp.logical_and(on_parity, jnp.logical_and(jnp.logical_not(valid), prev_valid)))
        def _(parity=parity):
            scatter_rows(1 - parity, 0, bm)

            @pl.when(r >= 2)
            def _():
                rows_done(parity).wait()

            rows_done(1 - parity).wait()

    @pl.when(r == 0)
    def _():
        ffn(0, False)


def _expert_ffn(blk_expert, blk_valid, blk_src, blk_wslot, blk_next, xs, w_gu, b_gu, w_down,
                b_down, *, bm, n_tok):
    n_grid = blk_expert.shape[0]
    n_exp, d_ff, d = w_down.shape
    assert d == ROW_TILE * LANES
    block_rows = bm * ROW_TILE
    grid_spec = pltpu.PrefetchScalarGridSpec(
        num_scalar_prefetch=5,
        grid=(n_grid,),
        in_specs=[
            pl.BlockSpec((block_rows, LANES), lambda r, be, valid, src, *_: (src[r], 0)),
            pl.BlockSpec((n_exp, 2 * d_ff), lambda r, *_: (0, 0)),
            pl.BlockSpec((n_exp, d), lambda r, *_: (0, 0)),
            pl.BlockSpec(memory_space=pl.ANY),
            pl.BlockSpec(memory_space=pl.ANY),
        ],
        out_specs=pl.BlockSpec(memory_space=pl.ANY),
        scratch_shapes=[pltpu.VMEM((2, d, 2 * d_ff), F32), pltpu.VMEM((2, d_ff, d), F32),
                        pltpu.VMEM((d, 2 * d_ff), BF16), pltpu.VMEM((d_ff, d), BF16),
                        pltpu.VMEM((block_rows, LANES), F32),
                        pltpu.VMEM((block_rows, LANES), F32), pltpu.VMEM((8, bm), I32),
                        pltpu.SMEM((2, bm), I32), pltpu.SemaphoreType.DMA((2,)),
                        pltpu.SemaphoreType.DMA((2,)), pltpu.SemaphoreType.DMA((2,)),
                        pltpu.SemaphoreType.DMA(())],
    )
    return pl.pallas_call(
        functools.partial(_expert_kernel, d_ff=d_ff, bm=bm, n_tok=n_tok),
        grid_spec=grid_spec,
        out_shape=jax.ShapeDtypeStruct(((TOP_K * n_tok + 2 * bm) * ROW_TILE, LANES), F32),
        compiler_params=_params(("arbitrary",)),
        name="expert_ffn",
    )(blk_expert, blk_valid, blk_src, blk_wslot, blk_next, xs, b_gu, b_down, w_gu, w_down)


def _combine_kernel(x_ref, gate_ref, g_ref, y0_ref, y1_ref, y2_ref, y3_ref, o_ref):
    tm, d = x_ref.shape
    gates = gate_ref[...]
    acc = x_ref[...]
    for k, y_ref in enumerate((y0_ref, y1_ref, y2_ref, y3_ref)):
        y = jnp.concatenate([y_ref[pl.ds(j, tm, stride=ROW_TILE), :]
                             for j in range(d // LANES)], axis=1)
        acc = acc + gates[:, k:k + 1] * y
    o_ref[...] = _rms(acc, g_ref[...])


def _combine(x2, gates, g_final, ys, *, tm):
    n_tok, d = x2.shape
    tiles = n_tok // tm
    slot_spec = lambda k: pl.BlockSpec((tm * ROW_TILE, LANES), lambda i: (k * tiles + i, 0))
    return pl.pallas_call(
        _combine_kernel,
        grid=(tiles,),
        in_specs=[
            pl.BlockSpec((tm, d), lambda i: (i, 0)),
            pl.BlockSpec((tm, LANES), lambda i: (i, 0)),
            pl.BlockSpec((1, d), lambda i: (0, 0)),
        ] + [slot_spec(k) for k in range(TOP_K)],
        out_specs=pl.BlockSpec((tm, d), lambda i: (i, 0)),
        out_shape=jax.ShapeDtypeStruct((n_tok, d), F32),
        compiler_params=_params(("arbitrary",)),
        name="combine",
    )(x2, gates, g_final, ys, ys, ys, ys)


def _routing_tables(route, cum, *, bm, n_grid):
    expert = route[:, 0:TOP_K]
    pos = route[:, TOP_K:2 * TOP_K]
    experts = jnp.arange(N_EXPERTS, dtype=I32)
    counts = cum[-1, 0, :N_EXPERTS].astype(I32)
    padded = (counts + bm - 1) // bm * bm
    pend = jnp.cumsum(padded)
    pstart = pend - padded
    dest = pos + jnp.sum(jnp.where(expert[:, :, None] == experts, pstart, 0), axis=2)

    blk = jnp.arange(n_grid, dtype=I32)
    blk_expert = jnp.minimum(jnp.sum(pend[None, :] <= (blk * bm)[:, None], axis=1),
                             N_EXPERTS - 1).astype(I32)
    n_used = pend[-1] // bm
    blk_valid = (blk < n_used).astype(I32)
    blk_src = jnp.minimum(blk, jnp.maximum(n_used - 1, 0)).astype(I32)
    zflag = (padded > 0).astype(I32)
    zstart = jnp.maximum(pend - bm, 0).astype(I32)
    wslot = (jnp.cumsum(zflag) - 1) % 2
    later = jnp.logical_and(experts[None, :] > experts[:, None], zflag[None, :] > 0)
    nxt = jnp.min(jnp.where(later, experts[None, :], N_EXPERTS), axis=1)
    nxt = jnp.where(nxt < N_EXPERTS, nxt, -1)
    owner = blk_expert[:, None] == experts[None, :]
    of_block = lambda v: jnp.sum(jnp.where(owner, v[None, :], 0), axis=1).astype(I32)
    return (dest.astype(I32).reshape(-1), blk_expert, blk_valid, blk_src, of_block(wslot),
            of_block(nxt), zstart, zflag, n_used.astype(I32).reshape(1))


def kernel(x, mem, positions, norm_mix, w_in, lambda_q1, lambda_k1, lambda_q2, lambda_k2,
           diff_subln, w_up_sb, w_up_diff, w_out, norm_mem_q, norm_mem_kv, w_mem_q, w_mem_kv,
           w_mem_o, norm_ffn, w_router, b_router, w_gate_up, b_gate_up, w_down, b_down,
           norm_final):
    b, s, d = x.shape
    n_tok = b * s
    depth = norm_mix.shape[0]
    sb_width = w_up_sb.shape[1]
    diff_width = w_up_diff.shape[1]
    n_in = w_in.shape[2]
    chunk = 512
    assert sb_width == chunk and diff_width == chunk and d == 2 * chunk
    sbq, sbk, sbv, dq, dk, dv = range(6)
    blocks_per_chunk = chunk // LANES
    scale = 1.0 / math.sqrt(SB_HEAD_DIM)
    assert SB_HEAD_DIM == DIFF_HEAD_DIM

    tm_in = min(512, n_tok)
    tq_sb = min(256, s)
    tq_diff = min(512, s)
    tm_post = min(512, s)
    tm_tok = min(256, n_tok)
    bm = 256
    n_grid = (n_tok * TOP_K) // bm + N_EXPERTS

    x2d = x.reshape(n_tok, d)
    for l in range(depth):
        lambda_init = 0.8 - 0.6 * math.exp(-0.3 * l)
        proj = _in_proj(x2d, norm_mix[l].reshape(1, d), w_in[l], positions,
                        tm=tm_in, chunk=chunk, rope_chunks=(dq, dk), scale_chunks=(sbq, dq),
                        scale=scale)
        proj3 = proj.reshape(b, s, n_in)
        o_sb = _sb_attention(proj3, q_col=sbq * blocks_per_chunk, k_col=sbk * blocks_per_chunk,
                             v_col=sbv * blocks_per_chunk, n_pairs=sb_width // LANES, tq=tq_sb)
        o_diff = _diff_attention(
            proj3, lambda_q1[l].reshape(1, -1), lambda_k1[l].reshape(1, -1),
            lambda_q2[l].reshape(1, -1), lambda_k2[l].reshape(1, -1),
            diff_subln[l].reshape(1, -1), q_col=dq * blocks_per_chunk,
            k_col=dk * blocks_per_chunk, v_col=dv * blocks_per_chunk,
            n_heads=diff_width // DIFF_V_DIM, tq=tq_diff, lambda_init=lambda_init)
        kv = _mem_kv(mem, norm_mem_kv[l].reshape(1, d), w_mem_kv[l].astype(BF16))
        w_router_pad = jnp.zeros((d, LANES), F32).at[:, :N_EXPERTS].set(w_router[l])
        b_router_pad = jnp.full((1, LANES), NEG_BIG, F32).at[0, :N_EXPERTS].set(b_router[l])
        x_res, hf, route, gates, cum = _post_mix(
            x2d, proj, o_sb.reshape(n_tok, sb_width), o_diff.reshape(n_tok, diff_width),
            w_up_sb[l].astype(BF16), w_up_diff[l].astype(BF16), w_out[l].astype(BF16),
            norm_mem_q[l].reshape(1, d), w_mem_q[l].astype(BF16), kv, w_mem_o[l].astype(BF16),
            norm_ffn[l].reshape(1, d), w_router_pad, b_router_pad,
            tm=tm_post, seq=s, gate_col=6 * chunk // d)
        (dest, blk_expert, blk_valid, blk_src, blk_wslot, blk_next, zstart, zflag,
         n_used) = _routing_tables(route, cum, bm=bm, n_grid=n_grid + 1)
        xs = _dispatch(zstart, zflag, n_used, dest, hf, tm=tm_tok, bm=bm, n_rows=n_grid * bm)
        ys = _expert_ffn(blk_expert, blk_valid, blk_src, blk_wslot, blk_next, xs, w_gate_up[l],
                         b_gate_up[l], w_down[l], b_down[l], bm=bm, n_tok=n_tok)
        g_next = norm_final if l == depth - 1 else None
        assert g_next is not None, "only depth 1 is wired: the final norm is fused into combine"
        x2d = _combine(x_res, gates, g_next.reshape(1, d), ys, tm=tm_tok)
    return x2d.reshape(b, s, d)
```

```python
import functools
import math

import jax
import jax.numpy as jnp
from jax import lax
from jax.experimental import pallas as pl
from jax.experimental.pallas import tpu as pltpu

F32 = jnp.float32
BF16 = jnp.bfloat16
I32 = jnp.int32
U32 = jnp.uint32

LANES = 128
VMEM_LIMIT_BYTES = 56 * 1024 * 1024
ROW_TILE = 8
PACKED_SUBROWS = 4
META_SUBROW = 4

NORM_EPS = 1e-6
ROPE_THETA = 10000.0
CHUNK = 64
SB_HEAD_DIM = 64
DIFF_HEAD_DIM = 64
DIFF_V_DIM = 128
MEM_HEADS = 4
N_EXPERTS = 32
TOP_K = 4
SWIGLU_LIMIT = 7.0
SWIGLU_ALPHA = 1.702

SB_DEAD_LOG = -105.0
NEG_BIG = -1e30


def _params(semantics):
    return pltpu.CompilerParams(dimension_semantics=semantics,
                                vmem_limit_bytes=VMEM_LIMIT_BYTES)


def _const_spec(shape):
    nd = len(shape)
    return pl.BlockSpec(shape, lambda *_: (0,) * nd)


def _rms(x, g):
    return x * lax.rsqrt(jnp.mean(x * x, axis=-1, keepdims=True) + NORM_EPS) * g


def _inproj_kernel(x_ref, g_ref, w32_ref, pos_ref, inv_ref, o_ref, w_ref, *, chunk, rope_chunks,
                   scale_chunks, scale):
    @pl.when(pl.program_id(0) == 0)
    def _():
        w_ref[...] = w32_ref[...].astype(BF16)

    tm = x_ref.shape[0]
    h = _rms(x_ref[...], g_ref[...]).astype(BF16)
    lane = lax.broadcasted_iota(I32, (1, chunk), 1)
    first_half = (lane % DIFF_HEAD_DIM) < (DIFF_HEAD_DIM // 2)
    pos = jnp.broadcast_to(pos_ref[0], (8, tm)).T[:, 0:1]
    ang = pos * inv_ref[...]
    cos_t = jnp.cos(ang)
    sin_t = jnp.where(first_half[:, :LANES], -jnp.sin(ang), jnp.sin(ang))
    for c in range(w_ref.shape[1] // chunk):
        cols = slice(c * chunk, (c + 1) * chunk)
        acc = jnp.dot(h, w_ref[:, cols], preferred_element_type=F32)
        if c in rope_chunks:
            cos = jnp.tile(cos_t, (1, chunk // LANES))
            sin = jnp.tile(sin_t, (1, chunk // LANES))
            partner = jnp.where(first_half,
                                pltpu.roll(acc, chunk - DIFF_HEAD_DIM // 2, 1),
                                pltpu.roll(acc, DIFF_HEAD_DIM // 2, 1))
            acc = acc * cos + partner * sin
        if c in scale_chunks:
            acc = acc * scale
        o_ref[:, cols] = acc.astype(BF16)


def _in_proj(x2d, g, w, positions, *, tm, chunk, rope_chunks, scale_chunks, scale):
    n_tok, d = x2d.shape
    half = DIFF_HEAD_DIM // 2
    inv_freq = ROPE_THETA ** (-jnp.arange(half, dtype=F32) / half)
    inv_freq = jnp.tile(inv_freq, LANES // half).reshape(1, LANES)
    pos = positions.reshape(n_tok // tm, 1, tm).astype(F32)
    n_in = w.shape[1]
    kern = functools.partial(_inproj_kernel, chunk=chunk, rope_chunks=rope_chunks,
                             scale_chunks=scale_chunks, scale=scale)
    return pl.pallas_call(
        kern,
        grid=(n_tok // tm,),
        in_specs=[pl.BlockSpec((tm, d), lambda i: (i, 0)),
                  _const_spec((1, d)),
                  _const_spec((d, n_in)),
                  pl.BlockSpec((1, 1, tm), lambda i: (i, 0, 0)),
                  _const_spec((1, LANES))],
        out_specs=pl.BlockSpec((tm, n_in), lambda i: (i, 0)),
        out_shape=jax.ShapeDtypeStruct((n_tok, n_in), BF16),
        scratch_shapes=[pltpu.VMEM((d, n_in), BF16)],
        compiler_params=_params(("arbitrary",)),
        name="in_proj",
    )(x2d, g, w, pos, inv_freq)


def _sb_kernel(q_ref, k_ref, v_ref, o_ref, acc_ref, lw0, lw1, cat0, cat1, *, tq):
    i = pl.program_id(2)
    q = q_ref[0]
    lane = lax.broadcasted_iota(I32, (1, LANES), 1)
    zero = jnp.zeros_like(q)
    q2 = jnp.concatenate([jnp.where(lane < SB_HEAD_DIM, q, zero),
                          jnp.where(lane >= SB_HEAD_DIM, q, zero)], axis=0)
    uj = lax.broadcasted_iota(I32, (2 * tq, tq), 0) % tq
    us = lax.broadcasted_iota(I32, (2 * tq, tq), 1)
    suffix = jnp.where(uj > us, 1.0, 0.0).astype(BF16)
    lws = (lw0, lw1)
    cats = (cat0, cat1)

    def causal_mask():
        return (lax.broadcasted_iota(I32, (2 * tq, tq), 1)
                < lax.broadcasted_iota(I32, (2 * tq, tq), 0) % tq)

    def logits(j, buf, masked):
        start = pl.multiple_of(j * tq, tq)
        kj = k_ref[0, pl.ds(start, tq), :]
        z = lax.dot_general(q2, kj, (((1,), (1,)), ((), ())), preferred_element_type=F32)
        sp = jnp.maximum(z, 0.0) + jnp.log(1.0 + jnp.exp(-jnp.abs(z)))
        log_rem = -sp
        if masked:
            log_rem = jnp.where(causal_mask(), log_rem, 0.0)
        lws[buf][...] = z - sp
        hi = log_rem.astype(BF16)
        cats[buf][:, 0:tq] = hi
        cats[buf][:, tq:2 * tq] = (log_rem - hi.astype(F32)).astype(BF16)
        return log_rem[:, 0:1]

    def suffix_sums(buf, first):
        after = jnp.dot(cats[buf][...], suffix, preferred_element_type=F32)
        lws[buf][...] += after
        return after[:, 0:1] + first

    def weigh(j, buf, carry, masked):
        start = pl.multiple_of(j * tq, tq)
        vj = v_ref[0, pl.ds(start, tq), :]
        w = jnp.exp(lws[buf][...] + carry)
        if masked:
            w = jnp.where(causal_mask(), w, 0.0)
        acc_ref[...] += jnp.dot(w.astype(BF16), vj, preferred_element_type=F32)

    def earlier_blocks(first_block, carry):
        def cond(state):
            j, _, alive = state
            return jnp.logical_and(j >= 0, alive > SB_DEAD_LOG)

        def body(state):
            j, carry, _ = state
            total = suffix_sums(0, logits(j, 0, False))
            weigh(j, 0, carry, False)
            carry = carry + total
            return j - 1, carry, jnp.max(carry)

        lax.while_loop(cond, body, (first_block, carry, jnp.max(carry)))

    acc_ref[...] = jnp.zeros_like(acc_ref)
    no_carry = jnp.zeros((2 * tq, 1), F32)

    @pl.when(i == 0)
    def _():
        suffix_sums(0, logits(0, 0, True))
        weigh(0, 0, no_carry, True)

    @pl.when(i > 0)
    def _():
        first_diag = logits(i, 0, True)
        first_prev = logits(i - 1, 1, False)
        total_diag = suffix_sums(0, first_diag)
        total_prev = suffix_sums(1, first_prev)
        weigh(i, 0, no_carry, True)
        weigh(i - 1, 1, total_diag, False)
        earlier_blocks(i - 2, total_diag + total_prev)

    o_ref[0] = jnp.where(lane < SB_HEAD_DIM, acc_ref[0:tq, :], acc_ref[tq:2 * tq, :]).astype(BF16)


def _sb_attention(proj3, *, q_col, k_col, v_col, n_pairs, tq):
    b, s, _ = proj3.shape
    return pl.pallas_call(
        functools.partial(_sb_kernel, tq=tq),
        grid=(b, n_pairs, s // tq),
        in_specs=[pl.BlockSpec((1, tq, LANES), lambda bi, p, i: (bi, i, q_col + p)),
                  pl.BlockSpec((1, s, LANES), lambda bi, p, i: (bi, 0, k_col + p)),
                  pl.BlockSpec((1, s, LANES), lambda bi, p, i: (bi, 0, v_col + p))],
        out_specs=pl.BlockSpec((1, tq, LANES), lambda bi, p, i: (bi, i, p)),
        out_shape=jax.ShapeDtypeStruct((b, s, n_pairs * LANES), BF16),
        scratch_shapes=[pltpu.VMEM((2 * tq, LANES), F32),
                        pltpu.VMEM((2 * tq, tq), F32), pltpu.VMEM((2 * tq, tq), F32),
                        pltpu.VMEM((2 * tq, 2 * tq), BF16), pltpu.VMEM((2 * tq, 2 * tq), BF16)],
        compiler_params=_params(("arbitrary", "arbitrary", "arbitrary")),
        name="sb_attention",
    )(proj3, proj3, proj3)


DIFF_ONES_ROWS = 16


def _diff_kernel(lq1_ref, lk1_ref, lq2_ref, lk2_ref, q_ref, k_ref, v_ref, g_ref, o_ref, vt_ref,
                 acc_ref, z0_ref, z1_ref, *, tq, lambda_init):
    i = pl.program_id(2)
    s_len = v_ref.shape[1]
    vd = v_ref.shape[2]
    half = tq // 2
    zs = (z0_ref, z1_ref)

    @pl.when(i == 0)
    def _():
        for c in range(s_len // tq):
            cols = slice(c * tq, (c + 1) * tq)
            vt_ref[0:vd, cols] = v_ref[0, cols, :].astype(F32).T.astype(BF16)
        rid = lax.broadcasted_iota(I32, (DIFF_ONES_ROWS, s_len), 0)
        vt_ref[vd:vd + DIFF_ONES_ROWS, :] = jnp.where(rid == 0, 1.0, 0.0).astype(BF16)

    q = q_ref[0]
    lane = lax.broadcasted_iota(I32, (1, LANES), 1)
    zero = jnp.zeros_like(q)
    q2 = jnp.concatenate([jnp.where(lane < DIFF_HEAD_DIM, q, zero),
                          jnp.where(lane >= DIFF_HEAD_DIM, q, zero)], axis=0)

    def scores(j, h, masked):
        start = pl.multiple_of(j * tq + h * half, half)
        kj = k_ref[0, pl.ds(start, half), :]
        zt = lax.dot_general(kj, q2, (((1,), (1,)), ((), ())), preferred_element_type=F32)
        if masked:
            qidx = lax.broadcasted_iota(I32, (half, 2 * tq), 1) % tq
            kidx = lax.broadcasted_iota(I32, (half, 2 * tq), 0) + h * half
            zt = jnp.where(kidx // CHUNK <= qidx // CHUNK, zt, -jnp.inf)
        zs[h][...] = zt
        return jnp.max(zt, axis=0, keepdims=True)

    def values(j, h, m, mblk):
        start = pl.multiple_of(j * tq + h * half, half)
        vtj = vt_ref[:, pl.ds(start, half)]
        m_new = jnp.maximum(m, mblk)
        pt = jnp.exp(zs[h][...] - m_new).astype(BF16)
        acc_ref[...] = jnp.exp(m - m_new) * acc_ref[...] + jnp.dot(vtj, pt,
                                                                   preferred_element_type=F32)
        return m_new

    def block(j, m, mb0, masked, next_kind):
        mb1 = scores(j, 1, masked)
        m = values(j, 0, m, mb0)
        nb0 = mb0 if next_kind is None else scores(j + 1, 0, next_kind == "masked")
        m = values(j, 1, m, mb1)
        return m, nb0

    acc_ref[...] = jnp.zeros_like(acc_ref)
    m0 = jnp.full((1, 2 * tq), -jnp.inf, F32)

    @pl.when(i == 0)
    def _():
        block(0, m0, scores(0, 0, True), True, None)

    @pl.when(i > 0)
    def _():
        state = (m0, scores(0, 0, False))
        state = lax.fori_loop(0, i - 1, lambda j, st: block(j, st[0], st[1], False, "plain"),
                              state)
        state = block(i - 1, state[0], state[1], False, "masked")
        block(i, state[0], state[1], True, None)

    lam = (jnp.exp(jnp.sum(lq1_ref[...] * lk1_ref[...], axis=1, keepdims=True))
           - jnp.exp(jnp.sum(lq2_ref[...] * lk2_ref[...], axis=1, keepdims=True))
           + lambda_init)
    ot = acc_ref[0:vd, :] / acc_ref[vd:vd + 1, :]
    ot = ot[:, 0:tq] - lam * ot[:, tq:2 * tq]
    o_ref[0] = (_rms(ot.T, g_ref[...]) * (1.0 - lambda_init)).astype(BF16)


def _diff_attention(proj3, lq1, lk1, lq2, lk2, subln, *, q_col, k_col, v_col, n_heads, tq,
                    lambda_init):
    b, s, _ = proj3.shape
    lam_spec = _const_spec((1, DIFF_HEAD_DIM))
    return pl.pallas_call(
        functools.partial(_diff_kernel, tq=tq, lambda_init=lambda_init),
        grid=(b, n_heads, s // tq),
        in_specs=[lam_spec, lam_spec, lam_spec, lam_spec,
                  pl.BlockSpec((1, tq, LANES), lambda bi, h, i: (bi, i, q_col + h)),
                  pl.BlockSpec((1, s, LANES), lambda bi, h, i: (bi, 0, k_col + h)),
                  pl.BlockSpec((1, s, LANES), lambda bi, h, i: (bi, 0, v_col + h)),
                  _const_spec((1, DIFF_V_DIM))],
        out_specs=pl.BlockSpec((1, tq, LANES), lambda bi, h, i: (bi, i, h)),
        out_shape=jax.ShapeDtypeStruct((b, s, n_heads * DIFF_V_DIM), BF16),
        scratch_shapes=[pltpu.VMEM((DIFF_V_DIM + DIFF_ONES_ROWS, s), BF16),
                        pltpu.VMEM((DIFF_V_DIM + DIFF_ONES_ROWS, 2 * tq), F32),
                        pltpu.VMEM((tq // 2, 2 * tq), F32), pltpu.VMEM((tq // 2, 2 * tq), F32)],
        compiler_params=_params(("arbitrary", "arbitrary", "arbitrary")),
        name="diff_attention",
    )(lq1, lk1, lq2, lk2, proj3, proj3, proj3, subln)


def _mem_kv_kernel(mem_ref, g_ref, w_ref, o_ref):
    h = _rms(mem_ref[0], g_ref[...]).astype(BF16)
    o_ref[0] = jnp.dot(h, w_ref[...], preferred_element_type=F32).astype(BF16)


def _mem_kv(mem, g, w_bf16):
    b, m, d = mem.shape
    n = w_bf16.shape[1]
    return pl.pallas_call(
        _mem_kv_kernel,
        grid=(b,),
        in_specs=[pl.BlockSpec((1, m, d), lambda i: (i, 0, 0)), _const_spec((1, d)),
                  _const_spec((d, n))],
        out_specs=pl.BlockSpec((1, m, n), lambda i: (i, 0, 0)),
        out_shape=jax.ShapeDtypeStruct((b, m, n), BF16),
        compiler_params=_params(("arbitrary",)),
        name="mem_kv",
    )(mem, g, w_bf16)


def _split_bf16(v):
    hi = v.astype(BF16)
    return hi, (v - hi.astype(F32)).astype(BF16)


def _pack_bf16_pairs(v):
    half = v.shape[1] // 2
    bits = lax.bitcast_convert_type(v.astype(BF16).astype(F32), U32)
    return (bits[:, :half] >> 16) | (bits[:, half:] & jnp.uint32(0xFFFF0000))


def _unpack_bf16_pairs(w):
    lo = lax.bitcast_convert_type(w << 16, F32)
    hi = lax.bitcast_convert_type(w & jnp.uint32(0xFFFF0000), F32)
    return jnp.concatenate([lo, hi], axis=1).astype(BF16)


def _postmix_kernel(x_ref, osb_ref, odf_ref, gsb_ref, gdf_ref, wus_ref, wud_ref, wout_ref,
                    gq_ref, wq_ref, kv_ref, wo_ref, gf_ref, wr_ref, br_ref,
                    x2_ref, hf_ref, route_ref, gate_ref, cum_ref, count_ref, *, tm, d_model):
    step = pl.program_id(0)

    @pl.when(step == 0)
    def _():
        count_ref[...] = jnp.zeros_like(count_ref)

    y_sb = jnp.dot(osb_ref[...], wus_ref[...], preferred_element_type=F32)
    y_df = jnp.dot(odf_ref[...], wud_ref[...], preferred_element_type=F32)
    mixed = (jax.nn.sigmoid(gsb_ref[...].astype(F32)) * y_sb
             + jax.nn.sigmoid(gdf_ref[...].astype(F32)) * y_df)
    x1 = x_ref[...] + jnp.dot(mixed.astype(BF16), wout_ref[...], preferred_element_type=F32)

    hq = _rms(x1, gq_ref[...]).astype(BF16)
    hd = d_model // MEM_HEADS
    q = jnp.dot(hq, wq_ref[...], preferred_element_type=F32) * (1.0 / math.sqrt(hd))
    q = q.astype(BF16)
    heads = []
    for h in range(MEM_HEADS):
        kh = kv_ref[0, :, h * hd:(h + 1) * hd]
        vh = kv_ref[0, :, d_model + h * hd:d_model + (h + 1) * hd]
        z = lax.dot_general(q[:, h * hd:(h + 1) * hd], kh, (((1,), (1,)), ((), ())),
                            preferred_element_type=F32)
        p = jnp.exp(z - jnp.max(z, axis=1, keepdims=True))
        l = jnp.sum(p, axis=1, keepdims=True)
        heads.append((jnp.dot(p.astype(BF16), vh, preferred_element_type=F32) / l).astype(BF16))
    x2 = x1 + jnp.dot(jnp.concatenate(heads, axis=1), wo_ref[...], preferred_element_type=F32)
    x2_ref[...] = x2

    hf = _rms(x2, gf_ref[...])
    packed = _pack_bf16_pairs(hf)
    for j in range(PACKED_SUBROWS):
        hf_ref[pl.ds(j, tm, stride=ROW_TILE), :] = packed[:, j * LANES:(j + 1) * LANES]
    h_hi, h_lo = _split_bf16(hf)
    w_hi, w_lo = _split_bf16(wr_ref[...])
    logits = (jnp.dot(h_hi, w_hi, preferred_element_type=F32)
              + jnp.dot(h_hi, w_lo, preferred_element_type=F32)
              + jnp.dot(h_lo, w_hi, preferred_element_type=F32)) + br_ref[...]
    lane = lax.broadcasted_iota(I32, (tm, LANES), 1)
    work = logits
    vals, idxs, hots = [], [], []
    for _ in range(TOP_K):
        mx = jnp.max(work, axis=1, keepdims=True)
        idx = jnp.min(jnp.where(work == mx, lane, LANES), axis=1, keepdims=True)
        hot = lane == idx
        work = jnp.where(hot, NEG_BIG, work)
        vals.append(mx)
        idxs.append(idx)
        hots.append(hot)
    exps = [jnp.exp(v - vals[0]) for v in vals]
    denom = exps[0] + exps[1] + exps[2] + exps[3]

    onehot_sum = jnp.zeros((tm, LANES), F32)
    for hot in hots:
        onehot_sum = onehot_sum + jnp.where(hot, 1.0, 0.0)
    r = lax.broadcasted_iota(I32, (tm, tm), 0)
    c = lax.broadcasted_iota(I32, (tm, tm), 1)
    lower = jnp.where(c < r, 1.0, 0.0).astype(BF16)
    rank = jnp.dot(lower, onehot_sum.astype(BF16), preferred_element_type=F32) + count_ref[...]
    route = jnp.zeros((tm, LANES), I32)
    gates = jnp.zeros((tm, LANES), F32)
    for k in range(TOP_K):
        pos = jnp.sum(jnp.where(hots[k], rank, 0.0), axis=1, keepdims=True).astype(I32)
        route = jnp.where(lane == k, idxs[k], route)
        route = jnp.where(lane == TOP_K + k, pos, route)
        gates = jnp.where(lane == k, exps[k] / denom, gates)
    route_ref[...] = route.astype(F32).T[0:2 * TOP_K, :].astype(I32)
    gate_ref[...] = gates
    meta = jnp.where(lane == 0, step * tm + lax.broadcasted_iota(I32, (tm, LANES), 0), 0)
    for k in range(TOP_K):
        meta = jnp.where(lane == 1 + k, idxs[k], meta)
    hf_ref[pl.ds(META_SUBROW, tm, stride=ROW_TILE), :] = meta.astype(U32)
    for j in range(META_SUBROW + 1, ROW_TILE):
        hf_ref[pl.ds(j, tm, stride=ROW_TILE), :] = jnp.zeros((tm, LANES), U32)
    count_ref[...] = count_ref[...] + jnp.sum(onehot_sum, axis=0, keepdims=True)
    cum_ref[0] = jnp.broadcast_to(count_ref[...], (8, LANES))


def _post_mix(x2d, proj2, o_sb, o_diff, w_up_sb, w_up_diff, w_out, g_memq, w_memq, kv, w_memo,
              g_ffn, w_router_pad, b_router_pad, *, tm, seq, gate_col):
    n_tok, d = x2d.shape
    n_tiles = n_tok // tm
    tiles_per_batch = seq // tm
    row = lambda i: (i, 0)
    in_specs = [
        pl.BlockSpec((tm, d), row),
        pl.BlockSpec((tm, o_sb.shape[1]), row),
        pl.BlockSpec((tm, o_diff.shape[1]), row),
        pl.BlockSpec((tm, d), lambda i: (i, gate_col)),
        pl.BlockSpec((tm, d), lambda i: (i, gate_col + 1)),
        _const_spec(w_up_sb.shape), _const_spec(w_up_diff.shape), _const_spec(w_out.shape),
        _const_spec((1, d)), _const_spec(w_memq.shape),
        pl.BlockSpec((1,) + kv.shape[1:], lambda i: (i // tiles_per_batch, 0, 0)),
        _const_spec(w_memo.shape), _const_spec((1, d)),
        _const_spec(w_router_pad.shape), _const_spec((1, LANES)),
    ]
    out_specs = [
        pl.BlockSpec((tm, d), row),
        pl.BlockSpec((tm * ROW_TILE, LANES), row),
        pl.BlockSpec((2 * TOP_K, tm), lambda i: (0, i)),
        pl.BlockSpec((tm, LANES), row),
        pl.BlockSpec((1, 8, LANES), lambda i: (i, 0, 0)),
    ]
    assert d // 2 == PACKED_SUBROWS * LANES
    out_shape = [
        jax.ShapeDtypeStruct((n_tok, d), F32),
        jax.ShapeDtypeStruct((n_tok * ROW_TILE, LANES), U32),
        jax.ShapeDtypeStruct((2 * TOP_K, n_tok), I32),
        jax.ShapeDtypeStruct((n_tok, LANES), F32),
        jax.ShapeDtypeStruct((n_tiles, 8, LANES), F32),
    ]
    return pl.pallas_call(
        functools.partial(_postmix_kernel, tm=tm, d_model=d),
        grid=(n_tiles,),
        in_specs=in_specs,
        out_specs=out_specs,
        out_shape=out_shape,
        scratch_shapes=[pltpu.VMEM((1, LANES), F32)],
        compiler_params=_params(("arbitrary",)),
        name="post_mix",
    )(x2d, o_sb, o_diff, proj2, proj2, w_up_sb, w_up_diff, w_out, g_memq, w_memq, kv, w_memo,
      g_ffn, w_router_pad, b_router_pad)


def _dispatch_kernel(zstart_ref, zflag_ref, nused_ref, dest_ref, hf_ref, xs_hbm, stage0, stage1,
                     zbuf, sem, zsem, *, tm, bm):
    s = pl.program_id(0)

    @pl.when(s == 0)
    def _():
        lane = lax.broadcasted_iota(I32, (bm, LANES), 1)
        zbuf[...] = jnp.zeros_like(zbuf)
        zbuf[pl.ds(META_SUBROW, bm, stride=ROW_TILE), :] = jnp.where(
            jnp.logical_and(lane >= 1, lane <= TOP_K), -1, 0).astype(U32)

        def fill_copy(start):
            start = pl.multiple_of(start * ROW_TILE, bm * ROW_TILE)
            return pltpu.make_async_copy(zbuf, xs_hbm.at[pl.ds(start, bm * ROW_TILE), :], zsem)

        n_blocks = xs_hbm.shape[0] // (bm * ROW_TILE)
        for action in ("start", "wait"):
            for e in range(N_EXPERTS):
                @pl.when(zflag_ref[e] != 0)
                def _(e=e, action=action):
                    getattr(fill_copy(zstart_ref[e]), action)()

            def trailing(blk, _, action=action):
                getattr(fill_copy(blk * bm), action)()
                return 0

            lax.fori_loop(nused_ref[0], n_blocks, trailing, 0)

    def tile_done(stage, parity):
        for _ in range(TOP_K):
            pltpu.make_async_copy(stage, xs_hbm.at[pl.ds(0, tm * ROW_TILE), :],
                                  sem.at[parity]).wait()

    def run(stage, other, parity):
        stage[...] = hf_ref[...]

        def issue(t, _):
            src = stage.at[pl.ds(pl.multiple_of(t * ROW_TILE, ROW_TILE), ROW_TILE), :]
            for k in range(TOP_K):
                row = pl.multiple_of(dest_ref[k * tm + t] * ROW_TILE, ROW_TILE)
                pltpu.make_async_copy(src, xs_hbm.at[pl.ds(row, ROW_TILE), :],
                                      sem.at[parity]).start(priority=k % 2)
            return 0

        lax.fori_loop(0, tm, issue, 0, unroll=2)

        @pl.when(s > 0)
        def _():
            tile_done(other, 1 - parity)

        @pl.when(s == pl.num_programs(0) - 1)
        def _():
            tile_done(stage, parity)

    @pl.when(s % 2 == 0)
    def _():
        run(stage0, stage1, 0)

    @pl.when(s % 2 == 1)
    def _():
        run(stage1, stage0, 1)


def _dispatch(zstart, zflag, n_used, dest_flat, hf_rows, *, tm, bm, n_rows):
    n_tok = hf_rows.shape[0] // ROW_TILE
    grid_spec = pltpu.PrefetchScalarGridSpec(
        num_scalar_prefetch=3,
        grid=(n_tok // tm,),
        in_specs=[
            pl.BlockSpec((tm * TOP_K,), lambda i, *_: (i,), memory_space=pltpu.SMEM),
            pl.BlockSpec((tm * ROW_TILE, LANES), lambda i, *_: (i, 0)),
        ],
        out_specs=pl.BlockSpec(memory_space=pl.ANY),
        scratch_shapes=[pltpu.VMEM((tm * ROW_TILE, LANES), U32),
                        pltpu.VMEM((tm * ROW_TILE, LANES), U32),
                        pltpu.VMEM((bm * ROW_TILE, LANES), U32),
                        pltpu.SemaphoreType.DMA((2,)), pltpu.SemaphoreType.DMA(())],
    )
    return pl.pallas_call(
        functools.partial(_dispatch_kernel, tm=tm, bm=bm),
        grid_spec=grid_spec,
        out_shape=jax.ShapeDtypeStruct((n_rows * ROW_TILE, LANES), U32),
        compiler_params=_params(("arbitrary",)),
        name="dispatch",
    )(zstart, zflag, n_used, dest_flat, hf_rows)


FFN_UP_CHUNKS = 8
FFN_DOWN_CHUNKS = 4


def _expert_kernel(be_ref, valid_ref, src_ref, wslot_ref, next_ref, xs_ref, bgu_ref, bd_ref,
                   wgu_hbm, wd_hbm, ys_hbm, wgu_f32, wd_f32, wgu_bf, wd_bf, ybuf0, ybuf1, slot_v,
                   slot_s, sem_w, sem_y, sem_s, sem_z, *, d_ff, bm, n_tok):
    r = pl.program_id(0)
    valid = valid_ref[r] != 0
    prev_valid = jnp.logical_and(r > 0, valid_ref[jnp.maximum(r - 1, 0)] != 0)
    fresh = jnp.logical_or(r == 0, be_ref[r] != be_ref[jnp.maximum(r - 1, 0)])
    expert = be_ref[r]
    ybufs = (ybuf0, ybuf1)

    def weight_copies(e, wslot):
        return (pltpu.make_async_copy(wgu_hbm.at[e], wgu_f32.at[wslot], sem_w.at[wslot]),
                pltpu.make_async_copy(wd_hbm.at[e], wd_f32.at[wslot], sem_w.at[wslot]))
    d = wd_bf.shape[1]
    n_slots = TOP_K * n_tok
    block_rows = bm * ROW_TILE

    def rows_done(parity):
        return pltpu.make_async_copy(ybufs[parity], ys_hbm.at[pl.ds(0, block_rows), :],
                                     sem_y.at[parity])

    def slots_copy(parity):
        return pltpu.make_async_copy(slot_v.at[0:1, :], slot_s.at[parity:parity + 1, :],
                                     sem_s.at[parity])

    def scatter_rows(prev, lo, hi):
        for i in range(lo, hi):
            dst = pl.multiple_of(slot_s[prev, i] * ROW_TILE, ROW_TILE)
            pltpu.make_async_copy(ybufs[prev].at[pl.ds(i * ROW_TILE, ROW_TILE), :],
                                  ys_hbm.at[pl.ds(dst, ROW_TILE), :],
                                  sem_y.at[prev]).start(priority=i % 2)

    @pl.when(r == 0)
    def _():
        for action in ("start", "wait"):
            for parity in range(2):
                if action == "start":
                    ybufs[parity][...] = jnp.zeros_like(ybufs[parity])
                trash = (n_slots + parity * bm) * ROW_TILE
                getattr(pltpu.make_async_copy(ybufs[parity],
                                              ys_hbm.at[pl.ds(trash, block_rows), :], sem_z),
                        action)()

    for parity in range(2):
        @pl.when(jnp.logical_and(prev_valid, r % 2 == parity))
        def _(parity=parity):
            slots_copy(1 - parity).wait()

    @pl.when(jnp.logical_and(valid, fresh))
    def _():
        wslot = wslot_ref[r]

        @pl.when(r == 0)
        def _():
            for c in weight_copies(expert, wslot):
                c.start()

        for c in weight_copies(expert, wslot):
            c.wait()
        wgu_bf[...] = wgu_f32[wslot].astype(BF16)
        wd_bf[...] = wd_f32[wslot].astype(BF16)

        @pl.when(next_ref[r] >= 0)
        def _():
            for c in weight_copies(next_ref[r], 1 - wslot):
                c.start()

    def ffn(cur, interleave):
        prev = 1 - cur
        ybuf = ybufs[cur]
        bounds = [bm * c // FFN_UP_CHUNKS for c in range(FFN_UP_CHUNKS + 1)]

        meta = xs_ref[pl.ds(META_SUBROW, bm, stride=ROW_TILE), :].astype(I32)
        row = lax.broadcasted_iota(I32, (bm, 1), 0)
        slot = n_slots + cur * bm + row
        for k in range(TOP_K):
            slot = jnp.where(meta[:, 1 + k:2 + k] == be_ref[r], k * n_tok + meta[:, 0:1], slot)
        slots = jnp.broadcast_to(slot.astype(F32), (bm, LANES)).T
        slot_v[...] = slots[0:8, :].astype(I32)
        slots_copy(cur).start()

        words = jnp.concatenate([xs_ref[pl.ds(j, bm, stride=ROW_TILE), :]
                                 for j in range(PACKED_SUBROWS)], axis=1)
        xb = _unpack_bf16_pairs(words)
        up_w = 2 * d_ff // FFN_UP_CHUNKS
        pairs = FFN_UP_CHUNKS // 2
        acts = []
        for c in range(pairs):
            halves = []
            for half in range(2):
                cols = slice(half * d_ff + c * up_w, half * d_ff + (c + 1) * up_w)
                halves.append(jnp.dot(xb, wgu_bf[:, cols], preferred_element_type=F32)
                              + bgu_ref[pl.ds(expert, 1), cols])
                if interleave:
                    step_no = 2 * c + half
                    scatter_rows(prev, bounds[step_no], bounds[step_no + 1])
            glu = jnp.minimum(halves[0], SWIGLU_LIMIT)
            lin = jnp.clip(halves[1], -SWIGLU_LIMIT, SWIGLU_LIMIT)
            acts.append((glu * jax.nn.sigmoid(SWIGLU_ALPHA * glu) * (lin + 1.0)).astype(BF16))
        act = jnp.concatenate(acts, axis=1)

        @pl.when(r >= 2)
        def _():
            rows_done(cur).wait()

        down_w = d // FFN_DOWN_CHUNKS
        for c in range(FFN_DOWN_CHUNKS):
            cols = slice(c * down_w, (c + 1) * down_w)
            y = (jnp.dot(act, wd_bf[:, cols], preferred_element_type=F32)
                 + bd_ref[pl.ds(expert, 1), cols])
            for j in range(down_w // LANES):
                sub = c * (down_w // LANES) + j
                ybuf[pl.ds(sub, bm, stride=ROW_TILE), :] = y[:, j * LANES:(j + 1) * LANES]

    for parity in range(2):
        on_parity = r % 2 == parity

        @pl.when(jnp.logical_and(on_parity, jnp.logical_and(valid, prev_valid)))
        def _(parity=parity):
            ffn(parity, True)

        @pl.when(jnp.logical_and(on_parity, jnp.logical_and(jnp.logical_not(valid), prev_valid)))
        def _(parity=parity):
            scatter_rows(1 - parity, 0, bm)

            @pl.when(r >= 2)
            def _():
                rows_done(parity).wait()

            rows_done(1 - parity).wait()

    @pl.when(r == 0)
    def _():
        ffn(0, False)


def _expert_ffn(blk_expert, blk_valid, blk_src, blk_wslot, blk_next, xs, w_gu, b_gu, w_down,
                b_down, *, bm, n_tok):
    n_grid = blk_expert.shape[0]
    n_exp, d_ff, d = w_down.shape
    assert d == ROW_TILE * LANES
    block_rows = bm * ROW_TILE
    grid_spec = pltpu.PrefetchScalarGridSpec(
        num_scalar_prefetch=5,
        grid=(n_grid,),
        in_specs=[
            pl.BlockSpec((block_rows, LANES), lambda r, be, valid, src, *_: (src[r], 0)),
            pl.BlockSpec((n_exp, 2 * d_ff), lambda r, *_: (0, 0)),
            pl.BlockSpec((n_exp, d), lambda r, *_: (0, 0)),
            pl.BlockSpec(memory_space=pl.ANY),
            pl.BlockSpec(memory_space=pl.ANY),
        ],
        out_specs=pl.BlockSpec(memory_space=pl.ANY),
        scratch_shapes=[pltpu.VMEM((2, d, 2 * d_ff), F32), pltpu.VMEM((2, d_ff, d), F32),
                        pltpu.VMEM((d, 2 * d_ff), BF16), pltpu.VMEM((d_ff, d), BF16),
                        pltpu.VMEM((block_rows, LANES), F32),
                        pltpu.VMEM((block_rows, LANES), F32), pltpu.VMEM((8, bm), I32),
                        pltpu.SMEM((2, bm), I32), pltpu.SemaphoreType.DMA((2,)),
                        pltpu.SemaphoreType.DMA((2,)), pltpu.SemaphoreType.DMA((2,)),
                        pltpu.SemaphoreType.DMA(())],
    )
    return pl.pallas_call(
        functools.partial(_expert_kernel, d_ff=d_ff, bm=bm, n_tok=n_tok),
        grid_spec=grid_spec,
        out_shape=jax.ShapeDtypeStruct(((TOP_K * n_tok + 2 * bm) * ROW_TILE, LANES), F32),
        compiler_params=_params(("arbitrary",)),
        name="expert_ffn",
    )(blk_expert, blk_valid, blk_src, blk_wslot, blk_next, xs, b_gu, b_down, w_gu, w_down)


def _combine_kernel(x_ref, gate_ref, g_ref, y0_ref, y1_ref, y2_ref, y3_ref, o_ref):
    tm, d = x_ref.shape
    gates = gate_ref[...]
    acc = x_ref[...]
    for k, y_ref in enumerate((y0_ref, y1_ref, y2_ref, y3_ref)):
        y = jnp.concatenate([y_ref[pl.ds(j, tm, stride=ROW_TILE), :]
                             for j in range(d // LANES)], axis=1)
        acc = acc + gates[:, k:k + 1] * y
    o_ref[...] = _rms(acc, g_ref[...])


def _combine(x2, gates, g_final, ys, *, tm):
    n_tok, d = x2.shape
    tiles = n_tok // tm
    slot_spec = lambda k: pl.BlockSpec((tm * ROW_TILE, LANES), lambda i: (k * tiles + i, 0))
    return pl.pallas_call(
        _combine_kernel,
        grid=(tiles,),
        in_specs=[
            pl.BlockSpec((tm, d), lambda i: (i, 0)),
            pl.BlockSpec((tm, LANES), lambda i: (i, 0)),
            pl.BlockSpec((1, d), lambda i: (0, 0)),
        ] + [slot_spec(k) for k in range(TOP_K)],
        out_specs=pl.BlockSpec((tm, d), lambda i: (i, 0)),
        out_shape=jax.ShapeDtypeStruct((n_tok, d), F32),
        compiler_params=_params(("arbitrary",)),
        name="combine",
    )(x2, gates, g_final, ys, ys, ys, ys)


def _routing_tables(route, cum, *, tm, bm, n_grid):
    expert = route[0:TOP_K]
    pos = route[TOP_K:2 * TOP_K]
    experts = jnp.arange(N_EXPERTS, dtype=I32)
    counts = cum[-1, 0, :N_EXPERTS].astype(I32)
    padded = (counts + bm - 1) // bm * bm
    pend = jnp.cumsum(padded)
    pstart = pend - padded
    dest = pos
    for e in range(N_EXPERTS):
        dest = dest + jnp.where(expert == e, pstart[e], 0)

    blk = jnp.arange(n_grid, dtype=I32)
    blk_expert = jnp.minimum(jnp.sum(pend[None, :] <= (blk * bm)[:, None], axis=1),
                             N_EXPERTS - 1).astype(I32)
    n_used = pend[-1] // bm
    blk_valid = (blk < n_used).astype(I32)
    blk_src = jnp.minimum(blk, jnp.maximum(n_used - 1, 0)).astype(I32)
    zflag = (padded > 0).astype(I32)
    zstart = jnp.maximum(pend - bm, 0).astype(I32)
    wslot = (jnp.cumsum(zflag) - 1) % 2
    later = jnp.logical_and(experts[None, :] > experts[:, None], zflag[None, :] > 0)
    nxt = jnp.min(jnp.where(later, experts[None, :], N_EXPERTS), axis=1)
    nxt = jnp.where(nxt < N_EXPERTS, nxt, -1)
    owner = blk_expert[:, None] == experts[None, :]
    of_block = lambda v: jnp.sum(jnp.where(owner, v[None, :], 0), axis=1).astype(I32)
    dest = dest.astype(I32).reshape(TOP_K, -1, tm).transpose(1, 0, 2).reshape(-1)
    return (dest, blk_expert, blk_valid, blk_src, of_block(wslot),
            of_block(nxt), zstart, zflag, n_used.astype(I32).reshape(1))


def kernel(x, mem, positions, norm_mix, w_in, lambda_q1, lambda_k1, lambda_q2, lambda_k2,
           diff_subln, w_up_sb, w_up_diff, w_out, norm_mem_q, norm_mem_kv, w_mem_q, w_mem_kv,
           w_mem_o, norm_ffn, w_router, b_router, w_gate_up, b_gate_up, w_down, b_down,
           norm_final):
    b, s, d = x.shape
    n_tok = b * s
    depth = norm_mix.shape[0]
    sb_width = w_up_sb.shape[1]
    diff_width = w_up_diff.shape[1]
    n_in = w_in.shape[2]
    chunk = 512
    assert sb_width == chunk and diff_width == chunk and d == 2 * chunk
    sbq, sbk, sbv, dq, dk, dv = range(6)
    blocks_per_chunk = chunk // LANES
    scale = 1.0 / math.sqrt(SB_HEAD_DIM)
    assert SB_HEAD_DIM == DIFF_HEAD_DIM

    tm_in = min(512, n_tok)
    tq_sb = min(256, s)
    tq_diff = min(512, s)
    tm_post = min(512, s)
    tm_tok = min(256, n_tok)
    bm = 256
    n_grid = (n_tok * TOP_K) // bm + N_EXPERTS

    x2d = x.reshape(n_tok, d)
    for l in range(depth):
        lambda_init = 0.8 - 0.6 * math.exp(-0.3 * l)
        proj = _in_proj(x2d, norm_mix[l].reshape(1, d), w_in[l], positions,
                        tm=tm_in, chunk=chunk, rope_chunks=(dq, dk), scale_chunks=(sbq, dq),
                        scale=scale)
        proj3 = proj.reshape(b, s, n_in)
        o_sb = _sb_attention(proj3, q_col=sbq * blocks_per_chunk, k_col=sbk * blocks_per_chunk,
                             v_col=sbv * blocks_per_chunk, n_pairs=sb_width // LANES, tq=tq_sb)
        o_diff = _diff_attention(
            proj3, lambda_q1[l].reshape(1, -1), lambda_k1[l].reshape(1, -1),
            lambda_q2[l].reshape(1, -1), lambda_k2[l].reshape(1, -1),
            diff_subln[l].reshape(1, -1), q_col=dq * blocks_per_chunk,
            k_col=dk * blocks_per_chunk, v_col=dv * blocks_per_chunk,
            n_heads=diff_width // DIFF_V_DIM, tq=tq_diff, lambda_init=lambda_init)
        kv = _mem_kv(mem, norm_mem_kv[l].reshape(1, d), w_mem_kv[l].astype(BF16))
        w_router_pad = jnp.zeros((d, LANES), F32).at[:, :N_EXPERTS].set(w_router[l])
        b_router_pad = jnp.full((1, LANES), NEG_BIG, F32).at[0, :N_EXPERTS].set(b_router[l])
        x_res, hf, route, gates, cum = _post_mix(
            x2d, proj, o_sb.reshape(n_tok, sb_width), o_diff.reshape(n_tok, diff_width),
            w_up_sb[l].astype(BF16), w_up_diff[l].astype(BF16), w_out[l].astype(BF16),
            norm_mem_q[l].reshape(1, d), w_mem_q[l].astype(BF16), kv, w_mem_o[l].astype(BF16),
            norm_ffn[l].reshape(1, d), w_router_pad, b_router_pad,
            tm=tm_post, seq=s, gate_col=6 * chunk // d)
        (dest, blk_expert, blk_valid, blk_src, blk_wslot, blk_next, zstart, zflag,
         n_used) = _routing_tables(route, cum, tm=tm_tok, bm=bm, n_grid=n_grid + 1)
        xs = _dispatch(zstart, zflag, n_used, dest, hf, tm=tm_tok, bm=bm, n_rows=n_grid * bm)
        ys = _expert_ffn(blk_expert, blk_valid, blk_src, blk_wslot, blk_next, xs, w_gate_up[l],
                         b_gate_up[l], w_down[l], b_down[l], bm=bm, n_tok=n_tok)
        g_next = norm_final if l == depth - 1 else None
        assert g_next is not None, "only depth 1 is wired: the final norm is fused into combine"
        x2d = _combine(x_res, gates, g_next.reshape(1, d), ys, tm=tm_tok)
    return x2d.reshape(b, s, d)
```

```python
import functools
import math

import jax
import jax.numpy as jnp
from jax import lax
from jax.experimental import pallas as pl
from jax.experimental.pallas import tpu as pltpu

F32 = jnp.float32
BF16 = jnp.bfloat16
I32 = jnp.int32
U32 = jnp.uint32

LANES = 128
VMEM_LIMIT_BYTES = 56 * 1024 * 1024
ROW_TILE = 8
PACKED_SUBROWS = 4
META_SUBROW = 4

NORM_EPS = 1e-6
ROPE_THETA = 10000.0
CHUNK = 64
SB_HEAD_DIM = 64
DIFF_HEAD_DIM = 64
DIFF_V_DIM = 128
MEM_HEADS = 4
N_EXPERTS = 32
TOP_K = 4
SWIGLU_LIMIT = 7.0
SWIGLU_ALPHA = 1.702

SB_DEAD_LOG = -105.0
NEG_BIG = -1e30


def _params(semantics):
    return pltpu.CompilerParams(dimension_semantics=semantics,
                                vmem_limit_bytes=VMEM_LIMIT_BYTES)


def _const_spec(shape):
    nd = len(shape)
    return pl.BlockSpec(shape, lambda *_: (0,) * nd)


def _rms(x, g):
    return x * lax.rsqrt(jnp.mean(x * x, axis=-1, keepdims=True) + NORM_EPS) * g


def _inproj_kernel(x_ref, g_ref, w32_ref, pos_ref, inv_ref, o_ref, w_ref, *, chunk, rope_chunks,
                   scale_chunks, scale):
    @pl.when(pl.program_id(0) == 0)
    def _():
        w_ref[...] = w32_ref[...].astype(BF16)

    tm = x_ref.shape[0]
    h = _rms(x_ref[...], g_ref[...]).astype(BF16)
    lane = lax.broadcasted_iota(I32, (1, chunk), 1)
    first_half = (lane % DIFF_HEAD_DIM) < (DIFF_HEAD_DIM // 2)
    pos = jnp.broadcast_to(pos_ref[0], (8, tm)).T[:, 0:1]
    ang = pos * inv_ref[...]
    cos_t = jnp.cos(ang)
    sin_t = jnp.where(first_half[:, :LANES], -jnp.sin(ang), jnp.sin(ang))
    for c in range(w_ref.shape[1] // chunk):
        cols = slice(c * chunk, (c + 1) * chunk)
        acc = jnp.dot(h, w_ref[:, cols], preferred_element_type=F32)
        if c in rope_chunks:
            cos = jnp.tile(cos_t, (1, chunk // LANES))
            sin = jnp.tile(sin_t, (1, chunk // LANES))
            partner = jnp.where(first_half,
                                pltpu.roll(acc, chunk - DIFF_HEAD_DIM // 2, 1),
                                pltpu.roll(acc, DIFF_HEAD_DIM // 2, 1))
            acc = acc * cos + partner * sin
        if c in scale_chunks:
            acc = acc * scale
        o_ref[:, cols] = acc.astype(BF16)


def _in_proj(x2d, g, w, positions, *, tm, chunk, rope_chunks, scale_chunks, scale):
    n_tok, d = x2d.shape
    half = DIFF_HEAD_DIM // 2
    inv_freq = ROPE_THETA ** (-jnp.arange(half, dtype=F32) / half)
    inv_freq = jnp.tile(inv_freq, LANES // half).reshape(1, LANES)
    pos = positions.reshape(n_tok // tm, 1, tm).astype(F32)
    n_in = w.shape[1]
    kern = functools.partial(_inproj_kernel, chunk=chunk, rope_chunks=rope_chunks,
                             scale_chunks=scale_chunks, scale=scale)
    return pl.pallas_call(
        kern,
        grid=(n_tok // tm,),
        in_specs=[pl.BlockSpec((tm, d), lambda i: (i, 0)),
                  _const_spec((1, d)),
                  _const_spec((d, n_in)),
                  pl.BlockSpec((1, 1, tm), lambda i: (i, 0, 0)),
                  _const_spec((1, LANES))],
        out_specs=pl.BlockSpec((tm, n_in), lambda i: (i, 0)),
        out_shape=jax.ShapeDtypeStruct((n_tok, n_in), BF16),
        scratch_shapes=[pltpu.VMEM((d, n_in), BF16)],
        compiler_params=_params(("arbitrary",)),
        name="in_proj",
    )(x2d, g, w, pos, inv_freq)


def _sb_kernel(q_ref, k_ref, v_ref, o_ref, acc_ref, lw0, lw1, cat0, cat1, *, tq):
    i = pl.program_id(2)
    q = q_ref[0]
    lane = lax.broadcasted_iota(I32, (1, LANES), 1)
    zero = jnp.zeros_like(q)
    q2 = jnp.concatenate([jnp.where(lane < SB_HEAD_DIM, q, zero),
                          jnp.where(lane >= SB_HEAD_DIM, q, zero)], axis=0)
    uj = lax.broadcasted_iota(I32, (2 * tq, tq), 0) % tq
    us = lax.broadcasted_iota(I32, (2 * tq, tq), 1)
    suffix = jnp.where(uj > us, 1.0, 0.0).astype(BF16)
    lws = (lw0, lw1)
    cats = (cat0, cat1)

    def causal_mask():
        return (lax.broadcasted_iota(I32, (2 * tq, tq), 1)
                < lax.broadcasted_iota(I32, (2 * tq, tq), 0) % tq)

    def logits(j, buf, masked):
        start = pl.multiple_of(j * tq, tq)
        kj = k_ref[0, pl.ds(start, tq), :]
        z = lax.dot_general(q2, kj, (((1,), (1,)), ((), ())), preferred_element_type=F32)
        sp = jnp.maximum(z, 0.0) + jnp.log(1.0 + jnp.exp(-jnp.abs(z)))
        log_rem = -sp
        if masked:
            log_rem = jnp.where(causal_mask(), log_rem, 0.0)
        lws[buf][...] = z - sp
        hi = log_rem.astype(BF16)
        cats[buf][:, 0:tq] = hi
        cats[buf][:, tq:2 * tq] = (log_rem - hi.astype(F32)).astype(BF16)
        return log_rem[:, 0:1]

    def suffix_sums(buf, first):
        after = jnp.dot(cats[buf][...], suffix, preferred_element_type=F32)
        lws[buf][...] += after
        return after[:, 0:1] + first

    def weigh(j, buf, carry, masked):
        start = pl.multiple_of(j * tq, tq)
        vj = v_ref[0, pl.ds(start, tq), :]
        w = jnp.exp(lws[buf][...] + carry)
        if masked:
            w = jnp.where(causal_mask(), w, 0.0)
        acc_ref[...] += jnp.dot(w.astype(BF16), vj, preferred_element_type=F32)

    def earlier_blocks(first_block, carry):
        def cond(state):
            j, _, alive = state
            return jnp.logical_and(j >= 0, alive > SB_DEAD_LOG)

        def body(state):
            j, carry, _ = state
            total = suffix_sums(0, logits(j, 0, False))
            weigh(j, 0, carry, False)
            carry = carry + total
            return j - 1, carry, jnp.max(carry)

        lax.while_loop(cond, body, (first_block, carry, jnp.max(carry)))

    acc_ref[...] = jnp.zeros_like(acc_ref)
    no_carry = jnp.zeros((2 * tq, 1), F32)

    @pl.when(i == 0)
    def _():
        suffix_sums(0, logits(0, 0, True))
        weigh(0, 0, no_carry, True)

    @pl.when(i > 0)
    def _():
        first_diag = logits(i, 0, True)
        first_prev = logits(i - 1, 1, False)
        total_diag = suffix_sums(0, first_diag)
        total_prev = suffix_sums(1, first_prev)
        weigh(i, 0, no_carry, True)
        weigh(i - 1, 1, total_diag, False)
        earlier_blocks(i - 2, total_diag + total_prev)

    o_ref[0] = jnp.where(lane < SB_HEAD_DIM, acc_ref[0:tq, :], acc_ref[tq:2 * tq, :]).astype(BF16)


def _sb_attention(proj3, *, q_col, k_col, v_col, n_pairs, tq):
    b, s, _ = proj3.shape
    return pl.pallas_call(
        functools.partial(_sb_kernel, tq=tq),
        grid=(b, n_pairs, s // tq),
        in_specs=[pl.BlockSpec((1, tq, LANES), lambda bi, p, i: (bi, i, q_col + p)),
                  pl.BlockSpec((1, s, LANES), lambda bi, p, i: (bi, 0, k_col + p)),
                  pl.BlockSpec((1, s, LANES), lambda bi, p, i: (bi, 0, v_col + p))],
        out_specs=pl.BlockSpec((1, tq, LANES), lambda bi, p, i: (bi, i, p)),
        out_shape=jax.ShapeDtypeStruct((b, s, n_pairs * LANES), BF16),
        scratch_shapes=[pltpu.VMEM((2 * tq, LANES), F32),
                        pltpu.VMEM((2 * tq, tq), F32), pltpu.VMEM((2 * tq, tq), F32),
                        pltpu.VMEM((2 * tq, 2 * tq), BF16), pltpu.VMEM((2 * tq, 2 * tq), BF16)],
        compiler_params=_params(("arbitrary", "arbitrary", "arbitrary")),
        name="sb_attention",
    )(proj3, proj3, proj3)


DIFF_ONES_ROWS = 16


def _diff_kernel(lq1_ref, lk1_ref, lq2_ref, lk2_ref, q_ref, k_ref, v_ref, g_ref, o_ref, vt_ref,
                 acc_ref, z0_ref, z1_ref, *, tq, lambda_init):
    i = pl.program_id(2)
    s_len = v_ref.shape[1]
    vd = v_ref.shape[2]
    half = tq // 2
    zs = (z0_ref, z1_ref)

    @pl.when(i == 0)
    def _():
        for c in range(s_len // tq):
            cols = slice(c * tq, (c + 1) * tq)
            vt_ref[0:vd, cols] = v_ref[0, cols, :].astype(F32).T.astype(BF16)
        rid = lax.broadcasted_iota(I32, (DIFF_ONES_ROWS, s_len), 0)
        vt_ref[vd:vd + DIFF_ONES_ROWS, :] = jnp.where(rid == 0, 1.0, 0.0).astype(BF16)

    q = q_ref[0]
    lane = lax.broadcasted_iota(I32, (1, LANES), 1)
    zero = jnp.zeros_like(q)
    q2 = jnp.concatenate([jnp.where(lane < DIFF_HEAD_DIM, q, zero),
                          jnp.where(lane >= DIFF_HEAD_DIM, q, zero)], axis=0)

    def scores(j, h, masked):
        start = pl.multiple_of(j * tq + h * half, half)
        kj = k_ref[0, pl.ds(start, half), :]
        zt = lax.dot_general(kj, q2, (((1,), (1,)), ((), ())), preferred_element_type=F32)
        if masked:
            qidx = lax.broadcasted_iota(I32, (half, 2 * tq), 1) % tq
            kidx = lax.broadcasted_iota(I32, (half, 2 * tq), 0) + h * half
            zt = jnp.where(kidx // CHUNK <= qidx // CHUNK, zt, -jnp.inf)
        zs[h][...] = zt
        return jnp.max(zt, axis=0, keepdims=True)

    def values(j, h, m, mblk):
        start = pl.multiple_of(j * tq + h * half, half)
        vtj = vt_ref[:, pl.ds(start, half)]
        m_new = jnp.maximum(m, mblk)
        pt = jnp.exp(zs[h][...] - m_new).astype(BF16)
        acc_ref[...] = jnp.exp(m - m_new) * acc_ref[...] + jnp.dot(vtj, pt,
                                                                   preferred_element_type=F32)
        return m_new

    def block(j, m, mb0, masked, next_kind):
        mb1 = scores(j, 1, masked)
        m = values(j, 0, m, mb0)
        nb0 = mb0 if next_kind is None else scores(j + 1, 0, next_kind == "masked")
        m = values(j, 1, m, mb1)
        return m, nb0

    acc_ref[...] = jnp.zeros_like(acc_ref)
    m0 = jnp.full((1, 2 * tq), -jnp.inf, F32)

    @pl.when(i == 0)
    def _():
        block(0, m0, scores(0, 0, True), True, None)

    @pl.when(i > 0)
    def _():
        state = (m0, scores(0, 0, False))
        state = lax.fori_loop(0, i - 1, lambda j, st: block(j, st[0], st[1], False, "plain"),
                              state)
        state = block(i - 1, state[0], state[1], False, "masked")
        block(i, state[0], state[1], True, None)

    lam = (jnp.exp(jnp.sum(lq1_ref[...] * lk1_ref[...], axis=1, keepdims=True))
           - jnp.exp(jnp.sum(lq2_ref[...] * lk2_ref[...], axis=1, keepdims=True))
           + lambda_init)
    ot = acc_ref[0:vd, :] / acc_ref[vd:vd + 1, :]
    ot = ot[:, 0:tq] - lam * ot[:, tq:2 * tq]
    o_ref[0] = (_rms(ot.T, g_ref[...]) * (1.0 - lambda_init)).astype(BF16)


def _diff_attention(proj3, lq1, lk1, lq2, lk2, subln, *, q_col, k_col, v_col, n_heads, tq,
                    lambda_init):
    b, s, _ = proj3.shape
    lam_spec = _const_spec((1, DIFF_HEAD_DIM))
    return pl.pallas_call(
        functools.partial(_diff_kernel, tq=tq, lambda_init=lambda_init),
        grid=(b, n_heads, s // tq),
        in_specs=[lam_spec, lam_spec, lam_spec, lam_spec,
                  pl.BlockSpec((1, tq, LANES), lambda bi, h, i: (bi, i, q_col + h)),
                  pl.BlockSpec((1, s, LANES), lambda bi, h, i: (bi, 0, k_col + h)),
                  pl.BlockSpec((1, s, LANES), lambda bi, h, i: (bi, 0, v_col + h)),
                  _const_spec((1, DIFF_V_DIM))],
        out_specs=pl.BlockSpec((1, tq, LANES), lambda bi, h, i: (bi, i, h)),
        out_shape=jax.ShapeDtypeStruct((b, s, n_heads * DIFF_V_DIM), BF16),
        scratch_shapes=[pltpu.VMEM((DIFF_V_DIM + DIFF_ONES_ROWS, s), BF16),
                        pltpu.VMEM((DIFF_V_DIM + DIFF_ONES_ROWS, 2 * tq), F32),
                        pltpu.VMEM((tq // 2, 2 * tq), F32), pltpu.VMEM((tq // 2, 2 * tq), F32)],
        compiler_params=_params(("arbitrary", "arbitrary", "arbitrary")),
        name="diff_attention",
    )(lq1, lk1, lq2, lk2, proj3, proj3, proj3, subln)


def _mem_kv_kernel(mem_ref, g_ref, w_ref, o_ref):
    h = _rms(mem_ref[0], g_ref[...]).astype(BF16)
    o_ref[0] = jnp.dot(h, w_ref[...], preferred_element_type=F32).astype(BF16)


def _mem_kv(mem, g, w_bf16):
    b, m, d = mem.shape
    n = w_bf16.shape[1]
    return pl.pallas_call(
        _mem_kv_kernel,
        grid=(b,),
        in_specs=[pl.BlockSpec((1, m, d), lambda i: (i, 0, 0)), _const_spec((1, d)),
                  _const_spec((d, n))],
        out_specs=pl.BlockSpec((1, m, n), lambda i: (i, 0, 0)),
        out_shape=jax.ShapeDtypeStruct((b, m, n), BF16),
        compiler_params=_params(("arbitrary",)),
        name="mem_kv",
    )(mem, g, w_bf16)


def _split_bf16(v):
    hi = v.astype(BF16)
    return hi, (v - hi.astype(F32)).astype(BF16)


def _pack_bf16_pairs(v):
    half = v.shape[1] // 2
    bits = lax.bitcast_convert_type(v.astype(BF16).astype(F32), U32)
    return (bits[:, :half] >> 16) | (bits[:, half:] & jnp.uint32(0xFFFF0000))


def _unpack_bf16_pairs(w):
    lo = lax.bitcast_convert_type(w << 16, F32)
    hi = lax.bitcast_convert_type(w & jnp.uint32(0xFFFF0000), F32)
    return jnp.concatenate([lo, hi], axis=1).astype(BF16)


def _postmix_kernel(x_ref, osb_ref, odf_ref, gsb_ref, gdf_ref, wus_ref, wud_ref, wout_ref,
                    gq_ref, wq_ref, kv_ref, wo_ref, gf_ref, wr_ref, br_ref,
                    x2_ref, hf_ref, route_ref, gate_ref, cum_ref, count_ref, *, tm, d_model):
    step = pl.program_id(0)

    @pl.when(step == 0)
    def _():
        count_ref[...] = jnp.zeros_like(count_ref)

    y_sb = jnp.dot(osb_ref[...], wus_ref[...], preferred_element_type=F32)
    y_df = jnp.dot(odf_ref[...], wud_ref[...], preferred_element_type=F32)
    mixed = (jax.nn.sigmoid(gsb_ref[...].astype(F32)) * y_sb
             + jax.nn.sigmoid(gdf_ref[...].astype(F32)) * y_df)
    x1 = x_ref[...] + jnp.dot(mixed.astype(BF16), wout_ref[...], preferred_element_type=F32)

    hq = _rms(x1, gq_ref[...]).astype(BF16)
    hd = d_model // MEM_HEADS
    q = jnp.dot(hq, wq_ref[...], preferred_element_type=F32) * (1.0 / math.sqrt(hd))
    q = q.astype(BF16)
    heads = []
    for h in range(MEM_HEADS):
        kh = kv_ref[0, :, h * hd:(h + 1) * hd]
        vh = kv_ref[0, :, d_model + h * hd:d_model + (h + 1) * hd]
        z = lax.dot_general(q[:, h * hd:(h + 1) * hd], kh, (((1,), (1,)), ((), ())),
                            preferred_element_type=F32)
        p = jnp.exp(z - jnp.max(z, axis=1, keepdims=True))
        l = jnp.sum(p, axis=1, keepdims=True)
        heads.append((jnp.dot(p.astype(BF16), vh, preferred_element_type=F32) / l).astype(BF16))
    x2 = x1 + jnp.dot(jnp.concatenate(heads, axis=1), wo_ref[...], preferred_element_type=F32)
    x2_ref[...] = x2

    hf = _rms(x2, gf_ref[...])
    packed = _pack_bf16_pairs(hf)
    for j in range(PACKED_SUBROWS):
        hf_ref[pl.ds(j, tm, stride=ROW_TILE), :] = packed[:, j * LANES:(j + 1) * LANES]
    h_hi, h_lo = _split_bf16(hf)
    w_hi, w_lo = _split_bf16(wr_ref[...])
    logits = (jnp.dot(h_hi, w_hi, preferred_element_type=F32)
              + jnp.dot(h_hi, w_lo, preferred_element_type=F32)
              + jnp.dot(h_lo, w_hi, preferred_element_type=F32)) + br_ref[...]
    lane = lax.broadcasted_iota(I32, (tm, LANES), 1)
    work = logits
    vals, idxs, hots = [], [], []
    for _ in range(TOP_K):
        mx = jnp.max(work, axis=1, keepdims=True)
        idx = jnp.min(jnp.where(work == mx, lane, LANES), axis=1, keepdims=True)
        hot = lane == idx
        work = jnp.where(hot, NEG_BIG, work)
        vals.append(mx)
        idxs.append(idx)
        hots.append(hot)
    exps = [jnp.exp(v - vals[0]) for v in vals]
    denom = exps[0] + exps[1] + exps[2] + exps[3]

    onehot_sum = jnp.zeros((tm, LANES), F32)
    for hot in hots:
        onehot_sum = onehot_sum + jnp.where(hot, 1.0, 0.0)
    r = lax.broadcasted_iota(I32, (tm, tm), 0)
    c = lax.broadcasted_iota(I32, (tm, tm), 1)
    lower = jnp.where(c < r, 1.0, 0.0).astype(BF16)
    rank = jnp.dot(lower, onehot_sum.astype(BF16), preferred_element_type=F32) + count_ref[...]
    route = jnp.zeros((tm, LANES), I32)
    gates = jnp.zeros((tm, LANES), F32)
    for k in range(TOP_K):
        pos = jnp.sum(jnp.where(hots[k], rank, 0.0), axis=1, keepdims=True).astype(I32)
        route = jnp.where(lane == k, idxs[k], route)
        route = jnp.where(lane == TOP_K + k, pos, route)
        gates = jnp.where(lane == k, exps[k] / denom, gates)
    route_ref[...] = route.astype(F32).T[0:2 * TOP_K, :].astype(I32)
    gate_ref[...] = gates
    meta = jnp.where(lane == 0, step * tm + lax.broadcasted_iota(I32, (tm, LANES), 0), 0)
    for k in range(TOP_K):
        meta = jnp.where(lane == 1 + k, idxs[k], meta)
    hf_ref[pl.ds(META_SUBROW, tm, stride=ROW_TILE), :] = meta.astype(U32)
    for j in range(META_SUBROW + 1, ROW_TILE):
        hf_ref[pl.ds(j, tm, stride=ROW_TILE), :] = jnp.zeros((tm, LANES), U32)
    count_ref[...] = count_ref[...] + jnp.sum(onehot_sum, axis=0, keepdims=True)
    cum_ref[0] = jnp.broadcast_to(count_ref[...], (8, LANES))


def _post_mix(x2d, proj2, o_sb, o_diff, w_up_sb, w_up_diff, w_out, g_memq, w_memq, kv, w_memo,
              g_ffn, w_router_pad, b_router_pad, *, tm, seq, gate_col):
    n_tok, d = x2d.shape
    n_tiles = n_tok // tm
    tiles_per_batch = seq // tm
    row = lambda i: (i, 0)
    in_specs = [
        pl.BlockSpec((tm, d), row),
        pl.BlockSpec((tm, o_sb.shape[1]), row),
        pl.BlockSpec((tm, o_diff.shape[1]), row),
        pl.BlockSpec((tm, d), lambda i: (i, gate_col)),
        pl.BlockSpec((tm, d), lambda i: (i, gate_col + 1)),
        _const_spec(w_up_sb.shape), _const_spec(w_up_diff.shape), _const_spec(w_out.shape),
        _const_spec((1, d)), _const_spec(w_memq.shape),
        pl.BlockSpec((1,) + kv.shape[1:], lambda i: (i // tiles_per_batch, 0, 0)),
        _const_spec(w_memo.shape), _const_spec((1, d)),
        _const_spec(w_router_pad.shape), _const_spec((1, LANES)),
    ]
    out_specs = [
        pl.BlockSpec((tm, d), row),
        pl.BlockSpec((tm * ROW_TILE, LANES), row),
        pl.BlockSpec((2 * TOP_K, tm), lambda i: (0, i)),
        pl.BlockSpec((tm, LANES), row),
        pl.BlockSpec((1, 8, LANES), lambda i: (i, 0, 0)),
    ]
    assert d // 2 == PACKED_SUBROWS * LANES
    out_shape = [
        jax.ShapeDtypeStruct((n_tok, d), F32),
        jax.ShapeDtypeStruct((n_tok * ROW_TILE, LANES), U32),
        jax.ShapeDtypeStruct((2 * TOP_K, n_tok), I32),
        jax.ShapeDtypeStruct((n_tok, LANES), F32),
        jax.ShapeDtypeStruct((n_tiles, 8, LANES), F32),
    ]
    return pl.pallas_call(
        functools.partial(_postmix_kernel, tm=tm, d_model=d),
        grid=(n_tiles,),
        in_specs=in_specs,
        out_specs=out_specs,
        out_shape=out_shape,
        scratch_shapes=[pltpu.VMEM((1, LANES), F32)],
        compiler_params=_params(("arbitrary",)),
        name="post_mix",
    )(x2d, o_sb, o_diff, proj2, proj2, w_up_sb, w_up_diff, w_out, g_memq, w_memq, kv, w_memo,
      g_ffn, w_router_pad, b_router_pad)


def _dispatch_kernel(zstart_ref, zflag_ref, nused_ref, dest_ref, hf_ref, xs_hbm, stage0, stage1,
                     zbuf, sem, zsem, *, tm, bm):
    s = pl.program_id(0)

    @pl.when(s == 0)
    def _():
        lane = lax.broadcasted_iota(I32, (bm, LANES), 1)
        zbuf[...] = jnp.zeros_like(zbuf)
        zbuf[pl.ds(META_SUBROW, bm, stride=ROW_TILE), :] = jnp.where(
            jnp.logical_and(lane >= 1, lane <= TOP_K), -1, 0).astype(U32)

        def fill_copy(start):
            start = pl.multiple_of(start * ROW_TILE, bm * ROW_TILE)
            return pltpu.make_async_copy(zbuf, xs_hbm.at[pl.ds(start, bm * ROW_TILE), :], zsem)

        n_blocks = xs_hbm.shape[0] // (bm * ROW_TILE)
        for action in ("start", "wait"):
            for e in range(N_EXPERTS):
                @pl.when(zflag_ref[e] != 0)
                def _(e=e, action=action):
                    getattr(fill_copy(zstart_ref[e]), action)()

            def trailing(blk, _, action=action):
                getattr(fill_copy(blk * bm), action)()
                return 0

            lax.fori_loop(nused_ref[0], n_blocks, trailing, 0)

    def tile_done(stage, parity):
        for _ in range(TOP_K):
            pltpu.make_async_copy(stage, xs_hbm.at[pl.ds(0, tm * ROW_TILE), :],
                                  sem.at[parity]).wait()

    def run(stage, other, parity):
        stage[...] = hf_ref[...]

        def issue(t, _):
            src = stage.at[pl.ds(pl.multiple_of(t * ROW_TILE, ROW_TILE), ROW_TILE), :]
            for k in range(TOP_K):
                row = pl.multiple_of(dest_ref[k * tm + t] * ROW_TILE, ROW_TILE)
                pltpu.make_async_copy(src, xs_hbm.at[pl.ds(row, ROW_TILE), :],
                                      sem.at[parity]).start(priority=k % 2)
            return 0

        lax.fori_loop(0, tm, issue, 0, unroll=2)

        @pl.when(s > 0)
        def _():
            tile_done(other, 1 - parity)

        @pl.when(s == pl.num_programs(0) - 1)
        def _():
            tile_done(stage, parity)

    @pl.when(s % 2 == 0)
    def _():
        run(stage0, stage1, 0)

    @pl.when(s % 2 == 1)
    def _():
        run(stage1, stage0, 1)


def _dispatch(zstart, zflag, n_used, dest_flat, hf_rows, *, tm, bm, n_rows):
    n_tok = hf_rows.shape[0] // ROW_TILE
    grid_spec = pltpu.PrefetchScalarGridSpec(
        num_scalar_prefetch=3,
        grid=(n_tok // tm,),
        in_specs=[
            pl.BlockSpec((tm * TOP_K,), lambda i, *_: (i,), memory_space=pltpu.SMEM),
            pl.BlockSpec((tm * ROW_TILE, LANES), lambda i, *_: (i, 0)),
        ],
        out_specs=pl.BlockSpec(memory_space=pl.ANY),
        scratch_shapes=[pltpu.VMEM((tm * ROW_TILE, LANES), U32),
                        pltpu.VMEM((tm * ROW_TILE, LANES), U32),
                        pltpu.VMEM((bm * ROW_TILE, LANES), U32),
                        pltpu.SemaphoreType.DMA((2,)), pltpu.SemaphoreType.DMA(())],
    )
    return pl.pallas_call(
        functools.partial(_dispatch_kernel, tm=tm, bm=bm),
        grid_spec=grid_spec,
        out_shape=jax.ShapeDtypeStruct((n_rows * ROW_TILE, LANES), U32),
        compiler_params=_params(("arbitrary",)),
        name="dispatch",
    )(zstart, zflag, n_used, dest_flat, hf_rows)


FFN_UP_CHUNKS = 8
FFN_DOWN_CHUNKS = 4


def _expert_kernel(be_ref, valid_ref, src_ref, wslot_ref, next_ref, xs_ref, bgu_ref, bd_ref,
                   wgu_hbm, wd_hbm, ys_hbm, wgu_f32, wd_f32, wgu_bf, wd_bf, ybuf0, ybuf1, slot_v,
                   slot_s, sem_w, sem_y, sem_s, sem_z, *, d_ff, bm, n_tok):
    r = pl.program_id(0)
    valid = valid_ref[r] != 0
    prev_valid = jnp.logical_and(r > 0, valid_ref[jnp.maximum(r - 1, 0)] != 0)
    fresh = jnp.logical_or(r == 0, be_ref[r] != be_ref[jnp.maximum(r - 1, 0)])
    expert = be_ref[r]
    ybufs = (ybuf0, ybuf1)

    def weight_copies(e, wslot):
        return (pltpu.make_async_copy(wgu_hbm.at[e], wgu_f32.at[wslot], sem_w.at[wslot]),
                pltpu.make_async_copy(wd_hbm.at[e], wd_f32.at[wslot], sem_w.at[wslot]))
    d = wd_bf.shape[1]
    n_slots = TOP_K * n_tok
    block_rows = bm * ROW_TILE

    def rows_done(parity):
        return pltpu.make_async_copy(ybufs[parity], ys_hbm.at[pl.ds(0, block_rows), :],
                                     sem_y.at[parity])

    def slots_copy(parity):
        return pltpu.make_async_copy(slot_v.at[0:1, :], slot_s.at[parity:parity + 1, :],
                                     sem_s.at[parity])

    def scatter_rows(prev, lo, hi):
        for i in range(lo, hi):
            dst = pl.multiple_of(slot_s[prev, i] * ROW_TILE, ROW_TILE)
            pltpu.make_async_copy(ybufs[prev].at[pl.ds(i * ROW_TILE, ROW_TILE), :],
                                  ys_hbm.at[pl.ds(dst, ROW_TILE), :],
                                  sem_y.at[prev]).start(priority=i % 2)

    @pl.when(r == 0)
    def _():
        for action in ("start", "wait"):
            for parity in range(2):
                if action == "start":
                    ybufs[parity][...] = jnp.zeros_like(ybufs[parity])
                trash = (n_slots + parity * bm) * ROW_TILE
                getattr(pltpu.make_async_copy(ybufs[parity],
                                              ys_hbm.at[pl.ds(trash, block_rows), :], sem_z),
                        action)()

    for parity in range(2):
        @pl.when(jnp.logical_and(prev_valid, r % 2 == parity))
        def _(parity=parity):
            slots_copy(1 - parity).wait()

    @pl.when(jnp.logical_and(valid, fresh))
    def _():
        wslot = wslot_ref[r]

        @pl.when(r == 0)
        def _():
            for c in weight_copies(expert, wslot):
                c.start()

        for c in weight_copies(expert, wslot):
            c.wait()
        wgu_bf[...] = wgu_f32[wslot].astype(BF16)
        wd_bf[...] = wd_f32[wslot].astype(BF16)

        @pl.when(next_ref[r] >= 0)
        def _():
            for c in weight_copies(next_ref[r], 1 - wslot):
                c.start()

    def ffn(cur, interleave):
        prev = 1 - cur
        ybuf = ybufs[cur]
        bounds = [bm * c // FFN_UP_CHUNKS for c in range(FFN_UP_CHUNKS + 1)]

        meta = xs_ref[pl.ds(META_SUBROW, bm, stride=ROW_TILE), :].astype(I32)
        row = lax.broadcasted_iota(I32, (bm, 1), 0)
        slot = n_slots + cur * bm + row
        for k in range(TOP_K):
            slot = jnp.where(meta[:, 1 + k:2 + k] == be_ref[r], k * n_tok + meta[:, 0:1], slot)
        slots = jnp.broadcast_to(slot.astype(F32), (bm, LANES)).T
        slot_v[...] = slots[0:8, :].astype(I32)
        slots_copy(cur).start()

        words = jnp.concatenate([xs_ref[pl.ds(j, bm, stride=ROW_TILE), :]
                                 for j in range(PACKED_SUBROWS)], axis=1)
        xb = _unpack_bf16_pairs(words)
        up_w = 2 * d_ff // FFN_UP_CHUNKS
        pairs = FFN_UP_CHUNKS // 2
        acts = []
        for c in range(pairs):
            halves = []
            for half in range(2):
                cols = slice(half * d_ff + c * up_w, half * d_ff + (c + 1) * up_w)
                halves.append(jnp.dot(xb, wgu_bf[:, cols], preferred_element_type=F32)
                              + bgu_ref[pl.ds(expert, 1), cols])
                if interleave:
                    step_no = 2 * c + half
                    scatter_rows(prev, bounds[step_no], bounds[step_no + 1])
            glu = jnp.minimum(halves[0], SWIGLU_LIMIT)
            lin = jnp.clip(halves[1], -SWIGLU_LIMIT, SWIGLU_LIMIT)
            acts.append((glu * jax.nn.sigmoid(SWIGLU_ALPHA * glu) * (lin + 1.0)).astype(BF16))
        act = jnp.concatenate(acts, axis=1)

        @pl.when(r >= 2)
        def _():
            rows_done(cur).wait()

        down_w = d // FFN_DOWN_CHUNKS
        for c in range(FFN_DOWN_CHUNKS):
            cols = slice(c * down_w, (c + 1) * down_w)
            y = (jnp.dot(act, wd_bf[:, cols], preferred_element_type=F32)
                 + bd_ref[pl.ds(expert, 1), cols])
            for j in range(down_w // LANES):
                sub = c * (down_w // LANES) + j
                ybuf[pl.ds(sub, bm, stride=ROW_TILE), :] = y[:, j * LANES:(j + 1) * LANES]

    for parity in range(2):
        on_parity = r % 2 == parity

        @pl.when(jnp.logical_and(on_parity, jnp.logical_and(valid, prev_valid)))
        def _(parity=parity):
            ffn(parity, True)

        @pl.when(jnp.logical_and(on_parity, jnp.logical_and(jnp.logical_not(valid), prev_valid)))
        def _(parity=parity):
            scatter_rows(1 - parity, 0, bm)

            @pl.when(r >= 2)
            def _():
                rows_done(parity).wait()

            rows_done(1 - parity).wait()

    @pl.when(r == 0)
    def _():
        ffn(0, False)


def _expert_ffn(blk_expert, blk_valid, blk_src, blk_wslot, blk_next, xs, w_gu, b_gu, w_down,
                b_down, *, bm, n_tok):
    n_grid = blk_expert.shape[0]
    n_exp, d_ff, d = w_down.shape
    assert d == ROW_TILE * LANES
    block_rows = bm * ROW_TILE
    grid_spec = pltpu.PrefetchScalarGridSpec(
        num_scalar_prefetch=5,
        grid=(n_grid,),
        in_specs=[
            pl.BlockSpec((block_rows, LANES), lambda r, be, valid, src, *_: (src[r], 0)),
            pl.BlockSpec((n_exp, 2 * d_ff), lambda r, *_: (0, 0)),
            pl.BlockSpec((n_exp, d), lambda r, *_: (0, 0)),
            pl.BlockSpec(memory_space=pl.ANY),
            pl.BlockSpec(memory_space=pl.ANY),
        ],
        out_specs=pl.BlockSpec(memory_space=pl.ANY),
        scratch_shapes=[pltpu.VMEM((2, d, 2 * d_ff), F32), pltpu.VMEM((2, d_ff, d), F32),
                        pltpu.VMEM((d, 2 * d_ff), BF16), pltpu.VMEM((d_ff, d), BF16),
                        pltpu.VMEM((block_rows, LANES), F32),
                        pltpu.VMEM((block_rows, LANES), F32), pltpu.VMEM((8, bm), I32),
                        pltpu.SMEM((2, bm), I32), pltpu.SemaphoreType.DMA((2,)),
                        pltpu.SemaphoreType.DMA((2,)), pltpu.SemaphoreType.DMA((2,)),
                        pltpu.SemaphoreType.DMA(())],
    )
    return pl.pallas_call(
        functools.partial(_expert_kernel, d_ff=d_ff, bm=bm, n_tok=n_tok),
        grid_spec=grid_spec,
        out_shape=jax.ShapeDtypeStruct(((TOP_K * n_tok + 2 * bm) * ROW_TILE, LANES), F32),
        compiler_params=_params(("arbitrary",)),
        name="expert_ffn",
    )(blk_expert, blk_valid, blk_src, blk_wslot, blk_next, xs, b_gu, b_down, w_gu, w_down)


def _combine_kernel(x_ref, gate_ref, g_ref, y0_ref, y1_ref, y2_ref, y3_ref, o_ref):
    tm, d = x_ref.shape
    gates = gate_ref[...]
    acc = x_ref[...]
    for k, y_ref in enumerate((y0_ref, y1_ref, y2_ref, y3_ref)):
        y = jnp.concatenate([y_ref[pl.ds(j, tm, stride=ROW_TILE), :]
                             for j in range(d // LANES)], axis=1)
        acc = acc + gates[:, k:k + 1] * y
    o_ref[...] = _rms(acc, g_ref[...])


def _combine(x2, gates, g_final, ys, *, tm):
    n_tok, d = x2.shape
    tiles = n_tok // tm
    slot_spec = lambda k: pl.BlockSpec((tm * ROW_TILE, LANES), lambda i: (k * tiles + i, 0))
    return pl.pallas_call(
        _combine_kernel,
        grid=(tiles,),
        in_specs=[
            pl.BlockSpec((tm, d), lambda i: (i, 0)),
            pl.BlockSpec((tm, LANES), lambda i: (i, 0)),
            pl.BlockSpec((1, d), lambda i: (0, 0)),
        ] + [slot_spec(k) for k in range(TOP_K)],
        out_specs=pl.BlockSpec((tm, d), lambda i: (i, 0)),
        out_shape=jax.ShapeDtypeStruct((n_tok, d), F32),
        compiler_params=_params(("arbitrary",)),
        name="combine",
    )(x2, gates, g_final, ys, ys, ys, ys)


def _routing_tables(route, cum, *, tm, bm, n_grid):
    expert = route[0:TOP_K]
    pos = route[TOP_K:2 * TOP_K]
    experts = jnp.arange(N_EXPERTS, dtype=I32)
    counts = cum[-1, 0, :N_EXPERTS].astype(I32)
    padded = (counts + bm - 1) // bm * bm
    pend = jnp.cumsum(padded)
    pstart = pend - padded
    owner_start = jnp.where(expert[None] == experts[:, None, None], pstart[:, None, None], 0)
    dest = pos + jnp.sum(owner_start, axis=0)

    blk = jnp.arange(n_grid, dtype=I32)
    blk_expert = jnp.minimum(jnp.sum(pend[None, :] <= (blk * bm)[:, None], axis=1),
                             N_EXPERTS - 1).astype(I32)
    n_used = pend[-1] // bm
    blk_valid = (blk < n_used).astype(I32)
    blk_src = jnp.minimum(blk, jnp.maximum(n_used - 1, 0)).astype(I32)
    zflag = (padded > 0).astype(I32)
    zstart = jnp.maximum(pend - bm, 0).astype(I32)
    wslot = (jnp.cumsum(zflag) - 1) % 2
    later = jnp.logical_and(experts[None, :] > experts[:, None], zflag[None, :] > 0)
    nxt = jnp.min(jnp.where(later, experts[None, :], N_EXPERTS), axis=1)
    nxt = jnp.where(nxt < N_EXPERTS, nxt, -1)
    owner = blk_expert[:, None] == experts[None, :]
    of_block = lambda v: jnp.sum(jnp.where(owner, v[None, :], 0), axis=1).astype(I32)
    dest = dest.astype(I32).reshape(TOP_K, -1, tm).transpose(1, 0, 2).reshape(-1)
    return (dest, blk_expert, blk_valid, blk_src, of_block(wslot),
            of_block(nxt), zstart, zflag, n_used.astype(I32).reshape(1))


def kernel(x, mem, positions, norm_mix, w_in, lambda_q1, lambda_k1, lambda_q2, lambda_k2,
           diff_subln, w_up_sb, w_up_diff, w_out, norm_mem_q, norm_mem_kv, w_mem_q, w_mem_kv,
           w_mem_o, norm_ffn, w_router, b_router, w_gate_up, b_gate_up, w_down, b_down,
           norm_final):
    b, s, d = x.shape
    n_tok = b * s
    depth = norm_mix.shape[0]
    sb_width = w_up_sb.shape[1]
    diff_width = w_up_diff.shape[1]
    n_in = w_in.shape[2]
    chunk = 512
    assert sb_width == chunk and diff_width == chunk and d == 2 * chunk
    sbq, sbk, sbv, dq, dk, dv = range(6)
    blocks_per_chunk = chunk // LANES
    scale = 1.0 / math.sqrt(SB_HEAD_DIM)
    assert SB_HEAD_DIM == DIFF_HEAD_DIM

    tm_in = min(512, n_tok)
    tq_sb = min(256, s)
    tq_diff = min(512, s)
    tm_post = min(512, s)
    tm_tok = min(256, n_tok)
    bm = 256
    n_grid = (n_tok * TOP_K) // bm + N_EXPERTS

    x2d = x.reshape(n_tok, d)
    for l in range(depth):
        lambda_init = 0.8 - 0.6 * math.exp(-0.3 * l)
        proj = _in_proj(x2d, norm_mix[l].reshape(1, d), w_in[l], positions,
                        tm=tm_in, chunk=chunk, rope_chunks=(dq, dk), scale_chunks=(sbq, dq),
                        scale=scale)
        proj3 = proj.reshape(b, s, n_in)
        o_sb = _sb_attention(proj3, q_col=sbq * blocks_per_chunk, k_col=sbk * blocks_per_chunk,
                             v_col=sbv * blocks_per_chunk, n_pairs=sb_width // LANES, tq=tq_sb)
        o_diff = _diff_attention(
            proj3, lambda_q1[l].reshape(1, -1), lambda_k1[l].reshape(1, -1),
            lambda_q2[l].reshape(1, -1), lambda_k2[l].reshape(1, -1),
            diff_subln[l].reshape(1, -1), q_col=dq * blocks_per_chunk,
            k_col=dk * blocks_per_chunk, v_col=dv * blocks_per_chunk,
            n_heads=diff_width // DIFF_V_DIM, tq=tq_diff, lambda_init=lambda_init)
        kv = _mem_kv(mem, norm_mem_kv[l].reshape(1, d), w_mem_kv[l].astype(BF16))
        w_router_pad = jnp.zeros((d, LANES), F32).at[:, :N_EXPERTS].set(w_router[l])
        b_router_pad = jnp.full((1, LANES), NEG_BIG, F32).at[0, :N_EXPERTS].set(b_router[l])
        x_res, hf, route, gates, cum = _post_mix(
            x2d, proj, o_sb.reshape(n_tok, sb_width), o_diff.reshape(n_tok, diff_width),
            w_up_sb[l].astype(BF16), w_up_diff[l].astype(BF16), w_out[l].astype(BF16),
            norm_mem_q[l].reshape(1, d), w_mem_q[l].astype(BF16), kv, w_mem_o[l].astype(BF16),
            norm_ffn[l].reshape(1, d), w_router_pad, b_router_pad,
            tm=tm_post, seq=s, gate_col=6 * chunk // d)
        (dest, blk_expert, blk_valid, blk_src, blk_wslot, blk_next, zstart, zflag,
         n_used) = _routing_tables(route, cum, tm=tm_tok, bm=bm, n_grid=n_grid + 1)
        xs = _dispatch(zstart, zflag, n_used, dest, hf, tm=tm_tok, bm=bm, n_rows=n_grid * bm)
        ys = _expert_ffn(blk_expert, blk_valid, blk_src, blk_wslot, blk_next, xs, w_gate_up[l],
                         b_gate_up[l], w_down[l], b_down[l], bm=bm, n_tok=n_tok)
        g_next = norm_final if l == depth - 1 else None
        assert g_next is not None, "only depth 1 is wired: the final norm is fused into combine"
        x2d = _combine(x_res, gates, g_next.reshape(1, d), ys, tm=tm_post)
    return x2d.reshape(b, s, d)
```

```python
import functools
import math

import jax
import jax.numpy as jnp
from jax import lax
from jax.experimental import pallas as pl
from jax.experimental.pallas import tpu as pltpu

F32 = jnp.float32
BF16 = jnp.bfloat16
I32 = jnp.int32
U32 = jnp.uint32

LANES = 128
SUBLANES = 8
VMEM_LIMIT_BYTES = 56 * 1024 * 1024
ROW_TILE = 8
PACKED_SUBROWS = 4
META_SUBROW = 4

NORM_EPS = 1e-6
ROPE_THETA = 10000.0
CHUNK = 64
SB_HEAD_DIM = 64
DIFF_HEAD_DIM = 64
DIFF_V_DIM = 128
MEM_HEADS = 4
N_EXPERTS = 32
TOP_K = 4
SWIGLU_LIMIT = 7.0
SWIGLU_ALPHA = 1.702

SB_DEAD_LOG = -105.0
NEG_BIG = -1e30


def _params(semantics):
    return pltpu.CompilerParams(dimension_semantics=semantics,
                                vmem_limit_bytes=VMEM_LIMIT_BYTES)


def _const_spec(shape):
    nd = len(shape)
    return pl.BlockSpec(shape, lambda *_: (0,) * nd)


def _rms(x, g):
    return x * lax.rsqrt(jnp.mean(x * x, axis=-1, keepdims=True) + NORM_EPS) * g


def _inproj_kernel(x_ref, g_ref, w32_ref, pos_ref, inv_ref, o_ref, w_ref, *, chunk, rope_chunks,
                   scale_chunks, scale):
    @pl.when(pl.program_id(0) == 0)
    def _():
        w_ref[...] = w32_ref[...].astype(BF16)

    tm = x_ref.shape[0]
    h = _rms(x_ref[...], g_ref[...]).astype(BF16)
    lane = lax.broadcasted_iota(I32, (1, chunk), 1)
    first_half = (lane % DIFF_HEAD_DIM) < (DIFF_HEAD_DIM // 2)
    pos = jnp.broadcast_to(pos_ref[0], (SUBLANES, tm)).T[:, 0:1]
    ang = pos * inv_ref[...]
    cos_t = jnp.cos(ang)
    sin_t = jnp.where(first_half[:, :LANES], -jnp.sin(ang), jnp.sin(ang))
    for c in range(w_ref.shape[1] // chunk):
        cols = slice(c * chunk, (c + 1) * chunk)
        acc = jnp.dot(h, w_ref[:, cols], preferred_element_type=F32)
        if c in rope_chunks:
            cos = jnp.tile(cos_t, (1, chunk // LANES))
            sin = jnp.tile(sin_t, (1, chunk // LANES))
            partner = jnp.where(first_half,
                                pltpu.roll(acc, chunk - DIFF_HEAD_DIM // 2, 1),
                                pltpu.roll(acc, DIFF_HEAD_DIM // 2, 1))
            acc = acc * cos + partner * sin
        if c in scale_chunks:
            acc = acc * scale
        o_ref[:, cols] = acc.astype(BF16)


def _in_proj(x2d, g, w, positions, *, tm, chunk, rope_chunks, scale_chunks, scale):
    n_tok, d = x2d.shape
    half = DIFF_HEAD_DIM // 2
    inv_freq = ROPE_THETA ** (-jnp.arange(half, dtype=F32) / half)
    inv_freq = jnp.tile(inv_freq, LANES // half).reshape(1, LANES)
    pos = positions.reshape(n_tok // tm, 1, tm).astype(F32)
    n_in = w.shape[1]
    kern = functools.partial(_inproj_kernel, chunk=chunk, rope_chunks=rope_chunks,
                             scale_chunks=scale_chunks, scale=scale)
    return pl.pallas_call(
        kern,
        grid=(n_tok // tm,),
        in_specs=[pl.BlockSpec((tm, d), lambda i: (i, 0)),
                  _const_spec((1, d)),
                  _const_spec((d, n_in)),
                  pl.BlockSpec((1, 1, tm), lambda i: (i, 0, 0)),
                  _const_spec((1, LANES))],
        out_specs=pl.BlockSpec((tm, n_in), lambda i: (i, 0)),
        out_shape=jax.ShapeDtypeStruct((n_tok, n_in), BF16),
        scratch_shapes=[pltpu.VMEM((d, n_in), BF16)],
        compiler_params=_params(("arbitrary",)),
        name="in_proj",
    )(x2d, g, w, pos, inv_freq)


def _sb_kernel(q_ref, k_ref, v_ref, o_ref, acc_ref, lw0, lw1, cat0, cat1, *, tq):
    i = pl.program_id(2)
    q = q_ref[0]
    lane = lax.broadcasted_iota(I32, (1, LANES), 1)
    zero = jnp.zeros_like(q)
    q2 = jnp.concatenate([jnp.where(lane < SB_HEAD_DIM, q, zero),
                          jnp.where(lane >= SB_HEAD_DIM, q, zero)], axis=0)
    uj = lax.broadcasted_iota(I32, (2 * tq, tq), 0) % tq
    us = lax.broadcasted_iota(I32, (2 * tq, tq), 1)
    suffix = jnp.where(uj > us, 1.0, 0.0).astype(BF16)
    lws = (lw0, lw1)
    cats = (cat0, cat1)

    def causal_mask():
        return (lax.broadcasted_iota(I32, (2 * tq, tq), 1)
                < lax.broadcasted_iota(I32, (2 * tq, tq), 0) % tq)

    def logits(j, buf, masked):
        start = pl.multiple_of(j * tq, tq)
        kj = k_ref[0, pl.ds(start, tq), :]
        z = lax.dot_general(q2, kj, (((1,), (1,)), ((), ())), preferred_element_type=F32)
        sp = jnp.maximum(z, 0.0) + jnp.log(1.0 + jnp.exp(-jnp.abs(z)))
        log_rem = -sp
        if masked:
            log_rem = jnp.where(causal_mask(), log_rem, 0.0)
        lws[buf][...] = z - sp
        hi = log_rem.astype(BF16)
        cats[buf][:, 0:tq] = hi
        cats[buf][:, tq:2 * tq] = (log_rem - hi.astype(F32)).astype(BF16)
        return log_rem[:, 0:1]

    def suffix_sums(buf, first):
        after = jnp.dot(cats[buf][...], suffix, preferred_element_type=F32)
        lws[buf][...] += after
        return after[:, 0:1] + first

    def weigh(j, buf, carry, masked):
        start = pl.multiple_of(j * tq, tq)
        vj = v_ref[0, pl.ds(start, tq), :]
        w = jnp.exp(lws[buf][...] + carry)
        if masked:
            w = jnp.where(causal_mask(), w, 0.0)
        acc_ref[...] += jnp.dot(w.astype(BF16), vj, preferred_element_type=F32)

    def earlier_blocks(first_block, carry):
        def cond(state):
            j, _, alive = state
            return jnp.logical_and(j >= 0, alive > SB_DEAD_LOG)

        def body(state):
            j, carry, _ = state
            total = suffix_sums(0, logits(j, 0, False))
            weigh(j, 0, carry, False)
            carry = carry + total
            return j - 1, carry, jnp.max(carry)

        lax.while_loop(cond, body, (first_block, carry, jnp.max(carry)))

    acc_ref[...] = jnp.zeros_like(acc_ref)
    no_carry = jnp.zeros((2 * tq, 1), F32)

    @pl.when(i == 0)
    def _():
        suffix_sums(0, logits(0, 0, True))
        weigh(0, 0, no_carry, True)

    @pl.when(i > 0)
    def _():
        first_diag = logits(i, 0, True)
        first_prev = logits(i - 1, 1, False)
        total_diag = suffix_sums(0, first_diag)
        total_prev = suffix_sums(1, first_prev)
        weigh(i, 0, no_carry, True)
        weigh(i - 1, 1, total_diag, False)
        earlier_blocks(i - 2, total_diag + total_prev)

    o_ref[0] = jnp.where(lane < SB_HEAD_DIM, acc_ref[0:tq, :], acc_ref[tq:2 * tq, :]).astype(BF16)


def _sb_attention(proj3, *, q_col, k_col, v_col, n_pairs, tq):
    b, s, _ = proj3.shape
    return pl.pallas_call(
        functools.partial(_sb_kernel, tq=tq),
        grid=(b, n_pairs, s // tq),
        in_specs=[pl.BlockSpec((1, tq, LANES), lambda bi, p, i: (bi, i, q_col + p)),
                  pl.BlockSpec((1, s, LANES), lambda bi, p, i: (bi, 0, k_col + p)),
                  pl.BlockSpec((1, s, LANES), lambda bi, p, i: (bi, 0, v_col + p))],
        out_specs=pl.BlockSpec((1, tq, LANES), lambda bi, p, i: (bi, i, p)),
        out_shape=jax.ShapeDtypeStruct((b, s, n_pairs * LANES), BF16),
        scratch_shapes=[pltpu.VMEM((2 * tq, LANES), F32),
                        pltpu.VMEM((2 * tq, tq), F32), pltpu.VMEM((2 * tq, tq), F32),
                        pltpu.VMEM((2 * tq, 2 * tq), BF16), pltpu.VMEM((2 * tq, 2 * tq), BF16)],
        compiler_params=_params(("arbitrary", "arbitrary", "arbitrary")),
        name="sb_attention",
    )(proj3, proj3, proj3)


DIFF_ONES_ROWS = 16


def _diff_kernel(lq1_ref, lk1_ref, lq2_ref, lk2_ref, q_ref, k_ref, v_ref, g_ref, o_ref, vt_ref,
                 acc_ref, z0_ref, z1_ref, *, tq, lambda_init):
    i = pl.program_id(2)
    s_len = v_ref.shape[1]
    vd = v_ref.shape[2]
    half = tq // 2
    zs = (z0_ref, z1_ref)

    @pl.when(i == 0)
    def _():
        for c in range(s_len // tq):
            cols = slice(c * tq, (c + 1) * tq)
            vt_ref[0:vd, cols] = v_ref[0, cols, :].astype(F32).T.astype(BF16)
        rid = lax.broadcasted_iota(I32, (DIFF_ONES_ROWS, s_len), 0)
        vt_ref[vd:vd + DIFF_ONES_ROWS, :] = jnp.where(rid == 0, 1.0, 0.0).astype(BF16)

    q = q_ref[0]
    lane = lax.broadcasted_iota(I32, (1, LANES), 1)
    zero = jnp.zeros_like(q)
    q2 = jnp.concatenate([jnp.where(lane < DIFF_HEAD_DIM, q, zero),
                          jnp.where(lane >= DIFF_HEAD_DIM, q, zero)], axis=0)

    def scores(j, h, masked):
        start = pl.multiple_of(j * tq + h * half, half)
        kj = k_ref[0, pl.ds(start, half), :]
        zt = lax.dot_general(kj, q2, (((1,), (1,)), ((), ())), preferred_element_type=F32)
        if masked:
            qidx = lax.broadcasted_iota(I32, (half, 2 * tq), 1) % tq
            kidx = lax.broadcasted_iota(I32, (half, 2 * tq), 0) + h * half
            zt = jnp.where(kidx // CHUNK <= qidx // CHUNK, zt, -jnp.inf)
        zs[h][...] = zt
        return jnp.max(zt, axis=0, keepdims=True)

    def values(j, h, m, mblk):
        start = pl.multiple_of(j * tq + h * half, half)
        vtj = vt_ref[:, pl.ds(start, half)]
        m_new = jnp.maximum(m, mblk)
        pt = jnp.exp(zs[h][...] - m_new).astype(BF16)
        acc_ref[...] = jnp.exp(m - m_new) * acc_ref[...] + jnp.dot(vtj, pt,
                                                                   preferred_element_type=F32)
        return m_new

    def block(j, m, mb0, masked, next_kind):
        mb1 = scores(j, 1, masked)
        m = values(j, 0, m, mb0)
        nb0 = mb0 if next_kind is None else scores(j + 1, 0, next_kind == "masked")
        m = values(j, 1, m, mb1)
        return m, nb0

    acc_ref[...] = jnp.zeros_like(acc_ref)
    m0 = jnp.full((1, 2 * tq), -jnp.inf, F32)

    @pl.when(i == 0)
    def _():
        block(0, m0, scores(0, 0, True), True, None)

    @pl.when(i > 0)
    def _():
        state = (m0, scores(0, 0, False))
        state = lax.fori_loop(0, i - 1, lambda j, st: block(j, st[0], st[1], False, "plain"),
                              state)
        state = block(i - 1, state[0], state[1], False, "masked")
        block(i, state[0], state[1], True, None)

    lam = (jnp.exp(jnp.sum(lq1_ref[...] * lk1_ref[...], axis=1, keepdims=True))
           - jnp.exp(jnp.sum(lq2_ref[...] * lk2_ref[...], axis=1, keepdims=True))
           + lambda_init)
    ot = acc_ref[0:vd, :] / acc_ref[vd:vd + 1, :]
    ot = ot[:, 0:tq] - lam * ot[:, tq:2 * tq]
    o_ref[0] = (_rms(ot.T, g_ref[...]) * (1.0 - lambda_init)).astype(BF16)


def _diff_attention(proj3, lq1, lk1, lq2, lk2, subln, *, q_col, k_col, v_col, n_heads, tq,
                    lambda_init):
    b, s, _ = proj3.shape
    lam_spec = _const_spec((1, DIFF_HEAD_DIM))
    return pl.pallas_call(
        functools.partial(_diff_kernel, tq=tq, lambda_init=lambda_init),
        grid=(b, n_heads, s // tq),
        in_specs=[lam_spec, lam_spec, lam_spec, lam_spec,
                  pl.BlockSpec((1, tq, LANES), lambda bi, h, i: (bi, i, q_col + h)),
                  pl.BlockSpec((1, s, LANES), lambda bi, h, i: (bi, 0, k_col + h)),
                  pl.BlockSpec((1, s, LANES), lambda bi, h, i: (bi, 0, v_col + h)),
                  _const_spec((1, DIFF_V_DIM))],
        out_specs=pl.BlockSpec((1, tq, LANES), lambda bi, h, i: (bi, i, h)),
        out_shape=jax.ShapeDtypeStruct((b, s, n_heads * DIFF_V_DIM), BF16),
        scratch_shapes=[pltpu.VMEM((DIFF_V_DIM + DIFF_ONES_ROWS, s), BF16),
                        pltpu.VMEM((DIFF_V_DIM + DIFF_ONES_ROWS, 2 * tq), F32),
                        pltpu.VMEM((tq // 2, 2 * tq), F32), pltpu.VMEM((tq // 2, 2 * tq), F32)],
        compiler_params=_params(("arbitrary", "arbitrary", "arbitrary")),
        name="diff_attention",
    )(lq1, lk1, lq2, lk2, proj3, proj3, proj3, subln)


def _mem_kv_kernel(mem_ref, g_ref, w_ref, o_ref):
    h = _rms(mem_ref[0], g_ref[...]).astype(BF16)
    o_ref[0] = jnp.dot(h, w_ref[...], preferred_element_type=F32).astype(BF16)


def _mem_kv(mem, g, w_bf16):
    b, m, d = mem.shape
    n = w_bf16.shape[1]
    return pl.pallas_call(
        _mem_kv_kernel,
        grid=(b,),
        in_specs=[pl.BlockSpec((1, m, d), lambda i: (i, 0, 0)), _const_spec((1, d)),
                  _const_spec((d, n))],
        out_specs=pl.BlockSpec((1, m, n), lambda i: (i, 0, 0)),
        out_shape=jax.ShapeDtypeStruct((b, m, n), BF16),
        compiler_params=_params(("arbitrary",)),
        name="mem_kv",
    )(mem, g, w_bf16)


def _split_bf16(v):
    hi = v.astype(BF16)
    return hi, (v - hi.astype(F32)).astype(BF16)


def _pack_bf16_pairs(v):
    half = v.shape[1] // 2
    bits = lax.bitcast_convert_type(v.astype(BF16).astype(F32), U32)
    return (bits[:, :half] >> 16) | (bits[:, half:] & jnp.uint32(0xFFFF0000))


def _unpack_bf16_pairs(w):
    lo = lax.bitcast_convert_type(w << 16, F32)
    hi = lax.bitcast_convert_type(w & jnp.uint32(0xFFFF0000), F32)
    return jnp.concatenate([lo, hi], axis=1).astype(BF16)


def _postmix_kernel(x_ref, osb_ref, odf_ref, gsb_ref, gdf_ref, wus_ref, wud_ref, wout_ref,
                    gq_ref, wq_ref, kv_ref, wo_ref, gf_ref, wr_ref, br_ref,
                    x2_ref, hf_ref, route_ref, gate_ref, cum_ref, count_ref, *, tm, d_model):
    step = pl.program_id(0)

    @pl.when(step == 0)
    def _():
        count_ref[...] = jnp.zeros_like(count_ref)

    y_sb = jnp.dot(osb_ref[...], wus_ref[...], preferred_element_type=F32)
    y_df = jnp.dot(odf_ref[...], wud_ref[...], preferred_element_type=F32)
    mixed = (jax.nn.sigmoid(gsb_ref[...].astype(F32)) * y_sb
             + jax.nn.sigmoid(gdf_ref[...].astype(F32)) * y_df)
    x1 = x_ref[...] + jnp.dot(mixed.astype(BF16), wout_ref[...], preferred_element_type=F32)

    hq = _rms(x1, gq_ref[...]).astype(BF16)
    hd = d_model // MEM_HEADS
    q = jnp.dot(hq, wq_ref[...], preferred_element_type=F32) * (1.0 / math.sqrt(hd))
    q = q.astype(BF16)
    heads = []
    for h in range(MEM_HEADS):
        kh = kv_ref[0, :, h * hd:(h + 1) * hd]
        vh = kv_ref[0, :, d_model + h * hd:d_model + (h + 1) * hd]
        z = lax.dot_general(q[:, h * hd:(h + 1) * hd], kh, (((1,), (1,)), ((), ())),
                            preferred_element_type=F32)
        p = jnp.exp(z - jnp.max(z, axis=1, keepdims=True))
        l = jnp.sum(p, axis=1, keepdims=True)
        heads.append((jnp.dot(p.astype(BF16), vh, preferred_element_type=F32) / l).astype(BF16))
    x2 = x1 + jnp.dot(jnp.concatenate(heads, axis=1), wo_ref[...], preferred_element_type=F32)
    x2_ref[...] = x2

    hf = _rms(x2, gf_ref[...])
    packed = _pack_bf16_pairs(hf)
    for j in range(PACKED_SUBROWS):
        hf_ref[pl.ds(j, tm, stride=ROW_TILE), :] = packed[:, j * LANES:(j + 1) * LANES]
    h_hi, h_lo = _split_bf16(hf)
    w_hi, w_lo = _split_bf16(wr_ref[...])
    logits = (jnp.dot(h_hi, w_hi, preferred_element_type=F32)
              + jnp.dot(h_hi, w_lo, preferred_element_type=F32)
              + jnp.dot(h_lo, w_hi, preferred_element_type=F32)) + br_ref[...]
    lane = lax.broadcasted_iota(I32, (tm, LANES), 1)
    work = logits
    vals, idxs, hots = [], [], []
    for _ in range(TOP_K):
        mx = jnp.max(work, axis=1, keepdims=True)
        idx = jnp.min(jnp.where(work == mx, lane, LANES), axis=1, keepdims=True)
        hot = lane == idx
        work = jnp.where(hot, NEG_BIG, work)
        vals.append(mx)
        idxs.append(idx)
        hots.append(hot)
    exps = [jnp.exp(v - vals[0]) for v in vals]
    denom = exps[0] + exps[1] + exps[2] + exps[3]

    onehot_sum = jnp.zeros((tm, LANES), F32)
    for hot in hots:
        onehot_sum = onehot_sum + jnp.where(hot, 1.0, 0.0)
    r = lax.broadcasted_iota(I32, (tm, tm), 0)
    c = lax.broadcasted_iota(I32, (tm, tm), 1)
    lower = jnp.where(c < r, 1.0, 0.0).astype(BF16)
    rank = jnp.dot(lower, onehot_sum.astype(BF16), preferred_element_type=F32) + count_ref[...]
    route = jnp.zeros((tm, LANES), I32)
    gates = jnp.zeros((tm, LANES), F32)
    for k in range(TOP_K):
        pos = jnp.sum(jnp.where(hots[k], rank, 0.0), axis=1, keepdims=True).astype(I32)
        route = jnp.where(lane == k, idxs[k], route)
        route = jnp.where(lane == TOP_K + k, pos, route)
        gates = jnp.where(lane == k, exps[k] / denom, gates)
    route_ref[...] = route.astype(F32).T[0:2 * TOP_K, :].astype(I32)
    gate_ref[...] = gates
    meta = jnp.where(lane == 0, step * tm + lax.broadcasted_iota(I32, (tm, LANES), 0), 0)
    for k in range(TOP_K):
        meta = jnp.where(lane == 1 + k, idxs[k], meta)
    hf_ref[pl.ds(META_SUBROW, tm, stride=ROW_TILE), :] = meta.astype(U32)
    for j in range(META_SUBROW + 1, ROW_TILE):
        hf_ref[pl.ds(j, tm, stride=ROW_TILE), :] = jnp.zeros((tm, LANES), U32)
    count_ref[...] = count_ref[...] + jnp.sum(onehot_sum, axis=0, keepdims=True)
    cum_ref[0] = jnp.broadcast_to(count_ref[...], (SUBLANES, LANES))


def _post_mix(x2d, proj2, o_sb, o_diff, w_up_sb, w_up_diff, w_out, g_memq, w_memq, kv, w_memo,
              g_ffn, w_router_pad, b_router_pad, *, tm, seq, gate_col):
    n_tok, d = x2d.shape
    n_tiles = n_tok // tm
    tiles_per_batch = seq // tm
    row = lambda i: (i, 0)
    in_specs = [
        pl.BlockSpec((tm, d), row),
        pl.BlockSpec((tm, o_sb.shape[1]), row),
        pl.BlockSpec((tm, o_diff.shape[1]), row),
        pl.BlockSpec((tm, d), lambda i: (i, gate_col)),
        pl.BlockSpec((tm, d), lambda i: (i, gate_col + 1)),
        _const_spec(w_up_sb.shape), _const_spec(w_up_diff.shape), _const_spec(w_out.shape),
        _const_spec((1, d)), _const_spec(w_memq.shape),
        pl.BlockSpec((1,) + kv.shape[1:], lambda i: (i // tiles_per_batch, 0, 0)),
        _const_spec(w_memo.shape), _const_spec((1, d)),
        _const_spec(w_router_pad.shape), _const_spec((1, LANES)),
    ]
    out_specs = [
        pl.BlockSpec((tm, d), row),
        pl.BlockSpec((tm * ROW_TILE, LANES), row),
        pl.BlockSpec((2 * TOP_K, tm), lambda i: (0, i)),
        pl.BlockSpec((tm, LANES), row),
        pl.BlockSpec((1, SUBLANES, LANES), lambda i: (i, 0, 0)),
    ]
    assert d // 2 == PACKED_SUBROWS * LANES
    out_shape = [
        jax.ShapeDtypeStruct((n_tok, d), F32),
        jax.ShapeDtypeStruct((n_tok * ROW_TILE, LANES), U32),
        jax.ShapeDtypeStruct((2 * TOP_K, n_tok), I32),
        jax.ShapeDtypeStruct((n_tok, LANES), F32),
        jax.ShapeDtypeStruct((n_tiles, SUBLANES, LANES), F32),
    ]
    return pl.pallas_call(
        functools.partial(_postmix_kernel, tm=tm, d_model=d),
        grid=(n_tiles,),
        in_specs=in_specs,
        out_specs=out_specs,
        out_shape=out_shape,
        scratch_shapes=[pltpu.VMEM((1, LANES), F32)],
        compiler_params=_params(("arbitrary",)),
        name="post_mix",
    )(x2d, o_sb, o_diff, proj2, proj2, w_up_sb, w_up_diff, w_out, g_memq, w_memq, kv, w_memo,
      g_ffn, w_router_pad, b_router_pad)


def _dispatch_kernel(zstart_ref, zflag_ref, nused_ref, dest_ref, hf_ref, xs_hbm, stage0, stage1,
                     zbuf, sem, zsem, *, tm, bm):
    s = pl.program_id(0)

    @pl.when(s == 0)
    def _():
        lane = lax.broadcasted_iota(I32, (bm, LANES), 1)
        zbuf[...] = jnp.zeros_like(zbuf)
        zbuf[pl.ds(META_SUBROW, bm, stride=ROW_TILE), :] = jnp.where(
            jnp.logical_and(lane >= 1, lane <= TOP_K), -1, 0).astype(U32)

        def fill_copy(start):
            start = pl.multiple_of(start * ROW_TILE, bm * ROW_TILE)
            return pltpu.make_async_copy(zbuf, xs_hbm.at[pl.ds(start, bm * ROW_TILE), :], zsem)

        n_blocks = xs_hbm.shape[0] // (bm * ROW_TILE)
        for action in ("start", "wait"):
            for e in range(N_EXPERTS):
                @pl.when(zflag_ref[e] != 0)
                def _(e=e, action=action):
                    getattr(fill_copy(zstart_ref[e]), action)()

            def trailing(blk, _, action=action):
                getattr(fill_copy(blk * bm), action)()
                return 0

            lax.fori_loop(nused_ref[0], n_blocks, trailing, 0)

    def tile_done(stage, parity):
        for _ in range(TOP_K):
            pltpu.make_async_copy(stage, xs_hbm.at[pl.ds(0, tm * ROW_TILE), :],
                                  sem.at[parity]).wait()

    def run(stage, other, parity):
        stage[...] = hf_ref[...]

        def issue(t, _):
            src = stage.at[pl.ds(pl.multiple_of(t * ROW_TILE, ROW_TILE), ROW_TILE), :]
            for k in range(TOP_K):
                row = pl.multiple_of(dest_ref[k * tm + t] * ROW_TILE, ROW_TILE)
                pltpu.make_async_copy(src, xs_hbm.at[pl.ds(row, ROW_TILE), :],
                                      sem.at[parity]).start(priority=k % 2)
            return 0

        lax.fori_loop(0, tm, issue, 0, unroll=2)

        @pl.when(s > 0)
        def _():
            tile_done(other, 1 - parity)

        @pl.when(s == pl.num_programs(0) - 1)
        def _():
            tile_done(stage, parity)

    @pl.when(s % 2 == 0)
    def _():
        run(stage0, stage1, 0)

    @pl.when(s % 2 == 1)
    def _():
        run(stage1, stage0, 1)


def _dispatch(zstart, zflag, n_used, dest_flat, hf_rows, *, tm, bm, n_rows):
    n_tok = hf_rows.shape[0] // ROW_TILE
    grid_spec = pltpu.PrefetchScalarGridSpec(
        num_scalar_prefetch=3,
        grid=(n_tok // tm,),
        in_specs=[
            pl.BlockSpec((tm * TOP_K,), lambda i, *_: (i,), memory_space=pltpu.SMEM),
            pl.BlockSpec((tm * ROW_TILE, LANES), lambda i, *_: (i, 0)),
        ],
        out_specs=pl.BlockSpec(memory_space=pl.ANY),
        scratch_shapes=[pltpu.VMEM((tm * ROW_TILE, LANES), U32),
                        pltpu.VMEM((tm * ROW_TILE, LANES), U32),
                        pltpu.VMEM((bm * ROW_TILE, LANES), U32),
                        pltpu.SemaphoreType.DMA((2,)), pltpu.SemaphoreType.DMA(())],
    )
    return pl.pallas_call(
        functools.partial(_dispatch_kernel, tm=tm, bm=bm),
        grid_spec=grid_spec,
        out_shape=jax.ShapeDtypeStruct((n_rows * ROW_TILE, LANES), U32),
        compiler_params=_params(("arbitrary",)),
        name="dispatch",
    )(zstart, zflag, n_used, dest_flat, hf_rows)


FFN_UP_CHUNKS = 8
FFN_DOWN_CHUNKS = 4


def _expert_kernel(be_ref, valid_ref, src_ref, wslot_ref, next_ref, xs_ref, bgu_ref, bd_ref,
                   wgu_hbm, wd_hbm, ys_hbm, wgu_f32, wd_f32, wgu_bf, wd_bf, ybuf0, ybuf1, slot_v,
                   slot_s, sem_w, sem_y, sem_s, sem_z, *, d_ff, bm, n_tok):
    r = pl.program_id(0)
    valid = valid_ref[r] != 0
    prev_valid = jnp.logical_and(r > 0, valid_ref[jnp.maximum(r - 1, 0)] != 0)
    fresh = jnp.logical_or(r == 0, be_ref[r] != be_ref[jnp.maximum(r - 1, 0)])
    expert = be_ref[r]
    ybufs = (ybuf0, ybuf1)

    def weight_copies(e, wslot):
        return (pltpu.make_async_copy(wgu_hbm.at[e], wgu_f32.at[wslot], sem_w.at[wslot]),
                pltpu.make_async_copy(wd_hbm.at[e], wd_f32.at[wslot], sem_w.at[wslot]))
    d = wd_bf.shape[1]
    n_slots = TOP_K * n_tok
    block_rows = bm * ROW_TILE

    def rows_done(parity):
        return pltpu.make_async_copy(ybufs[parity], ys_hbm.at[pl.ds(0, block_rows), :],
                                     sem_y.at[parity])

    def slots_copy(parity):
        return pltpu.make_async_copy(slot_v.at[0:1, :], slot_s.at[parity:parity + 1, :],
                                     sem_s.at[parity])

    def scatter_rows(prev, lo, hi):
        for i in range(lo, hi):
            dst = pl.multiple_of(slot_s[prev, i] * ROW_TILE, ROW_TILE)
            pltpu.make_async_copy(ybufs[prev].at[pl.ds(i * ROW_TILE, ROW_TILE), :],
                                  ys_hbm.at[pl.ds(dst, ROW_TILE), :],
                                  sem_y.at[prev]).start(priority=i % 2)

    @pl.when(r == 0)
    def _():
        for action in ("start", "wait"):
            for parity in range(2):
                if action == "start":
                    ybufs[parity][...] = jnp.zeros_like(ybufs[parity])
                trash = (n_slots + parity * bm) * ROW_TILE
                getattr(pltpu.make_async_copy(ybufs[parity],
                                              ys_hbm.at[pl.ds(trash, block_rows), :], sem_z),
                        action)()

    for parity in range(2):
        @pl.when(jnp.logical_and(prev_valid, r % 2 == parity))
        def _(parity=parity):
            slots_copy(1 - parity).wait()

    @pl.when(jnp.logical_and(valid, fresh))
    def _():
        wslot = wslot_ref[r]

        @pl.when(r == 0)
        def _():
            for c in weight_copies(expert, wslot):
                c.start()

        for c in weight_copies(expert, wslot):
            c.wait()
        wgu_bf[...] = wgu_f32[wslot].astype(BF16)
        wd_bf[...] = wd_f32[wslot].astype(BF16)

        @pl.when(next_ref[r] >= 0)
        def _():
            for c in weight_copies(next_ref[r], 1 - wslot):
                c.start()

    def ffn(cur, interleave):
        prev = 1 - cur
        ybuf = ybufs[cur]
        bounds = [bm * c // FFN_UP_CHUNKS for c in range(FFN_UP_CHUNKS + 1)]

        meta = xs_ref[pl.ds(META_SUBROW, bm, stride=ROW_TILE), :].astype(I32)
        row = lax.broadcasted_iota(I32, (bm, 1), 0)
        slot = n_slots + cur * bm + row
        for k in range(TOP_K):
            slot = jnp.where(meta[:, 1 + k:2 + k] == be_ref[r], k * n_tok + meta[:, 0:1], slot)
        slots = jnp.broadcast_to(slot.astype(F32), (bm, LANES)).T
        slot_v[...] = slots[0:SUBLANES, :].astype(I32)
        slots_copy(cur).start()

        words = jnp.concatenate([xs_ref[pl.ds(j, bm, stride=ROW_TILE), :]
                                 for j in range(PACKED_SUBROWS)], axis=1)
        xb = _unpack_bf16_pairs(words)
        up_w = 2 * d_ff // FFN_UP_CHUNKS
        pairs = FFN_UP_CHUNKS // 2
        acts = []
        for c in range(pairs):
            halves = []
            for half in range(2):
                cols = slice(half * d_ff + c * up_w, half * d_ff + (c + 1) * up_w)
                halves.append(jnp.dot(xb, wgu_bf[:, cols], preferred_element_type=F32)
                              + bgu_ref[pl.ds(expert, 1), cols])
                if interleave:
                    step_no = 2 * c + half
                    scatter_rows(prev, bounds[step_no], bounds[step_no + 1])
            glu = jnp.minimum(halves[0], SWIGLU_LIMIT)
            lin = jnp.clip(halves[1], -SWIGLU_LIMIT, SWIGLU_LIMIT)
            acts.append((glu * jax.nn.sigmoid(SWIGLU_ALPHA * glu) * (lin + 1.0)).astype(BF16))
        act = jnp.concatenate(acts, axis=1)

        @pl.when(r >= 2)
        def _():
            rows_done(cur).wait()

        down_w = d // FFN_DOWN_CHUNKS
        for c in range(FFN_DOWN_CHUNKS):
            cols = slice(c * down_w, (c + 1) * down_w)
            y = (jnp.dot(act, wd_bf[:, cols], preferred_element_type=F32)
                 + bd_ref[pl.ds(expert, 1), cols])
            for j in range(down_w // LANES):
                sub = c * (down_w // LANES) + j
                ybuf[pl.ds(sub, bm, stride=ROW_TILE), :] = y[:, j * LANES:(j + 1) * LANES]

    for parity in range(2):
        on_parity = r % 2 == parity

        @pl.when(jnp.logical_and(on_parity, jnp.logical_and(valid, prev_valid)))
        def _(parity=parity):
            ffn(parity, True)

        @pl.when(jnp.logical_and(on_parity, jnp.logical_and(jnp.logical_not(valid), prev_valid)))
        def _(parity=parity):
            scatter_rows(1 - parity, 0, bm)

            @pl.when(r >= 2)
            def _():
                rows_done(parity).wait()

            rows_done(1 - parity).wait()

    @pl.when(r == 0)
    def _():
        ffn(0, False)


def _expert_ffn(blk_expert, blk_valid, blk_src, blk_wslot, blk_next, xs, w_gu, b_gu, w_down,
                b_down, *, bm, n_tok):
    n_grid = blk_expert.shape[0]
    n_exp, d_ff, d = w_down.shape
    assert d == ROW_TILE * LANES
    block_rows = bm * ROW_TILE
    grid_spec = pltpu.PrefetchScalarGridSpec(
        num_scalar_prefetch=5,
        grid=(n_grid,),
        in_specs=[
            pl.BlockSpec((block_rows, LANES), lambda r, be, valid, src, *_: (src[r], 0)),
            pl.BlockSpec((n_exp, 2 * d_ff), lambda r, *_: (0, 0)),
            pl.BlockSpec((n_exp, d), lambda r, *_: (0, 0)),
            pl.BlockSpec(memory_space=pl.ANY),
            pl.BlockSpec(memory_space=pl.ANY),
        ],
        out_specs=pl.BlockSpec(memory_space=pl.ANY),
        scratch_shapes=[pltpu.VMEM((2, d, 2 * d_ff), F32), pltpu.VMEM((2, d_ff, d), F32),
                        pltpu.VMEM((d, 2 * d_ff), BF16), pltpu.VMEM((d_ff, d), BF16),
                        pltpu.VMEM((block_rows, LANES), F32),
                        pltpu.VMEM((block_rows, LANES), F32), pltpu.VMEM((SUBLANES, bm), I32),
                        pltpu.SMEM((2, bm), I32), pltpu.SemaphoreType.DMA((2,)),
                        pltpu.SemaphoreType.DMA((2,)), pltpu.SemaphoreType.DMA((2,)),
                        pltpu.SemaphoreType.DMA(())],
    )
    return pl.pallas_call(
        functools.partial(_expert_kernel, d_ff=d_ff, bm=bm, n_tok=n_tok),
        grid_spec=grid_spec,
        out_shape=jax.ShapeDtypeStruct(((TOP_K * n_tok + 2 * bm) * ROW_TILE, LANES), F32),
        compiler_params=_params(("arbitrary",)),
        name="expert_ffn",
    )(blk_expert, blk_valid, blk_src, blk_wslot, blk_next, xs, b_gu, b_down, w_gu, w_down)


def _combine_kernel(x_ref, gate_ref, g_ref, y0_ref, y1_ref, y2_ref, y3_ref, o_ref):
    tm, d = x_ref.shape
    gates = gate_ref[...]
    acc = x_ref[...]
    for k, y_ref in enumerate((y0_ref, y1_ref, y2_ref, y3_ref)):
        y = jnp.concatenate([y_ref[pl.ds(j, tm, stride=ROW_TILE), :]
                             for j in range(d // LANES)], axis=1)
        acc = acc + gates[:, k:k + 1] * y
    o_ref[...] = _rms(acc, g_ref[...])


def _combine(x2, gates, g_final, ys, *, tm):
    n_tok, d = x2.shape
    tiles = n_tok // tm
    slot_spec = lambda k: pl.BlockSpec((tm * ROW_TILE, LANES), lambda i: (k * tiles + i, 0))
    return pl.pallas_call(
        _combine_kernel,
        grid=(tiles,),
        in_specs=[
            pl.BlockSpec((tm, d), lambda i: (i, 0)),
            pl.BlockSpec((tm, LANES), lambda i: (i, 0)),
            pl.BlockSpec((1, d), lambda i: (0, 0)),
        ] + [slot_spec(k) for k in range(TOP_K)],
        out_specs=pl.BlockSpec((tm, d), lambda i: (i, 0)),
        out_shape=jax.ShapeDtypeStruct((n_tok, d), F32),
        compiler_params=_params(("arbitrary",)),
        name="combine",
    )(x2, gates, g_final, ys, ys, ys, ys)


def _routing_tables(route, cum, *, tm, bm, n_grid):
    expert = route[0:TOP_K]
    pos = route[TOP_K:2 * TOP_K]
    experts = jnp.arange(N_EXPERTS, dtype=I32)
    counts = cum[-1, 0, :N_EXPERTS].astype(I32)
    padded = (counts + bm - 1) // bm * bm
    pend = jnp.cumsum(padded)
    pstart = pend - padded
    owner_start = jnp.where(expert[None] == experts[:, None, None], pstart[:, None, None], 0)
    dest = pos + jnp.sum(owner_start, axis=0)

    blk = jnp.arange(n_grid, dtype=I32)
    blk_expert = jnp.minimum(jnp.sum(pend[None, :] <= (blk * bm)[:, None], axis=1),
                             N_EXPERTS - 1).astype(I32)
    n_used = pend[-1] // bm
    blk_valid = (blk < n_used).astype(I32)
    blk_src = jnp.minimum(blk, jnp.maximum(n_used - 1, 0)).astype(I32)
    zflag = (padded > 0).astype(I32)
    zstart = jnp.maximum(pend - bm, 0).astype(I32)
    wslot = (jnp.cumsum(zflag) - 1) % 2
    later = jnp.logical_and(experts[None, :] > experts[:, None], zflag[None, :] > 0)
    nxt = jnp.min(jnp.where(later, experts[None, :], N_EXPERTS), axis=1)
    nxt = jnp.where(nxt < N_EXPERTS, nxt, -1)
    owner = blk_expert[:, None] == experts[None, :]
    of_block = lambda v: jnp.sum(jnp.where(owner, v[None, :], 0), axis=1).astype(I32)
    dest = dest.astype(I32).reshape(TOP_K, -1, tm).transpose(1, 0, 2).reshape(-1)
    return (dest, blk_expert, blk_valid, blk_src, of_block(wslot),
            of_block(nxt), zstart, zflag, n_used.astype(I32).reshape(1))


def kernel(x, mem, positions, norm_mix, w_in, lambda_q1, lambda_k1, lambda_q2, lambda_k2,
           diff_subln, w_up_sb, w_up_diff, w_out, norm_mem_q, norm_mem_kv, w_mem_q, w_mem_kv,
           w_mem_o, norm_ffn, w_router, b_router, w_gate_up, b_gate_up, w_down, b_down,
           norm_final):
    b, s, d = x.shape
    n_tok = b * s
    depth = norm_mix.shape[0]
    sb_width = w_up_sb.shape[1]
    diff_width = w_up_diff.shape[1]
    n_in = w_in.shape[2]
    chunk = 512
    assert sb_width == chunk and diff_width == chunk and d == 2 * chunk
    sbq, sbk, sbv, dq, dk, dv = range(6)
    blocks_per_chunk = chunk // LANES
    scale = 1.0 / math.sqrt(SB_HEAD_DIM)
    assert SB_HEAD_DIM == DIFF_HEAD_DIM

    tm_in = min(512, n_tok)
    tq_sb = min(256, s)
    tq_diff = min(512, s)
    tm_post = min(512, s)
    tm_tok = min(512, n_tok)
    bm = 256
    n_grid = (n_tok * TOP_K) // bm + N_EXPERTS

    x2d = x.reshape(n_tok, d)
    for l in range(depth):
        lambda_init = 0.8 - 0.6 * math.exp(-0.3 * l)
        proj = _in_proj(x2d, norm_mix[l].reshape(1, d), w_in[l], positions,
                        tm=tm_in, chunk=chunk, rope_chunks=(dq, dk), scale_chunks=(sbq, dq),
                        scale=scale)
        proj3 = proj.reshape(b, s, n_in)
        o_sb = _sb_attention(proj3, q_col=sbq * blocks_per_chunk, k_col=sbk * blocks_per_chunk,
                             v_col=sbv * blocks_per_chunk, n_pairs=sb_width // LANES, tq=tq_sb)
        o_diff = _diff_attention(
            proj3, lambda_q1[l].reshape(1, -1), lambda_k1[l].reshape(1, -1),
            lambda_q2[l].reshape(1, -1), lambda_k2[l].reshape(1, -1),
            diff_subln[l].reshape(1, -1), q_col=dq * blocks_per_chunk,
            k_col=dk * blocks_per_chunk, v_col=dv * blocks_per_chunk,
            n_heads=diff_width // DIFF_V_DIM, tq=tq_diff, lambda_init=lambda_init)
        kv = _mem_kv(mem, norm_mem_kv[l].reshape(1, d), w_mem_kv[l].astype(BF16))
        w_router_pad = jnp.zeros((d, LANES), F32).at[:, :N_EXPERTS].set(w_router[l])
        b_router_pad = jnp.full((1, LANES), NEG_BIG, F32).at[0, :N_EXPERTS].set(b_router[l])
        x_res, hf, route, gates, cum = _post_mix(
            x2d, proj, o_sb.reshape(n_tok, sb_width), o_diff.reshape(n_tok, diff_width),
            w_up_sb[l].astype(BF16), w_up_diff[l].astype(BF16), w_out[l].astype(BF16),
            norm_mem_q[l].reshape(1, d), w_mem_q[l].astype(BF16), kv, w_mem_o[l].astype(BF16),
            norm_ffn[l].reshape(1, d), w_router_pad, b_router_pad,
            tm=tm_post, seq=s, gate_col=6 * chunk // d)
        (dest, blk_expert, blk_valid, blk_src, blk_wslot, blk_next, zstart, zflag,
         n_used) = _routing_tables(route, cum, tm=tm_tok, bm=bm, n_grid=n_grid + 1)
        xs = _dispatch(zstart, zflag, n_used, dest, hf, tm=tm_tok, bm=bm, n_rows=n_grid * bm)
        ys = _expert_ffn(blk_expert, blk_valid, blk_src, blk_wslot, blk_next, xs, w_gate_up[l],
                         b_gate_up[l], w_down[l], b_down[l], bm=bm, n_tok=n_tok)
        g_next = norm_final if l == depth - 1 else None
        assert g_next is not None, "only depth 1 is wired: the final norm is fused into combine"
        x2d = _combine(x_res, gates, g_next.reshape(1, d), ys, tm=tm_post)
    return x2d.reshape(b, s, d)
```
